```python
import math
import jax, jax.numpy as jnp
from jax import lax
import numpy as np

D_MODEL = 4096
BATCH = 2
SEQ = 4096
DEPTH = 1

N_META = 16
SSM_D_INNER = 2 * D_MODEL
SSM_HEAD_DIM = 64
SSM_HEADS = SSM_D_INNER // SSM_HEAD_DIM
SSM_GROUPS = 8
SSM_STATE = 128
SSM_CONV = 4
SSM_CHUNK = 128
SSM_CONV_DIM = SSM_D_INNER + 2 * SSM_GROUPS * SSM_STATE
DT_MIN = 1e-3
DT_MAX = 1e-1
MLA_HEADS = 64
MLA_Q_RANK = D_MODEL // 4
MLA_KV_RANK = 512
MLA_NOPE = 128
MLA_ROPE = 64
MLA_V = 128
ROPE_THETA = 10000.0
ATTN_BLOCK = 128
N_BRANCH = 2
IN_WIDTHS = (SSM_D_INNER, SSM_CONV_DIM, SSM_HEADS, MLA_Q_RANK, MLA_KV_RANK, MLA_ROPE, N_BRANCH * D_MODEL)
IN_TOTAL = sum(IN_WIDTHS)
N_EXPERTS = 64
N_EXPERT_GROUPS = 8
TOPK_GROUPS = 4
TOP_K = 8
EXPERT_FF = 768
SHARED_FF = 768
ROUTED_SCALE = 2.5
MOE_BLOCK = 128
DN_ALPHA = (2.0 * DEPTH) ** 0.25
DN_BETA = (8.0 * DEPTH) ** -0.25
LN_EPS = 1e-5
RMS_EPS = 1e-6

kernel_name = 'hybrid_ssd_mla_moe_deepnorm_layer'


def layer_norm(x, g, b):
    xf = x.astype(jnp.float32)
    mu = jnp.mean(xf, axis=-1, keepdims=True)
    var = jnp.mean(jnp.square(xf - mu), axis=-1, keepdims=True)
    return ((xf - mu) * lax.rsqrt(var + LN_EPS) * g + b).astype(x.dtype)


def rms_norm(x, g):
    xf = x.astype(jnp.float32)
    return (xf * lax.rsqrt(jnp.mean(jnp.square(xf), axis=-1, keepdims=True) + RMS_EPS) * g).astype(x.dtype)


def rope_tables(pos):
    inv_freq = ROPE_THETA ** (-jnp.arange(0, MLA_ROPE, 2, dtype=jnp.float32) / MLA_ROPE)
    ang = pos.astype(jnp.float32)[..., None] * inv_freq
    return jnp.cos(ang), jnp.sin(ang)


def rotary(u, cos, sin):
    u1, u2 = jnp.split(u.astype(jnp.float32), 2, axis=-1)
    return jnp.concatenate([u1 * cos - u2 * sin, u2 * cos + u1 * sin], axis=-1)


def causal_depthwise_conv(u, w, b):
    k = w.shape[0]
    out = lax.conv_general_dilated(u, w[:, None, :].astype(u.dtype), window_strides=(1,), padding=[(k - 1, 0)],
                                   dimension_numbers=('NWC', 'WIO', 'NWC'), feature_group_count=u.shape[-1])
    return out + b


def ssd_mixer(z, xbc, dt_raw, conv_w, conv_b, dt_bias, a_log, d_skip, norm_g):
    bsz, seq_len, _ = z.shape
    q, g, hg, p, n = SSM_CHUNK, SSM_GROUPS, SSM_HEADS // SSM_GROUPS, SSM_HEAD_DIM, SSM_STATE
    xbc = jax.nn.silu(causal_depthwise_conv(xbc, conv_w, conv_b))
    xs, b_in, c_in = jnp.split(xbc, [SSM_D_INNER, SSM_D_INNER + g * n], axis=-1)
    dt = jax.nn.softplus((dt_raw + dt_bias).astype(jnp.float32))
    a = -jnp.exp(a_log.astype(jnp.float32)).reshape(g, hg)
    pad = (-N_META) % q
    pad_end = (-(pad + seq_len)) % q
    nc = (pad + seq_len + pad_end) // q

    def to_chunks(u, *tail):
        u = jnp.pad(u.astype(jnp.float32), ((0, 0), (pad, pad_end)) + ((0, 0),) * (u.ndim - 2))
        return u.reshape(bsz, nc, q, *tail)

    xs = to_chunks(xs, g, hg, p)
    b_c = to_chunks(b_in, g, n)
    c_c = to_chunks(c_in, g, n)
    dt_c = to_chunks(dt, g, hg)
    a_cs = jnp.cumsum(dt_c * a, axis=2)
    xdt = xs * dt_c[..., None]
    causal = jnp.tril(jnp.ones((q, q), dtype=bool))
    seg = a_cs[:, :, :, None] - a_cs[:, :, None, :]
    decay = jnp.exp(jnp.where(causal[:, :, None, None], seg, -jnp.inf))
    cb = jnp.einsum('bclgn,bcsgn->bclsg', c_c, b_c)
    y_diag = jnp.einsum('bclsgh,bcsghp->bclghp', decay * cb[..., None], xdt)
    decay_end = jnp.exp(a_cs[:, :, -1:] - a_cs)
    states = jnp.einsum('bcsgn,bcsghp->bcghpn', b_c, xdt * decay_end[..., None])
    chunk_decay = jnp.exp(a_cs[:, :, -1])

    def carry_state(h_prev, inp):
        st, dec = inp
        return h_prev * dec[..., None, None] + st, h_prev

    h0 = jnp.zeros((bsz, g, hg, p, n), jnp.float32)
    _, prev = lax.scan(carry_state, h0, (jnp.moveaxis(states, 1, 0), jnp.moveaxis(chunk_decay, 1, 0)))
    prev = jnp.moveaxis(prev, 0, 1)
    y_off = jnp.einsum('bclgn,bcghpn->bclghp', c_c, prev) * jnp.exp(a_cs)[..., None]
    y = y_diag + y_off + xs * d_skip.astype(jnp.float32).reshape(g, hg, 1)
    y = y.reshape(bsz, nc * q, SSM_D_INNER)[:, pad:pad + seq_len]
    yz = y * jax.nn.silu(z.astype(jnp.float32))
    yg = yz.reshape(bsz, seq_len, g, SSM_D_INNER // g)
    yg = yg * lax.rsqrt(jnp.mean(jnp.square(yg), axis=-1, keepdims=True) + RMS_EPS)
    return (yg.reshape(bsz, seq_len, SSM_D_INNER) * norm_g).astype(z.dtype)


def mla_mixer(q_a, kv_a, k_rope, cos, sin, q_a_norm_g, w_q_b, kv_a_norm_g, w_kv_b):
    bsz, seq_len, _ = q_a.shape
    hh = MLA_HEADS
    qf = (rms_norm(q_a, q_a_norm_g) @ w_q_b).reshape(bsz, seq_len, hh, MLA_NOPE + MLA_ROPE)
    q_nope, q_rope = jnp.split(qf, [MLA_NOPE], axis=-1)
    q_rope = rotary(q_rope, cos[:, :, None, :], sin[:, :, None, :]).astype(q_nope.dtype)
    kv = (rms_norm(kv_a, kv_a_norm_g) @ w_kv_b).reshape(bsz, seq_len, hh, MLA_NOPE + MLA_V)
    k_nope, v = jnp.split(kv, [MLA_NOPE], axis=-1)
    k_r = rotary(k_rope, cos, sin).astype(k_nope.dtype)
    scale = (MLA_NOPE + MLA_ROPE) ** -0.5
    nq = -(-seq_len // ATTN_BLOCK)
    lq = nq * ATTN_BLOCK

    def to_blocks(u):
        u = jnp.pad(u, ((0, 0), (0, lq - seq_len), (0, 0), (0, 0)))
        return jnp.moveaxis(u.reshape(bsz, nq, ATTN_BLOCK, *u.shape[2:]), 1, 0)

    key_pos = jnp.arange(seq_len)

    def attend_block(args):
        qn_b, qr_b, start = args
        s = (jnp.einsum('bqhd,bkhd->bhqk', qn_b, k_nope) + jnp.einsum('bqhr,bkr->bhqk', qr_b, k_r)).astype(jnp.float32) * scale
        qpos = start + jnp.arange(ATTN_BLOCK)
        s = jnp.where(key_pos[None, :] <= qpos[:, None], s, -jnp.inf)
        prob = jax.nn.softmax(s, axis=-1)
        return jnp.einsum('bhqk,bkhd->bqhd', prob.astype(v.dtype), v)

    starts = jnp.arange(nq, dtype=jnp.int32) * ATTN_BLOCK
    out = lax.map(attend_block, (to_blocks(q_nope), to_blocks(q_rope), starts))
    return jnp.moveaxis(out, 0, 1).reshape(bsz, lq, hh * MLA_V)[:, :seq_len]


def moe_ffn(h, w_router, router_bias, w_exp_gate, w_exp_up, w_exp_down, w_sh_gate, w_sh_up, w_sh_down):
    bsz, seq_len, d = h.shape
    t = bsz * seq_len
    xt = h.reshape(t, d)
    scores = jax.nn.sigmoid((xt @ w_router).astype(jnp.float32))
    choice = scores + router_bias.astype(jnp.float32)
    grp_score = lax.top_k(choice.reshape(t, N_EXPERT_GROUPS, -1), 2)[0].sum(-1)
    _, top_grp = lax.top_k(grp_score, TOPK_GROUPS)
    grp_mask = jnp.any(top_grp[..., None] == jnp.arange(N_EXPERT_GROUPS), axis=1)
    exp_mask = jnp.repeat(grp_mask, N_EXPERTS // N_EXPERT_GROUPS, axis=-1)
    _, top_e = lax.top_k(jnp.where(exp_mask, choice, -jnp.inf), TOP_K)
    wts = jnp.take_along_axis(scores, top_e, axis=-1)
    wts = wts / jnp.sum(wts, axis=-1, keepdims=True) * ROUTED_SCALE
    tk = t * TOP_K
    flat_e = top_e.reshape(tk)
    flat_tok = jnp.arange(tk, dtype=jnp.int32) // TOP_K
    flat_w = wts.reshape(tk)
    order = jnp.argsort(flat_e, stable=True)
    sorted_e = flat_e[order]
    counts = jnp.zeros((N_EXPERTS,), jnp.int32).at[flat_e].add(1)
    padded = (counts + MOE_BLOCK - 1) // MOE_BLOCK * MOE_BLOCK
    pad_end = jnp.cumsum(padded)
    pad_start = pad_end - padded
    start = jnp.cumsum(counts) - counts
    dest = pad_start[sorted_e] + jnp.arange(tk, dtype=jnp.int32) - start[sorted_e]
    n_blocks = -(-tk // MOE_BLOCK) + N_EXPERTS
    n_rows = n_blocks * MOE_BLOCK
    row_tok = jnp.full((n_rows,), t, jnp.int32).at[dest].set(flat_tok[order])
    row_w = jnp.zeros((n_rows,), jnp.float32).at[dest].set(flat_w[order])
    block_e = jnp.minimum(jnp.searchsorted(pad_end, jnp.arange(n_blocks, dtype=jnp.int32) * MOE_BLOCK, side='right'), N_EXPERTS - 1)
    x_pad = jnp.concatenate([xt, jnp.zeros((1, d), xt.dtype)], axis=0)

    def expert_block(acc, inp):
        tok, wt, e = inp
        xb = x_pad[tok]
        hb = jax.nn.silu(xb @ w_exp_gate[e]) * (xb @ w_exp_up[e])
        yb = (hb @ w_exp_down[e]).astype(jnp.float32) * wt[:, None]
        return acc.at[tok].add(yb), None

    acc0 = jnp.zeros((t + 1, d), jnp.float32)
    acc, _ = lax.scan(expert_block, acc0, (row_tok.reshape(n_blocks, MOE_BLOCK), row_w.reshape(n_blocks, MOE_BLOCK), block_e))
    shared = (jax.nn.silu(xt @ w_sh_gate) * (xt @ w_sh_up)) @ w_sh_down
    return (acc[:t] + shared.astype(jnp.float32)).astype(h.dtype).reshape(bsz, seq_len, d)


def setup_inputs(seed: int = 0) -> dict:
    key = jax.random.key(seed)
    ks = iter(jax.random.split(key, 48))

    def nrm(shape, scale):
        return jax.random.normal(next(ks), shape, jnp.float32) * scale

    def gain(shape):
        return 1.0 + nrm(shape, 0.01)

    x = nrm((BATCH, SEQ, D_MODEL), 1.0)
    positions = jnp.arange(SEQ, dtype=jnp.int32)[None, :] + jax.random.randint(next(ks), (BATCH, 1), 0, 1024, dtype=jnp.int32)
    meta_tokens = nrm((N_META, D_MODEL), 1.0)
    ln_in_g = gain((D_MODEL,))
    ln_in_b = nrm((D_MODEL,), 0.01)
    w_in = nrm((DEPTH, D_MODEL, IN_TOTAL), D_MODEL ** -0.5)
    b_gate = nrm((DEPTH, N_BRANCH * D_MODEL), 0.01)
    conv_w = nrm((DEPTH, SSM_CONV, SSM_CONV_DIM), SSM_CONV ** -0.5)
    conv_b = nrm((DEPTH, SSM_CONV_DIM), 0.01)
    u = jax.random.uniform(next(ks), (DEPTH, SSM_HEADS), jnp.float32)
    dt0 = jnp.exp(u * (math.log(DT_MAX) - math.log(DT_MIN)) + math.log(DT_MIN))
    dt_bias = dt0 + jnp.log(-jnp.expm1(-dt0))
    a_log = jnp.log(jax.random.uniform(next(ks), (DEPTH, SSM_HEADS), jnp.float32, 1.0, 16.0))
    d_skip = gain((DEPTH, SSM_HEADS))
    ssm_norm_g = gain((DEPTH, SSM_D_INNER))
    w_ssm_proj = nrm((DEPTH, SSM_D_INNER, D_MODEL), SSM_D_INNER ** -0.5 * DN_BETA)
    q_a_norm_g = gain((DEPTH, MLA_Q_RANK))
    w_q_b = nrm((DEPTH, MLA_Q_RANK, MLA_HEADS * (MLA_NOPE + MLA_ROPE)), MLA_Q_RANK ** -0.5)
    kv_a_norm_g = gain((DEPTH, MLA_KV_RANK))
    kv_scale = jnp.tile(jnp.concatenate([jnp.ones((MLA_NOPE,), jnp.float32), jnp.full((MLA_V,), DN_BETA, jnp.float32)]), MLA_HEADS)
    w_kv_b = nrm((DEPTH, MLA_KV_RANK, MLA_HEADS * (MLA_NOPE + MLA_V)), MLA_KV_RANK ** -0.5) * kv_scale
    w_attn_proj = nrm((DEPTH, MLA_HEADS * MLA_V, D_MODEL), (MLA_HEADS * MLA_V) ** -0.5 * DN_BETA)
    w_out = nrm((DEPTH, D_MODEL, D_MODEL), D_MODEL ** -0.5 * DN_BETA)
    ln1_g = gain((DEPTH, D_MODEL))
    ln1_b = nrm((DEPTH, D_MODEL), 0.01)
    w_router = nrm((DEPTH, D_MODEL, N_EXPERTS), D_MODEL ** -0.5)
    router_bias = nrm((DEPTH, N_EXPERTS), 0.01)
    w_exp_gate = nrm((DEPTH, N_EXPERTS, D_MODEL, EXPERT_FF), D_MODEL ** -0.5)
    w_exp_up = nrm((DEPTH, N_EXPERTS, D_MODEL, EXPERT_FF), D_MODEL ** -0.5 * DN_BETA)
    w_exp_down = nrm((DEPTH, N_EXPERTS, EXPERT_FF, D_MODEL), EXPERT_FF ** -0.5 * DN_BETA)
    w_sh_gate = nrm((DEPTH, D_MODEL, SHARED_FF), D_MODEL ** -0.5)
    w_sh_up = nrm((DEPTH, D_MODEL, SHARED_FF), D_MODEL ** -0.5 * DN_BETA)
    w_sh_down = nrm((DEPTH, SHARED_FF, D_MODEL), SHARED_FF ** -0.5 * DN_BETA)
    ln2_g = gain((DEPTH, D_MODEL))
    ln2_b = nrm((DEPTH, D_MODEL), 0.01)
    return {'x': x, 'positions': positions, 'meta_tokens': meta_tokens, 'ln_in_g': ln_in_g, 'ln_in_b': ln_in_b,
            'w_in': w_in, 'b_gate': b_gate, 'conv_w': conv_w, 'conv_b': conv_b, 'dt_bias': dt_bias, 'a_log': a_log,
            'd_skip': d_skip, 'ssm_norm_g': ssm_norm_g, 'w_ssm_proj': w_ssm_proj, 'q_a_norm_g': q_a_norm_g,
            'w_q_b': w_q_b, 'kv_a_norm_g': kv_a_norm_g, 'w_kv_b': w_kv_b, 'w_attn_proj': w_attn_proj, 'w_out': w_out,
            'ln1_g': ln1_g, 'ln1_b': ln1_b, 'w_router': w_router, 'router_bias': router_bias,
            'w_exp_gate': w_exp_gate, 'w_exp_up': w_exp_up, 'w_exp_down': w_exp_down,
            'w_sh_gate': w_sh_gate, 'w_sh_up': w_sh_up, 'w_sh_down': w_sh_down, 'ln2_g': ln2_g, 'ln2_b': ln2_b}


def reference(x, positions, meta_tokens, ln_in_g, ln_in_b, w_in, b_gate, conv_w, conv_b, dt_bias, a_log, d_skip,
              ssm_norm_g, w_ssm_proj, q_a_norm_g, w_q_b, kv_a_norm_g, w_kv_b, w_attn_proj, w_out, ln1_g, ln1_b,
              w_router, router_bias, w_exp_gate, w_exp_up, w_exp_down, w_sh_gate, w_sh_up, w_sh_down, ln2_g, ln2_b):
    bsz = x.shape[0]
    meta = jnp.broadcast_to(meta_tokens[None].astype(x.dtype), (bsz, N_META, D_MODEL))
    h = layer_norm(jnp.concatenate([meta, x], axis=1), ln_in_g, ln_in_b)
    pos = jnp.concatenate([jnp.broadcast_to(jnp.arange(N_META, dtype=jnp.int32), (bsz, N_META)),
                           positions.astype(jnp.int32) + N_META], axis=1)
    cos, sin = rope_tables(pos)
    offs = np.cumsum(IN_WIDTHS)[:-1].tolist()
    for l in range(DEPTH):
        proj = h @ w_in[l]
        z, xbc, dt_raw, q_a, kv_a, k_rope, gate_pre = jnp.split(proj, offs, axis=-1)
        gates = jax.nn.sigmoid((gate_pre + b_gate[l]).astype(jnp.float32)).astype(h.dtype)
        g_ssm, g_attn = jnp.split(gates, N_BRANCH, axis=-1)
        y_ssm = ssd_mixer(z, xbc, dt_raw, conv_w[l], conv_b[l], dt_bias[l], a_log[l], d_skip[l], ssm_norm_g[l]) @ w_ssm_proj[l]
        y_attn = mla_mixer(q_a, kv_a, k_rope, cos, sin, q_a_norm_g[l], w_q_b[l], kv_a_norm_g[l], w_kv_b[l]) @ w_attn_proj[l]
        mixed = (g_ssm * y_ssm + g_attn * y_attn) @ w_out[l]
        h = layer_norm(DN_ALPHA * h + mixed, ln1_g[l], ln1_b[l])
        ffn = moe_ffn(h, w_router[l], router_bias[l], w_exp_gate[l], w_exp_up[l], w_exp_down[l],
                      w_sh_gate[l], w_sh_up[l], w_sh_down[l])
        h = layer_norm(DN_ALPHA * h + ffn, ln2_g[l], ln2_b[l])
    return h[:, N_META:, :]
```

```python
import functools
import math

import jax
import jax.numpy as jnp
from jax import lax
from jax.experimental import pallas as pl
from jax.experimental.pallas import tpu as pltpu

F32 = jnp.float32
BF16 = jnp.bfloat16
U32 = jnp.uint32
I32 = jnp.int32

N_META = 16
CHUNK = 128
META_PAD = CHUNK - N_META
SSM_HEAD_DIM = 64
SSM_GROUPS = 8
SSM_STATE = 128
SSM_CONV = 4
MLA_HEADS = 64
MLA_NOPE = 128
MLA_ROPE = 64
MLA_V = 128
MLA_QK = MLA_NOPE + MLA_ROPE
ROPE_THETA = 10000.0
N_EXPERTS = 64
N_EXPERT_GROUPS = 8
TOPK_GROUPS = 4
TOP_K = 8
ROUTED_SCALE = 2.5
LN_EPS = 1e-5
RMS_EPS = 1e-6
NEG = -1e30
LANES = 128
MOE_ROWS = 256
VMEM_BIG = 56 * 1024 * 1024
VMEM_MID = 44 * 1024 * 1024


def _cparams(sem, vmem=VMEM_MID):
    return pltpu.CompilerParams(dimension_semantics=sem, vmem_limit_bytes=vmem)


def _pick(n, cands):
    for c in cands:
        if n % c == 0:
            return c
    raise ValueError(f"no tile for {n} in {cands}")


def _ln_body(x_ref, g_ref, b_ref, of_ref, ob_ref):
    x = x_ref[...]
    mu = jnp.mean(x, axis=-1, keepdims=True)
    xc = x - mu
    var = jnp.mean(xc * xc, axis=-1, keepdims=True)
    y = xc * lax.rsqrt(var + LN_EPS) * g_ref[...] + b_ref[...]
    of_ref[...] = y
    ob_ref[...] = y.astype(BF16)


def _layer_norm(x3, g, b):
    bsz, rows, d = x3.shape
    tm = _pick(rows, (256, 128))
    spec = pl.BlockSpec((None, tm, d), lambda bi, i: (bi, i, 0))
    vec = pl.BlockSpec((1, d), lambda bi, i: (0, 0))
    return pl.pallas_call(
        _ln_body,
        grid=(bsz, rows // tm),
        in_specs=[spec, vec, vec],
        out_specs=[spec, spec],
        out_shape=[jax.ShapeDtypeStruct(x3.shape, F32), jax.ShapeDtypeStruct(x3.shape, BF16)],
        compiler_params=_cparams(("parallel", "parallel")),
        name="layer_norm",
    )(x3, g.reshape(1, d), b.reshape(1, d))


def _pack_halves(y):
    n = y.shape[1] // 2
    lo = pltpu.bitcast(y[:, :n].astype(BF16).astype(F32), U32) >> 16
    hi = pltpu.bitcast(y[:, n:].astype(BF16).astype(F32), U32) & jnp.uint32(0xFFFF0000)
    return hi | lo


def _unpack_halves(w):
    lo = pltpu.bitcast(w << 16, F32)
    hi = pltpu.bitcast(w & jnp.uint32(0xFFFF0000), F32)
    return lo, hi


def _ln_pack_body(x_ref, g_ref, b_ref, of_ref, op_ref):
    x = x_ref[...]
    mu = jnp.mean(x, axis=-1, keepdims=True)
    xc = x - mu
    var = jnp.mean(xc * xc, axis=-1, keepdims=True)
    y = xc * lax.rsqrt(var + LN_EPS) * g_ref[...] + b_ref[...]
    of_ref[...] = y
    op_ref[...] = _pack_halves(y)


def _layer_norm_pack(x2, g, b):
    rows, d = x2.shape
    tm = _pick(rows, (256, 128))
    vec = pl.BlockSpec((1, d), lambda i: (0, 0))
    return pl.pallas_call(
        _ln_pack_body,
        grid=(rows // tm,),
        in_specs=[pl.BlockSpec((tm, d), lambda i: (i, 0)), vec, vec],
        out_specs=[pl.BlockSpec((tm, d), lambda i: (i, 0)), pl.BlockSpec((tm, d // 2), lambda i: (i, 0))],
        out_shape=[jax.ShapeDtypeStruct((rows, d), F32), jax.ShapeDtypeStruct((rows, d // 2), U32)],
        compiler_params=_cparams(("parallel",)),
        name="layer_norm_pack",
    )(x2, g.reshape(1, d), b.reshape(1, d))


def _mm_body(*refs, has_gain, has_bias, act, rope, kadd, scale):
    it = iter(refs)
    a_ref, w_ref = next(it), next(it)
    gain_ref = next(it) if has_gain else None
    bias_ref = next(it) if has_bias else None
    rope_refs = [next(it) for _ in range(3)] if rope else None
    kadd_refs = [next(it) for _ in range(5)] if kadd else None
    o_ref = next(it)
    a = a_ref[...]
    if has_gain:
        af = a.astype(F32)
        a = af * lax.rsqrt(jnp.mean(af * af, axis=-1, keepdims=True) + RMS_EPS) * gain_ref[...]
    r = jnp.dot(a.astype(BF16), w_ref[...], preferred_element_type=F32)
    tn = r.shape[1]
    if has_bias:
        r = r + bias_ref[...]
    if act == "sigmoid":
        r = jax.nn.sigmoid(r)
    if rope:
        rep = tn // rope_refs[0].shape[1]
        c, slo, shi = (jnp.concatenate([t[...]] * rep, axis=1) for t in rope_refs)
        r = r * c + pltpu.roll(r, tn - MLA_ROPE // 2, 1) * slo + pltpu.roll(r, MLA_ROPE // 2, 1) * shi
    if kadd:
        kr_ref, clo, slo, chi, shi = kadd_refs
        kr = kr_ref[...]
        krr = pltpu.roll(kr, MLA_ROPE // 2, 1)
        piece = jnp.concatenate([jnp.zeros_like(kr), kr * clo[...] + krr * slo[...], kr * chi[...] + krr * shi[...]], axis=1)
        r = r + jnp.concatenate([piece] * (tn // piece.shape[1]), axis=1)
    if scale is not None:
        r = r * scale
    o_ref[...] = r.astype(o_ref.dtype)


def _mm(a3, w, *, out_dtype, tm, tn, n_row_tiles=None, gain=None, bias=None, act=None, rope=None, kadd=None,
        scale=None, name="mm"):
    bsz, rows, k = a3.shape
    n = w.shape[1]
    ni = rows // tm if n_row_tiles is None else n_row_tiles
    assert n % tn == 0 and (n_row_tiles is not None or rows % tm == 0)
    in_specs = [pl.BlockSpec((None, tm, k), lambda bi, i, j: (bi, i, 0)),
                pl.BlockSpec((k, tn), lambda bi, i, j: (0, j))]
    args = [a3, w]
    if gain is not None:
        in_specs.append(pl.BlockSpec((1, k), lambda bi, i, j: (0, 0)))
        args.append(gain.reshape(1, k).astype(F32))
    if bias is not None:
        in_specs.append(pl.BlockSpec((1, tn), lambda bi, i, j: (0, j)))
        args.append(bias.reshape(1, n).astype(F32))
    for t in (rope or ()) + (kadd or ()):
        in_specs.append(pl.BlockSpec((None, tm, t.shape[2]), lambda bi, i, j: (bi, i, 0)))
        args.append(t)
    body = functools.partial(_mm_body, has_gain=gain is not None, has_bias=bias is not None, act=act,
                             rope=rope is not None, kadd=kadd is not None, scale=scale)
    return pl.pallas_call(
        body,
        grid=(bsz, ni, n // tn),
        in_specs=in_specs,
        out_specs=pl.BlockSpec((None, tm, tn), lambda bi, i, j: (bi, i, j)),
        out_shape=jax.ShapeDtypeStruct((bsz, ni * tm, n), out_dtype),
        compiler_params=_cparams(("parallel", "parallel", "arbitrary"), VMEM_BIG),
        name=name,
    )(*args)


def _mmk_body(*refs, n_pairs, has_gate, has_res, res_scale):
    it = iter(refs)
    a_refs = [next(it) for _ in range(n_pairs)]
    w_refs = [next(it) for _ in range(n_pairs)]
    g_refs = [next(it) for _ in range(n_pairs)] if has_gate else None
    res_ref = next(it) if has_res else None
    o_ref = next(it)
    acc_refs = [next(it) for _ in range(n_pairs)]
    kk = pl.program_id(3)

    @pl.when(kk == 0)
    def _():
        for acc in acc_refs:
            acc[...] = jnp.zeros_like(acc)

    for a_ref, w_ref, acc in zip(a_refs, w_refs, acc_refs):
        acc[...] += jnp.dot(a_ref[...], w_ref[...], preferred_element_type=F32)

    @pl.when(kk == pl.num_programs(3) - 1)
    def _():
        r = None
        for p, acc in enumerate(acc_refs):
            t = acc[...]
            if has_gate:
                t = t * g_refs[p][...].astype(F32)
            r = t if r is None else r + t
        if has_res:
            r = r + res_scale * res_ref[...]
        o_ref[...] = r.astype(o_ref.dtype)


def _mm_ktiled(a_list, w_list, *, out_dtype, rows, tm, tn, nk, gates=None, res=None, res_scale=1.0, name="mmk"):
    bsz = a_list[0].shape[0]
    n = w_list[0].shape[1]
    npairs = len(a_list)
    tks = [a.shape[2] // nk for a in a_list]
    assert all(a.shape[2] == tk * nk and tk % LANES == 0 for a, tk in zip(a_list, tks))
    o_spec = pl.BlockSpec((None, tm, tn), lambda bi, i, j, kk: (bi, i, j))
    in_specs = ([pl.BlockSpec((None, tm, tk), lambda bi, i, j, kk: (bi, i, kk)) for tk in tks]
                + [pl.BlockSpec((tk, tn), lambda bi, i, j, kk: (kk, j)) for tk in tks])
    args = list(a_list) + list(w_list)
    if gates is not None:
        in_specs += [o_spec] * npairs
        args += list(gates)
    if res is not None:
        in_specs.append(o_spec)
        args.append(res)
    body = functools.partial(_mmk_body, n_pairs=npairs, has_gate=gates is not None, has_res=res is not None,
                             res_scale=res_scale)
    return pl.pallas_call(
        body,
        grid=(bsz, rows // tm, n // tn, nk),
        in_specs=in_specs,
        out_specs=o_spec,
        out_shape=jax.ShapeDtypeStruct((bsz, rows, n), out_dtype),
        scratch_shapes=[pltpu.VMEM((tm, tn), F32) for _ in range(npairs)],
        compiler_params=_cparams(("parallel", "parallel", "parallel", "arbitrary"), VMEM_BIG),
        name=name,
    )(*args)


def _silu(x):
    return x * jax.nn.sigmoid(x)


def _ssd_body(xs_ref, b_ref, c_ref, z_ref, dt_ref, wx_ref, wb_ref, wc_ref, bx_ref, bb_ref, bc_ref,
              dtb_ref, alog_ref, dskip_ref, ng_ref, e_ref, o_ref,
              state_ref, extx_ref, extb_ref, extc_ref, y_ref, *, hg):
    c = pl.program_id(2)
    q = CHUNK
    p = SSM_HEAD_DIM
    first = c == 0
    row = lax.broadcasted_iota(I32, (q, 1), 0)
    live = jnp.logical_or(jnp.logical_not(first), row >= META_PAD)

    @pl.when(first)
    def _():
        state_ref[...] = jnp.zeros_like(state_ref)
        extx_ref[0:8, :] = jnp.zeros((8, extx_ref.shape[1]), F32)
        extb_ref[0:8, :] = jnp.zeros((8, extb_ref.shape[1]), F32)
        extc_ref[0:8, :] = jnp.zeros((8, extc_ref.shape[1]), F32)

    def conv(ext_ref, u_ref, w_ref, bias_ref):
        ext_ref[8:8 + q, :] = jnp.where(live, u_ref[...].astype(F32), 0.0)
        acc = bias_ref[...]
        for k in range(SSM_CONV):
            acc = acc + ext_ref[pl.ds(8 - (SSM_CONV - 1) + k, q), :] * w_ref[k:k + 1, :]
        ext_ref[0:8, :] = ext_ref[q:q + 8, :]
        return _silu(acc)

    xs = conv(extx_ref, xs_ref, wx_ref, bx_ref)
    bm = conv(extb_ref, b_ref, wb_ref, bb_ref)
    cm = conv(extc_ref, c_ref, wc_ref, bc_ref)

    x = dt_ref[...] + dtb_ref[...]
    dt = jnp.maximum(x, 0.0) + jnp.log1p(jnp.exp(-jnp.abs(x)))
    dt = jnp.where(live, dt, 0.0)
    a = -jnp.exp(alog_ref[...])
    r_i = lax.broadcasted_iota(I32, (q, q), 0)
    c_i = lax.broadcasted_iota(I32, (q, q), 1)
    causal = r_i >= c_i
    hi = lax.Precision.HIGHEST
    a_cs = jnp.dot(causal.astype(F32), dt * a, precision=hi, preferred_element_type=F32)
    a_cs_t = a_cs.T
    ea = jnp.exp(a_cs)
    decay_end = jnp.exp(a_cs[q - 1:q, :] - a_cs)
    e = e_ref[...]
    dt_x = jnp.dot(dt, e, precision=hi, preferred_element_type=F32)
    ea_x = jnp.dot(ea, e, precision=hi, preferred_element_type=F32)
    de_x = jnp.dot(decay_end, e, precision=hi, preferred_element_type=F32)

    xdt = xs * dt_x
    xdt_b = xdt.astype(BF16)
    cm_b = cm.astype(BF16)
    bm_b = bm.astype(BF16)
    cb = lax.dot_general(cm_b, bm_b, (((1,), (1,)), ((), ())), preferred_element_type=F32)
    prev = state_ref[...]
    y_off = jnp.dot(cm_b, prev.astype(BF16), preferred_element_type=F32) * ea_x
    for j in range(hg):
        seg = a_cs[:, j:j + 1] - a_cs_t[j:j + 1, :]
        m = (jnp.exp(jnp.where(causal, seg, NEG)) * cb).astype(BF16)
        y_ref[:, j * p:(j + 1) * p] = jnp.dot(m, xdt_b[:, j * p:(j + 1) * p], preferred_element_type=F32)
    y = y_ref[...] + y_off + xs * dskip_ref[...]
    state_ref[...] = prev * ea_x[q - 1:q, :] + jnp.dot(bm.T.astype(BF16), (xdt * de_x).astype(BF16),
                                                        preferred_element_type=F32)
    yz = y * _silu(z_ref[...].astype(F32))
    yn = yz * lax.rsqrt(jnp.mean(yz * yz, axis=-1, keepdims=True) + RMS_EPS) * ng_ref[...]
    o_ref[...] = yn.astype(o_ref.dtype)


def _ssd(xbc3, z3, dt_g, conv_w, conv_b, dtb_g, alog_g, dskip_x, norm_g, d_inner, heads):
    bsz, rows, conv_dim = xbc3.shape
    g, n, p, q = SSM_GROUPS, SSM_STATE, SSM_HEAD_DIM, CHUNK
    hg = heads // g
    gw = hg * p
    nc = rows // q
    assert gw % LANES == 0 and d_inner % n == 0 and hg <= LANES
    boff = d_inner // n
    expand = (jnp.arange(LANES)[:, None] == (jnp.arange(gw) // p)[None, :]).astype(F32)

    def chunk(c):
        return (c + nc - 1) % nc

    in_specs = [
        pl.BlockSpec((None, q, gw), lambda b, gi, c: (b, chunk(c), gi)),
        pl.BlockSpec((None, q, n), lambda b, gi, c: (b, chunk(c), boff + gi)),
        pl.BlockSpec((None, q, n), lambda b, gi, c: (b, chunk(c), boff + g + gi)),
        pl.BlockSpec((None, q, gw), lambda b, gi, c: (b, chunk(c), gi)),
        pl.BlockSpec((None, None, q, LANES), lambda b, gi, c: (b, gi, chunk(c), 0)),
        pl.BlockSpec((SSM_CONV, gw), lambda b, gi, c: (0, gi)),
        pl.BlockSpec((SSM_CONV, n), lambda b, gi, c: (0, boff + gi)),
        pl.BlockSpec((SSM_CONV, n), lambda b, gi, c: (0, boff + g + gi)),
        pl.BlockSpec((1, gw), lambda b, gi, c: (0, gi)),
        pl.BlockSpec((1, n), lambda b, gi, c: (0, boff + gi)),
        pl.BlockSpec((1, n), lambda b, gi, c: (0, boff + g + gi)),
        pl.BlockSpec((None, 1, LANES), lambda b, gi, c: (gi, 0, 0)),
        pl.BlockSpec((None, 1, LANES), lambda b, gi, c: (gi, 0, 0)),
        pl.BlockSpec((1, gw), lambda b, gi, c: (0, gi)),
        pl.BlockSpec((1, gw), lambda b, gi, c: (0, gi)),
        pl.BlockSpec((LANES, gw), lambda b, gi, c: (0, 0)),
    ]
    return pl.pallas_call(
        functools.partial(_ssd_body, hg=hg),
        grid=(bsz, g, nc),
        in_specs=in_specs,
        out_specs=pl.BlockSpec((None, q, gw), lambda b, gi, c: (b, chunk(c), gi)),
        out_shape=jax.ShapeDtypeStruct((bsz, rows, d_inner), BF16),
        scratch_shapes=[pltpu.VMEM((n, gw), F32), pltpu.VMEM((q + 8, gw), F32), pltpu.VMEM((q + 8, n), F32),
                        pltpu.VMEM((q + 8, n), F32), pltpu.VMEM((q, gw), F32)],
        compiler_params=_cparams(("parallel", "parallel", "arbitrary")),
        name="ssd",
    )(xbc3, xbc3, xbc3, z3, dt_g, conv_w, conv_w, conv_w, conv_b, conv_b, conv_b, dtb_g, alog_g,
      dskip_x, norm_g, expand)


def _attn_body(q_ref, km_ref, vm_ref, k_ref, v_ref, o_ref, m_ref, l_ref, acc_ref, *, heads):
    qi, ki = pl.program_id(2), pl.program_id(3)
    nk = pl.num_programs(3)
    tq = q_ref.shape[0]

    @pl.when(ki == 0)
    def _():
        m_ref[...] = jnp.full_like(m_ref, NEG)
        l_ref[...] = jnp.zeros_like(l_ref)
        acc_ref[...] = jnp.zeros_like(acc_ref)

    def step(kr, vr, mask_fn):
        for h in range(heads):
            qh = q_ref[:, h * MLA_QK:(h + 1) * MLA_QK]
            kh = kr[:, h * MLA_QK:(h + 1) * MLA_QK]
            s = lax.dot_general(qh, kh, (((1,), (1,)), ((), ())), preferred_element_type=F32)
            if mask_fn is not None:
                s = mask_fn(s)
            m_prev = m_ref[h]
            m_new = jnp.maximum(m_prev, jnp.max(s, axis=1, keepdims=True))
            alpha = jnp.exp2(m_prev - m_new)
            pr = jnp.exp2(s - m_new[:, :1])
            l_ref[h] = alpha * l_ref[h] + jnp.sum(pr, axis=1, keepdims=True)
            sl = slice(h * MLA_V, (h + 1) * MLA_V)
            acc_ref[:, sl] = acc_ref[:, sl] * alpha + jnp.dot(pr.astype(BF16), vr[:, sl], preferred_element_type=F32)
            m_ref[h] = m_new

    def meta_mask(s):
        col = lax.broadcasted_iota(I32, s.shape, 1)
        return jnp.where(col >= META_PAD, s, NEG)

    def causal_mask(s):
        r_i = lax.broadcasted_iota(I32, s.shape, 0)
        c_i = lax.broadcasted_iota(I32, s.shape, 1)
        return jnp.where(r_i >= c_i, s, NEG)

    @pl.when(ki == 0)
    def _():
        step(km_ref, vm_ref, meta_mask)

    @pl.when(jnp.logical_and(ki >= 1, ki - 1 < qi))
    def _():
        step(k_ref, v_ref, None)

    @pl.when(ki - 1 == qi)
    def _():
        step(k_ref, v_ref, causal_mask)

    @pl.when(ki == nk - 1)
    def _():
        for h in range(heads):
            sl = slice(h * MLA_V, (h + 1) * MLA_V)
            o_ref[:, sl] = (acc_ref[:, sl] / l_ref[h]).astype(o_ref.dtype)


def _attention(q3, k3, v3, seq):
    bsz = q3.shape[0]
    hp = 2
    t = _pick(seq, (512, 256, 128))
    nq = seq // t
    meta_blk = seq // CHUNK
    qw, vw = hp * MLA_QK, hp * MLA_V

    def kv_idx(b, h, qi, ki):
        return (b, jnp.minimum(jnp.maximum(ki - 1, 0), qi), h)

    return pl.pallas_call(
        functools.partial(_attn_body, heads=hp),
        grid=(bsz, MLA_HEADS // hp, nq, nq + 1),
        in_specs=[
            pl.BlockSpec((None, t, qw), lambda b, h, qi, ki: (b, qi, h)),
            pl.BlockSpec((None, CHUNK, qw), lambda b, h, qi, ki: (b, meta_blk, h)),
            pl.BlockSpec((None, CHUNK, vw), lambda b, h, qi, ki: (b, meta_blk, h)),
            pl.BlockSpec((None, t, qw), kv_idx),
            pl.BlockSpec((None, t, vw), kv_idx),
        ],
        out_specs=pl.BlockSpec((None, t, vw), lambda b, h, qi, ki: (b, qi, h)),
        out_shape=jax.ShapeDtypeStruct((bsz, seq, MLA_HEADS * MLA_V), BF16),
        scratch_shapes=[pltpu.VMEM((hp, t, LANES), F32), pltpu.VMEM((hp, t, LANES), F32), pltpu.VMEM((t, vw), F32)],
        compiler_params=_cparams(("parallel", "parallel", "parallel", "arbitrary")),
        name="mla_attention",
    )(q3, k3, v3, k3, v3)


def _router_body(h_ref, wr_ref, rb_ref, e_ref, pos_ref, w_ref, cnt_ref, carry_ref):
    i = pl.program_id(0)
    ne, ng = N_EXPERTS, N_EXPERT_GROUPS
    gs = ne // ng
    tm = h_ref.shape[0]

    @pl.when(i == 0)
    def _():
        carry_ref[...] = jnp.zeros_like(carry_ref)

    logits = lax.dot_general(wr_ref[...], h_ref[...], (((1,), (1,)), ((), ())), precision=lax.Precision.HIGHEST,
                             preferred_element_type=F32)
    scores = jax.nn.sigmoid(logits)
    choice = scores + rb_ref[...]
    sub = lax.broadcasted_iota(I32, (gs, tm), 0)
    grp_rows = []
    for g in range(ng):
        blk = choice[g * gs:(g + 1) * gs, :]
        m1 = jnp.max(blk, axis=0, keepdims=True)
        first = jnp.min(jnp.where(blk == m1, sub, gs), axis=0, keepdims=True)
        m2 = jnp.max(jnp.where(sub == first, -jnp.inf, blk), axis=0, keepdims=True)
        grp_rows.append(m1 + m2)
    grp = jnp.concatenate(grp_rows, axis=0)
    grank = jnp.zeros((ng, tm), I32)
    gidx = lax.broadcasted_iota(I32, (ng, tm), 0)
    for g in range(ng):
        rowv = grp[g:g + 1, :]
        beats = jnp.logical_or(rowv > grp, jnp.logical_and(rowv == grp, gidx > g))
        grank = grank + beats.astype(I32)
    gsel = (grank < TOPK_GROUPS).astype(F32)
    esel = jnp.concatenate([jnp.broadcast_to(gsel[g:g + 1, :], (gs, tm)) for g in range(ng)], axis=0)
    masked = jnp.where(esel > 0.0, choice, -jnp.inf)
    eidx = lax.broadcasted_iota(I32, (ne, tm), 0)
    rank = jnp.zeros((ne, tm), I32)
    for e in range(ne):
        rowv = masked[e:e + 1, :]
        beats = jnp.logical_or(rowv > masked, jnp.logical_and(rowv == masked, eidx > e))
        rank = rank + beats.astype(I32)
    top = jnp.logical_and(rank < TOP_K, esel > 0.0)
    topf = top.astype(F32)
    wsel = jnp.where(top, scores, 0.0)
    wn = wsel / jnp.sum(wsel, axis=0, keepdims=True) * ROUTED_SCALE
    r_i = lax.broadcasted_iota(I32, (tm, tm), 0)
    c_i = lax.broadcasted_iota(I32, (tm, tm), 1)
    before = (r_i < c_i).astype(BF16)
    pos = jnp.dot(topf.astype(BF16), before, preferred_element_type=F32) + carry_ref[:, :1]
    carry_ref[...] = carry_ref[...] + jnp.sum(topf, axis=1, keepdims=True)
    cnt_ref[...] = carry_ref[...]
    e_rows, p_rows, w_rows = [], [], []
    for k in range(TOP_K):
        hit = rank == k
        e_rows.append(jnp.sum(jnp.where(hit, eidx, 0), axis=0, keepdims=True))
        p_rows.append(jnp.sum(jnp.where(hit, pos, 0.0), axis=0, keepdims=True))
        w_rows.append(jnp.sum(jnp.where(hit, wn, 0.0), axis=0, keepdims=True))
    e_ref[...] = jnp.concatenate(e_rows, axis=0)
    pos_ref[...] = jnp.concatenate(p_rows, axis=0).astype(I32)
    w_ref[...] = jnp.concatenate(w_rows, axis=0)


def _router(h2, w_router, router_bias):
    t, d = h2.shape
    tm = _pick(t, (256, 128))
    slot = pl.BlockSpec((TOP_K, tm), lambda i: (0, i))
    rb = jnp.broadcast_to(router_bias.astype(F32)[:, None], (N_EXPERTS, tm))
    return pl.pallas_call(
        _router_body,
        grid=(t // tm,),
        in_specs=[pl.BlockSpec((tm, d), lambda i: (i, 0)), pl.BlockSpec((N_EXPERTS, d), lambda i: (0, 0)),
                  pl.BlockSpec((N_EXPERTS, tm), lambda i: (0, 0))],
        out_specs=[slot, slot, slot, pl.BlockSpec((N_EXPERTS, LANES), lambda i: (0, 0))],
        out_shape=[jax.ShapeDtypeStruct((TOP_K, t), I32), jax.ShapeDtypeStruct((TOP_K, t), I32),
                   jax.ShapeDtypeStruct((TOP_K, t), F32), jax.ShapeDtypeStruct((N_EXPERTS, LANES), F32)],
        scratch_shapes=[pltpu.VMEM((N_EXPERTS, LANES), F32)],
        compiler_params=_cparams(("arbitrary",)),
        name="moe_router",
    )(h2, w_router.T.astype(F32), rb)


def _load_slots(dest_hbm, dest_ref, sem):
    cp = pltpu.make_async_copy(dest_hbm.at[pl.program_id(0)], dest_ref, sem)
    cp.start()
    cp.wait()


def _dispatch_body(cnt_ref, start_ref, dest_hbm, x_ref, xs_ref, dest_ref, zero_ref, sem_ref, *, block_rows):
    i = pl.program_id(0)
    tm = x_ref.shape[0]
    _load_slots(dest_hbm, dest_ref, sem_ref.at[2])

    def row_copy(tok, k):
        return pltpu.make_async_copy(x_ref.at[pl.ds(tok, 1), :], xs_ref.at[pl.ds(dest_ref[k, tok], 1), :], sem_ref.at[0])

    def issue(tok, carry):
        for k in range(TOP_K):
            row_copy(tok, k).start()
        return carry

    lax.fori_loop(0, tm, issue, 0)

    @pl.when(i == 0)
    def _():
        zero_ref[...] = jnp.zeros_like(zero_ref)

        def fill(e, carry):
            cnt = cnt_ref[e]
            padded = (cnt + block_rows - 1) // block_rows * block_rows
            base = start_ref[e] + cnt

            def zcopy(r):
                return pltpu.make_async_copy(zero_ref.at[pl.ds(0, 1), :], xs_ref.at[pl.ds(base + r, 1), :], sem_ref.at[1])

            def zstart(r, c2):
                zcopy(r).start()
                return c2

            def zwait(r, c2):
                zcopy(r).wait()
                return c2

            lax.fori_loop(0, padded - cnt, zstart, 0)
            lax.fori_loop(0, padded - cnt, zwait, 0)
            return carry

        lax.fori_loop(0, N_EXPERTS, fill, 0)

    def drain(tok, carry):
        for k in range(TOP_K):
            row_copy(tok, k).wait()
        return carry

    lax.fori_loop(0, tm, drain, 0)


def _slot_tiles(dest, tm):
    k, t = dest.shape
    return dest.reshape(k, t // tm, tm).transpose(1, 0, 2)


def _dispatch(hp2, dest, counts, starts, n_rows):
    t, w = hp2.shape
    tm = _pick(t, (256, 128))
    grid_spec = pltpu.PrefetchScalarGridSpec(
        num_scalar_prefetch=2,
        grid=(t // tm,),
        in_specs=[pl.BlockSpec(memory_space=pl.ANY),
                  pl.BlockSpec((tm, w), lambda i, c, s: (i, 0))],
        out_specs=pl.BlockSpec(memory_space=pl.ANY),
        scratch_shapes=[pltpu.SMEM((TOP_K, tm), I32), pltpu.VMEM((8, w), U32), pltpu.SemaphoreType.DMA((3,))],
    )
    return pl.pallas_call(
        functools.partial(_dispatch_body, block_rows=MOE_ROWS),
        grid_spec=grid_spec,
        out_shape=jax.ShapeDtypeStruct((n_rows, w), U32),
        compiler_params=pltpu.CompilerParams(dimension_semantics=("arbitrary",)),
        name="moe_dispatch",
    )(counts, starts, _slot_tiles(dest, tm), hp2)


def _expert_body(be_ref, nu_ref, x_ref, wg_ref, wu_ref, wd_ref, o_ref):
    j = pl.program_id(0)

    @pl.when(j < nu_ref[0])
    def _():
        half = x_ref.shape[1]
        lo, hi = _unpack_halves(x_ref[...])
        lo, hi = lo.astype(BF16), hi.astype(BF16)

        def proj(w_ref):
            return (jnp.dot(lo, w_ref[:half, :], preferred_element_type=F32)
                    + jnp.dot(hi, w_ref[half:, :], preferred_element_type=F32))

        hmid = (_silu(proj(wg_ref)) * proj(wu_ref)).astype(BF16)
        o_ref[...] = _pack_halves(jnp.dot(hmid, wd_ref[...], preferred_element_type=F32))


def _experts(xs, wg, wu, wd, block_e, n_used):
    rows, w = xs.shape
    _, d, ff = wg.shape
    bm = MOE_ROWS
    nb = rows // bm

    def row_idx(j, be, nu):
        return (jnp.minimum(j, nu[0] - 1), 0)

    def w_idx(j, be, nu):
        return (be[jnp.minimum(j, nu[0] - 1)], 0, 0)

    grid_spec = pltpu.PrefetchScalarGridSpec(
        num_scalar_prefetch=2,
        grid=(nb,),
        in_specs=[pl.BlockSpec((bm, w), row_idx),
                  pl.BlockSpec((None, d, ff), w_idx), pl.BlockSpec((None, d, ff), w_idx),
                  pl.BlockSpec((None, ff, d), w_idx)],
        out_specs=pl.BlockSpec((bm, w), row_idx),
    )
    return pl.pallas_call(
        _expert_body,
        grid_spec=grid_spec,
        out_shape=jax.ShapeDtypeStruct((rows, w), U32),
        compiler_params=_cparams(("arbitrary",), 60 * 1024 * 1024),
        name="moe_experts",
    )(block_e, n_used, xs, wg, wu, wd)


def _combine_body(dest_hbm, wt_ref, h_ref, ysh_ref, g_ref, b_ref, y_ref, o_ref, dest_ref, gbuf_ref, sem_ref, *, alpha):
    tm = h_ref.shape[0]
    _load_slots(dest_hbm, dest_ref, sem_ref.at[1])

    def row_copy(tok, k):
        return pltpu.make_async_copy(y_ref.at[pl.ds(dest_ref[k, tok], 1), :], gbuf_ref.at[k, pl.ds(tok, 1), :],
                                     sem_ref.at[0])

    def issue(tok, carry):
        for k in range(TOP_K):
            row_copy(tok, k).start()
        return carry

    def drain(tok, carry):
        for k in range(TOP_K):
            row_copy(tok, k).wait()
        return carry

    lax.fori_loop(0, tm, issue, 0)
    lax.fori_loop(0, tm, drain, 0)
    lo, hi = _unpack_halves(ysh_ref[...])
    wt = wt_ref[...]
    for k in range(TOP_K):
        glo, ghi = _unpack_halves(gbuf_ref[k])
        wk = wt[:, k:k + 1]
        lo = lo + glo * wk
        hi = hi + ghi * wk
    x = alpha * h_ref[...] + jnp.concatenate([lo, hi], axis=1)
    mu = jnp.mean(x, axis=-1, keepdims=True)
    xc = x - mu
    var = jnp.mean(xc * xc, axis=-1, keepdims=True)
    o_ref[...] = xc * lax.rsqrt(var + LN_EPS) * g_ref[...] + b_ref[...]


def _combine(h2, ysh, y_sorted, dest, wt, g, b, alpha):
    t, d = h2.shape
    w = d // 2
    tm = _pick(t, (128,))
    vec = pl.BlockSpec((1, d), lambda i: (0, 0))
    return pl.pallas_call(
        functools.partial(_combine_body, alpha=alpha),
        grid=(t // tm,),
        in_specs=[pl.BlockSpec(memory_space=pl.ANY),
                  pl.BlockSpec((tm, TOP_K), lambda i: (i, 0)),
                  pl.BlockSpec((tm, d), lambda i: (i, 0)),
                  pl.BlockSpec((tm, w), lambda i: (i, 0)),
                  vec, vec,
                  pl.BlockSpec(memory_space=pl.ANY)],
        out_specs=pl.BlockSpec((tm, d), lambda i: (i, 0)),
        out_shape=jax.ShapeDtypeStruct((t, d), F32),
        scratch_shapes=[pltpu.SMEM((TOP_K, tm), I32), pltpu.VMEM((TOP_K, tm, w), U32), pltpu.SemaphoreType.DMA((2,))],
        compiler_params=_cparams(("arbitrary",)),
        name="moe_combine",
    )(_slot_tiles(dest, tm), wt, h2, ysh, g.reshape(1, d), b.reshape(1, d), y_sorted)


def _rope_tables(pos):
    inv_freq = ROPE_THETA ** (-jnp.arange(0, MLA_ROPE, 2, dtype=F32) / MLA_ROPE)
    ang = pos.astype(F32)[..., None] * inv_freq
    return jnp.cos(ang), jnp.sin(ang)


def kernel(x, positions, meta_tokens, ln_in_g, ln_in_b, w_in, b_gate, conv_w, conv_b, dt_bias, a_log, d_skip,
           ssm_norm_g, w_ssm_proj, q_a_norm_g, w_q_b, kv_a_norm_g, w_kv_b, w_attn_proj, w_out, ln1_g, ln1_b,
           w_router, router_bias, w_exp_gate, w_exp_up, w_exp_down, w_sh_gate, w_sh_up, w_sh_down, ln2_g, ln2_b):
    bsz, seq, d = x.shape
    depth = w_in.shape[0]
    heads = dt_bias.shape[-1]
    d_inner = w_ssm_proj.shape[1]
    conv_dim = conv_w.shape[-1]
    q_rank = w_q_b.shape[1]
    kv_rank = w_kv_b.shape[1]
    hh = MLA_HEADS
    g = SSM_GROUPS
    hg = heads // g
    assert seq % CHUNK == 0 and d % (2 * LANES) == 0
    lp = seq + CHUNK
    t = bsz * seq
    alpha = (2.0 * depth) ** 0.25

    meta = jnp.broadcast_to(meta_tokens[None].astype(x.dtype), (bsz, N_META, d))
    hcat = jnp.concatenate([x, jnp.zeros((bsz, META_PAD, d), x.dtype), meta], axis=1)
    pos = jnp.concatenate([positions.astype(I32) + N_META, jnp.zeros((bsz, META_PAD), I32),
                           jnp.broadcast_to(jnp.arange(N_META, dtype=I32), (bsz, N_META))], axis=1)
    cos, sin = _rope_tables(pos)
    zr = jnp.zeros_like(cos)
    ones = jnp.ones((bsz, lp, MLA_NOPE), F32)
    zn = jnp.zeros((bsz, lp, MLA_NOPE), F32)
    rope_c = jnp.concatenate([ones, cos, cos] * 2, axis=-1)[:, :seq]
    rope_slo = jnp.concatenate([zn, -sin, zr] * 2, axis=-1)[:, :seq]
    rope_shi = jnp.concatenate([zn, zr, sin] * 2, axis=-1)[:, :seq]
    k_clo = jnp.concatenate([cos, cos, zr, zr], axis=-1)
    k_slo = jnp.concatenate([-sin, sin, zr, zr], axis=-1)
    k_chi = jnp.concatenate([zr, zr, cos, cos], axis=-1)
    k_shi = jnp.concatenate([zr, zr, -sin, sin], axis=-1)

    h_f32, h_b16 = _layer_norm(hcat, ln_in_g, ln_in_b)
    tm_all = _pick(lp, (1408, 1152, 1024, 896, 768, 640, 512, 384, 256, 128))
    tm_seq = _pick(seq, (1024, 512, 256, 128))

    out = None
    for l in range(depth):
        offs = [0]
        for wdt in (d_inner, conv_dim, heads, q_rank, kv_rank, MLA_ROPE, 2 * d):
            offs.append(offs[-1] + wdt)
        wl = w_in[l]
        w_z = wl[:, offs[0]:offs[1]].astype(BF16)
        w_xbc = wl[:, offs[1]:offs[2]].astype(BF16)
        w_kr = wl[:, offs[5]:offs[6]]
        small_cols = q_rank + kv_rank + 2 * MLA_ROPE + heads
        small_pad = (-small_cols) % 256
        w_small = jnp.concatenate([wl[:, offs[3]:offs[4]], wl[:, offs[4]:offs[5]], w_kr, w_kr, wl[:, offs[2]:offs[3]],
                                   jnp.zeros((d, small_pad), F32)], axis=1).astype(BF16)
        w_g = wl[:, offs[6]:offs[7]].astype(BF16)

        z3 = _mm(h_b16, w_z, out_dtype=BF16, tm=tm_all, tn=_pick(d_inner, (512, 256, 128)), name="in_proj_z")
        xbc3 = _mm(h_b16, w_xbc, out_dtype=BF16, tm=tm_all, tn=_pick(conv_dim, (512, 256, 128)), name="in_proj_xbc")
        small = _mm(h_b16, w_small, out_dtype=F32, tm=tm_all, tn=256, name="in_proj_small")
        gates = _mm(h_b16, w_g, out_dtype=BF16, tm=tm_seq, tn=512, n_row_tiles=seq // tm_seq,
                    bias=b_gate[l], act="sigmoid", name="in_proj_gates")
        o = 0
        q_a = small[:, :, o:o + q_rank]; o += q_rank
        kv_a = small[:, :, o:o + kv_rank]; o += kv_rank
        kr2 = small[:, :, o:o + 2 * MLA_ROPE]; o += 2 * MLA_ROPE
        dt_raw = small[:, :, o:o + heads]

        dt_g = jnp.pad(dt_raw.reshape(bsz, lp, g, hg).transpose(0, 2, 1, 3), ((0, 0), (0, 0), (0, 0), (0, LANES - hg)))
        pad_h = lambda v: jnp.pad(v.astype(F32).reshape(g, 1, hg), ((0, 0), (0, 0), (0, LANES - hg)))
        y_ssm = _ssd(xbc3, z3, dt_g, conv_w[l].astype(F32), conv_b[l].reshape(1, conv_dim).astype(F32),
                     pad_h(dt_bias[l]), pad_h(a_log[l]),
                     jnp.repeat(d_skip[l].astype(F32), SSM_HEAD_DIM).reshape(1, d_inner),
                     ssm_norm_g[l].reshape(1, d_inner).astype(F32), d_inner, heads)

        qscale = (MLA_QK ** -0.5) * math.log2(math.e)
        q3 = _mm(q_a, w_q_b[l].astype(BF16), out_dtype=BF16, tm=tm_seq, tn=2 * 2 * MLA_QK, n_row_tiles=seq // tm_seq,
                 gain=q_a_norm_g[l], rope=(rope_c, rope_slo, rope_shi), scale=qscale, name="q_proj")
        wkv = w_kv_b[l].reshape(kv_rank, hh, MLA_NOPE + MLA_V)
        w_k = jnp.pad(wkv[:, :, :MLA_NOPE], ((0, 0), (0, 0), (0, MLA_ROPE))).reshape(kv_rank, hh * MLA_QK).astype(BF16)
        w_v = wkv[:, :, MLA_NOPE:].reshape(kv_rank, hh * MLA_V).astype(BF16)
        k3 = _mm(kv_a, w_k, out_dtype=BF16, tm=tm_all, tn=2 * 2 * MLA_QK, gain=kv_a_norm_g[l],
                 kadd=(kr2, k_clo, k_slo, k_chi, k_shi), name="k_proj")
        v3 = _mm(kv_a, w_v, out_dtype=BF16, tm=tm_all, tn=512, gain=kv_a_norm_g[l], name="v_proj")
        y_attn = _attention(q3, k3, v3, seq)

        g_ssm, g_attn = gates[:, :, :d], gates[:, :, d:]
        mixed = _mm_ktiled([y_ssm, y_attn], [w_ssm_proj[l].astype(BF16), w_attn_proj[l].astype(BF16)],
                           out_dtype=BF16, rows=seq, tm=tm_seq, tn=_pick(d, (1024, 512)), nk=4,
                           gates=[g_ssm, g_attn], name="branch_proj")
        pre1 = _mm_ktiled([mixed], [w_out[l].astype(BF16)], out_dtype=F32, rows=seq, tm=tm_seq,
                          tn=_pick(d, (1024, 512)), nk=2, res=h_f32, res_scale=alpha, name="out_proj")
        h1, h1p = _layer_norm_pack(pre1.reshape(t, d), ln1_g[l], ln1_b[l])

        e_slot, pos_slot, w_slot, cnt = _router(h1, w_router[l], router_bias[l])
        counts = cnt[:, 0].astype(I32)
        padded = (counts + MOE_ROWS - 1) // MOE_ROWS * MOE_ROWS
        ends = jnp.cumsum(padded)
        starts = ends - padded
        onehot = e_slot[None] == jnp.arange(N_EXPERTS, dtype=I32)[:, None, None]
        dest = pos_slot + jnp.sum(jnp.where(onehot, starts[:, None, None], 0), axis=0)
        n_blocks = -(-(t * TOP_K) // MOE_ROWS) + N_EXPERTS
        blk0 = jnp.arange(n_blocks, dtype=I32) * MOE_ROWS
        block_e = jnp.minimum(jnp.sum((ends[None, :] <= blk0[:, None]).astype(I32), axis=1), N_EXPERTS - 1)
        n_used = (ends[-1] // MOE_ROWS).reshape(1)
        xs = _dispatch(h1p, dest, counts, starts, n_blocks * MOE_ROWS)
        y_sorted = _experts(xs, w_exp_gate[l].astype(BF16), w_exp_up[l].astype(BF16), w_exp_down[l].astype(BF16),
                            block_e, n_used)
        y_shared = _experts(h1p, w_sh_gate[l][None].astype(BF16), w_sh_up[l][None].astype(BF16),
                            w_sh_down[l][None].astype(BF16), jnp.zeros((t // MOE_ROWS,), I32),
                            jnp.full((1,), t // MOE_ROWS, I32))
        out = _combine(h1, y_shared, y_sorted, dest, w_slot.T, ln2_g[l], ln2_b[l], alpha)
        if l + 1 < depth:
            raise NotImplementedError("stacked layers need the meta rows carried through the channel mixer")
    return out.reshape(bsz, seq, d)
```

```python
import functools
import math

import jax
import jax.numpy as jnp
from jax import lax
from jax.experimental import pallas as pl
from jax.experimental.pallas import tpu as pltpu

F32 = jnp.float32
BF16 = jnp.bfloat16
U32 = jnp.uint32
I32 = jnp.int32

N_META = 16
CHUNK = 128
META_PAD = CHUNK - N_META
SSM_HEAD_DIM = 64
SSM_GROUPS = 8
SSM_STATE = 128
SSM_CONV = 4
MLA_HEADS = 64
MLA_NOPE = 128
MLA_ROPE = 64
MLA_V = 128
MLA_QK = MLA_NOPE + MLA_ROPE
ROPE_THETA = 10000.0
N_EXPERTS = 64
N_EXPERT_GROUPS = 8
TOPK_GROUPS = 4
TOP_K = 8
ROUTED_SCALE = 2.5
LN_EPS = 1e-5
RMS_EPS = 1e-6
NEG = -1e30
LANES = 128
MLA_QKP = 2 * LANES
MOE_ROWS = 256
VMEM_BIG = 56 * 1024 * 1024
VMEM_MID = 44 * 1024 * 1024


def _cparams(sem, vmem=VMEM_MID):
    return pltpu.CompilerParams(dimension_semantics=sem, vmem_limit_bytes=vmem)


def _pick(n, cands):
    for c in cands:
        if n % c == 0:
            return c
    raise ValueError(f"no tile for {n} in {cands}")


def _ln_body(x_ref, g_ref, b_ref, of_ref, ob_ref):
    x = x_ref[...]
    mu = jnp.mean(x, axis=-1, keepdims=True)
    xc = x - mu
    var = jnp.mean(xc * xc, axis=-1, keepdims=True)
    y = xc * lax.rsqrt(var + LN_EPS) * g_ref[...] + b_ref[...]
    of_ref[...] = y
    ob_ref[...] = y.astype(BF16)


def _layer_norm(x3, g, b):
    bsz, rows, d = x3.shape
    tm = _pick(rows, (256, 128))
    spec = pl.BlockSpec((None, tm, d), lambda bi, i: (bi, i, 0))
    vec = pl.BlockSpec((1, d), lambda bi, i: (0, 0))
    return pl.pallas_call(
        _ln_body,
        grid=(bsz, rows // tm),
        in_specs=[spec, vec, vec],
        out_specs=[spec, spec],
        out_shape=[jax.ShapeDtypeStruct(x3.shape, F32), jax.ShapeDtypeStruct(x3.shape, BF16)],
        compiler_params=_cparams(("parallel", "parallel")),
        name="layer_norm",
    )(x3, g.reshape(1, d), b.reshape(1, d))


def _pack_halves(y):
    n = y.shape[1] // 2
    lo = pltpu.bitcast(y[:, :n].astype(BF16).astype(F32), U32) >> 16
    hi = pltpu.bitcast(y[:, n:].astype(BF16).astype(F32), U32) & jnp.uint32(0xFFFF0000)
    return hi | lo


def _unpack_halves(w):
    lo = pltpu.bitcast(w << 16, F32)
    hi = pltpu.bitcast(w & jnp.uint32(0xFFFF0000), F32)
    return lo, hi


def _ln_pack_body(x_ref, g_ref, b_ref, of_ref, op_ref):
    x = x_ref[...]
    mu = jnp.mean(x, axis=-1, keepdims=True)
    xc = x - mu
    var = jnp.mean(xc * xc, axis=-1, keepdims=True)
    y = xc * lax.rsqrt(var + LN_EPS) * g_ref[...] + b_ref[...]
    of_ref[...] = y
    op_ref[...] = _pack_halves(y)


def _layer_norm_pack(x2, g, b):
    rows, d = x2.shape
    tm = _pick(rows, (256, 128))
    vec = pl.BlockSpec((1, d), lambda i: (0, 0))
    return pl.pallas_call(
        _ln_pack_body,
        grid=(rows // tm,),
        in_specs=[pl.BlockSpec((tm, d), lambda i: (i, 0)), vec, vec],
        out_specs=[pl.BlockSpec((tm, d), lambda i: (i, 0)), pl.BlockSpec((tm, d // 2), lambda i: (i, 0))],
        out_shape=[jax.ShapeDtypeStruct((rows, d), F32), jax.ShapeDtypeStruct((rows, d // 2), U32)],
        compiler_params=_cparams(("parallel",)),
        name="layer_norm_pack",
    )(x2, g.reshape(1, d), b.reshape(1, d))


def _mm_body(*refs, has_gain, has_bias, act, rope, kadd, scale):
    it = iter(refs)
    a_ref, w_ref = next(it), next(it)
    gain_ref = next(it) if has_gain else None
    bias_ref = next(it) if has_bias else None
    rope_refs = [next(it) for _ in range(3)] if rope else None
    kadd_refs = [next(it) for _ in range(3)] if kadd else None
    o_ref = next(it)
    a = a_ref[...]
    if has_gain:
        af = a.astype(F32)
        a = af * lax.rsqrt(jnp.mean(af * af, axis=-1, keepdims=True) + RMS_EPS) * gain_ref[...]
    r = jnp.dot(a.astype(BF16), w_ref[...], preferred_element_type=F32)
    tn = r.shape[1]
    if has_bias:
        r = r + bias_ref[...]
    if act == "sigmoid":
        r = jax.nn.sigmoid(r)
    if rope:
        rep = tn // rope_refs[0].shape[1]
        c, slo, shi = (jnp.concatenate([t[...]] * rep, axis=1) for t in rope_refs)
        r = r * c + pltpu.roll(r, tn - MLA_ROPE // 2, 1) * slo + pltpu.roll(r, MLA_ROPE // 2, 1) * shi
    if kadd:
        kr_ref, kc, ks = kadd_refs
        kr = kr_ref[...]
        piece = jnp.concatenate([jnp.zeros_like(kr), kr * kc[...] + pltpu.roll(kr, MLA_ROPE // 2, 1) * ks[...]], axis=1)
        r = r + jnp.concatenate([piece] * (tn // piece.shape[1]), axis=1)
    if scale is not None:
        r = r * scale
    o_ref[...] = r.astype(o_ref.dtype)


def _mm(a3, w, *, out_dtype, tm, tn, n_row_tiles=None, gain=None, bias=None, act=None, rope=None, kadd=None,
        scale=None, name="mm"):
    bsz, rows, k = a3.shape
    n = w.shape[1]
    ni = rows // tm if n_row_tiles is None else n_row_tiles
    assert n % tn == 0 and (n_row_tiles is not None or rows % tm == 0)
    in_specs = [pl.BlockSpec((None, tm, k), lambda bi, i, j: (bi, i, 0)),
                pl.BlockSpec((k, tn), lambda bi, i, j: (0, j))]
    args = [a3, w]
    if gain is not None:
        in_specs.append(pl.BlockSpec((1, k), lambda bi, i, j: (0, 0)))
        args.append(gain.reshape(1, k).astype(F32))
    if bias is not None:
        in_specs.append(pl.BlockSpec((1, tn), lambda bi, i, j: (0, j)))
        args.append(bias.reshape(1, n).astype(F32))
    for t in (rope or ()) + (kadd or ()):
        in_specs.append(pl.BlockSpec((None, tm, t.shape[2]), lambda bi, i, j: (bi, i, 0)))
        args.append(t)
    body = functools.partial(_mm_body, has_gain=gain is not None, has_bias=bias is not None, act=act,
                             rope=rope is not None, kadd=kadd is not None, scale=scale)
    return pl.pallas_call(
        body,
        grid=(bsz, ni, n // tn),
        in_specs=in_specs,
        out_specs=pl.BlockSpec((None, tm, tn), lambda bi, i, j: (bi, i, j)),
        out_shape=jax.ShapeDtypeStruct((bsz, ni * tm, n), out_dtype),
        compiler_params=_cparams(("parallel", "parallel", "arbitrary"), VMEM_BIG),
        name=name,
    )(*args)


def _mm_t_body(a_ref, wt_ref, gain_ref, o_ref):
    af = a_ref[...].astype(F32)
    a = af * lax.rsqrt(jnp.mean(af * af, axis=-1, keepdims=True) + RMS_EPS) * gain_ref[...]
    o_ref[...] = lax.dot_general(wt_ref[...], a.astype(BF16), (((1,), (1,)), ((), ())),
                                 preferred_element_type=F32).astype(o_ref.dtype)


def _mm_t(a3, wt, gain, *, out_dtype, tm, tn, name):
    bsz, rows, k = a3.shape
    n = wt.shape[0]
    return pl.pallas_call(
        _mm_t_body,
        grid=(bsz, rows // tm, n // tn),
        in_specs=[pl.BlockSpec((None, tm, k), lambda bi, i, j: (bi, i, 0)),
                  pl.BlockSpec((tn, k), lambda bi, i, j: (j, 0)),
                  pl.BlockSpec((1, k), lambda bi, i, j: (0, 0))],
        out_specs=pl.BlockSpec((None, tn, tm), lambda bi, i, j: (bi, j, i)),
        out_shape=jax.ShapeDtypeStruct((bsz, n, rows), out_dtype),
        compiler_params=_cparams(("parallel", "parallel", "arbitrary"), VMEM_BIG),
        name=name,
    )(a3, wt, gain.reshape(1, k).astype(F32))


def _mmk_body(*refs, n_pairs, has_gate, has_res, res_scale):
    it = iter(refs)
    a_refs = [next(it) for _ in range(n_pairs)]
    w_refs = [next(it) for _ in range(n_pairs)]
    g_refs = [next(it) for _ in range(n_pairs)] if has_gate else None
    res_ref = next(it) if has_res else None
    o_ref = next(it)
    acc_refs = [next(it) for _ in range(n_pairs)]
    kk = pl.program_id(3)

    @pl.when(kk == 0)
    def _():
        for acc in acc_refs:
            acc[...] = jnp.zeros_like(acc)

    for a_ref, w_ref, acc in zip(a_refs, w_refs, acc_refs):
        acc[...] += jnp.dot(a_ref[...], w_ref[...], preferred_element_type=F32)

    @pl.when(kk == pl.num_programs(3) - 1)
    def _():
        r = None
        for p, acc in enumerate(acc_refs):
            t = acc[...]
            if has_gate:
                t = t * g_refs[p][...].astype(F32)
            r = t if r is None else r + t
        if has_res:
            r = r + res_scale * res_ref[...]
        o_ref[...] = r.astype(o_ref.dtype)


def _mm_ktiled(a_list, w_list, *, out_dtype, rows, tm, tn, nk, gates=None, res=None, res_scale=1.0, name="mmk"):
    bsz = a_list[0].shape[0]
    n = w_list[0].shape[1]
    npairs = len(a_list)
    tks = [a.shape[2] // nk for a in a_list]
    assert all(a.shape[2] == tk * nk and tk % LANES == 0 for a, tk in zip(a_list, tks))
    o_spec = pl.BlockSpec((None, tm, tn), lambda bi, i, j, kk: (bi, i, j))
    in_specs = ([pl.BlockSpec((None, tm, tk), lambda bi, i, j, kk: (bi, i, kk)) for tk in tks]
                + [pl.BlockSpec((tk, tn), lambda bi, i, j, kk: (kk, j)) for tk in tks])
    args = list(a_list) + list(w_list)
    if gates is not None:
        for p in range(npairs):
            in_specs.append(pl.BlockSpec((None, tm, tn), lambda bi, i, j, kk, p=p: (bi, i, p * (n // tn) + j)))
            args.append(gates)
    if res is not None:
        in_specs.append(o_spec)
        args.append(res)
    body = functools.partial(_mmk_body, n_pairs=npairs, has_gate=gates is not None, has_res=res is not None,
                             res_scale=res_scale)
    return pl.pallas_call(
        body,
        grid=(bsz, rows // tm, n // tn, nk),
        in_specs=in_specs,
        out_specs=o_spec,
        out_shape=jax.ShapeDtypeStruct((bsz, rows, n), out_dtype),
        scratch_shapes=[pltpu.VMEM((tm, tn), F32) for _ in range(npairs)],
        compiler_params=_cparams(("parallel", "parallel", "parallel", "arbitrary"), VMEM_BIG),
        name=name,
    )(*args)


def _silu(x):
    return x * jax.nn.sigmoid(x)


def _ssd_body(xs_ref, b_ref, c_ref, z_ref, dt_ref, wx_ref, wb_ref, wc_ref, bx_ref, bb_ref, bc_ref,
              dtb_ref, alog_ref, dskip_ref, ng_ref, e_ref, o_ref,
              state_ref, extx_ref, extb_ref, extc_ref, y_ref, *, hg):
    c = pl.program_id(2)
    q = CHUNK
    p = SSM_HEAD_DIM
    first = c == 0
    row = lax.broadcasted_iota(I32, (q, 1), 0)
    live = jnp.logical_or(jnp.logical_not(first), row >= META_PAD)

    @pl.when(first)
    def _():
        state_ref[...] = jnp.zeros_like(state_ref)
        extx_ref[0:8, :] = jnp.zeros((8, extx_ref.shape[1]), F32)
        extb_ref[0:8, :] = jnp.zeros((8, extb_ref.shape[1]), F32)
        extc_ref[0:8, :] = jnp.zeros((8, extc_ref.shape[1]), F32)

    def conv(ext_ref, u_ref, w_ref, bias_ref):
        ext_ref[8:8 + q, :] = jnp.where(live, u_ref[...].astype(F32), 0.0)
        acc = bias_ref[...]
        for k in range(SSM_CONV):
            acc = acc + ext_ref[pl.ds(8 - (SSM_CONV - 1) + k, q), :] * w_ref[k:k + 1, :]
        ext_ref[0:8, :] = ext_ref[q:q + 8, :]
        return _silu(acc)

    xs = conv(extx_ref, xs_ref, wx_ref, bx_ref)
    bm = conv(extb_ref, b_ref, wb_ref, bb_ref)
    cm = conv(extc_ref, c_ref, wc_ref, bc_ref)

    x = dt_ref[...] + dtb_ref[...]
    dt = jnp.maximum(x, 0.0) + jnp.log1p(jnp.exp(-jnp.abs(x)))
    dt = jnp.where(live, dt, 0.0)
    a = -jnp.exp(alog_ref[...])
    r_i = lax.broadcasted_iota(I32, (q, q), 0)
    c_i = lax.broadcasted_iota(I32, (q, q), 1)
    causal = r_i >= c_i
    hi = lax.Precision.HIGHEST
    a_cs = jnp.dot(causal.astype(F32), dt * a, precision=hi, preferred_element_type=F32)
    a_cs_t = a_cs.T
    ea = jnp.exp(a_cs)
    decay_end = jnp.exp(a_cs[q - 1:q, :] - a_cs)
    e = e_ref[...]
    dt_x = jnp.dot(dt, e, precision=hi, preferred_element_type=F32)
    ea_x = jnp.dot(ea, e, precision=hi, preferred_element_type=F32)
    de_x = jnp.dot(decay_end, e, precision=hi, preferred_element_type=F32)

    xdt = xs * dt_x
    xdt_b = xdt.astype(BF16)
    cm_b = cm.astype(BF16)
    bm_b = bm.astype(BF16)
    cb = lax.dot_general(cm_b, bm_b, (((1,), (1,)), ((), ())), preferred_element_type=F32)
    prev = state_ref[...]
    y_off = jnp.dot(cm_b, prev.astype(BF16), preferred_element_type=F32) * ea_x
    for j in range(hg):
        seg = a_cs[:, j:j + 1] - a_cs_t[j:j + 1, :]
        m = (jnp.exp(jnp.where(causal, seg, NEG)) * cb).astype(BF16)
        y_ref[:, j * p:(j + 1) * p] = jnp.dot(m, xdt_b[:, j * p:(j + 1) * p], preferred_element_type=F32)
    y = y_ref[...] + y_off + xs * dskip_ref[...]
    state_ref[...] = prev * ea_x[q - 1:q, :] + jnp.dot(bm.T.astype(BF16), (xdt * de_x).astype(BF16),
                                                        preferred_element_type=F32)
    yz = y * _silu(z_ref[...].astype(F32))
    yn = yz * lax.rsqrt(jnp.mean(yz * yz, axis=-1, keepdims=True) + RMS_EPS) * ng_ref[...]
    o_ref[...] = yn.astype(o_ref.dtype)


def _ssd(xbc3, z3, dt_g, conv_w, conv_b, dtb_g, alog_g, dskip_x, norm_g, d_inner, heads):
    bsz, rows, conv_dim = xbc3.shape
    g, n, p, q = SSM_GROUPS, SSM_STATE, SSM_HEAD_DIM, CHUNK
    hg = heads // g
    gw = hg * p
    nc = rows // q
    assert gw % LANES == 0 and d_inner % n == 0 and hg <= LANES
    boff = d_inner // n
    expand = (jnp.arange(LANES)[:, None] == (jnp.arange(gw) // p)[None, :]).astype(F32)

    def chunk(c):
        return (c + nc - 1) % nc

    in_specs = [
        pl.BlockSpec((None, q, gw), lambda b, gi, c: (b, chunk(c), gi)),
        pl.BlockSpec((None, q, n), lambda b, gi, c: (b, chunk(c), boff + gi)),
        pl.BlockSpec((None, q, n), lambda b, gi, c: (b, chunk(c), boff + g + gi)),
        pl.BlockSpec((None, q, gw), lambda b, gi, c: (b, chunk(c), gi)),
        pl.BlockSpec((None, None, q, LANES), lambda b, gi, c: (b, gi, chunk(c), 0)),
        pl.BlockSpec((SSM_CONV, gw), lambda b, gi, c: (0, gi)),
        pl.BlockSpec((SSM_CONV, n), lambda b, gi, c: (0, boff + gi)),
        pl.BlockSpec((SSM_CONV, n), lambda b, gi, c: (0, boff + g + gi)),
        pl.BlockSpec((1, gw), lambda b, gi, c: (0, gi)),
        pl.BlockSpec((1, n), lambda b, gi, c: (0, boff + gi)),
        pl.BlockSpec((1, n), lambda b, gi, c: (0, boff + g + gi)),
        pl.BlockSpec((None, 1, LANES), lambda b, gi, c: (gi, 0, 0)),
        pl.BlockSpec((None, 1, LANES), lambda b, gi, c: (gi, 0, 0)),
        pl.BlockSpec((1, gw), lambda b, gi, c: (0, gi)),
        pl.BlockSpec((1, gw), lambda b, gi, c: (0, gi)),
        pl.BlockSpec((LANES, gw), lambda b, gi, c: (0, 0)),
    ]
    return pl.pallas_call(
        functools.partial(_ssd_body, hg=hg),
        grid=(bsz, g, nc),
        in_specs=in_specs,
        out_specs=pl.BlockSpec((None, q, gw), lambda b, gi, c: (b, chunk(c), gi)),
        out_shape=jax.ShapeDtypeStruct((bsz, rows, d_inner), BF16),
        scratch_shapes=[pltpu.VMEM((n, gw), F32), pltpu.VMEM((q + 8, gw), F32), pltpu.VMEM((q + 8, n), F32),
                        pltpu.VMEM((q + 8, n), F32), pltpu.VMEM((q, gw), F32)],
        compiler_params=_cparams(("parallel", "parallel", "arbitrary")),
        name="ssd",
    )(xbc3, xbc3, xbc3, z3, dt_g, conv_w, conv_w, conv_w, conv_b, conv_b, conv_b, dtb_g, alog_g,
      dskip_x, norm_g, expand)


def _attn_body(qi_ref, ki_ref, q_ref, km_ref, vm_ref, k_ref, v_ref, o_ref, m_ref, l_ref, acc_ref, *, heads):
    step_id = pl.program_id(2)
    qi, ki = qi_ref[step_id], ki_ref[step_id]

    @pl.when(ki < 0)
    def _():
        m_ref[...] = jnp.full_like(m_ref, NEG)
        l_ref[...] = jnp.zeros_like(l_ref)
        acc_ref[...] = jnp.zeros_like(acc_ref)

    def step(kr, vr, mask_fn):
        for h in range(heads):
            qh = q_ref[:, h * MLA_QKP:(h + 1) * MLA_QKP]
            kh = kr[:, h * MLA_QKP:(h + 1) * MLA_QKP]
            st = lax.dot_general(kh, qh, (((1,), (1,)), ((), ())), preferred_element_type=F32)
            if mask_fn is not None:
                st = mask_fn(st)
            m_prev = m_ref[h]
            m_new = jnp.maximum(m_prev, jnp.max(st, axis=0, keepdims=True))
            alpha = jnp.exp2(m_prev - m_new)
            pt = jnp.exp2(st - m_new)
            l_ref[h] = alpha * l_ref[h] + jnp.sum(pt, axis=0, keepdims=True)
            sl = slice(h * MLA_V, (h + 1) * MLA_V)
            acc_ref[sl, :] = acc_ref[sl, :] * alpha + jnp.dot(vr[sl, :], pt.astype(BF16), preferred_element_type=F32)
            m_ref[h] = m_new

    def meta_mask(st):
        key = lax.broadcasted_iota(I32, st.shape, 0)
        return jnp.where(key >= META_PAD, st, NEG)

    def causal_mask(st):
        key = lax.broadcasted_iota(I32, st.shape, 0)
        qry = lax.broadcasted_iota(I32, st.shape, 1)
        return jnp.where(key <= qry, st, NEG)

    @pl.when(ki < 0)
    def _():
        step(km_ref, vm_ref, meta_mask)

    @pl.when(jnp.logical_and(ki >= 0, ki < qi))
    def _():
        step(k_ref, v_ref, None)

    @pl.when(ki == qi)
    def _():
        step(k_ref, v_ref, causal_mask)
        for h in range(heads):
            sl = slice(h * MLA_V, (h + 1) * MLA_V)
            o_ref[:, sl] = (acc_ref[sl, :] / l_ref[h]).T.astype(o_ref.dtype)


def _attention(q3, k3, vt3, seq):
    bsz = q3.shape[0]
    hp = 4
    t = _pick(seq, (512, 256, 128))
    nq = seq // t
    meta_blk = seq // CHUNK
    qw, vw = hp * MLA_QKP, hp * MLA_V
    qi_tab = jnp.asarray([qi for qi in range(nq) for _ in range(qi + 2)], I32)
    ki_tab = jnp.asarray([ki for qi in range(nq) for ki in range(-1, qi + 1)], I32)

    def q_idx(b, h, s, qt, kt):
        return (b, qt[s], h)

    def k_idx(b, h, s, qt, kt):
        return (b, jnp.maximum(kt[s], 0), h)

    def vt_idx(b, h, s, qt, kt):
        return (b, h, jnp.maximum(kt[s], 0))

    grid_spec = pltpu.PrefetchScalarGridSpec(
        num_scalar_prefetch=2,
        grid=(bsz, MLA_HEADS // hp, int(qi_tab.shape[0])),
        in_specs=[
            pl.BlockSpec((None, t, qw), q_idx),
            pl.BlockSpec((None, CHUNK, qw), lambda b, h, s, qt, kt: (b, meta_blk, h)),
            pl.BlockSpec((None, vw, CHUNK), lambda b, h, s, qt, kt: (b, h, meta_blk)),
            pl.BlockSpec((None, t, qw), k_idx),
            pl.BlockSpec((None, vw, t), vt_idx),
        ],
        out_specs=pl.BlockSpec((None, t, vw), q_idx),
        scratch_shapes=[pltpu.VMEM((hp, 1, t), F32), pltpu.VMEM((hp, 1, t), F32), pltpu.VMEM((vw, t), F32)],
    )
    return pl.pallas_call(
        functools.partial(_attn_body, heads=hp),
        grid_spec=grid_spec,
        out_shape=jax.ShapeDtypeStruct((bsz, seq, MLA_HEADS * MLA_V), BF16),
        compiler_params=_cparams(("parallel", "parallel", "arbitrary")),
        name="mla_attention",
    )(qi_tab, ki_tab, q3, k3, vt3, k3, vt3)


def _router_body(h_ref, wr_ref, rb_ref, e_ref, pos_ref, w_ref, cnt_ref, carry_ref):
    i = pl.program_id(0)
    ne, ng = N_EXPERTS, N_EXPERT_GROUPS
    gs = ne // ng
    tm = h_ref.shape[0]

    @pl.when(i == 0)
    def _():
        carry_ref[...] = jnp.zeros_like(carry_ref)

    logits = lax.dot_general(wr_ref[...], h_ref[...], (((1,), (1,)), ((), ())), precision=lax.Precision.HIGHEST,
                             preferred_element_type=F32)
    scores = jax.nn.sigmoid(logits)
    choice = scores + rb_ref[...]
    sub = lax.broadcasted_iota(I32, (gs, tm), 0)
    grp_rows = []
    for g in range(ng):
        blk = choice[g * gs:(g + 1) * gs, :]
        m1 = jnp.max(blk, axis=0, keepdims=True)
        first = jnp.min(jnp.where(blk == m1, sub, gs), axis=0, keepdims=True)
        m2 = jnp.max(jnp.where(sub == first, -jnp.inf, blk), axis=0, keepdims=True)
        grp_rows.append(m1 + m2)
    grp = jnp.concatenate(grp_rows, axis=0)
    grank = jnp.zeros((ng, tm), I32)
    gidx = lax.broadcasted_iota(I32, (ng, tm), 0)
    for g in range(ng):
        rowv = grp[g:g + 1, :]
        beats = jnp.logical_or(rowv > grp, jnp.logical_and(rowv == grp, gidx > g))
        grank = grank + beats.astype(I32)
    gsel = (grank < TOPK_GROUPS).astype(F32)
    esel = jnp.concatenate([jnp.broadcast_to(gsel[g:g + 1, :], (gs, tm)) for g in range(ng)], axis=0)
    masked = jnp.where(esel > 0.0, choice, -jnp.inf)
    eidx = lax.broadcasted_iota(I32, (ne, tm), 0)
    rank = jnp.zeros((ne, tm), I32)
    for e in range(ne):
        rowv = masked[e:e + 1, :]
        beats = jnp.logical_or(rowv > masked, jnp.logical_and(rowv == masked, eidx > e))
        rank = rank + beats.astype(I32)
    top = jnp.logical_and(rank < TOP_K, esel > 0.0)
    topf = top.astype(F32)
    wsel = jnp.where(top, scores, 0.0)
    wn = wsel / jnp.sum(wsel, axis=0, keepdims=True) * ROUTED_SCALE
    r_i = lax.broadcasted_iota(I32, (tm, tm), 0)
    c_i = lax.broadcasted_iota(I32, (tm, tm), 1)
    before = (r_i < c_i).astype(BF16)
    pos = jnp.dot(topf.astype(BF16), before, preferred_element_type=F32) + carry_ref[:, :1]
    carry_ref[...] = carry_ref[...] + jnp.sum(topf, axis=1, keepdims=True)
    cnt_ref[...] = carry_ref[...]
    e_rows, p_rows, w_rows = [], [], []
    for k in range(TOP_K):
        hit = rank == k
        e_rows.append(jnp.sum(jnp.where(hit, eidx, 0), axis=0, keepdims=True))
        p_rows.append(jnp.sum(jnp.where(hit, pos, 0.0), axis=0, keepdims=True))
        w_rows.append(jnp.sum(jnp.where(hit, wn, 0.0), axis=0, keepdims=True))
    e_ref[...] = jnp.concatenate(e_rows, axis=0)
    pos_ref[...] = jnp.concatenate(p_rows, axis=0).astype(I32)
    w_ref[...] = jnp.concatenate(w_rows, axis=0)


def _router(h2, w_router, router_bias):
    t, d = h2.shape
    tm = _pick(t, (256, 128))
    slot = pl.BlockSpec((TOP_K, tm), lambda i: (0, i))
    rb = jnp.broadcast_to(router_bias.astype(F32)[:, None], (N_EXPERTS, tm))
    return pl.pallas_call(
        _router_body,
        grid=(t // tm,),
        in_specs=[pl.BlockSpec((tm, d), lambda i: (i, 0)), pl.BlockSpec((N_EXPERTS, d), lambda i: (0, 0)),
                  pl.BlockSpec((N_EXPERTS, tm), lambda i: (0, 0))],
        out_specs=[slot, slot, slot, pl.BlockSpec((N_EXPERTS, LANES), lambda i: (0, 0))],
        out_shape=[jax.ShapeDtypeStruct((TOP_K, t), I32), jax.ShapeDtypeStruct((TOP_K, t), I32),
                   jax.ShapeDtypeStruct((TOP_K, t), F32), jax.ShapeDtypeStruct((N_EXPERTS, LANES), F32)],
        scratch_shapes=[pltpu.VMEM((N_EXPERTS, LANES), F32)],
        compiler_params=_cparams(("arbitrary",)),
        name="moe_router",
    )(h2, w_router.T.astype(F32), rb)


def _load_slots(dest_hbm, dest_ref, sem):
    cp = pltpu.make_async_copy(dest_hbm.at[pl.program_id(0)], dest_ref, sem)
    cp.start()
    cp.wait()


def _dispatch_body(cnt_ref, start_ref, dest_hbm, x_ref, xs_ref, dest_ref, zero_ref, sem_ref, *, block_rows):
    i = pl.program_id(0)
    tm = x_ref.shape[0]
    _load_slots(dest_hbm, dest_ref, sem_ref.at[2])

    def row_copy(tok, k):
        return pltpu.make_async_copy(x_ref.at[pl.ds(tok, 1), :], xs_ref.at[pl.ds(dest_ref[k, tok], 1), :], sem_ref.at[0])

    def issue(tok, carry):
        for k in range(TOP_K):
            row_copy(tok, k).start()
        return carry

    lax.fori_loop(0, tm, issue, 0)

    @pl.when(i == 0)
    def _():
        zero_ref[...] = jnp.zeros_like(zero_ref)

        def fill(e, carry):
            cnt = cnt_ref[e]
            padded = (cnt + block_rows - 1) // block_rows * block_rows
            base = start_ref[e] + cnt

            def zcopy(r):
                return pltpu.make_async_copy(zero_ref.at[pl.ds(0, 1), :], xs_ref.at[pl.ds(base + r, 1), :], sem_ref.at[1])

            def zstart(r, c2):
                zcopy(r).start()
                return c2

            def zwait(r, c2):
                zcopy(r).wait()
                return c2

            lax.fori_loop(0, padded - cnt, zstart, 0)
            lax.fori_loop(0, padded - cnt, zwait, 0)
            return carry

        lax.fori_loop(0, N_EXPERTS, fill, 0)

    def drain(tok, carry):
        for k in range(TOP_K):
            row_copy(tok, k).wait()
        return carry

    lax.fori_loop(0, tm, drain, 0)


def _slot_tiles(dest, tm):
    k, t = dest.shape
    return dest.reshape(k, t // tm, tm).transpose(1, 0, 2)


def _dispatch(hp2, dest, counts, starts, n_rows):
    t, w = hp2.shape
    tm = _pick(t, (256, 128))
    grid_spec = pltpu.PrefetchScalarGridSpec(
        num_scalar_prefetch=2,
        grid=(t // tm,),
        in_specs=[pl.BlockSpec(memory_space=pl.ANY),
                  pl.BlockSpec((tm, w), lambda i, c, s: (i, 0))],
        out_specs=pl.BlockSpec(memory_space=pl.ANY),
        scratch_shapes=[pltpu.SMEM((TOP_K, tm), I32), pltpu.VMEM((8, w), U32), pltpu.SemaphoreType.DMA((3,))],
    )
    return pl.pallas_call(
        functools.partial(_dispatch_body, block_rows=MOE_ROWS),
        grid_spec=grid_spec,
        out_shape=jax.ShapeDtypeStruct((n_rows, w), U32),
        compiler_params=pltpu.CompilerParams(dimension_semantics=("arbitrary",)),
        name="moe_dispatch",
    )(counts, starts, _slot_tiles(dest, tm), hp2)


def _expert_body(be_ref, nu_ref, x_ref, wg_ref, wu_ref, wd_ref, o_ref):
    j = pl.program_id(0)

    @pl.when(j < nu_ref[0])
    def _():
        half = x_ref.shape[1]
        lo, hi = _unpack_halves(x_ref[...])
        lo, hi = lo.astype(BF16), hi.astype(BF16)

        def proj(w_ref):
            return (jnp.dot(lo, w_ref[:half, :], preferred_element_type=F32)
                    + jnp.dot(hi, w_ref[half:, :], preferred_element_type=F32))

        hmid = (_silu(proj(wg_ref)) * proj(wu_ref)).astype(BF16)
        o_ref[...] = _pack_halves(jnp.dot(hmid, wd_ref[...], preferred_element_type=F32))


def _experts(xs, wg, wu, wd, block_e, n_used):
    rows, w = xs.shape
    _, d, ff = wg.shape
    bm = MOE_ROWS
    nb = rows // bm

    def row_idx(j, be, nu):
        return (jnp.minimum(j, nu[0] - 1), 0)

    def w_idx(j, be, nu):
        return (be[jnp.minimum(j, nu[0] - 1)], 0, 0)

    grid_spec = pltpu.PrefetchScalarGridSpec(
        num_scalar_prefetch=2,
        grid=(nb,),
        in_specs=[pl.BlockSpec((bm, w), row_idx),
                  pl.BlockSpec((None, d, ff), w_idx), pl.BlockSpec((None, d, ff), w_idx),
                  pl.BlockSpec((None, ff, d), w_idx)],
        out_specs=pl.BlockSpec((bm, w), row_idx),
    )
    return pl.pallas_call(
        _expert_body,
        grid_spec=grid_spec,
        out_shape=jax.ShapeDtypeStruct((rows, w), U32),
        compiler_params=_cparams(("arbitrary",), 60 * 1024 * 1024),
        name="moe_experts",
    )(block_e, n_used, xs, wg, wu, wd)


def _combine_body(dest_hbm, wt_ref, h_ref, ysh_ref, g_ref, b_ref, y_ref, o_ref, dest_ref, gbuf_ref, sem_ref, *, alpha):
    tm = h_ref.shape[0]
    _load_slots(dest_hbm, dest_ref, sem_ref.at[1])

    def row_copy(tok, k):
        return pltpu.make_async_copy(y_ref.at[pl.ds(dest_ref[k, tok], 1), :], gbuf_ref.at[k, pl.ds(tok, 1), :],
                                     sem_ref.at[0])

    def issue(tok, carry):
        for k in range(TOP_K):
            row_copy(tok, k).start()
        return carry

    def drain(tok, carry):
        for k in range(TOP_K):
            row_copy(tok, k).wait()
        return carry

    lax.fori_loop(0, tm, issue, 0)
    lax.fori_loop(0, tm, drain, 0)
    lo, hi = _unpack_halves(ysh_ref[...])
    wt = wt_ref[...]
    for k in range(TOP_K):
        glo, ghi = _unpack_halves(gbuf_ref[k])
        wk = wt[:, k:k + 1]
        lo = lo + glo * wk
        hi = hi + ghi * wk
    x = alpha * h_ref[...] + jnp.concatenate([lo, hi], axis=1)
    mu = jnp.mean(x, axis=-1, keepdims=True)
    xc = x - mu
    var = jnp.mean(xc * xc, axis=-1, keepdims=True)
    o_ref[...] = xc * lax.rsqrt(var + LN_EPS) * g_ref[...] + b_ref[...]


def _combine(h2, ysh, y_sorted, dest, wt, g, b, alpha):
    t, d = h2.shape
    w = d // 2
    tm = _pick(t, (128,))
    vec = pl.BlockSpec((1, d), lambda i: (0, 0))
    return pl.pallas_call(
        functools.partial(_combine_body, alpha=alpha),
        grid=(t // tm,),
        in_specs=[pl.BlockSpec(memory_space=pl.ANY),
                  pl.BlockSpec((tm, TOP_K), lambda i: (i, 0)),
                  pl.BlockSpec((tm, d), lambda i: (i, 0)),
                  pl.BlockSpec((tm, w), lambda i: (i, 0)),
                  vec, vec,
                  pl.BlockSpec(memory_space=pl.ANY)],
        out_specs=pl.BlockSpec((tm, d), lambda i: (i, 0)),
        out_shape=jax.ShapeDtypeStruct((t, d), F32),
        scratch_shapes=[pltpu.SMEM((TOP_K, tm), I32), pltpu.VMEM((TOP_K, tm, w), U32), pltpu.SemaphoreType.DMA((2,))],
        compiler_params=_cparams(("arbitrary",)),
        name="moe_combine",
    )(_slot_tiles(dest, tm), wt, h2, ysh, g.reshape(1, d), b.reshape(1, d), y_sorted)


def _rope_tables(pos):
    inv_freq = ROPE_THETA ** (-jnp.arange(0, MLA_ROPE, 2, dtype=F32) / MLA_ROPE)
    ang = pos.astype(F32)[..., None] * inv_freq
    return jnp.cos(ang), jnp.sin(ang)


def kernel(x, positions, meta_tokens, ln_in_g, ln_in_b, w_in, b_gate, conv_w, conv_b, dt_bias, a_log, d_skip,
           ssm_norm_g, w_ssm_proj, q_a_norm_g, w_q_b, kv_a_norm_g, w_kv_b, w_attn_proj, w_out, ln1_g, ln1_b,
           w_router, router_bias, w_exp_gate, w_exp_up, w_exp_down, w_sh_gate, w_sh_up, w_sh_down, ln2_g, ln2_b):
    bsz, seq, d = x.shape
    depth = w_in.shape[0]
    heads = dt_bias.shape[-1]
    d_inner = w_ssm_proj.shape[1]
    conv_dim = conv_w.shape[-1]
    q_rank = w_q_b.shape[1]
    kv_rank = w_kv_b.shape[1]
    hh = MLA_HEADS
    g = SSM_GROUPS
    hg = heads // g
    assert seq % CHUNK == 0 and d % (2 * LANES) == 0
    lp = seq + CHUNK
    t = bsz * seq
    alpha = (2.0 * depth) ** 0.25

    meta = jnp.broadcast_to(meta_tokens[None].astype(x.dtype), (bsz, N_META, d))
    hcat = jnp.concatenate([x, jnp.zeros((bsz, META_PAD, d), x.dtype), meta], axis=1)
    pos = jnp.concatenate([positions.astype(I32) + N_META, jnp.zeros((bsz, META_PAD), I32),
                           jnp.broadcast_to(jnp.arange(N_META, dtype=I32), (bsz, N_META))], axis=1)
    cos, sin = _rope_tables(pos)
    zr = jnp.zeros_like(cos)
    ones = jnp.ones((bsz, lp, MLA_NOPE), F32)
    zn = jnp.zeros((bsz, lp, MLA_NOPE), F32)
    rope_c = jnp.concatenate([ones, cos, cos, zr, zr], axis=-1)[:, :seq]
    rope_slo = jnp.concatenate([zn, -sin, zr, zr, zr], axis=-1)[:, :seq]
    rope_shi = jnp.concatenate([zn, zr, sin, zr, zr], axis=-1)[:, :seq]
    k_c = jnp.concatenate([cos, cos, zr, zr], axis=-1)
    k_s = jnp.concatenate([-sin, sin, zr, zr], axis=-1)

    h_f32, h_b16 = _layer_norm(hcat, ln_in_g, ln_in_b)
    tm_all = _pick(lp, (1408, 1152, 1024, 896, 768, 640, 512, 384, 256, 128))
    tm_seq = _pick(seq, (1024, 512, 256, 128))

    out = None
    for l in range(depth):
        offs = [0]
        for wdt in (d_inner, conv_dim, heads, q_rank, kv_rank, MLA_ROPE, 2 * d):
            offs.append(offs[-1] + wdt)
        wl = w_in[l]
        w_z = wl[:, offs[0]:offs[1]].astype(BF16)
        w_xbc = wl[:, offs[1]:offs[2]].astype(BF16)
        w_kr = wl[:, offs[5]:offs[6]]
        small_cols = q_rank + kv_rank + 2 * MLA_ROPE + heads
        small_pad = (-small_cols) % 256
        w_small = jnp.concatenate([wl[:, offs[3]:offs[4]], wl[:, offs[4]:offs[5]], w_kr, w_kr, wl[:, offs[2]:offs[3]],
                                   jnp.zeros((d, small_pad), F32)], axis=1).astype(BF16)
        w_g = wl[:, offs[6]:offs[7]].astype(BF16)

        z3 = _mm(h_b16, w_z, out_dtype=BF16, tm=tm_all, tn=_pick(d_inner, (512, 256, 128)), name="in_proj_z")
        xbc3 = _mm(h_b16, w_xbc, out_dtype=BF16, tm=tm_all, tn=_pick(conv_dim, (512, 256, 128)), name="in_proj_xbc")
        small = _mm(h_b16, w_small, out_dtype=F32, tm=tm_all, tn=256, name="in_proj_small")
        gates = _mm(h_b16, w_g, out_dtype=BF16, tm=tm_seq, tn=512, n_row_tiles=seq // tm_seq,
                    bias=b_gate[l], act="sigmoid", name="in_proj_gates")
        o = 0
        q_a = small[:, :, o:o + q_rank]; o += q_rank
        kv_a = small[:, :, o:o + kv_rank]; o += kv_rank
        kr2 = small[:, :, o:o + 2 * MLA_ROPE]; o += 2 * MLA_ROPE
        dt_raw = small[:, :, o:o + heads]

        dt_g = jnp.pad(dt_raw.reshape(bsz, lp, g, hg).transpose(0, 2, 1, 3), ((0, 0), (0, 0), (0, 0), (0, LANES - hg)))
        pad_h = lambda v: jnp.pad(v.astype(F32).reshape(g, 1, hg), ((0, 0), (0, 0), (0, LANES - hg)))
        y_ssm = _ssd(xbc3, z3, dt_g, conv_w[l].astype(F32), conv_b[l].reshape(1, conv_dim).astype(F32),
                     pad_h(dt_bias[l]), pad_h(a_log[l]),
                     jnp.repeat(d_skip[l].astype(F32), SSM_HEAD_DIM).reshape(1, d_inner),
                     ssm_norm_g[l].reshape(1, d_inner).astype(F32), d_inner, heads)

        qscale = (MLA_QK ** -0.5) * math.log2(math.e)
        head_pad = ((0, 0), (0, 0), (0, MLA_QKP - MLA_QK))
        w_q = jnp.pad(w_q_b[l].reshape(q_rank, hh, MLA_QK), head_pad).reshape(q_rank, hh * MLA_QKP).astype(BF16)
        q3 = _mm(q_a, w_q, out_dtype=BF16, tm=tm_seq, tn=4 * MLA_QKP, n_row_tiles=seq // tm_seq,
                 gain=q_a_norm_g[l], rope=(rope_c, rope_slo, rope_shi), scale=qscale, name="q_proj")
        wkv = w_kv_b[l].reshape(kv_rank, hh, MLA_NOPE + MLA_V)
        w_k = jnp.pad(wkv[:, :, :MLA_NOPE], ((0, 0), (0, 0), (0, MLA_QKP - MLA_NOPE)))
        w_k = w_k.reshape(kv_rank, hh * MLA_QKP).astype(BF16)
        w_vt = wkv[:, :, MLA_NOPE:].reshape(kv_rank, hh * MLA_V).T.astype(BF16)
        k3 = _mm(kv_a, w_k, out_dtype=BF16, tm=tm_all, tn=4 * MLA_QKP, gain=kv_a_norm_g[l],
                 kadd=(kr2, k_c, k_s), name="k_proj")
        vt3 = _mm_t(kv_a, w_vt, kv_a_norm_g[l], out_dtype=BF16, tm=tm_all, tn=512, name="v_proj")
        y_attn = _attention(q3, k3, vt3, seq)

        mixed = _mm_ktiled([y_ssm, y_attn], [w_ssm_proj[l].astype(BF16), w_attn_proj[l].astype(BF16)],
                           out_dtype=BF16, rows=seq, tm=tm_seq, tn=_pick(d, (1024, 512)), nk=4,
                           gates=gates, name="branch_proj")
        pre1 = _mm_ktiled([mixed], [w_out[l].astype(BF16)], out_dtype=F32, rows=seq, tm=tm_seq,
                          tn=_pick(d, (1024, 512)), nk=2, res=h_f32, res_scale=alpha, name="out_proj")
        h1, h1p = _layer_norm_pack(pre1.reshape(t, d), ln1_g[l], ln1_b[l])

        e_slot, pos_slot, w_slot, cnt = _router(h1, w_router[l], router_bias[l])
        counts = cnt[:, 0].astype(I32)
        padded = (counts + MOE_ROWS - 1) // MOE_ROWS * MOE_ROWS
        ends = jnp.cumsum(padded)
        starts = ends - padded
        onehot = e_slot[None] == jnp.arange(N_EXPERTS, dtype=I32)[:, None, None]
        dest = pos_slot + jnp.sum(jnp.where(onehot, starts[:, None, None], 0), axis=0)
        n_blocks = -(-(t * TOP_K) // MOE_ROWS) + N_EXPERTS
        blk0 = jnp.arange(n_blocks, dtype=I32) * MOE_ROWS
        block_e = jnp.minimum(jnp.sum((ends[None, :] <= blk0[:, None]).astype(I32), axis=1), N_EXPERTS - 1)
        n_used = (ends[-1] // MOE_ROWS).reshape(1)
        xs = _dispatch(h1p, dest, counts, starts, n_blocks * MOE_ROWS)
        y_sorted = _experts(xs, w_exp_gate[l].astype(BF16), w_exp_up[l].astype(BF16), w_exp_down[l].astype(BF16),
                            block_e, n_used)
        y_shared = _experts(h1p, w_sh_gate[l][None].astype(BF16), w_sh_up[l][None].astype(BF16),
                            w_sh_down[l][None].astype(BF16), jnp.zeros((t // MOE_ROWS,), I32),
                            jnp.full((1,), t // MOE_ROWS, I32))
        out = _combine(h1, y_shared, y_sorted, dest, w_slot.T, ln2_g[l], ln2_b[l], alpha)
        if l + 1 < depth:
            raise NotImplementedError("stacked layers need the meta rows carried through the channel mixer")
    return out.reshape(bsz, seq, d)
```

```python
import functools
import math

import jax
import jax.numpy as jnp
from jax import lax
from jax.experimental import pallas as pl
from jax.experimental.pallas import tpu as pltpu

F32 = jnp.float32
BF16 = jnp.bfloat16
U32 = jnp.uint32
I32 = jnp.int32

N_META = 16
CHUNK = 128
META_PAD = CHUNK - N_META
SSM_HEAD_DIM = 64
SSM_GROUPS = 8
SSM_STATE = 128
SSM_CONV = 4
MLA_HEADS = 64
MLA_NOPE = 128
MLA_ROPE = 64
MLA_V = 128
MLA_QK = MLA_NOPE + MLA_ROPE
ROPE_THETA = 10000.0
N_EXPERTS = 64
N_EXPERT_GROUPS = 8
TOPK_GROUPS = 4
TOP_K = 8
ROUTED_SCALE = 2.5
LN_EPS = 1e-5
RMS_EPS = 1e-6
NEG = -1e30
LANES = 128
MLA_QKP = 2 * LANES
MOE_ROWS = 256
MLA_VP = MLA_V + 16
VMEM_BIG = 56 * 1024 * 1024
VMEM_MID = 44 * 1024 * 1024


def _cparams(sem, vmem=VMEM_MID):
    return pltpu.CompilerParams(dimension_semantics=sem, vmem_limit_bytes=vmem)


def _pick(n, cands):
    for c in cands:
        if n % c == 0:
            return c
    raise ValueError(f"no tile for {n} in {cands}")


def _ln_body(x_ref, g_ref, b_ref, of_ref, ob_ref):
    x = x_ref[...]
    mu = jnp.mean(x, axis=-1, keepdims=True)
    xc = x - mu
    var = jnp.mean(xc * xc, axis=-1, keepdims=True)
    y = xc * lax.rsqrt(var + LN_EPS) * g_ref[...] + b_ref[...]
    of_ref[...] = y
    ob_ref[...] = y.astype(BF16)


def _layer_norm(x3, g, b):
    bsz, rows, d = x3.shape
    tm = _pick(rows, (256, 128))
    spec = pl.BlockSpec((None, tm, d), lambda bi, i: (bi, i, 0))
    vec = pl.BlockSpec((1, d), lambda bi, i: (0, 0))
    return pl.pallas_call(
        _ln_body,
        grid=(bsz, rows // tm),
        in_specs=[spec, vec, vec],
        out_specs=[spec, spec],
        out_shape=[jax.ShapeDtypeStruct(x3.shape, F32), jax.ShapeDtypeStruct(x3.shape, BF16)],
        compiler_params=_cparams(("parallel", "parallel")),
        name="layer_norm",
    )(x3, g.reshape(1, d), b.reshape(1, d))


def _pack_halves(y):
    n = y.shape[1] // 2
    lo = pltpu.bitcast(y[:, :n].astype(BF16).astype(F32), U32) >> 16
    hi = pltpu.bitcast(y[:, n:].astype(BF16).astype(F32), U32) & jnp.uint32(0xFFFF0000)
    return hi | lo


def _unpack_halves(w):
    lo = pltpu.bitcast(w << 16, F32)
    hi = pltpu.bitcast(w & jnp.uint32(0xFFFF0000), F32)
    return lo, hi


def _ln_pack_body(x_ref, g_ref, b_ref, of_ref, op_ref):
    x = x_ref[...]
    mu = jnp.mean(x, axis=-1, keepdims=True)
    xc = x - mu
    var = jnp.mean(xc * xc, axis=-1, keepdims=True)
    y = xc * lax.rsqrt(var + LN_EPS) * g_ref[...] + b_ref[...]
    of_ref[...] = y
    op_ref[...] = _pack_halves(y)


def _layer_norm_pack(x2, g, b):
    rows, d = x2.shape
    tm = _pick(rows, (256, 128))
    vec = pl.BlockSpec((1, d), lambda i: (0, 0))
    return pl.pallas_call(
        _ln_pack_body,
        grid=(rows // tm,),
        in_specs=[pl.BlockSpec((tm, d), lambda i: (i, 0)), vec, vec],
        out_specs=[pl.BlockSpec((tm, d), lambda i: (i, 0)), pl.BlockSpec((tm, d // 2), lambda i: (i, 0))],
        out_shape=[jax.ShapeDtypeStruct((rows, d), F32), jax.ShapeDtypeStruct((rows, d // 2), U32)],
        compiler_params=_cparams(("parallel",)),
        name="layer_norm_pack",
    )(x2, g.reshape(1, d), b.reshape(1, d))


def _mm_body(*refs, has_gain, has_bias, act, rope, kadd, scale):
    it = iter(refs)
    a_ref, w_ref = next(it), next(it)
    gain_ref = next(it) if has_gain else None
    bias_ref = next(it) if has_bias else None
    rope_refs = [next(it) for _ in range(3)] if rope else None
    kadd_refs = [next(it) for _ in range(3)] if kadd else None
    o_ref = next(it)
    a = a_ref[...]
    if has_gain:
        af = a.astype(F32)
        a = af * lax.rsqrt(jnp.mean(af * af, axis=-1, keepdims=True) + RMS_EPS) * gain_ref[...]
    r = jnp.dot(a.astype(BF16), w_ref[...], preferred_element_type=F32)
    tn = r.shape[1]
    if has_bias:
        r = r + bias_ref[...]
    if act == "sigmoid":
        r = jax.nn.sigmoid(r)
    if rope:
        rep = tn // rope_refs[0].shape[1]
        c, slo, shi = (jnp.concatenate([t[...]] * rep, axis=1) for t in rope_refs)
        r = r * c + pltpu.roll(r, tn - MLA_ROPE // 2, 1) * slo + pltpu.roll(r, MLA_ROPE // 2, 1) * shi
    if kadd:
        kr_ref, kc, ks = kadd_refs
        kr = kr_ref[...]
        piece = jnp.concatenate([jnp.zeros_like(kr), kr * kc[...] + pltpu.roll(kr, MLA_ROPE // 2, 1) * ks[...]], axis=1)
        r = r + jnp.concatenate([piece] * (tn // piece.shape[1]), axis=1)
    if scale is not None:
        r = r * scale
    o_ref[...] = r.astype(o_ref.dtype)


def _mm(a3, w, *, out_dtype, tm, tn, n_row_tiles=None, gain=None, bias=None, act=None, rope=None, kadd=None,
        scale=None, name="mm"):
    bsz, rows, k = a3.shape
    n = w.shape[1]
    ni = rows // tm if n_row_tiles is None else n_row_tiles
    assert n % tn == 0 and (n_row_tiles is not None or rows % tm == 0)
    in_specs = [pl.BlockSpec((None, tm, k), lambda bi, i, j: (bi, i, 0)),
                pl.BlockSpec((k, tn), lambda bi, i, j: (0, j))]
    args = [a3, w]
    if gain is not None:
        in_specs.append(pl.BlockSpec((1, k), lambda bi, i, j: (0, 0)))
        args.append(gain.reshape(1, k).astype(F32))
    if bias is not None:
        in_specs.append(pl.BlockSpec((1, tn), lambda bi, i, j: (0, j)))
        args.append(bias.reshape(1, n).astype(F32))
    for t in (rope or ()) + (kadd or ()):
        in_specs.append(pl.BlockSpec((None, tm, t.shape[2]), lambda bi, i, j: (bi, i, 0)))
        args.append(t)
    body = functools.partial(_mm_body, has_gain=gain is not None, has_bias=bias is not None, act=act,
                             rope=rope is not None, kadd=kadd is not None, scale=scale)
    return pl.pallas_call(
        body,
        grid=(bsz, ni, n // tn),
        in_specs=in_specs,
        out_specs=pl.BlockSpec((None, tm, tn), lambda bi, i, j: (bi, i, j)),
        out_shape=jax.ShapeDtypeStruct((bsz, ni * tm, n), out_dtype),
        compiler_params=_cparams(("parallel", "parallel", "arbitrary"), VMEM_BIG),
        name=name,
    )(*args)


def _mm_t_body(a_ref, wt_ref, gain_ref, bias_ref, o_ref):
    af = a_ref[...].astype(F32)
    a = af * lax.rsqrt(jnp.mean(af * af, axis=-1, keepdims=True) + RMS_EPS) * gain_ref[...]
    r = lax.dot_general(wt_ref[...], a.astype(BF16), (((1,), (1,)), ((), ())), preferred_element_type=F32)
    o_ref[...] = (r + bias_ref[...]).astype(o_ref.dtype)


def _mm_t(a3, wt, gain, bias_col, *, out_dtype, tm, tn, name):
    bsz, rows, k = a3.shape
    n = wt.shape[0]
    return pl.pallas_call(
        _mm_t_body,
        grid=(bsz, rows // tm, n // tn),
        in_specs=[pl.BlockSpec((None, tm, k), lambda bi, i, j: (bi, i, 0)),
                  pl.BlockSpec((tn, k), lambda bi, i, j: (j, 0)),
                  pl.BlockSpec((1, k), lambda bi, i, j: (0, 0)),
                  pl.BlockSpec((tn, 1), lambda bi, i, j: (j, 0))],
        out_specs=pl.BlockSpec((None, tn, tm), lambda bi, i, j: (bi, j, i)),
        out_shape=jax.ShapeDtypeStruct((bsz, n, rows), out_dtype),
        compiler_params=_cparams(("parallel", "parallel", "arbitrary"), VMEM_BIG),
        name=name,
    )(a3, wt, gain.reshape(1, k).astype(F32), bias_col.reshape(n, 1).astype(F32))


def _mmk_body(*refs, n_pairs, has_gate, has_res, res_scale):
    it = iter(refs)
    a_refs = [next(it) for _ in range(n_pairs)]
    w_refs = [next(it) for _ in range(n_pairs)]
    g_refs = [next(it) for _ in range(n_pairs)] if has_gate else None
    res_ref = next(it) if has_res else None
    o_ref = next(it)
    acc_refs = [next(it) for _ in range(n_pairs)]
    kk = pl.program_id(3)

    @pl.when(kk == 0)
    def _():
        for acc in acc_refs:
            acc[...] = jnp.zeros_like(acc)

    for a_ref, w_ref, acc in zip(a_refs, w_refs, acc_refs):
        acc[...] += jnp.dot(a_ref[...], w_ref[...], preferred_element_type=F32)

    @pl.when(kk == pl.num_programs(3) - 1)
    def _():
        r = None
        for p, acc in enumerate(acc_refs):
            t = acc[...]
            if has_gate:
                t = t * g_refs[p][...].astype(F32)
            r = t if r is None else r + t
        if has_res:
            r = r + res_scale * res_ref[...]
        o_ref[...] = r.astype(o_ref.dtype)


def _mm_ktiled(a_list, w_list, *, out_dtype, rows, tm, tn, nk, gates=None, res=None, res_scale=1.0, name="mmk"):
    bsz = a_list[0].shape[0]
    n = w_list[0].shape[1]
    npairs = len(a_list)
    tks = [a.shape[2] // nk for a in a_list]
    assert all(a.shape[2] == tk * nk and tk % LANES == 0 for a, tk in zip(a_list, tks))
    o_spec = pl.BlockSpec((None, tm, tn), lambda bi, i, j, kk: (bi, i, j))
    in_specs = ([pl.BlockSpec((None, tm, tk), lambda bi, i, j, kk: (bi, i, kk)) for tk in tks]
                + [pl.BlockSpec((tk, tn), lambda bi, i, j, kk: (kk, j)) for tk in tks])
    args = list(a_list) + list(w_list)
    if gates is not None:
        for p in range(npairs):
            in_specs.append(pl.BlockSpec((None, tm, tn), lambda bi, i, j, kk, p=p: (bi, i, p * (n // tn) + j)))
            args.append(gates)
    if res is not None:
        in_specs.append(o_spec)
        args.append(res)
    body = functools.partial(_mmk_body, n_pairs=npairs, has_gate=gates is not None, has_res=res is not None,
                             res_scale=res_scale)
    return pl.pallas_call(
        body,
        grid=(bsz, rows // tm, n // tn, nk),
        in_specs=in_specs,
        out_specs=o_spec,
        out_shape=jax.ShapeDtypeStruct((bsz, rows, n), out_dtype),
        scratch_shapes=[pltpu.VMEM((tm, tn), F32) for _ in range(npairs)],
        compiler_params=_cparams(("parallel", "parallel", "parallel", "arbitrary"), VMEM_BIG),
        name=name,
    )(*args)


def _silu(x):
    return x * jax.nn.sigmoid(x)


def _ssd_body(xs_ref, b_ref, c_ref, z_ref, dt_ref, wx_ref, wb_ref, wc_ref, bx_ref, bb_ref, bc_ref,
              dtb_ref, alog_ref, dskip_ref, ng_ref, e_ref, o_ref,
              state_ref, extx_ref, extb_ref, extc_ref, y_ref, *, hg):
    c = pl.program_id(2)
    q = CHUNK
    p = SSM_HEAD_DIM
    first = c == 0
    row = lax.broadcasted_iota(I32, (q, 1), 0)
    live = jnp.logical_or(jnp.logical_not(first), row >= META_PAD)

    @pl.when(first)
    def _():
        state_ref[...] = jnp.zeros_like(state_ref)
        extx_ref[0:8, :] = jnp.zeros((8, extx_ref.shape[1]), F32)
        extb_ref[0:8, :] = jnp.zeros((8, extb_ref.shape[1]), F32)
        extc_ref[0:8, :] = jnp.zeros((8, extc_ref.shape[1]), F32)

    def conv(ext_ref, u_ref, w_ref, bias_ref):
        @pl.when(first)
        def _():
            ext_ref[8:8 + q, :] = jnp.where(live, u_ref[...].astype(F32), 0.0)

        @pl.when(jnp.logical_not(first))
        def _():
            ext_ref[8:8 + q, :] = u_ref[...].astype(F32)

        acc = bias_ref[...]
        for k in range(SSM_CONV):
            acc = acc + ext_ref[pl.ds(8 - (SSM_CONV - 1) + k, q), :] * w_ref[k:k + 1, :]
        ext_ref[0:8, :] = ext_ref[q:q + 8, :]
        return _silu(acc)

    xs = conv(extx_ref, xs_ref, wx_ref, bx_ref)
    bm = conv(extb_ref, b_ref, wb_ref, bb_ref)
    cm = conv(extc_ref, c_ref, wc_ref, bc_ref)

    x = dt_ref[...] + dtb_ref[...]
    dt = jnp.maximum(x, 0.0) + jnp.log1p(jnp.exp(-jnp.abs(x)))
    dt = jnp.where(live, dt, 0.0)
    a = -jnp.exp(alog_ref[...])
    r_i = lax.broadcasted_iota(I32, (q, q), 0)
    c_i = lax.broadcasted_iota(I32, (q, q), 1)
    causal = r_i >= c_i
    hi = lax.Precision.HIGHEST
    a_cs = jnp.dot(causal.astype(F32), dt * a, precision=hi, preferred_element_type=F32)
    a_cs_t = a_cs.T
    ea = jnp.exp(a_cs)
    decay_end = jnp.exp(a_cs[q - 1:q, :] - a_cs)
    e = e_ref[...]

    def spread(v):
        v_hi = v.astype(BF16)
        v_lo = (v - v_hi.astype(F32)).astype(BF16)
        return jnp.dot(v_hi, e, preferred_element_type=F32) + jnp.dot(v_lo, e, preferred_element_type=F32)

    dt_x = spread(dt)
    ea_x = spread(ea)
    de_x = spread(decay_end)

    xdt = xs * dt_x
    xdt_b = xdt.astype(BF16)
    cm_b = cm.astype(BF16)
    bm_b = bm.astype(BF16)
    cb = lax.dot_general(cm_b, bm_b, (((1,), (1,)), ((), ())), preferred_element_type=F32)
    prev = state_ref[...]
    y_off = jnp.dot(cm_b, prev.astype(BF16), preferred_element_type=F32) * ea_x
    for j in range(hg):
        seg = a_cs[:, j:j + 1] - a_cs_t[j:j + 1, :]
        m = (jnp.exp(jnp.where(causal, seg, NEG)) * cb).astype(BF16)
        y_ref[:, j * p:(j + 1) * p] = jnp.dot(m, xdt_b[:, j * p:(j + 1) * p], preferred_element_type=F32)
    y = y_ref[...] + y_off + xs * dskip_ref[...]
    state_ref[...] = prev * ea_x[q - 1:q, :] + jnp.dot(bm.T.astype(BF16), (xdt * de_x).astype(BF16),
                                                        preferred_element_type=F32)
    yz = y * _silu(z_ref[...].astype(F32))
    yn = yz * lax.rsqrt(jnp.mean(yz * yz, axis=-1, keepdims=True) + RMS_EPS) * ng_ref[...]
    o_ref[...] = yn.astype(o_ref.dtype)


def _ssd(xbc3, z3, dt_g, conv_w, conv_b, dtb_g, alog_g, dskip_x, norm_g, d_inner, heads):
    bsz, rows, conv_dim = xbc3.shape
    g, n, p, q = SSM_GROUPS, SSM_STATE, SSM_HEAD_DIM, CHUNK
    hg = heads // g
    gw = hg * p
    nc = rows // q
    assert gw % LANES == 0 and d_inner % n == 0 and hg <= LANES
    boff = d_inner // n
    expand = (jnp.arange(LANES)[:, None] == (jnp.arange(gw) // p)[None, :]).astype(BF16)

    def chunk(c):
        return (c + nc - 1) % nc

    in_specs = [
        pl.BlockSpec((None, q, gw), lambda b, gi, c: (b, chunk(c), gi)),
        pl.BlockSpec((None, q, n), lambda b, gi, c: (b, chunk(c), boff + gi)),
        pl.BlockSpec((None, q, n), lambda b, gi, c: (b, chunk(c), boff + g + gi)),
        pl.BlockSpec((None, q, gw), lambda b, gi, c: (b, chunk(c), gi)),
        pl.BlockSpec((None, None, q, LANES), lambda b, gi, c: (b, gi, chunk(c), 0)),
        pl.BlockSpec((SSM_CONV, gw), lambda b, gi, c: (0, gi)),
        pl.BlockSpec((SSM_CONV, n), lambda b, gi, c: (0, boff + gi)),
        pl.BlockSpec((SSM_CONV, n), lambda b, gi, c: (0, boff + g + gi)),
        pl.BlockSpec((1, gw), lambda b, gi, c: (0, gi)),
        pl.BlockSpec((1, n), lambda b, gi, c: (0, boff + gi)),
        pl.BlockSpec((1, n), lambda b, gi, c: (0, boff + g + gi)),
        pl.BlockSpec((None, 1, LANES), lambda b, gi, c: (gi, 0, 0)),
        pl.BlockSpec((None, 1, LANES), lambda b, gi, c: (gi, 0, 0)),
        pl.BlockSpec((1, gw), lambda b, gi, c: (0, gi)),
        pl.BlockSpec((1, gw), lambda b, gi, c: (0, gi)),
        pl.BlockSpec((LANES, gw), lambda b, gi, c: (0, 0)),
    ]
    return pl.pallas_call(
        functools.partial(_ssd_body, hg=hg),
        grid=(bsz, g, nc),
        in_specs=in_specs,
        out_specs=pl.BlockSpec((None, q, gw), lambda b, gi, c: (b, chunk(c), gi)),
        out_shape=jax.ShapeDtypeStruct((bsz, rows, d_inner), BF16),
        scratch_shapes=[pltpu.VMEM((n, gw), F32), pltpu.VMEM((q + 8, gw), F32), pltpu.VMEM((q + 8, n), F32),
                        pltpu.VMEM((q + 8, n), F32), pltpu.VMEM((q, gw), F32)],
        compiler_params=_cparams(("parallel", "parallel", "arbitrary")),
        name="ssd",
    )(xbc3, xbc3, xbc3, z3, dt_g, conv_w, conv_w, conv_w, conv_b, conv_b, conv_b, dtb_g, alog_g,
      dskip_x, norm_g, expand)


def _attn_body(qi_ref, ki_ref, q_ref, km_ref, vm_ref, k_ref, v_ref, o_ref, m_ref, acc_ref, *, heads):
    step_id = pl.program_id(2)
    qi, ki = qi_ref[step_id], ki_ref[step_id]

    @pl.when(ki < 0)
    def _():
        m_ref[...] = jnp.full_like(m_ref, NEG)
        acc_ref[...] = jnp.zeros_like(acc_ref)

    def step(kr, vr, mask_fn):
        def scores(h):
            qh = q_ref[:, h * MLA_QKP:(h + 1) * MLA_QKP]
            kh = kr[:, h * MLA_QKP:(h + 1) * MLA_QKP]
            st = lax.dot_general(kh, qh, (((1,), (1,)), ((), ())), preferred_element_type=F32)
            return st if mask_fn is None else mask_fn(st)

        def probs(h, st):
            m_prev = m_ref[h]
            m_new = jnp.maximum(m_prev, jnp.max(st, axis=0, keepdims=True))
            m_ref[h] = m_new
            return jnp.exp2(m_prev - m_new), jnp.exp2((st - m_new).astype(BF16))

        def accumulate(h, alpha, pt):
            sl = slice(h * MLA_VP, (h + 1) * MLA_VP)
            acc_ref[sl, :] = acc_ref[sl, :] * alpha + jnp.dot(vr[sl, :], pt, preferred_element_type=F32)

        st_q, pr_q = {}, {}
        for stage in range(heads + 2):
            if stage < heads:
                st_q[stage] = scores(stage)
            if 0 <= stage - 1 < heads:
                pr_q[stage - 1] = probs(stage - 1, st_q.pop(stage - 1))
            if 0 <= stage - 2 < heads:
                accumulate(stage - 2, *pr_q.pop(stage - 2))

    def meta_mask(st):
        key = lax.broadcasted_iota(I32, st.shape, 0)
        return jnp.where(key >= META_PAD, st, NEG)

    def causal_mask(st):
        key = lax.broadcasted_iota(I32, st.shape, 0)
        qry = lax.broadcasted_iota(I32, st.shape, 1)
        return jnp.where(key <= qry, st, NEG)

    @pl.when(ki < 0)
    def _():
        step(km_ref, vm_ref, meta_mask)

    @pl.when(jnp.logical_and(ki >= 0, ki < qi))
    def _():
        step(k_ref, v_ref, None)

    @pl.when(ki == qi)
    def _():
        step(k_ref, v_ref, causal_mask)
        for h in range(heads):
            num = acc_ref[h * MLA_VP:h * MLA_VP + MLA_V, :]
            den = acc_ref[h * MLA_VP + MLA_V:h * MLA_VP + MLA_V + 1, :]
            o_ref[:, h * MLA_V:(h + 1) * MLA_V] = (num / den).T.astype(o_ref.dtype)


def _attention(q3, k3, vt3, seq):
    bsz = q3.shape[0]
    hp = 8
    t = _pick(seq, (512, 256, 128))
    nq = seq // t
    meta_blk = seq // CHUNK
    qw, vw = hp * MLA_QKP, hp * MLA_VP
    qi_tab = jnp.asarray([qi for qi in range(nq) for _ in range(qi + 2)], I32)
    ki_tab = jnp.asarray([ki for qi in range(nq) for ki in range(-1, qi + 1)], I32)

    def q_idx(b, h, s, qt, kt):
        return (b, qt[s], h)

    def k_idx(b, h, s, qt, kt):
        return (b, jnp.maximum(kt[s], 0), h)

    def vt_idx(b, h, s, qt, kt):
        return (b, h, jnp.maximum(kt[s], 0))

    grid_spec = pltpu.PrefetchScalarGridSpec(
        num_scalar_prefetch=2,
        grid=(bsz, MLA_HEADS // hp, int(qi_tab.shape[0])),
        in_specs=[
            pl.BlockSpec((None, t, qw), q_idx),
            pl.BlockSpec((None, CHUNK, qw), lambda b, h, s, qt, kt: (b, meta_blk, h)),
            pl.BlockSpec((None, vw, CHUNK), lambda b, h, s, qt, kt: (b, h, meta_blk)),
            pl.BlockSpec((None, t, qw), k_idx),
            pl.BlockSpec((None, vw, t), vt_idx),
        ],
        out_specs=pl.BlockSpec((None, t, hp * MLA_V), q_idx),
        scratch_shapes=[pltpu.VMEM((hp, 1, t), F32), pltpu.VMEM((vw, t), F32)],
    )
    return pl.pallas_call(
        functools.partial(_attn_body, heads=hp),
        grid_spec=grid_spec,
        out_shape=jax.ShapeDtypeStruct((bsz, seq, MLA_HEADS * MLA_V), BF16),
        compiler_params=_cparams(("parallel", "parallel", "arbitrary")),
        name="mla_attention",
    )(qi_tab, ki_tab, q3, k3, vt3, k3, vt3)


def _router_body(h_ref, wr_ref, rb_ref, e_ref, pos_ref, w_ref, cnt_ref, carry_ref):
    i = pl.program_id(0)
    ne, ng = N_EXPERTS, N_EXPERT_GROUPS
    gs = ne // ng
    tm = h_ref.shape[0]

    @pl.when(i == 0)
    def _():
        carry_ref[...] = jnp.zeros_like(carry_ref)

    logits = lax.dot_general(wr_ref[...], h_ref[...], (((1,), (1,)), ((), ())), precision=lax.Precision.HIGHEST,
                             preferred_element_type=F32)
    scores = jax.nn.sigmoid(logits)
    choice = scores + rb_ref[...]
    sub = lax.broadcasted_iota(I32, (gs, tm), 0)
    grp_rows = []
    for g in range(ng):
        blk = choice[g * gs:(g + 1) * gs, :]
        m1 = jnp.max(blk, axis=0, keepdims=True)
        first = jnp.min(jnp.where(blk == m1, sub, gs), axis=0, keepdims=True)
        m2 = jnp.max(jnp.where(sub == first, -jnp.inf, blk), axis=0, keepdims=True)
        grp_rows.append(m1 + m2)
    grp = jnp.concatenate(grp_rows, axis=0)
    grank = jnp.zeros((ng, tm), I32)
    gidx = lax.broadcasted_iota(I32, (ng, tm), 0)
    for g in range(ng):
        rowv = grp[g:g + 1, :]
        beats = jnp.logical_or(rowv > grp, jnp.logical_and(rowv == grp, gidx > g))
        grank = grank + beats.astype(I32)
    gsel = (grank < TOPK_GROUPS).astype(F32)
    esel = jnp.concatenate([jnp.broadcast_to(gsel[g:g + 1, :], (gs, tm)) for g in range(ng)], axis=0)
    masked = jnp.where(esel > 0.0, choice, -jnp.inf)
    eidx = lax.broadcasted_iota(I32, (ne, tm), 0)
    rank = jnp.zeros((ne, tm), I32)
    for e in range(ne):
        rowv = masked[e:e + 1, :]
        beats = jnp.logical_or(rowv > masked, jnp.logical_and(rowv == masked, eidx > e))
        rank = rank + beats.astype(I32)
    top = jnp.logical_and(rank < TOP_K, esel > 0.0)
    topf = top.astype(F32)
    wsel = jnp.where(top, scores, 0.0)
    wn = wsel / jnp.sum(wsel, axis=0, keepdims=True) * ROUTED_SCALE
    r_i = lax.broadcasted_iota(I32, (tm, tm), 0)
    c_i = lax.broadcasted_iota(I32, (tm, tm), 1)
    before = (r_i < c_i).astype(BF16)
    pos = jnp.dot(topf.astype(BF16), before, preferred_element_type=F32) + carry_ref[:, :1]
    carry_ref[...] = carry_ref[...] + jnp.sum(topf, axis=1, keepdims=True)
    cnt_ref[...] = carry_ref[...]
    e_rows, p_rows, w_rows = [], [], []
    for k in range(TOP_K):
        hit = rank == k
        e_rows.append(jnp.sum(jnp.where(hit, eidx, 0), axis=0, keepdims=True))
        p_rows.append(jnp.sum(jnp.where(hit, pos, 0.0), axis=0, keepdims=True))
        w_rows.append(jnp.sum(jnp.where(hit, wn, 0.0), axis=0, keepdims=True))
    e_ref[...] = jnp.concatenate(e_rows, axis=0)
    pos_ref[...] = jnp.concatenate(p_rows, axis=0).astype(I32)
    w_ref[...] = jnp.concatenate(w_rows, axis=0)


def _router(h2, w_router, router_bias):
    t, d = h2.shape
    tm = _pick(t, (256, 128))
    slot = pl.BlockSpec((TOP_K, tm), lambda i: (0, i))
    rb = jnp.broadcast_to(router_bias.astype(F32)[:, None], (N_EXPERTS, tm))
    return pl.pallas_call(
        _router_body,
        grid=(t // tm,),
        in_specs=[pl.BlockSpec((tm, d), lambda i: (i, 0)), pl.BlockSpec((N_EXPERTS, d), lambda i: (0, 0)),
                  pl.BlockSpec((N_EXPERTS, tm), lambda i: (0, 0))],
        out_specs=[slot, slot, slot, pl.BlockSpec((N_EXPERTS, LANES), lambda i: (0, 0))],
        out_shape=[jax.ShapeDtypeStruct((TOP_K, t), I32), jax.ShapeDtypeStruct((TOP_K, t), I32),
                   jax.ShapeDtypeStruct((TOP_K, t), F32), jax.ShapeDtypeStruct((N_EXPERTS, LANES), F32)],
        scratch_shapes=[pltpu.VMEM((N_EXPERTS, LANES), F32)],
        compiler_params=_cparams(("arbitrary",)),
        name="moe_router",
    )(h2, w_router.T.astype(F32), rb)


def _load_slots(dest_hbm, dest_ref, sem):
    cp = pltpu.make_async_copy(dest_hbm.at[pl.program_id(0)], dest_ref, sem)
    cp.start()
    cp.wait()


def _dispatch_body(cnt_ref, start_ref, dest_hbm, x_ref, xs_ref, dest_ref, zero_ref, sem_ref, *, block_rows):
    i = pl.program_id(0)
    tm = x_ref.shape[0]
    _load_slots(dest_hbm, dest_ref, sem_ref.at[2])

    def row_copy(tok, k):
        return pltpu.make_async_copy(x_ref.at[pl.ds(tok, 1), :], xs_ref.at[pl.ds(dest_ref[k, tok], 1), :], sem_ref.at[0])

    def issue(tok, carry):
        for k in range(TOP_K):
            row_copy(tok, k).start()
        return carry

    lax.fori_loop(0, tm, issue, 0)

    @pl.when(i == 0)
    def _():
        zero_ref[...] = jnp.zeros_like(zero_ref)

        def fill(e, carry):
            cnt = cnt_ref[e]
            padded = (cnt + block_rows - 1) // block_rows * block_rows
            base = start_ref[e] + cnt

            def zcopy(r):
                return pltpu.make_async_copy(zero_ref.at[pl.ds(0, 1), :], xs_ref.at[pl.ds(base + r, 1), :], sem_ref.at[1])

            def zstart(r, c2):
                zcopy(r).start()
                return c2

            def zwait(r, c2):
                zcopy(r).wait()
                return c2

            lax.fori_loop(0, padded - cnt, zstart, 0)
            lax.fori_loop(0, padded - cnt, zwait, 0)
            return carry

        lax.fori_loop(0, N_EXPERTS, fill, 0)

    def drain(tok, carry):
        for k in range(TOP_K):
            row_copy(tok, k).wait()
        return carry

    lax.fori_loop(0, tm, drain, 0)


def _slot_tiles(dest, tm):
    k, t = dest.shape
    return dest.reshape(k, t // tm, tm).transpose(1, 0, 2)


def _dispatch(hp2, dest, counts, starts, n_rows):
    t, w = hp2.shape
    tm = _pick(t, (256, 128))
    grid_spec = pltpu.PrefetchScalarGridSpec(
        num_scalar_prefetch=2,
        grid=(t // tm,),
        in_specs=[pl.BlockSpec(memory_space=pl.ANY),
                  pl.BlockSpec((tm, w), lambda i, c, s: (i, 0))],
        out_specs=pl.BlockSpec(memory_space=pl.ANY),
        scratch_shapes=[pltpu.SMEM((TOP_K, tm), I32), pltpu.VMEM((8, w), U32), pltpu.SemaphoreType.DMA((3,))],
    )
    return pl.pallas_call(
        functools.partial(_dispatch_body, block_rows=MOE_ROWS),
        grid_spec=grid_spec,
        out_shape=jax.ShapeDtypeStruct((n_rows, w), U32),
        compiler_params=pltpu.CompilerParams(dimension_semantics=("arbitrary",)),
        name="moe_dispatch",
    )(counts, starts, _slot_tiles(dest, tm), hp2)


def _expert_body(be_ref, nu_ref, x_ref, wg_ref, wu_ref, wd_ref, o_ref):
    j = pl.program_id(0)

    @pl.when(j < nu_ref[0])
    def _():
        half = x_ref.shape[1]
        lo, hi = _unpack_halves(x_ref[...])
        lo, hi = lo.astype(BF16), hi.astype(BF16)

        def proj(w_ref):
            return (jnp.dot(lo, w_ref[:half, :], preferred_element_type=F32)
                    + jnp.dot(hi, w_ref[half:, :], preferred_element_type=F32))

        hmid = (_silu(proj(wg_ref)) * proj(wu_ref)).astype(BF16)
        o_ref[...] = _pack_halves(jnp.dot(hmid, wd_ref[...], preferred_element_type=F32))


def _experts(xs, wg, wu, wd, block_e, n_used):
    rows, w = xs.shape
    _, d, ff = wg.shape
    bm = MOE_ROWS
    nb = rows // bm

    def row_idx(j, be, nu):
        return (jnp.minimum(j, nu[0] - 1), 0)

    def w_idx(j, be, nu):
        return (be[jnp.minimum(j, nu[0] - 1)], 0, 0)

    grid_spec = pltpu.PrefetchScalarGridSpec(
        num_scalar_prefetch=2,
        grid=(nb,),
        in_specs=[pl.BlockSpec((bm, w), row_idx),
                  pl.BlockSpec((None, d, ff), w_idx), pl.BlockSpec((None, d, ff), w_idx),
                  pl.BlockSpec((None, ff, d), w_idx)],
        out_specs=pl.BlockSpec((bm, w), row_idx),
    )
    return pl.pallas_call(
        _expert_body,
        grid_spec=grid_spec,
        out_shape=jax.ShapeDtypeStruct((rows, w), U32),
        compiler_params=_cparams(("arbitrary",), 60 * 1024 * 1024),
        name="moe_experts",
    )(block_e, n_used, xs, wg, wu, wd)


def _combine_body(dest_hbm, wt_ref, h_ref, ysh_ref, g_ref, b_ref, y_ref, o_ref, dest_ref, gbuf_ref, sem_ref, *, alpha):
    tm = h_ref.shape[0]
    _load_slots(dest_hbm, dest_ref, sem_ref.at[1])

    def row_copy(tok, k):
        return pltpu.make_async_copy(y_ref.at[pl.ds(dest_ref[k, tok], 1), :], gbuf_ref.at[k, pl.ds(tok, 1), :],
                                     sem_ref.at[0])

    def issue(tok, carry):
        for k in range(TOP_K):
            row_copy(tok, k).start()
        return carry

    def drain(tok, carry):
        for k in range(TOP_K):
            row_copy(tok, k).wait()
        return carry

    lax.fori_loop(0, tm, issue, 0)
    lax.fori_loop(0, tm, drain, 0)
    lo, hi = _unpack_halves(ysh_ref[...])
    wt = wt_ref[...]
    for k in range(TOP_K):
        glo, ghi = _unpack_halves(gbuf_ref[k])
        wk = wt[:, k:k + 1]
        lo = lo + glo * wk
        hi = hi + ghi * wk
    x = alpha * h_ref[...] + jnp.concatenate([lo, hi], axis=1)
    mu = jnp.mean(x, axis=-1, keepdims=True)
    xc = x - mu
    var = jnp.mean(xc * xc, axis=-1, keepdims=True)
    o_ref[...] = xc * lax.rsqrt(var + LN_EPS) * g_ref[...] + b_ref[...]


def _combine(h2, ysh, y_sorted, dest, wt, g, b, alpha):
    t, d = h2.shape
    w = d // 2
    tm = _pick(t, (128,))
    vec = pl.BlockSpec((1, d), lambda i: (0, 0))
    return pl.pallas_call(
        functools.partial(_combine_body, alpha=alpha),
        grid=(t // tm,),
        in_specs=[pl.BlockSpec(memory_space=pl.ANY),
                  pl.BlockSpec((tm, TOP_K), lambda i: (i, 0)),
                  pl.BlockSpec((tm, d), lambda i: (i, 0)),
                  pl.BlockSpec((tm, w), lambda i: (i, 0)),
                  vec, vec,
                  pl.BlockSpec(memory_space=pl.ANY)],
        out_specs=pl.BlockSpec((tm, d), lambda i: (i, 0)),
        out_shape=jax.ShapeDtypeStruct((t, d), F32),
        scratch_shapes=[pltpu.SMEM((TOP_K, tm), I32), pltpu.VMEM((TOP_K, tm, w), U32), pltpu.SemaphoreType.DMA((2,))],
        compiler_params=_cparams(("arbitrary",)),
        name="moe_combine",
    )(_slot_tiles(dest, tm), wt, h2, ysh, g.reshape(1, d), b.reshape(1, d), y_sorted)


def _rope_tables(pos):
    inv_freq = ROPE_THETA ** (-jnp.arange(0, MLA_ROPE, 2, dtype=F32) / MLA_ROPE)
    ang = pos.astype(F32)[..., None] * inv_freq
    return jnp.cos(ang), jnp.sin(ang)


def kernel(x, positions, meta_tokens, ln_in_g, ln_in_b, w_in, b_gate, conv_w, conv_b, dt_bias, a_log, d_skip,
           ssm_norm_g, w_ssm_proj, q_a_norm_g, w_q_b, kv_a_norm_g, w_kv_b, w_attn_proj, w_out, ln1_g, ln1_b,
           w_router, router_bias, w_exp_gate, w_exp_up, w_exp_down, w_sh_gate, w_sh_up, w_sh_down, ln2_g, ln2_b):
    bsz, seq, d = x.shape
    depth = w_in.shape[0]
    heads = dt_bias.shape[-1]
    d_inner = w_ssm_proj.shape[1]
    conv_dim = conv_w.shape[-1]
    q_rank = w_q_b.shape[1]
    kv_rank = w_kv_b.shape[1]
    hh = MLA_HEADS
    g = SSM_GROUPS
    hg = heads // g
    assert seq % CHUNK == 0 and d % (2 * LANES) == 0
    lp = seq + CHUNK
    t = bsz * seq
    alpha = (2.0 * depth) ** 0.25

    meta = jnp.broadcast_to(meta_tokens[None].astype(x.dtype), (bsz, N_META, d))
    hcat = jnp.concatenate([x, jnp.zeros((bsz, META_PAD, d), x.dtype), meta], axis=1)
    pos = jnp.concatenate([positions.astype(I32) + N_META, jnp.zeros((bsz, META_PAD), I32),
                           jnp.broadcast_to(jnp.arange(N_META, dtype=I32), (bsz, N_META))], axis=1)
    cos, sin = _rope_tables(pos)
    zr = jnp.zeros_like(cos)
    ones = jnp.ones((bsz, lp, MLA_NOPE), F32)
    zn = jnp.zeros((bsz, lp, MLA_NOPE), F32)
    rope_c = jnp.concatenate([ones, cos, cos, zr, zr], axis=-1)[:, :seq]
    rope_slo = jnp.concatenate([zn, -sin, zr, zr, zr], axis=-1)[:, :seq]
    rope_shi = jnp.concatenate([zn, zr, sin, zr, zr], axis=-1)[:, :seq]
    k_c = jnp.concatenate([cos, cos, zr, zr], axis=-1)
    k_s = jnp.concatenate([-sin, sin, zr, zr], axis=-1)

    h_f32, h_b16 = _layer_norm(hcat, ln_in_g, ln_in_b)
    tm_all = _pick(lp, (1408, 1152, 1024, 896, 768, 640, 512, 384, 256, 128))
    tm_seq = _pick(seq, (1024, 512, 256, 128))

    out = None
    for l in range(depth):
        offs = [0]
        for wdt in (d_inner, conv_dim, heads, q_rank, kv_rank, MLA_ROPE, 2 * d):
            offs.append(offs[-1] + wdt)
        wl = w_in[l]
        w_z = wl[:, offs[0]:offs[1]].astype(BF16)
        w_xbc = wl[:, offs[1]:offs[2]].astype(BF16)
        w_kr = wl[:, offs[5]:offs[6]]
        small_cols = q_rank + kv_rank + 2 * MLA_ROPE + heads
        small_pad = (-small_cols) % 256
        w_small = jnp.concatenate([wl[:, offs[3]:offs[4]], wl[:, offs[4]:offs[5]], w_kr, w_kr, wl[:, offs[2]:offs[3]],
                                   jnp.zeros((d, small_pad), F32)], axis=1).astype(BF16)
        w_g = wl[:, offs[6]:offs[7]].astype(BF16)

        z3 = _mm(h_b16, w_z, out_dtype=BF16, tm=tm_all, tn=_pick(d_inner, (512, 256, 128)), name="in_proj_z")
        xbc3 = _mm(h_b16, w_xbc, out_dtype=BF16, tm=tm_all, tn=_pick(conv_dim, (512, 256, 128)), name="in_proj_xbc")
        small = _mm(h_b16, w_small, out_dtype=F32, tm=tm_all, tn=256, name="in_proj_small")
        gates = _mm(h_b16, w_g, out_dtype=BF16, tm=tm_seq, tn=512, n_row_tiles=seq // tm_seq,
                    bias=b_gate[l], act="sigmoid", name="in_proj_gates")
        o = 0
        q_a = small[:, :, o:o + q_rank]; o += q_rank
        kv_a = small[:, :, o:o + kv_rank]; o += kv_rank
        kr2 = small[:, :, o:o + 2 * MLA_ROPE]; o += 2 * MLA_ROPE
        dt_raw = small[:, :, o:o + heads]

        dt_g = jnp.pad(dt_raw.reshape(bsz, lp, g, hg).transpose(0, 2, 1, 3), ((0, 0), (0, 0), (0, 0), (0, LANES - hg)))
        pad_h = lambda v: jnp.pad(v.astype(F32).reshape(g, 1, hg), ((0, 0), (0, 0), (0, LANES - hg)))
        y_ssm = _ssd(xbc3, z3, dt_g, conv_w[l].astype(F32), conv_b[l].reshape(1, conv_dim).astype(F32),
                     pad_h(dt_bias[l]), pad_h(a_log[l]),
                     jnp.repeat(d_skip[l].astype(F32), SSM_HEAD_DIM).reshape(1, d_inner),
                     ssm_norm_g[l].reshape(1, d_inner).astype(F32), d_inner, heads)

        qscale = (MLA_QK ** -0.5) * math.log2(math.e)
        head_pad = ((0, 0), (0, 0), (0, MLA_QKP - MLA_QK))
        w_q = jnp.pad(w_q_b[l].reshape(q_rank, hh, MLA_QK), head_pad).reshape(q_rank, hh * MLA_QKP).astype(BF16)
        q3 = _mm(q_a, w_q, out_dtype=BF16, tm=tm_seq, tn=4 * MLA_QKP, n_row_tiles=seq // tm_seq,
                 gain=q_a_norm_g[l], rope=(rope_c, rope_slo, rope_shi), scale=qscale, name="q_proj")
        wkv = w_kv_b[l].reshape(kv_rank, hh, MLA_NOPE + MLA_V)
        w_k = jnp.pad(wkv[:, :, :MLA_NOPE], ((0, 0), (0, 0), (0, MLA_QKP - MLA_NOPE)))
        w_k = w_k.reshape(kv_rank, hh * MLA_QKP).astype(BF16)
        w_vt = jnp.pad(wkv[:, :, MLA_NOPE:], ((0, 0), (0, 0), (0, MLA_VP - MLA_V)))
        w_vt = w_vt.reshape(kv_rank, hh * MLA_VP).T.astype(BF16)
        ones_rows = jnp.tile(jnp.concatenate([jnp.zeros((MLA_V,), F32), jnp.ones((MLA_VP - MLA_V,), F32)]), hh)
        k3 = _mm(kv_a, w_k, out_dtype=BF16, tm=tm_all, tn=4 * MLA_QKP, gain=kv_a_norm_g[l],
                 kadd=(kr2, k_c, k_s), name="k_proj")
        vt3 = _mm_t(kv_a, w_vt, kv_a_norm_g[l], ones_rows, out_dtype=BF16, tm=tm_all, tn=4 * MLA_VP,
                    name="v_proj")
        y_attn = _attention(q3, k3, vt3, seq)

        mixed = _mm_ktiled([y_ssm, y_attn], [w_ssm_proj[l].astype(BF16), w_attn_proj[l].astype(BF16)],
                           out_dtype=BF16, rows=seq, tm=tm_seq, tn=_pick(d, (1024, 512)), nk=4,
                           gates=gates, name="branch_proj")
        pre1 = _mm_ktiled([mixed], [w_out[l].astype(BF16)], out_dtype=F32, rows=seq, tm=tm_seq,
                          tn=_pick(d, (1024, 512)), nk=2, res=h_f32, res_scale=alpha, name="out_proj")
        h1, h1p = _layer_norm_pack(pre1.reshape(t, d), ln1_g[l], ln1_b[l])

        e_slot, pos_slot, w_slot, cnt = _router(h1, w_router[l], router_bias[l])
        counts = cnt[:, 0].astype(I32)
        padded = (counts + MOE_ROWS - 1) // MOE_ROWS * MOE_ROWS
        ends = jnp.cumsum(padded)
        starts = ends - padded
        onehot = e_slot[None] == jnp.arange(N_EXPERTS, dtype=I32)[:, None, None]
        dest = pos_slot + jnp.sum(jnp.where(onehot, starts[:, None, None], 0), axis=0)
        n_blocks = -(-(t * TOP_K) // MOE_ROWS) + N_EXPERTS
        blk0 = jnp.arange(n_blocks, dtype=I32) * MOE_ROWS
        block_e = jnp.minimum(jnp.sum((ends[None, :] <= blk0[:, None]).astype(I32), axis=1), N_EXPERTS - 1)
        n_used = (ends[-1] // MOE_ROWS).reshape(1)
        xs = _dispatch(h1p, dest, counts, starts, n_blocks * MOE_ROWS)
        y_sorted = _experts(xs, w_exp_gate[l].astype(BF16), w_exp_up[l].astype(BF16), w_exp_down[l].astype(BF16),
                            block_e, n_used)
        y_shared = _experts(h1p, w_sh_gate[l][None].astype(BF16), w_sh_up[l][None].astype(BF16),
                            w_sh_down[l][None].astype(BF16), jnp.zeros((t // MOE_ROWS,), I32),
                            jnp.full((1,), t // MOE_ROWS, I32))
        out = _combine(h1, y_shared, y_sorted, dest, w_slot.T, ln2_g[l], ln2_b[l], alpha)
        if l + 1 < depth:
            raise NotImplementedError("stacked layers need the meta rows carried through the channel mixer")
    return out.reshape(bsz, seq, d)
```

```python
import functools
import math

import jax
import jax.numpy as jnp
from jax import lax
from jax.experimental import pallas as pl
from jax.experimental.pallas import tpu as pltpu

F32 = jnp.float32
BF16 = jnp.bfloat16
U32 = jnp.uint32
I32 = jnp.int32

N_META = 16
CHUNK = 128
META_PAD = CHUNK - N_META
SSM_HEAD_DIM = 64
SSM_GROUPS = 8
SSM_STATE = 128
SSM_CONV = 4
MLA_HEADS = 64
MLA_NOPE = 128
MLA_ROPE = 64
MLA_V = 128
MLA_QK = MLA_NOPE + MLA_ROPE
ROPE_THETA = 10000.0
N_EXPERTS = 64
N_EXPERT_GROUPS = 8
TOPK_GROUPS = 4
TOP_K = 8
ROUTED_SCALE = 2.5
LN_EPS = 1e-5
RMS_EPS = 1e-6
NEG = -1e30
LANES = 128
MLA_QKP = 2 * LANES
MOE_ROWS = 256
MLA_VP = MLA_V + 16
VMEM_BIG = 56 * 1024 * 1024
VMEM_MID = 44 * 1024 * 1024


def _cparams(sem, vmem=VMEM_MID):
    return pltpu.CompilerParams(dimension_semantics=sem, vmem_limit_bytes=vmem)


def _pick(n, cands):
    for c in cands:
        if n % c == 0:
            return c
    raise ValueError(f"no tile for {n} in {cands}")


def _ln_body(x_ref, g_ref, b_ref, of_ref, ob_ref):
    x = x_ref[...]
    mu = jnp.mean(x, axis=-1, keepdims=True)
    xc = x - mu
    var = jnp.mean(xc * xc, axis=-1, keepdims=True)
    y = xc * lax.rsqrt(var + LN_EPS) * g_ref[...] + b_ref[...]
    of_ref[...] = y
    ob_ref[...] = y.astype(BF16)


def _layer_norm(x3, g, b):
    bsz, rows, d = x3.shape
    tm = _pick(rows, (256, 128))
    spec = pl.BlockSpec((None, tm, d), lambda bi, i: (bi, i, 0))
    vec = pl.BlockSpec((1, d), lambda bi, i: (0, 0))
    return pl.pallas_call(
        _ln_body,
        grid=(bsz, rows // tm),
        in_specs=[spec, vec, vec],
        out_specs=[spec, spec],
        out_shape=[jax.ShapeDtypeStruct(x3.shape, F32), jax.ShapeDtypeStruct(x3.shape, BF16)],
        compiler_params=_cparams(("parallel", "parallel")),
        name="layer_norm",
    )(x3, g.reshape(1, d), b.reshape(1, d))


def _pack_halves(y):
    n = y.shape[1] // 2
    lo = pltpu.bitcast(y[:, :n].astype(BF16).astype(F32), U32) >> 16
    hi = pltpu.bitcast(y[:, n:].astype(BF16).astype(F32), U32) & jnp.uint32(0xFFFF0000)
    return hi | lo


def _unpack_halves(w):
    lo = pltpu.bitcast(w << 16, F32)
    hi = pltpu.bitcast(w & jnp.uint32(0xFFFF0000), F32)
    return lo, hi


def _ln_pack_body(x_ref, g_ref, b_ref, of_ref, op_ref):
    x = x_ref[...]
    mu = jnp.mean(x, axis=-1, keepdims=True)
    xc = x - mu
    var = jnp.mean(xc * xc, axis=-1, keepdims=True)
    y = xc * lax.rsqrt(var + LN_EPS) * g_ref[...] + b_ref[...]
    of_ref[...] = y
    op_ref[...] = _pack_halves(y)


def _layer_norm_pack(x2, g, b):
    rows, d = x2.shape
    tm = _pick(rows, (256, 128))
    vec = pl.BlockSpec((1, d), lambda i: (0, 0))
    return pl.pallas_call(
        _ln_pack_body,
        grid=(rows // tm,),
        in_specs=[pl.BlockSpec((tm, d), lambda i: (i, 0)), vec, vec],
        out_specs=[pl.BlockSpec((tm, d), lambda i: (i, 0)), pl.BlockSpec((tm, d // 2), lambda i: (i, 0))],
        out_shape=[jax.ShapeDtypeStruct((rows, d), F32), jax.ShapeDtypeStruct((rows, d // 2), U32)],
        compiler_params=_cparams(("parallel",)),
        name="layer_norm_pack",
    )(x2, g.reshape(1, d), b.reshape(1, d))


def _mm_body(*refs, has_gain, has_bias, act, rope, kadd, scale, has_side):
    it = iter(refs)
    a_ref, w_ref = next(it), next(it)
    gain_ref = next(it) if has_gain else None
    bias_ref = next(it) if has_bias else None
    rope_refs = [next(it) for _ in range(3)] if rope else None
    kadd_refs = [next(it) for _ in range(3)] if kadd else None
    side_in = next(it) if has_side else None
    o_ref = next(it)
    if has_side:
        next(it)[...] = side_in[...].astype(BF16)
    a = a_ref[...]
    if has_gain:
        af = a.astype(F32)
        a = af * lax.rsqrt(jnp.mean(af * af, axis=-1, keepdims=True) + RMS_EPS) * gain_ref[...]
    r = jnp.dot(a.astype(BF16), w_ref[...].astype(BF16), preferred_element_type=F32)
    tn = r.shape[1]
    if has_bias:
        r = r + bias_ref[...]
    if act == "sigmoid":
        r = jax.nn.sigmoid(r)
    if rope:
        rep = tn // rope_refs[0].shape[1]
        c, slo, shi = (jnp.concatenate([t[...]] * rep, axis=1) for t in rope_refs)
        r = r * c + pltpu.roll(r, tn - MLA_ROPE // 2, 1) * slo + pltpu.roll(r, MLA_ROPE // 2, 1) * shi
    if kadd:
        kr_ref, kc, ks = kadd_refs
        kr = kr_ref[...]
        piece = jnp.concatenate([jnp.zeros_like(kr), kr * kc[...] + pltpu.roll(kr, MLA_ROPE // 2, 1) * ks[...]], axis=1)
        r = r + jnp.concatenate([piece] * (tn // piece.shape[1]), axis=1)
    if scale is not None:
        r = r * scale
    o_ref[...] = r.astype(o_ref.dtype)


def _side_specs(side, n_steps, lin):
    rows, cols = side.shape
    assert rows % (n_steps * 16) == 0
    blk = (rows // n_steps, cols)
    return pl.BlockSpec(blk, lambda *ids: (lin(*ids), 0)), jax.ShapeDtypeStruct(side.shape, BF16)


def _mm(a3, w, *, out_dtype, tm, tn, n_row_tiles=None, w_cols=None, gain=None, bias=None, act=None, rope=None,
        kadd=None, scale=None, side=None, name="mm"):
    bsz, rows, k = a3.shape
    col0, n = (0, w.shape[1]) if w_cols is None else w_cols
    ni = rows // tm if n_row_tiles is None else n_row_tiles
    nj = n // tn
    assert n % tn == 0 and col0 % tn == 0 and (n_row_tiles is not None or rows % tm == 0)
    in_specs = [pl.BlockSpec((None, tm, k), lambda bi, i, j: (bi, i, 0)),
                pl.BlockSpec((k, tn), lambda bi, i, j: (0, col0 // tn + j))]
    args = [a3, w]
    if gain is not None:
        in_specs.append(pl.BlockSpec((1, k), lambda bi, i, j: (0, 0)))
        args.append(gain.reshape(1, k).astype(F32))
    if bias is not None:
        in_specs.append(pl.BlockSpec((1, tn), lambda bi, i, j: (0, j)))
        args.append(bias.reshape(1, n).astype(F32))
    for t in (rope or ()) + (kadd or ()):
        in_specs.append(pl.BlockSpec((None, tm, t.shape[2]), lambda bi, i, j: (bi, i, 0)))
        args.append(t)
    out_specs = [pl.BlockSpec((None, tm, tn), lambda bi, i, j: (bi, i, j))]
    out_shape = [jax.ShapeDtypeStruct((bsz, ni * tm, n), out_dtype)]
    if side is not None:
        spec, shape = _side_specs(side, bsz * ni * nj, lambda bi, i, j: (bi * ni + i) * nj + j)
        in_specs.append(spec)
        args.append(side)
        out_specs.append(spec)
        out_shape.append(shape)
    body = functools.partial(_mm_body, has_gain=gain is not None, has_bias=bias is not None, act=act,
                             rope=rope is not None, kadd=kadd is not None, scale=scale, has_side=side is not None)
    outs = pl.pallas_call(
        body,
        grid=(bsz, ni, nj),
        in_specs=in_specs,
        out_specs=out_specs,
        out_shape=out_shape,
        compiler_params=_cparams(("parallel", "parallel", "arbitrary"), VMEM_BIG),
        name=name,
    )(*args)
    return outs[0] if side is None else outs


def _mm_t_body(a_ref, wt_ref, gain_ref, bias_ref, o_ref):
    af = a_ref[...].astype(F32)
    a = af * lax.rsqrt(jnp.mean(af * af, axis=-1, keepdims=True) + RMS_EPS) * gain_ref[...]
    r = lax.dot_general(wt_ref[...], a.astype(BF16), (((1,), (1,)), ((), ())), preferred_element_type=F32)
    o_ref[...] = (r + bias_ref[...]).astype(o_ref.dtype)


def _mm_t(a3, wt, gain, bias_col, *, out_dtype, tm, tn, name):
    bsz, rows, k = a3.shape
    n = wt.shape[0]
    return pl.pallas_call(
        _mm_t_body,
        grid=(bsz, rows // tm, n // tn),
        in_specs=[pl.BlockSpec((None, tm, k), lambda bi, i, j: (bi, i, 0)),
                  pl.BlockSpec((tn, k), lambda bi, i, j: (j, 0)),
                  pl.BlockSpec((1, k), lambda bi, i, j: (0, 0)),
                  pl.BlockSpec((tn, 1), lambda bi, i, j: (j, 0))],
        out_specs=pl.BlockSpec((None, tn, tm), lambda bi, i, j: (bi, j, i)),
        out_shape=jax.ShapeDtypeStruct((bsz, n, rows), out_dtype),
        compiler_params=_cparams(("parallel", "parallel", "arbitrary"), VMEM_BIG),
        name=name,
    )(a3, wt, gain.reshape(1, k).astype(F32), bias_col.reshape(n, 1).astype(F32))


def _mmk_body(*refs, n_pairs, has_gate, has_res, res_scale, has_side):
    it = iter(refs)
    a_refs = [next(it) for _ in range(n_pairs)]
    w_refs = [next(it) for _ in range(n_pairs)]
    g_refs = [next(it) for _ in range(n_pairs)] if has_gate else None
    res_ref = next(it) if has_res else None
    side_in = next(it) if has_side else None
    o_ref = next(it)
    if has_side:
        next(it)[...] = side_in[...].astype(BF16)
    acc_refs = [next(it) for _ in range(n_pairs)]
    kk = pl.program_id(3)

    @pl.when(kk == 0)
    def _():
        for acc in acc_refs:
            acc[...] = jnp.zeros_like(acc)

    for a_ref, w_ref, acc in zip(a_refs, w_refs, acc_refs):
        acc[...] += jnp.dot(a_ref[...], w_ref[...], preferred_element_type=F32)

    @pl.when(kk == pl.num_programs(3) - 1)
    def _():
        r = None
        for p, acc in enumerate(acc_refs):
            t = acc[...]
            if has_gate:
                t = t * g_refs[p][...].astype(F32)
            r = t if r is None else r + t
        if has_res:
            r = r + res_scale * res_ref[...]
        o_ref[...] = r.astype(o_ref.dtype)


def _mm_ktiled(a_list, w_list, *, out_dtype, rows, tm, tn, nk, gates=None, res=None, res_scale=1.0, side=None,
               name="mmk"):
    bsz = a_list[0].shape[0]
    n = w_list[0].shape[1]
    npairs = len(a_list)
    tks = [a.shape[2] // nk for a in a_list]
    assert all(a.shape[2] == tk * nk and tk % LANES == 0 for a, tk in zip(a_list, tks))
    o_spec = pl.BlockSpec((None, tm, tn), lambda bi, i, j, kk: (bi, i, j))
    in_specs = ([pl.BlockSpec((None, tm, tk), lambda bi, i, j, kk: (bi, i, kk)) for tk in tks]
                + [pl.BlockSpec((tk, tn), lambda bi, i, j, kk: (kk, j)) for tk in tks])
    args = list(a_list) + list(w_list)
    if gates is not None:
        for p in range(npairs):
            in_specs.append(pl.BlockSpec((None, tm, tn), lambda bi, i, j, kk, p=p: (bi, i, p * (n // tn) + j)))
            args.append(gates)
    if res is not None:
        in_specs.append(o_spec)
        args.append(res)
    ni, nj = rows // tm, n // tn
    out_specs = [o_spec]
    out_shape = [jax.ShapeDtypeStruct((bsz, rows, n), out_dtype)]
    if side is not None:
        spec, shape = _side_specs(side, bsz * ni * nj * nk, lambda bi, i, j, kk: ((bi * ni + i) * nj + j) * nk + kk)
        in_specs.append(spec)
        args.append(side)
        out_specs.append(spec)
        out_shape.append(shape)
    body = functools.partial(_mmk_body, n_pairs=npairs, has_gate=gates is not None, has_res=res is not None,
                             res_scale=res_scale, has_side=side is not None)
    outs = pl.pallas_call(
        body,
        grid=(bsz, ni, nj, nk),
        in_specs=in_specs,
        out_specs=out_specs,
        out_shape=out_shape,
        scratch_shapes=[pltpu.VMEM((tm, tn), F32) for _ in range(npairs)],
        compiler_params=_cparams(("parallel", "parallel", "parallel", "arbitrary"), VMEM_BIG),
        name=name,
    )(*args)
    return outs[0] if side is None else outs


def _silu(x):
    return x * jax.nn.sigmoid(x)


def _ssd_body(xs_ref, b_ref, c_ref, z_ref, dt_ref, wx_ref, wb_ref, wc_ref, bx_ref, bb_ref, bc_ref,
              dtb_ref, alog_ref, dskip_ref, ng_ref, e_ref, o_ref,
              state_ref, extx_ref, extb_ref, extc_ref, y_ref, *, hg):
    c = pl.program_id(2)
    q = CHUNK
    p = SSM_HEAD_DIM
    first = c == 0
    row = lax.broadcasted_iota(I32, (q, 1), 0)
    live = jnp.logical_or(jnp.logical_not(first), row >= META_PAD)

    @pl.when(first)
    def _():
        state_ref[...] = jnp.zeros_like(state_ref)
        extx_ref[0:8, :] = jnp.zeros((8, extx_ref.shape[1]), F32)
        extb_ref[0:8, :] = jnp.zeros((8, extb_ref.shape[1]), F32)
        extc_ref[0:8, :] = jnp.zeros((8, extc_ref.shape[1]), F32)

    def conv(ext_ref, u_ref, w_ref, bias_ref):
        @pl.when(first)
        def _():
            ext_ref[8:8 + q, :] = jnp.where(live, u_ref[...].astype(F32), 0.0)

        @pl.when(jnp.logical_not(first))
        def _():
            ext_ref[8:8 + q, :] = u_ref[...].astype(F32)

        acc = bias_ref[...]
        for k in range(SSM_CONV):
            acc = acc + ext_ref[pl.ds(8 - (SSM_CONV - 1) + k, q), :] * w_ref[k:k + 1, :]
        ext_ref[0:8, :] = ext_ref[q:q + 8, :]
        return _silu(acc)

    xs = conv(extx_ref, xs_ref, wx_ref, bx_ref)
    bm = conv(extb_ref, b_ref, wb_ref, bb_ref)
    cm = conv(extc_ref, c_ref, wc_ref, bc_ref)

    x = dt_ref[...] + dtb_ref[...]
    dt = jnp.maximum(x, 0.0) + jnp.log1p(jnp.exp(-jnp.abs(x)))
    dt = jnp.where(live, dt, 0.0)
    a = -jnp.exp(alog_ref[...])
    r_i = lax.broadcasted_iota(I32, (q, q), 0)
    c_i = lax.broadcasted_iota(I32, (q, q), 1)
    causal = r_i >= c_i
    hi = lax.Precision.HIGHEST
    a_cs = jnp.dot(causal.astype(F32), dt * a, precision=hi, preferred_element_type=F32)
    a_cs_t = a_cs.T
    ea = jnp.exp(a_cs)
    decay_end = jnp.exp(a_cs[q - 1:q, :] - a_cs)
    e = e_ref[...]

    def spread(v):
        v_hi = v.astype(BF16)
        v_lo = (v - v_hi.astype(F32)).astype(BF16)
        return jnp.dot(v_hi, e, preferred_element_type=F32) + jnp.dot(v_lo, e, preferred_element_type=F32)

    dt_x = spread(dt)
    ea_x = spread(ea)
    de_x = spread(decay_end)

    xdt = xs * dt_x
    xdt_b = xdt.astype(BF16)
    cm_b = cm.astype(BF16)
    bm_b = bm.astype(BF16)
    cb = lax.dot_general(cm_b, bm_b, (((1,), (1,)), ((), ())), preferred_element_type=F32)
    prev = state_ref[...]
    y_off = jnp.dot(cm_b, prev.astype(BF16), preferred_element_type=F32) * ea_x
    def decay_scores(j):
        seg = a_cs[:, j:j + 1] - a_cs_t[j:j + 1, :]
        return (jnp.exp(jnp.where(causal, seg, NEG)) * cb).astype(BF16)

    m_next = decay_scores(0)
    for j in range(hg):
        m = m_next
        if j + 1 < hg:
            m_next = decay_scores(j + 1)
        y_ref[:, j * p:(j + 1) * p] = jnp.dot(m, xdt_b[:, j * p:(j + 1) * p], preferred_element_type=F32)
    y = y_ref[...] + y_off + xs * dskip_ref[...]
    state_ref[...] = prev * ea_x[q - 1:q, :] + jnp.dot(bm.T.astype(BF16), (xdt * de_x).astype(BF16),
                                                        preferred_element_type=F32)
    yz = y * _silu(z_ref[...].astype(F32))
    yn = yz * lax.rsqrt(jnp.mean(yz * yz, axis=-1, keepdims=True) + RMS_EPS) * ng_ref[...]
    o_ref[...] = yn.astype(o_ref.dtype)


def _ssd(xbc3, z3, dt_g, conv_w, conv_b, dtb_g, alog_g, dskip_x, norm_g, d_inner, heads):
    bsz, rows, conv_dim = xbc3.shape
    g, n, p, q = SSM_GROUPS, SSM_STATE, SSM_HEAD_DIM, CHUNK
    hg = heads // g
    gw = hg * p
    nc = rows // q
    assert gw % LANES == 0 and d_inner % n == 0 and hg <= LANES
    boff = d_inner // n
    expand = (jnp.arange(LANES)[:, None] == (jnp.arange(gw) // p)[None, :]).astype(BF16)

    def chunk(c):
        return (c + nc - 1) % nc

    in_specs = [
        pl.BlockSpec((None, q, gw), lambda b, gi, c: (b, chunk(c), gi)),
        pl.BlockSpec((None, q, n), lambda b, gi, c: (b, chunk(c), boff + gi)),
        pl.BlockSpec((None, q, n), lambda b, gi, c: (b, chunk(c), boff + g + gi)),
        pl.BlockSpec((None, q, gw), lambda b, gi, c: (b, chunk(c), gi)),
        pl.BlockSpec((None, None, q, LANES), lambda b, gi, c: (b, gi, chunk(c), 0)),
        pl.BlockSpec((SSM_CONV, gw), lambda b, gi, c: (0, gi)),
        pl.BlockSpec((SSM_CONV, n), lambda b, gi, c: (0, boff + gi)),
        pl.BlockSpec((SSM_CONV, n), lambda b, gi, c: (0, boff + g + gi)),
        pl.BlockSpec((1, gw), lambda b, gi, c: (0, gi)),
        pl.BlockSpec((1, n), lambda b, gi, c: (0, boff + gi)),
        pl.BlockSpec((1, n), lambda b, gi, c: (0, boff + g + gi)),
        pl.BlockSpec((None, 1, LANES), lambda b, gi, c: (gi, 0, 0)),
        pl.BlockSpec((None, 1, LANES), lambda b, gi, c: (gi, 0, 0)),
        pl.BlockSpec((1, gw), lambda b, gi, c: (0, gi)),
        pl.BlockSpec((1, gw), lambda b, gi, c: (0, gi)),
        pl.BlockSpec((LANES, gw), lambda b, gi, c: (0, 0)),
    ]
    return pl.pallas_call(
        functools.partial(_ssd_body, hg=hg),
        grid=(bsz, g, nc),
        in_specs=in_specs,
        out_specs=pl.BlockSpec((None, q, gw), lambda b, gi, c: (b, chunk(c), gi)),
        out_shape=jax.ShapeDtypeStruct((bsz, rows, d_inner), BF16),
        scratch_shapes=[pltpu.VMEM((n, gw), F32), pltpu.VMEM((q + 8, gw), F32), pltpu.VMEM((q + 8, n), F32),
                        pltpu.VMEM((q + 8, n), F32), pltpu.VMEM((q, gw), F32)],
        compiler_params=_cparams(("parallel", "parallel", "arbitrary")),
        name="ssd",
    )(xbc3, xbc3, xbc3, z3, dt_g, conv_w, conv_w, conv_w, conv_b, conv_b, conv_b, dtb_g, alog_g,
      dskip_x, norm_g, expand)


def _attn_body(qi_ref, ki_ref, q_ref, km_ref, vm_ref, k_ref, v_ref, o_ref, m_ref, acc_ref, *, heads):
    step_id = pl.program_id(2)
    qi, ki = qi_ref[step_id], ki_ref[step_id]

    @pl.when(ki < 0)
    def _():
        m_ref[...] = jnp.full_like(m_ref, NEG)
        acc_ref[...] = jnp.zeros_like(acc_ref)

    def step(kr, vr, mask_fn):
        def scores(h):
            qh = q_ref[:, h * MLA_QKP:(h + 1) * MLA_QKP]
            kh = kr[:, h * MLA_QKP:(h + 1) * MLA_QKP]
            st = lax.dot_general(kh, qh, (((1,), (1,)), ((), ())), preferred_element_type=F32)
            return st if mask_fn is None else mask_fn(st)

        def probs(h, st):
            m_prev = m_ref[h]
            m_new = jnp.maximum(m_prev, jnp.max(st, axis=0, keepdims=True))
            m_ref[h] = m_new
            return jnp.exp2(m_prev - m_new), jnp.exp2((st - m_new).astype(BF16))

        def accumulate(h, alpha, pt):
            sl = slice(h * MLA_VP, (h + 1) * MLA_VP)
            acc_ref[sl, :] = acc_ref[sl, :] * alpha + jnp.dot(vr[sl, :], pt, preferred_element_type=F32)

        st_q, pr_q = {}, {}
        for stage in range(heads + 2):
            if stage < heads:
                st_q[stage] = scores(stage)
            if 0 <= stage - 1 < heads:
                pr_q[stage - 1] = probs(stage - 1, st_q.pop(stage - 1))
            if 0 <= stage - 2 < heads:
                accumulate(stage - 2, *pr_q.pop(stage - 2))

    def meta_mask(st):
        key = lax.broadcasted_iota(I32, st.shape, 0)
        return jnp.where(key >= META_PAD, st, NEG)

    def causal_mask(st):
        key = lax.broadcasted_iota(I32, st.shape, 0)
        qry = lax.broadcasted_iota(I32, st.shape, 1)
        return jnp.where(key <= qry, st, NEG)

    @pl.when(ki < 0)
    def _():
        step(km_ref, vm_ref, meta_mask)

    @pl.when(jnp.logical_and(ki >= 0, ki < qi))
    def _():
        step(k_ref, v_ref, None)

    @pl.when(ki == qi)
    def _():
        step(k_ref, v_ref, causal_mask)
        for h in range(heads):
            num = acc_ref[h * MLA_VP:h * MLA_VP + MLA_V, :]
            den = acc_ref[h * MLA_VP + MLA_V:h * MLA_VP + MLA_V + 1, :]
            o_ref[:, h * MLA_V:(h + 1) * MLA_V] = (num / den).T.astype(o_ref.dtype)


def _attention(q3, k3, vt3, seq):
    bsz = q3.shape[0]
    hp = 8
    t = _pick(seq, (512, 256, 128))
    nq = seq // t
    meta_blk = seq // CHUNK
    qw, vw = hp * MLA_QKP, hp * MLA_VP
    qi_tab = jnp.asarray([qi for qi in range(nq) for _ in range(qi + 2)], I32)
    ki_tab = jnp.asarray([ki for qi in range(nq) for ki in range(-1, qi + 1)], I32)

    def q_idx(b, h, s, qt, kt):
        return (b, qt[s], h)

    def k_idx(b, h, s, qt, kt):
        return (b, jnp.maximum(kt[s], 0), h)

    def vt_idx(b, h, s, qt, kt):
        return (b, h, jnp.maximum(kt[s], 0))

    grid_spec = pltpu.PrefetchScalarGridSpec(
        num_scalar_prefetch=2,
        grid=(bsz, MLA_HEADS // hp, int(qi_tab.shape[0])),
        in_specs=[
            pl.BlockSpec((None, t, qw), q_idx),
            pl.BlockSpec((None, CHUNK, qw), lambda b, h, s, qt, kt: (b, meta_blk, h)),
            pl.BlockSpec((None, vw, CHUNK), lambda b, h, s, qt, kt: (b, h, meta_blk)),
            pl.BlockSpec((None, t, qw), k_idx),
            pl.BlockSpec((None, vw, t), vt_idx),
        ],
        out_specs=pl.BlockSpec((None, t, hp * MLA_V), q_idx),
        scratch_shapes=[pltpu.VMEM((hp, 1, t), F32), pltpu.VMEM((vw, t), F32)],
    )
    return pl.pallas_call(
        functools.partial(_attn_body, heads=hp),
        grid_spec=grid_spec,
        out_shape=jax.ShapeDtypeStruct((bsz, seq, MLA_HEADS * MLA_V), BF16),
        compiler_params=_cparams(("parallel", "parallel", "arbitrary")),
        name="mla_attention",
    )(qi_tab, ki_tab, q3, k3, vt3, k3, vt3)


def _router_body(h_ref, wr_ref, rb_ref, e_ref, pos_ref, w_ref, cnt_ref, carry_ref):
    i = pl.program_id(0)
    ne, ng = N_EXPERTS, N_EXPERT_GROUPS
    gs = ne // ng
    tm = h_ref.shape[0]

    @pl.when(i == 0)
    def _():
        carry_ref[...] = jnp.zeros_like(carry_ref)

    logits = lax.dot_general(wr_ref[...], h_ref[...], (((1,), (1,)), ((), ())), precision=lax.Precision.HIGHEST,
                             preferred_element_type=F32)
    scores = jax.nn.sigmoid(logits)
    choice = scores + rb_ref[...]
    sub = lax.broadcasted_iota(I32, (gs, tm), 0)
    grp_rows = []
    for g in range(ng):
        blk = choice[g * gs:(g + 1) * gs, :]
        m1 = jnp.max(blk, axis=0, keepdims=True)
        first = jnp.min(jnp.where(blk == m1, sub, gs), axis=0, keepdims=True)
        m2 = jnp.max(jnp.where(sub == first, -jnp.inf, blk), axis=0, keepdims=True)
        grp_rows.append(m1 + m2)
    grp = jnp.concatenate(grp_rows, axis=0)
    grank = jnp.zeros((ng, tm), I32)
    gidx = lax.broadcasted_iota(I32, (ng, tm), 0)
    for g in range(ng):
        rowv = grp[g:g + 1, :]
        beats = jnp.logical_or(rowv > grp, jnp.logical_and(rowv == grp, gidx > g))
        grank = grank + beats.astype(I32)
    gsel = (grank < TOPK_GROUPS).astype(F32)
    esel = jnp.concatenate([jnp.broadcast_to(gsel[g:g + 1, :], (gs, tm)) for g in range(ng)], axis=0)
    masked = jnp.where(esel > 0.0, choice, -jnp.inf)
    eidx = lax.broadcasted_iota(I32, (ne, tm), 0)
    rank = jnp.zeros((ne, tm), I32)
    for e in range(ne):
        rowv = masked[e:e + 1, :]
        beats = jnp.logical_or(rowv > masked, jnp.logical_and(rowv == masked, eidx > e))
        rank = rank + beats.astype(I32)
    top = jnp.logical_and(rank < TOP_K, esel > 0.0)
    topf = top.astype(F32)
    wsel = jnp.where(top, scores, 0.0)
    wn = wsel / jnp.sum(wsel, axis=0, keepdims=True) * ROUTED_SCALE
    r_i = lax.broadcasted_iota(I32, (tm, tm), 0)
    c_i = lax.broadcasted_iota(I32, (tm, tm), 1)
    before = (r_i < c_i).astype(BF16)
    pos = jnp.dot(topf.astype(BF16), before, preferred_element_type=F32) + carry_ref[:, :1]
    carry_ref[...] = carry_ref[...] + jnp.sum(topf, axis=1, keepdims=True)
    cnt_ref[...] = carry_ref[...]
    e_rows, p_rows, w_rows = [], [], []
    for k in range(TOP_K):
        hit = rank == k
        e_rows.append(jnp.sum(jnp.where(hit, eidx, 0), axis=0, keepdims=True))
        p_rows.append(jnp.sum(jnp.where(hit, pos, 0.0), axis=0, keepdims=True))
        w_rows.append(jnp.sum(jnp.where(hit, wn, 0.0), axis=0, keepdims=True))
    e_ref[...] = jnp.concatenate(e_rows, axis=0)
    pos_ref[...] = jnp.concatenate(p_rows, axis=0).astype(I32)
    w_ref[...] = jnp.concatenate(w_rows, axis=0)


def _router(h2, w_router, router_bias):
    t, d = h2.shape
    tm = _pick(t, (256, 128))
    slot = pl.BlockSpec((TOP_K, tm), lambda i: (0, i))
    rb = jnp.broadcast_to(router_bias.astype(F32)[:, None], (N_EXPERTS, tm))
    return pl.pallas_call(
        _router_body,
        grid=(t // tm,),
        in_specs=[pl.BlockSpec((tm, d), lambda i: (i, 0)), pl.BlockSpec((N_EXPERTS, d), lambda i: (0, 0)),
                  pl.BlockSpec((N_EXPERTS, tm), lambda i: (0, 0))],
        out_specs=[slot, slot, slot, pl.BlockSpec((N_EXPERTS, LANES), lambda i: (0, 0))],
        out_shape=[jax.ShapeDtypeStruct((TOP_K, t), I32), jax.ShapeDtypeStruct((TOP_K, t), I32),
                   jax.ShapeDtypeStruct((TOP_K, t), F32), jax.ShapeDtypeStruct((N_EXPERTS, LANES), F32)],
        scratch_shapes=[pltpu.VMEM((N_EXPERTS, LANES), F32)],
        compiler_params=_cparams(("arbitrary",)),
        name="moe_router",
    )(h2, w_router.T.astype(F32), rb)


def _load_slots(dest_hbm, dest_ref, sem):
    cp = pltpu.make_async_copy(dest_hbm.at[pl.program_id(0)], dest_ref, sem)
    cp.start()
    cp.wait()


def _dispatch_body(cnt_ref, start_ref, dest_hbm, x_ref, xs_ref, dest_ref, zero_ref, sem_ref, *, block_rows):
    i = pl.program_id(0)
    tm = x_ref.shape[0]
    _load_slots(dest_hbm, dest_ref, sem_ref.at[2])

    def row_copy(tok, k):
        return pltpu.make_async_copy(x_ref.at[pl.ds(tok, 1), :], xs_ref.at[pl.ds(dest_ref[k, tok], 1), :], sem_ref.at[0])

    def issue(tok, carry):
        for k in range(TOP_K):
            row_copy(tok, k).start()
        return carry

    lax.fori_loop(0, tm, issue, 0)

    @pl.when(i == 0)
    def _():
        zero_ref[...] = jnp.zeros_like(zero_ref)

        def fill(e, carry):
            cnt = cnt_ref[e]
            padded = (cnt + block_rows - 1) // block_rows * block_rows
            base = start_ref[e] + cnt

            def zcopy(r):
                return pltpu.make_async_copy(zero_ref.at[pl.ds(0, 1), :], xs_ref.at[pl.ds(base + r, 1), :], sem_ref.at[1])

            def zstart(r, c2):
                zcopy(r).start()
                return c2

            def zwait(r, c2):
                zcopy(r).wait()
                return c2

            lax.fori_loop(0, padded - cnt, zstart, 0)
            lax.fori_loop(0, padded - cnt, zwait, 0)
            return carry

        lax.fori_loop(0, N_EXPERTS, fill, 0)

    def drain(tok, carry):
        for k in range(TOP_K):
            row_copy(tok, k).wait()
        return carry

    lax.fori_loop(0, tm, drain, 0)


def _slot_tiles(dest, tm):
    k, t = dest.shape
    return dest.reshape(k, t // tm, tm).transpose(1, 0, 2)


def _dispatch(hp2, dest, counts, starts, n_rows):
    t, w = hp2.shape
    tm = _pick(t, (256, 128))
    grid_spec = pltpu.PrefetchScalarGridSpec(
        num_scalar_prefetch=2,
        grid=(t // tm,),
        in_specs=[pl.BlockSpec(memory_space=pl.ANY),
                  pl.BlockSpec((tm, w), lambda i, c, s: (i, 0))],
        out_specs=pl.BlockSpec(memory_space=pl.ANY),
        scratch_shapes=[pltpu.SMEM((TOP_K, tm), I32), pltpu.VMEM((8, w), U32), pltpu.SemaphoreType.DMA((3,))],
    )
    return pl.pallas_call(
        functools.partial(_dispatch_body, block_rows=MOE_ROWS),
        grid_spec=grid_spec,
        out_shape=jax.ShapeDtypeStruct((n_rows, w), U32),
        compiler_params=pltpu.CompilerParams(dimension_semantics=("arbitrary",)),
        name="moe_dispatch",
    )(counts, starts, _slot_tiles(dest, tm), hp2)


def _expert_body(be_ref, nu_ref, x_ref, wg_ref, wu_ref, wd_ref, o_ref):
    j = pl.program_id(0)

    @pl.when(j < nu_ref[0])
    def _():
        half = x_ref.shape[1]
        lo, hi = _unpack_halves(x_ref[...])
        lo, hi = lo.astype(BF16), hi.astype(BF16)

        def proj(w_ref):
            return (jnp.dot(lo, w_ref[:half, :], preferred_element_type=F32)
                    + jnp.dot(hi, w_ref[half:, :], preferred_element_type=F32))

        hmid = (_silu(proj(wg_ref)) * proj(wu_ref)).astype(BF16)
        o_ref[...] = _pack_halves(jnp.dot(hmid, wd_ref[...], preferred_element_type=F32))


def _experts(xs, wg, wu, wd, block_e, n_used):
    rows, w = xs.shape
    _, d, ff = wg.shape
    bm = MOE_ROWS
    nb = rows // bm

    def row_idx(j, be, nu):
        return (jnp.minimum(j, nu[0] - 1), 0)

    def w_idx(j, be, nu):
        return (be[jnp.minimum(j, nu[0] - 1)], 0, 0)

    grid_spec = pltpu.PrefetchScalarGridSpec(
        num_scalar_prefetch=2,
        grid=(nb,),
        in_specs=[pl.BlockSpec((bm, w), row_idx),
                  pl.BlockSpec((None, d, ff), w_idx), pl.BlockSpec((None, d, ff), w_idx),
                  pl.BlockSpec((None, ff, d), w_idx)],
        out_specs=pl.BlockSpec((bm, w), row_idx),
    )
    return pl.pallas_call(
        _expert_body,
        grid_spec=grid_spec,
        out_shape=jax.ShapeDtypeStruct((rows, w), U32),
        compiler_params=_cparams(("arbitrary",), 60 * 1024 * 1024),
        name="moe_experts",
    )(block_e, n_used, xs, wg, wu, wd)


def _combine_body(dest_hbm, wt_ref, h_ref, ysh_ref, g_ref, b_ref, y_ref, o_ref, dest_ref, gbuf_ref, sem_ref, *, alpha):
    tm = h_ref.shape[0]
    _load_slots(dest_hbm, dest_ref, sem_ref.at[1])

    def row_copy(tok, k):
        return pltpu.make_async_copy(y_ref.at[pl.ds(dest_ref[k, tok], 1), :], gbuf_ref.at[k, pl.ds(tok, 1), :],
                                     sem_ref.at[0])

    def issue(tok, carry):
        for k in range(TOP_K):
            row_copy(tok, k).start()
        return carry

    def drain(tok, carry):
        for k in range(TOP_K):
            row_copy(tok, k).wait()
        return carry

    lax.fori_loop(0, tm, issue, 0)
    lax.fori_loop(0, tm, drain, 0)
    lo, hi = _unpack_halves(ysh_ref[...])
    wt = wt_ref[...]
    for k in range(TOP_K):
        glo, ghi = _unpack_halves(gbuf_ref[k])
        wk = wt[:, k:k + 1]
        lo = lo + glo * wk
        hi = hi + ghi * wk
    x = alpha * h_ref[...] + jnp.concatenate([lo, hi], axis=1)
    mu = jnp.mean(x, axis=-1, keepdims=True)
    xc = x - mu
    var = jnp.mean(xc * xc, axis=-1, keepdims=True)
    o_ref[...] = xc * lax.rsqrt(var + LN_EPS) * g_ref[...] + b_ref[...]


def _combine(h2, ysh, y_sorted, dest, wt, g, b, alpha):
    t, d = h2.shape
    w = d // 2
    tm = _pick(t, (128,))
    vec = pl.BlockSpec((1, d), lambda i: (0, 0))
    return pl.pallas_call(
        functools.partial(_combine_body, alpha=alpha),
        grid=(t // tm,),
        in_specs=[pl.BlockSpec(memory_space=pl.ANY),
                  pl.BlockSpec((tm, TOP_K), lambda i: (i, 0)),
                  pl.BlockSpec((tm, d), lambda i: (i, 0)),
                  pl.BlockSpec((tm, w), lambda i: (i, 0)),
                  vec, vec,
                  pl.BlockSpec(memory_space=pl.ANY)],
        out_specs=pl.BlockSpec((tm, d), lambda i: (i, 0)),
        out_shape=jax.ShapeDtypeStruct((t, d), F32),
        scratch_shapes=[pltpu.SMEM((TOP_K, tm), I32), pltpu.VMEM((TOP_K, tm, w), U32), pltpu.SemaphoreType.DMA((2,))],
        compiler_params=_cparams(("arbitrary",)),
        name="moe_combine",
    )(_slot_tiles(dest, tm), wt, h2, ysh, g.reshape(1, d), b.reshape(1, d), y_sorted)


def _rope_tables(pos):
    inv_freq = ROPE_THETA ** (-jnp.arange(0, MLA_ROPE, 2, dtype=F32) / MLA_ROPE)
    ang = pos.astype(F32)[..., None] * inv_freq
    return jnp.cos(ang), jnp.sin(ang)


def kernel(x, positions, meta_tokens, ln_in_g, ln_in_b, w_in, b_gate, conv_w, conv_b, dt_bias, a_log, d_skip,
           ssm_norm_g, w_ssm_proj, q_a_norm_g, w_q_b, kv_a_norm_g, w_kv_b, w_attn_proj, w_out, ln1_g, ln1_b,
           w_router, router_bias, w_exp_gate, w_exp_up, w_exp_down, w_sh_gate, w_sh_up, w_sh_down, ln2_g, ln2_b):
    bsz, seq, d = x.shape
    depth = w_in.shape[0]
    heads = dt_bias.shape[-1]
    d_inner = w_ssm_proj.shape[1]
    conv_dim = conv_w.shape[-1]
    q_rank = w_q_b.shape[1]
    kv_rank = w_kv_b.shape[1]
    hh = MLA_HEADS
    g = SSM_GROUPS
    hg = heads // g
    assert seq % CHUNK == 0 and d % (2 * LANES) == 0
    lp = seq + CHUNK
    t = bsz * seq
    alpha = (2.0 * depth) ** 0.25

    meta = jnp.broadcast_to(meta_tokens[None].astype(x.dtype), (bsz, N_META, d))
    hcat = jnp.concatenate([x, jnp.zeros((bsz, META_PAD, d), x.dtype), meta], axis=1)
    pos = jnp.concatenate([positions.astype(I32) + N_META, jnp.zeros((bsz, META_PAD), I32),
                           jnp.broadcast_to(jnp.arange(N_META, dtype=I32), (bsz, N_META))], axis=1)
    cos, sin = _rope_tables(pos)
    zr = jnp.zeros_like(cos)
    ones = jnp.ones((bsz, lp, MLA_NOPE), F32)
    zn = jnp.zeros((bsz, lp, MLA_NOPE), F32)
    rope_c = jnp.concatenate([ones, cos, cos, zr, zr], axis=-1)[:, :seq]
    rope_slo = jnp.concatenate([zn, -sin, zr, zr, zr], axis=-1)[:, :seq]
    rope_shi = jnp.concatenate([zn, zr, sin, zr, zr], axis=-1)[:, :seq]
    k_c = jnp.concatenate([cos, cos, zr, zr], axis=-1)
    k_s = jnp.concatenate([-sin, sin, zr, zr], axis=-1)

    h_f32, h_b16 = _layer_norm(hcat, ln_in_g, ln_in_b)
    tm_all = _pick(lp, (1408, 1152, 1024, 896, 768, 640, 512, 384, 256, 128))
    tm_seq = _pick(seq, (1024, 512, 256, 128))

    out = None
    for l in range(depth):
        offs = [0]
        for wdt in (d_inner, conv_dim, heads, q_rank, kv_rank, MLA_ROPE, 2 * d):
            offs.append(offs[-1] + wdt)
        wl = w_in[l]
        w_kr = wl[:, offs[5]:offs[6]]
        small_cols = q_rank + kv_rank + 2 * MLA_ROPE + heads
        small_pad = (-small_cols) % 256
        w_small = jnp.concatenate([wl[:, offs[3]:offs[4]], wl[:, offs[4]:offs[5]], w_kr, w_kr, wl[:, offs[2]:offs[3]],
                                   jnp.zeros((d, small_pad), F32)], axis=1).astype(BF16)
        w_g = wl[:, offs[6]:offs[7]].astype(BF16)

        n_exp, _, ff = w_exp_gate[l].shape
        z3 = _mm(h_b16, wl, w_cols=(offs[0], d_inner), out_dtype=BF16, tm=tm_all, tn=512, name="in_proj_z")
        xbc3 = _mm(h_b16, wl, w_cols=(offs[1], conv_dim), out_dtype=BF16, tm=tm_all, tn=512, name="in_proj_xbc")
        small = _mm(h_b16, w_small, out_dtype=F32, tm=tm_all, tn=256, name="in_proj_small")
        gates, wg_b = _mm(h_b16, w_g, out_dtype=BF16, tm=tm_seq, tn=512, n_row_tiles=seq // tm_seq,
                          bias=b_gate[l], act="sigmoid", side=w_exp_gate[l].reshape(n_exp * d, ff),
                          name="in_proj_gates")
        o = 0
        q_a = small[:, :, o:o + q_rank]; o += q_rank
        kv_a = small[:, :, o:o + kv_rank]; o += kv_rank
        kr2 = small[:, :, o:o + 2 * MLA_ROPE]; o += 2 * MLA_ROPE
        dt_raw = small[:, :, o:o + heads]

        dt_g = jnp.pad(dt_raw.reshape(bsz, lp, g, hg).transpose(0, 2, 1, 3), ((0, 0), (0, 0), (0, 0), (0, LANES - hg)))
        pad_h = lambda v: jnp.pad(v.astype(F32).reshape(g, 1, hg), ((0, 0), (0, 0), (0, LANES - hg)))
        y_ssm = _ssd(xbc3, z3, dt_g, conv_w[l].astype(F32), conv_b[l].reshape(1, conv_dim).astype(F32),
                     pad_h(dt_bias[l]), pad_h(a_log[l]),
                     jnp.repeat(d_skip[l].astype(F32), SSM_HEAD_DIM).reshape(1, d_inner),
                     ssm_norm_g[l].reshape(1, d_inner).astype(F32), d_inner, heads)

        qscale = (MLA_QK ** -0.5) * math.log2(math.e)
        head_pad = ((0, 0), (0, 0), (0, MLA_QKP - MLA_QK))
        w_q = jnp.pad(w_q_b[l].reshape(q_rank, hh, MLA_QK), head_pad).reshape(q_rank, hh * MLA_QKP).astype(BF16)
        q3, wd_b = _mm(q_a, w_q, out_dtype=BF16, tm=tm_seq, tn=4 * MLA_QKP, n_row_tiles=seq // tm_seq,
                       gain=q_a_norm_g[l], rope=(rope_c, rope_slo, rope_shi), scale=qscale,
                       side=w_exp_down[l].reshape(n_exp * ff, d), name="q_proj")
        wkv = w_kv_b[l].reshape(kv_rank, hh, MLA_NOPE + MLA_V)
        w_k = jnp.pad(wkv[:, :, :MLA_NOPE], ((0, 0), (0, 0), (0, MLA_QKP - MLA_NOPE)))
        w_k = w_k.reshape(kv_rank, hh * MLA_QKP).astype(BF16)
        w_vt = jnp.pad(wkv[:, :, MLA_NOPE:], ((0, 0), (0, 0), (0, MLA_VP - MLA_V)))
        w_vt = w_vt.reshape(kv_rank, hh * MLA_VP).T.astype(BF16)
        ones_rows = jnp.tile(jnp.concatenate([jnp.zeros((MLA_V,), F32), jnp.ones((MLA_VP - MLA_V,), F32)]), hh)
        k3 = _mm(kv_a, w_k, out_dtype=BF16, tm=tm_all, tn=4 * MLA_QKP, gain=kv_a_norm_g[l],
                 kadd=(kr2, k_c, k_s), name="k_proj")
        vt3 = _mm_t(kv_a, w_vt, kv_a_norm_g[l], ones_rows, out_dtype=BF16, tm=tm_all, tn=4 * MLA_VP,
                    name="v_proj")
        y_attn = _attention(q3, k3, vt3, seq)

        mixed = _mm_ktiled([y_ssm, y_attn], [w_ssm_proj[l].astype(BF16), w_attn_proj[l].astype(BF16)],
                           out_dtype=BF16, rows=seq, tm=tm_seq, tn=_pick(d, (1024, 512)), nk=4,
                           gates=gates, name="branch_proj")
        pre1, wu_b = _mm_ktiled([mixed], [w_out[l].astype(BF16)], out_dtype=F32, rows=seq, tm=tm_seq,
                                tn=_pick(d, (1024, 512)), nk=4, res=h_f32, res_scale=alpha,
                                side=w_exp_up[l].reshape(n_exp * d, ff), name="out_proj")
        h1, h1p = _layer_norm_pack(pre1.reshape(t, d), ln1_g[l], ln1_b[l])

        e_slot, pos_slot, w_slot, cnt = _router(h1, w_router[l], router_bias[l])
        counts = cnt[:, 0].astype(I32)
        padded = (counts + MOE_ROWS - 1) // MOE_ROWS * MOE_ROWS
        ends = jnp.cumsum(padded)
        starts = ends - padded
        onehot = e_slot[None] == jnp.arange(N_EXPERTS, dtype=I32)[:, None, None]
        dest = pos_slot + jnp.sum(jnp.where(onehot, starts[:, None, None], 0), axis=0)
        n_blocks = -(-(t * TOP_K) // MOE_ROWS) + N_EXPERTS
        blk0 = jnp.arange(n_blocks, dtype=I32) * MOE_ROWS
        block_e = jnp.minimum(jnp.sum((ends[None, :] <= blk0[:, None]).astype(I32), axis=1), N_EXPERTS - 1)
        n_used = (ends[-1] // MOE_ROWS).reshape(1)
        xs = _dispatch(h1p, dest, counts, starts, n_blocks * MOE_ROWS)
        y_sorted = _experts(xs, wg_b.reshape(n_exp, d, ff), wu_b.reshape(n_exp, d, ff), wd_b.reshape(n_exp, ff, d),
                            block_e, n_used)
        y_shared = _experts(h1p, w_sh_gate[l][None].astype(BF16), w_sh_up[l][None].astype(BF16),
                            w_sh_down[l][None].astype(BF16), jnp.zeros((t // MOE_ROWS,), I32),
                            jnp.full((1,), t // MOE_ROWS, I32))
        out = _combine(h1, y_shared, y_sorted, dest, w_slot.T, ln2_g[l], ln2_b[l], alpha)
        if l + 1 < depth:
            raise NotImplementedError("stacked layers need the meta rows carried through the channel mixer")
    return out.reshape(bsz, seq, d)
```

```python
import functools
import math

import jax
import jax.numpy as jnp
from jax import lax
from jax.experimental import pallas as pl
from jax.experimental.pallas import tpu as pltpu

F32 = jnp.float32
BF16 = jnp.bfloat16
U32 = jnp.uint32
I32 = jnp.int32

N_META = 16
CHUNK = 128
META_PAD = CHUNK - N_META
SSM_HEAD_DIM = 64
SSM_GROUPS = 8
SSM_STATE = 128
SSM_CONV = 4
MLA_HEADS = 64
MLA_NOPE = 128
MLA_ROPE = 64
MLA_V = 128
MLA_QK = MLA_NOPE + MLA_ROPE
ROPE_THETA = 10000.0
N_EXPERTS = 64
N_EXPERT_GROUPS = 8
TOPK_GROUPS = 4
TOP_K = 8
ROUTED_SCALE = 2.5
LN_EPS = 1e-5
RMS_EPS = 1e-6
NEG = -1e30
LANES = 128
MLA_QKP = 2 * LANES
MOE_ROWS = 256
MLA_VP = MLA_V + 16
VMEM_BIG = 56 * 1024 * 1024
VMEM_MID = 44 * 1024 * 1024


def _cparams(sem, vmem=VMEM_MID):
    return pltpu.CompilerParams(dimension_semantics=sem, vmem_limit_bytes=vmem)


def _pick(n, cands):
    for c in cands:
        if n % c == 0:
            return c
    raise ValueError(f"no tile for {n} in {cands}")


def _ln_body(x_ref, g_ref, b_ref, of_ref, ob_ref):
    x = x_ref[...]
    mu = jnp.mean(x, axis=-1, keepdims=True)
    xc = x - mu
    var = jnp.mean(xc * xc, axis=-1, keepdims=True)
    y = xc * lax.rsqrt(var + LN_EPS) * g_ref[...] + b_ref[...]
    of_ref[...] = y
    ob_ref[...] = y.astype(BF16)


def _layer_norm(x3, g, b):
    bsz, rows, d = x3.shape
    tm = _pick(rows, (256, 128))
    spec = pl.BlockSpec((None, tm, d), lambda bi, i: (bi, i, 0))
    vec = pl.BlockSpec((1, d), lambda bi, i: (0, 0))
    return pl.pallas_call(
        _ln_body,
        grid=(bsz, rows // tm),
        in_specs=[spec, vec, vec],
        out_specs=[spec, spec],
        out_shape=[jax.ShapeDtypeStruct(x3.shape, F32), jax.ShapeDtypeStruct(x3.shape, BF16)],
        compiler_params=_cparams(("parallel", "parallel")),
        name="layer_norm",
    )(x3, g.reshape(1, d), b.reshape(1, d))


def _pack_halves(y):
    n = y.shape[1] // 2
    lo = pltpu.bitcast(y[:, :n].astype(BF16).astype(F32), U32) >> 16
    hi = pltpu.bitcast(y[:, n:].astype(BF16).astype(F32), U32) & jnp.uint32(0xFFFF0000)
    return hi | lo


def _unpack_halves(w):
    lo = pltpu.bitcast(w << 16, F32)
    hi = pltpu.bitcast(w & jnp.uint32(0xFFFF0000), F32)
    return lo, hi


def _ln_pack_body(x_ref, g_ref, b_ref, of_ref, op_ref):
    x = x_ref[...]
    mu = jnp.mean(x, axis=-1, keepdims=True)
    xc = x - mu
    var = jnp.mean(xc * xc, axis=-1, keepdims=True)
    y = xc * lax.rsqrt(var + LN_EPS) * g_ref[...] + b_ref[...]
    of_ref[...] = y
    op_ref[...] = _pack_halves(y)


def _layer_norm_pack(x2, g, b):
    rows, d = x2.shape
    tm = _pick(rows, (256, 128))
    vec = pl.BlockSpec((1, d), lambda i: (0, 0))
    return pl.pallas_call(
        _ln_pack_body,
        grid=(rows // tm,),
        in_specs=[pl.BlockSpec((tm, d), lambda i: (i, 0)), vec, vec],
        out_specs=[pl.BlockSpec((tm, d), lambda i: (i, 0)), pl.BlockSpec((tm, d // 2), lambda i: (i, 0))],
        out_shape=[jax.ShapeDtypeStruct((rows, d), F32), jax.ShapeDtypeStruct((rows, d // 2), U32)],
        compiler_params=_cparams(("parallel",)),
        name="layer_norm_pack",
    )(x2, g.reshape(1, d), b.reshape(1, d))


def _mm_body(*refs, has_gain, has_bias, act, rope, kadd, scale, has_side, w_rows):
    it = iter(refs)
    a_ref, w_ref = next(it), next(it)
    gain_ref = next(it) if has_gain else None
    bias_ref = next(it) if has_bias else None
    rope_refs = [next(it) for _ in range(3)] if rope else None
    kadd_refs = [next(it) for _ in range(3)] if kadd else None
    side_in = next(it) if has_side else None
    o_ref = next(it)
    if has_side:
        next(it)[...] = side_in[...].astype(BF16)
    a = a_ref[...]
    if has_gain:
        af = a.astype(F32)
        a = af * lax.rsqrt(jnp.mean(af * af, axis=-1, keepdims=True) + RMS_EPS) * gain_ref[...]
    if w_rows:
        r = lax.dot_general(a.astype(BF16), w_ref[...].astype(BF16), (((1,), (1,)), ((), ())),
                            preferred_element_type=F32)
    else:
        r = jnp.dot(a.astype(BF16), w_ref[...].astype(BF16), preferred_element_type=F32)
    tn = r.shape[1]
    if has_bias:
        r = r + bias_ref[...]
    if act == "sigmoid":
        r = jax.nn.sigmoid(r)
    if rope:
        rep = tn // rope_refs[0].shape[1]
        c, slo, shi = (jnp.concatenate([t[...]] * rep, axis=1) for t in rope_refs)
        r = r * c + pltpu.roll(r, tn - MLA_ROPE // 2, 1) * slo + pltpu.roll(r, MLA_ROPE // 2, 1) * shi
    if kadd:
        kr_ref, kc, ks = kadd_refs
        kr = kr_ref[...]
        piece = jnp.concatenate([jnp.zeros_like(kr), kr * kc[...] + pltpu.roll(kr, MLA_ROPE // 2, 1) * ks[...]], axis=1)
        r = r + jnp.concatenate([piece] * (tn // piece.shape[1]), axis=1)
    if scale is not None:
        r = r * scale
    o_ref[...] = r.astype(o_ref.dtype)


def _side_specs(side, n_steps, lin):
    rows, cols = side.shape
    nb = max(c for c in range(1, n_steps + 1) if rows % c == 0 and (rows // c) % 16 == 0)
    spec = pl.BlockSpec((rows // nb, cols), lambda *ids: (jnp.minimum(lin(*ids), nb - 1), 0))
    return spec, jax.ShapeDtypeStruct(side.shape, BF16)


def _mm(a3, w, *, out_dtype, tm, tn, n_row_tiles=None, w_rows=None, gain=None, bias=None, act=None, rope=None,
        kadd=None, scale=None, side=None, name="mm"):
    bsz, rows, k = a3.shape
    row0, n = (0, w.shape[1]) if w_rows is None else w_rows
    ni = rows // tm if n_row_tiles is None else n_row_tiles
    nj = n // tn
    assert n % tn == 0 and row0 % tn == 0 and (n_row_tiles is not None or rows % tm == 0)
    if w_rows is None:
        w_spec = pl.BlockSpec((k, tn), lambda bi, i, j: (0, j))
    else:
        w_spec = pl.BlockSpec((tn, k), lambda bi, i, j: (row0 // tn + j, 0))
    in_specs = [pl.BlockSpec((None, tm, k), lambda bi, i, j: (bi, i, 0)), w_spec]
    args = [a3, w]
    if gain is not None:
        in_specs.append(pl.BlockSpec((1, k), lambda bi, i, j: (0, 0)))
        args.append(gain.reshape(1, k).astype(F32))
    if bias is not None:
        in_specs.append(pl.BlockSpec((1, tn), lambda bi, i, j: (0, j)))
        args.append(bias.reshape(1, n).astype(F32))
    for t in (rope or ()) + (kadd or ()):
        in_specs.append(pl.BlockSpec((None, tm, t.shape[2]), lambda bi, i, j: (bi, i, 0)))
        args.append(t)
    out_specs = [pl.BlockSpec((None, tm, tn), lambda bi, i, j: (bi, i, j))]
    out_shape = [jax.ShapeDtypeStruct((bsz, ni * tm, n), out_dtype)]
    if side is not None:
        spec, shape = _side_specs(side, bsz * ni * nj, lambda bi, i, j: (bi * ni + i) * nj + j)
        in_specs.append(spec)
        args.append(side)
        out_specs.append(spec)
        out_shape.append(shape)
    body = functools.partial(_mm_body, has_gain=gain is not None, has_bias=bias is not None, act=act,
                             rope=rope is not None, kadd=kadd is not None, scale=scale, has_side=side is not None,
                             w_rows=w_rows is not None)
    outs = pl.pallas_call(
        body,
        grid=(bsz, ni, nj),
        in_specs=in_specs,
        out_specs=out_specs,
        out_shape=out_shape,
        compiler_params=_cparams(("parallel", "parallel", "arbitrary"), VMEM_BIG),
        name=name,
    )(*args)
    return outs[0] if side is None else outs


def _mm_t_body(a_ref, wt_ref, gain_ref, bias_ref, o_ref):
    af = a_ref[...].astype(F32)
    a = af * lax.rsqrt(jnp.mean(af * af, axis=-1, keepdims=True) + RMS_EPS) * gain_ref[...]
    r = lax.dot_general(wt_ref[...], a.astype(BF16), (((1,), (1,)), ((), ())), preferred_element_type=F32)
    o_ref[...] = (r + bias_ref[...]).astype(o_ref.dtype)


def _mm_t(a3, wt, gain, bias_col, *, out_dtype, tm, tn, name):
    bsz, rows, k = a3.shape
    n = wt.shape[0]
    return pl.pallas_call(
        _mm_t_body,
        grid=(bsz, rows // tm, n // tn),
        in_specs=[pl.BlockSpec((None, tm, k), lambda bi, i, j: (bi, i, 0)),
                  pl.BlockSpec((tn, k), lambda bi, i, j: (j, 0)),
                  pl.BlockSpec((1, k), lambda bi, i, j: (0, 0)),
                  pl.BlockSpec((tn, 1), lambda bi, i, j: (j, 0))],
        out_specs=pl.BlockSpec((None, tn, tm), lambda bi, i, j: (bi, j, i)),
        out_shape=jax.ShapeDtypeStruct((bsz, n, rows), out_dtype),
        compiler_params=_cparams(("parallel", "parallel", "arbitrary"), VMEM_BIG),
        name=name,
    )(a3, wt, gain.reshape(1, k).astype(F32), bias_col.reshape(n, 1).astype(F32))


def _mmk_body(*refs, n_pairs, has_gate, has_res, res_scale, has_side):
    it = iter(refs)
    a_refs = [next(it) for _ in range(n_pairs)]
    w_refs = [next(it) for _ in range(n_pairs)]
    g_refs = [next(it) for _ in range(n_pairs)] if has_gate else None
    res_ref = next(it) if has_res else None
    side_in = next(it) if has_side else None
    o_ref = next(it)
    if has_side:
        next(it)[...] = side_in[...].astype(BF16)
    acc_refs = [next(it) for _ in range(n_pairs)]
    kk = pl.program_id(3)

    @pl.when(kk == 0)
    def _():
        for acc in acc_refs:
            acc[...] = jnp.zeros_like(acc)

    for a_ref, w_ref, acc in zip(a_refs, w_refs, acc_refs):
        acc[...] += jnp.dot(a_ref[...], w_ref[...], preferred_element_type=F32)

    @pl.when(kk == pl.num_programs(3) - 1)
    def _():
        r = None
        for p, acc in enumerate(acc_refs):
            t = acc[...]
            if has_gate:
                t = t * g_refs[p][...].astype(F32)
            r = t if r is None else r + t
        if has_res:
            r = r + res_scale * res_ref[...]
        o_ref[...] = r.astype(o_ref.dtype)


def _mm_ktiled(a_list, w_list, *, out_dtype, rows, tm, tn, nk, gates=None, res=None, res_scale=1.0, side=None,
               name="mmk"):
    bsz = a_list[0].shape[0]
    n = w_list[0].shape[1]
    npairs = len(a_list)
    tks = [a.shape[2] // nk for a in a_list]
    assert all(a.shape[2] == tk * nk and tk % LANES == 0 for a, tk in zip(a_list, tks))
    o_spec = pl.BlockSpec((None, tm, tn), lambda bi, i, j, kk: (bi, i, j))
    in_specs = ([pl.BlockSpec((None, tm, tk), lambda bi, i, j, kk: (bi, i, kk)) for tk in tks]
                + [pl.BlockSpec((tk, tn), lambda bi, i, j, kk: (kk, j)) for tk in tks])
    args = list(a_list) + list(w_list)
    if gates is not None:
        for p in range(npairs):
            in_specs.append(pl.BlockSpec((None, tm, tn), lambda bi, i, j, kk, p=p: (bi, i, p * (n // tn) + j)))
            args.append(gates)
    if res is not None:
        in_specs.append(o_spec)
        args.append(res)
    ni, nj = rows // tm, n // tn
    out_specs = [o_spec]
    out_shape = [jax.ShapeDtypeStruct((bsz, rows, n), out_dtype)]
    if side is not None:
        spec, shape = _side_specs(side, bsz * ni * nj * nk, lambda bi, i, j, kk: ((bi * ni + i) * nj + j) * nk + kk)
        in_specs.append(spec)
        args.append(side)
        out_specs.append(spec)
        out_shape.append(shape)
    body = functools.partial(_mmk_body, n_pairs=npairs, has_gate=gates is not None, has_res=res is not None,
                             res_scale=res_scale, has_side=side is not None)
    outs = pl.pallas_call(
        body,
        grid=(bsz, ni, nj, nk),
        in_specs=in_specs,
        out_specs=out_specs,
        out_shape=out_shape,
        scratch_shapes=[pltpu.VMEM((tm, tn), F32) for _ in range(npairs)],
        compiler_params=_cparams(("parallel", "parallel", "parallel", "arbitrary"), VMEM_BIG),
        name=name,
    )(*args)
    return outs[0] if side is None else outs


def _silu(x):
    return x * jax.nn.sigmoid(x)


def _ssd_body(xs_ref, b_ref, c_ref, z_ref, dt_ref, wx_ref, wb_ref, wc_ref, bx_ref, bb_ref, bc_ref,
              dtb_ref, alog_ref, dskip_ref, ng_ref, e_ref, o_ref,
              state_ref, extx_ref, extb_ref, extc_ref, y_ref, *, hg):
    c = pl.program_id(2)
    q = CHUNK
    p = SSM_HEAD_DIM
    first = c == 0
    row = lax.broadcasted_iota(I32, (q, 1), 0)
    live = jnp.logical_or(jnp.logical_not(first), row >= META_PAD)

    @pl.when(first)
    def _():
        state_ref[...] = jnp.zeros_like(state_ref)
        extx_ref[0:8, :] = jnp.zeros((8, extx_ref.shape[1]), F32)
        extb_ref[0:8, :] = jnp.zeros((8, extb_ref.shape[1]), F32)
        extc_ref[0:8, :] = jnp.zeros((8, extc_ref.shape[1]), F32)

    def conv(ext_ref, u_ref, w_ref, bias_ref):
        @pl.when(first)
        def _():
            ext_ref[8:8 + q, :] = jnp.where(live, u_ref[...].astype(F32), 0.0)

        @pl.when(jnp.logical_not(first))
        def _():
            ext_ref[8:8 + q, :] = u_ref[...].astype(F32)

        acc = bias_ref[...]
        for k in range(SSM_CONV):
            acc = acc + ext_ref[pl.ds(8 - (SSM_CONV - 1) + k, q), :] * w_ref[k:k + 1, :]
        ext_ref[0:8, :] = ext_ref[q:q + 8, :]
        return _silu(acc)

    xs = conv(extx_ref, xs_ref, wx_ref, bx_ref)
    bm = conv(extb_ref, b_ref, wb_ref, bb_ref)
    cm = conv(extc_ref, c_ref, wc_ref, bc_ref)

    x = dt_ref[...] + dtb_ref[...]
    dt = jnp.maximum(x, 0.0) + jnp.log1p(jnp.exp(-jnp.abs(x)))
    dt = jnp.where(live, dt, 0.0)
    a = -jnp.exp(alog_ref[...])
    r_i = lax.broadcasted_iota(I32, (q, q), 0)
    c_i = lax.broadcasted_iota(I32, (q, q), 1)
    causal = r_i >= c_i
    hi = lax.Precision.HIGHEST
    a_cs = jnp.dot(causal.astype(F32), dt * a, precision=hi, preferred_element_type=F32)
    a_cs_t = a_cs.T
    ea = jnp.exp(a_cs)
    decay_end = jnp.exp(a_cs[q - 1:q, :] - a_cs)
    e = e_ref[...]

    def spread(v):
        v_hi = v.astype(BF16)
        v_lo = (v - v_hi.astype(F32)).astype(BF16)
        return jnp.dot(v_hi, e, preferred_element_type=F32) + jnp.dot(v_lo, e, preferred_element_type=F32)

    dt_x = spread(dt)
    ea_x = spread(ea)
    de_x = spread(decay_end)

    xdt = xs * dt_x
    xdt_b = xdt.astype(BF16)
    cm_b = cm.astype(BF16)
    bm_b = bm.astype(BF16)
    cb = lax.dot_general(cm_b, bm_b, (((1,), (1,)), ((), ())), preferred_element_type=F32)
    prev = state_ref[...]
    y_off = jnp.dot(cm_b, prev.astype(BF16), preferred_element_type=F32) * ea_x
    def decay_scores(j):
        seg = a_cs[:, j:j + 1] - a_cs_t[j:j + 1, :]
        return (jnp.exp(jnp.where(causal, seg, NEG)) * cb).astype(BF16)

    m_next = decay_scores(0)
    for j in range(hg):
        m = m_next
        if j + 1 < hg:
            m_next = decay_scores(j + 1)
        y_ref[:, j * p:(j + 1) * p] = jnp.dot(m, xdt_b[:, j * p:(j + 1) * p], preferred_element_type=F32)
    y = y_ref[...] + y_off + xs * dskip_ref[...]
    state_ref[...] = prev * ea_x[q - 1:q, :] + jnp.dot(bm.T.astype(BF16), (xdt * de_x).astype(BF16),
                                                        preferred_element_type=F32)
    yz = y * _silu(z_ref[...].astype(F32))
    yn = yz * lax.rsqrt(jnp.mean(yz * yz, axis=-1, keepdims=True) + RMS_EPS) * ng_ref[...]
    o_ref[...] = yn.astype(o_ref.dtype)


def _ssd(xbc3, z3, dt_g, conv_w, conv_b, dtb_g, alog_g, dskip_x, norm_g, d_inner, heads):
    bsz, rows, conv_dim = xbc3.shape
    g, n, p, q = SSM_GROUPS, SSM_STATE, SSM_HEAD_DIM, CHUNK
    hg = heads // g
    gw = hg * p
    nc = rows // q
    assert gw % LANES == 0 and d_inner % n == 0 and hg <= LANES
    boff = d_inner // n
    expand = (jnp.arange(LANES)[:, None] == (jnp.arange(gw) // p)[None, :]).astype(BF16)

    def chunk(c):
        return (c + nc - 1) % nc

    in_specs = [
        pl.BlockSpec((None, q, gw), lambda b, gi, c: (b, chunk(c), gi)),
        pl.BlockSpec((None, q, n), lambda b, gi, c: (b, chunk(c), boff + gi)),
        pl.BlockSpec((None, q, n), lambda b, gi, c: (b, chunk(c), boff + g + gi)),
        pl.BlockSpec((None, q, gw), lambda b, gi, c: (b, chunk(c), gi)),
        pl.BlockSpec((None, None, q, LANES), lambda b, gi, c: (b, gi, chunk(c), 0)),
        pl.BlockSpec((SSM_CONV, gw), lambda b, gi, c: (0, gi)),
        pl.BlockSpec((SSM_CONV, n), lambda b, gi, c: (0, boff + gi)),
        pl.BlockSpec((SSM_CONV, n), lambda b, gi, c: (0, boff + g + gi)),
        pl.BlockSpec((1, gw), lambda b, gi, c: (0, gi)),
        pl.BlockSpec((1, n), lambda b, gi, c: (0, boff + gi)),
        pl.BlockSpec((1, n), lambda b, gi, c: (0, boff + g + gi)),
        pl.BlockSpec((None, 1, LANES), lambda b, gi, c: (gi, 0, 0)),
        pl.BlockSpec((None, 1, LANES), lambda b, gi, c: (gi, 0, 0)),
        pl.BlockSpec((1, gw), lambda b, gi, c: (0, gi)),
        pl.BlockSpec((1, gw), lambda b, gi, c: (0, gi)),
        pl.BlockSpec((LANES, gw), lambda b, gi, c: (0, 0)),
    ]
    return pl.pallas_call(
        functools.partial(_ssd_body, hg=hg),
        grid=(bsz, g, nc),
        in_specs=in_specs,
        out_specs=pl.BlockSpec((None, q, gw), lambda b, gi, c: (b, chunk(c), gi)),
        out_shape=jax.ShapeDtypeStruct((bsz, rows, d_inner), BF16),
        scratch_shapes=[pltpu.VMEM((n, gw), F32), pltpu.VMEM((q + 8, gw), F32), pltpu.VMEM((q + 8, n), F32),
                        pltpu.VMEM((q + 8, n), F32), pltpu.VMEM((q, gw), F32)],
        compiler_params=_cparams(("parallel", "parallel", "arbitrary")),
        name="ssd",
    )(xbc3, xbc3, xbc3, z3, dt_g, conv_w, conv_w, conv_w, conv_b, conv_b, conv_b, dtb_g, alog_g,
      dskip_x, norm_g, expand)


def _attn_body(qi_ref, ki_ref, q_ref, km_ref, vm_ref, k_ref, v_ref, *rest, heads, n_side):
    side_in, o_ref, side_out = rest[:n_side], rest[n_side], rest[n_side + 1:2 * n_side + 1]
    m_ref, acc_ref = rest[2 * n_side + 1:]
    for src, dst in zip(side_in, side_out):
        dst[...] = src[...].astype(BF16)
    step_id = pl.program_id(2)
    qi, ki = qi_ref[step_id], ki_ref[step_id]

    @pl.when(ki < 0)
    def _():
        m_ref[...] = jnp.full_like(m_ref, NEG)
        acc_ref[...] = jnp.zeros_like(acc_ref)

    def step(kr, vr, mask_fn):
        def scores(h):
            qh = q_ref[:, h * MLA_QKP:(h + 1) * MLA_QKP]
            kh = kr[:, h * MLA_QKP:(h + 1) * MLA_QKP]
            st = lax.dot_general(kh, qh, (((1,), (1,)), ((), ())), preferred_element_type=F32)
            return st if mask_fn is None else mask_fn(st)

        def probs(h, st):
            m_prev = m_ref[h]
            m_new = jnp.maximum(m_prev, jnp.max(st, axis=0, keepdims=True))
            m_ref[h] = m_new
            return jnp.exp2(m_prev - m_new), jnp.exp2((st - m_new).astype(BF16))

        def accumulate(h, alpha, pt):
            sl = slice(h * MLA_VP, (h + 1) * MLA_VP)
            acc_ref[sl, :] = acc_ref[sl, :] * alpha + jnp.dot(vr[sl, :], pt, preferred_element_type=F32)

        st_q, pr_q = {}, {}
        for stage in range(heads + 2):
            if stage < heads:
                st_q[stage] = scores(stage)
            if 0 <= stage - 1 < heads:
                pr_q[stage - 1] = probs(stage - 1, st_q.pop(stage - 1))
            if 0 <= stage - 2 < heads:
                accumulate(stage - 2, *pr_q.pop(stage - 2))

    def meta_mask(st):
        key = lax.broadcasted_iota(I32, st.shape, 0)
        return jnp.where(key >= META_PAD, st, NEG)

    def causal_mask(st):
        key = lax.broadcasted_iota(I32, st.shape, 0)
        qry = lax.broadcasted_iota(I32, st.shape, 1)
        return jnp.where(key <= qry, st, NEG)

    @pl.when(ki < 0)
    def _():
        step(km_ref, vm_ref, meta_mask)

    @pl.when(jnp.logical_and(ki >= 0, ki < qi))
    def _():
        step(k_ref, v_ref, None)

    @pl.when(ki == qi)
    def _():
        step(k_ref, v_ref, causal_mask)
        for h in range(heads):
            num = acc_ref[h * MLA_VP:h * MLA_VP + MLA_V, :]
            den = acc_ref[h * MLA_VP + MLA_V:h * MLA_VP + MLA_V + 1, :]
            o_ref[:, h * MLA_V:(h + 1) * MLA_V] = (num / den).T.astype(o_ref.dtype)


def _attention(q3, k3, vt3, seq, sides=()):
    bsz = q3.shape[0]
    hp = 8
    t = _pick(seq, (512, 256, 128))
    nq = seq // t
    meta_blk = seq // CHUNK
    qw, vw = hp * MLA_QKP, hp * MLA_VP
    qi_tab = jnp.asarray([qi for qi in range(nq) for _ in range(qi + 2)], I32)
    ki_tab = jnp.asarray([ki for qi in range(nq) for ki in range(-1, qi + 1)], I32)

    def q_idx(b, h, s, qt, kt):
        return (b, qt[s], h)

    def k_idx(b, h, s, qt, kt):
        return (b, jnp.maximum(kt[s], 0), h)

    def vt_idx(b, h, s, qt, kt):
        return (b, h, jnp.maximum(kt[s], 0))

    nh, npairs = MLA_HEADS // hp, int(qi_tab.shape[0])
    side_specs = [_side_specs(s_arr, bsz * nh * npairs, lambda b, h, s, qt, kt: (b * nh + h) * npairs + s)
                  for s_arr in sides]
    grid_spec = pltpu.PrefetchScalarGridSpec(
        num_scalar_prefetch=2,
        grid=(bsz, nh, npairs),
        in_specs=[
            pl.BlockSpec((None, t, qw), q_idx),
            pl.BlockSpec((None, CHUNK, qw), lambda b, h, s, qt, kt: (b, meta_blk, h)),
            pl.BlockSpec((None, vw, CHUNK), lambda b, h, s, qt, kt: (b, h, meta_blk)),
            pl.BlockSpec((None, t, qw), k_idx),
            pl.BlockSpec((None, vw, t), vt_idx),
        ] + [spec for spec, _ in side_specs],
        out_specs=[pl.BlockSpec((None, t, hp * MLA_V), q_idx)] + [spec for spec, _ in side_specs],
        scratch_shapes=[pltpu.VMEM((hp, 1, t), F32), pltpu.VMEM((vw, t), F32)],
    )
    return pl.pallas_call(
        functools.partial(_attn_body, heads=hp, n_side=len(sides)),
        grid_spec=grid_spec,
        out_shape=[jax.ShapeDtypeStruct((bsz, seq, MLA_HEADS * MLA_V), BF16)] + [shape for _, shape in side_specs],
        compiler_params=_cparams(("parallel", "parallel", "arbitrary")),
        name="mla_attention",
    )(qi_tab, ki_tab, q3, k3, vt3, k3, vt3, *sides)


def _router_body(h_ref, wr_ref, rb_ref, e_ref, pos_ref, w_ref, cnt_ref, carry_ref):
    i = pl.program_id(0)
    ne, ng = N_EXPERTS, N_EXPERT_GROUPS
    gs = ne // ng
    tm = h_ref.shape[0]

    @pl.when(i == 0)
    def _():
        carry_ref[...] = jnp.zeros_like(carry_ref)

    logits = lax.dot_general(wr_ref[...], h_ref[...], (((1,), (1,)), ((), ())), precision=lax.Precision.HIGHEST,
                             preferred_element_type=F32)
    scores = jax.nn.sigmoid(logits)
    choice = scores + rb_ref[...]
    sub = lax.broadcasted_iota(I32, (gs, tm), 0)
    grp_rows = []
    for g in range(ng):
        blk = choice[g * gs:(g + 1) * gs, :]
        m1 = jnp.max(blk, axis=0, keepdims=True)
        first = jnp.min(jnp.where(blk == m1, sub, gs), axis=0, keepdims=True)
        m2 = jnp.max(jnp.where(sub == first, -jnp.inf, blk), axis=0, keepdims=True)
        grp_rows.append(m1 + m2)
    grp = jnp.concatenate(grp_rows, axis=0)
    grank = jnp.zeros((ng, tm), I32)
    gidx = lax.broadcasted_iota(I32, (ng, tm), 0)
    for g in range(ng):
        rowv = grp[g:g + 1, :]
        beats = jnp.logical_or(rowv > grp, jnp.logical_and(rowv == grp, gidx > g))
        grank = grank + beats.astype(I32)
    gsel = (grank < TOPK_GROUPS).astype(F32)
    esel = jnp.concatenate([jnp.broadcast_to(gsel[g:g + 1, :], (gs, tm)) for g in range(ng)], axis=0)
    masked = jnp.where(esel > 0.0, choice, -jnp.inf)
    eidx = lax.broadcasted_iota(I32, (ne, tm), 0)
    rank = jnp.zeros((ne, tm), I32)
    for e in range(ne):
        rowv = masked[e:e + 1, :]
        beats = jnp.logical_or(rowv > masked, jnp.logical_and(rowv == masked, eidx > e))
        rank = rank + beats.astype(I32)
    top = jnp.logical_and(rank < TOP_K, esel > 0.0)
    topf = top.astype(F32)
    wsel = jnp.where(top, scores, 0.0)
    wn = wsel / jnp.sum(wsel, axis=0, keepdims=True) * ROUTED_SCALE
    r_i = lax.broadcasted_iota(I32, (tm, tm), 0)
    c_i = lax.broadcasted_iota(I32, (tm, tm), 1)
    before = (r_i < c_i).astype(BF16)
    pos = jnp.dot(topf.astype(BF16), before, preferred_element_type=F32) + carry_ref[:, :1]
    carry_ref[...] = carry_ref[...] + jnp.sum(topf, axis=1, keepdims=True)
    cnt_ref[...] = carry_ref[...]
    e_rows, p_rows, w_rows = [], [], []
    for k in range(TOP_K):
        hit = rank == k
        e_rows.append(jnp.sum(jnp.where(hit, eidx, 0), axis=0, keepdims=True))
        p_rows.append(jnp.sum(jnp.where(hit, pos, 0.0), axis=0, keepdims=True))
        w_rows.append(jnp.sum(jnp.where(hit, wn, 0.0), axis=0, keepdims=True))
    e_ref[...] = jnp.concatenate(e_rows, axis=0)
    pos_ref[...] = jnp.concatenate(p_rows, axis=0).astype(I32)
    w_ref[...] = jnp.concatenate(w_rows, axis=0)


def _router(h2, w_router, router_bias):
    t, d = h2.shape
    tm = _pick(t, (256, 128))
    slot = pl.BlockSpec((TOP_K, tm), lambda i: (0, i))
    rb = jnp.broadcast_to(router_bias.astype(F32)[:, None], (N_EXPERTS, tm))
    return pl.pallas_call(
        _router_body,
        grid=(t // tm,),
        in_specs=[pl.BlockSpec((tm, d), lambda i: (i, 0)), pl.BlockSpec((N_EXPERTS, d), lambda i: (0, 0)),
                  pl.BlockSpec((N_EXPERTS, tm), lambda i: (0, 0))],
        out_specs=[slot, slot, slot, pl.BlockSpec((N_EXPERTS, LANES), lambda i: (0, 0))],
        out_shape=[jax.ShapeDtypeStruct((TOP_K, t), I32), jax.ShapeDtypeStruct((TOP_K, t), I32),
                   jax.ShapeDtypeStruct((TOP_K, t), F32), jax.ShapeDtypeStruct((N_EXPERTS, LANES), F32)],
        scratch_shapes=[pltpu.VMEM((N_EXPERTS, LANES), F32)],
        compiler_params=_cparams(("arbitrary",)),
        name="moe_router",
    )(h2, w_router.T.astype(F32), rb)


def _load_slots(dest_hbm, dest_ref, sem):
    cp = pltpu.make_async_copy(dest_hbm.at[pl.program_id(0)], dest_ref, sem)
    cp.start()
    cp.wait()


def _dispatch_body(cnt_ref, start_ref, dest_hbm, x_ref, xs_ref, dest_ref, zero_ref, sem_ref, *, block_rows):
    i = pl.program_id(0)
    tm = x_ref.shape[0]
    _load_slots(dest_hbm, dest_ref, sem_ref.at[2])

    def row_copy(tok, k):
        return pltpu.make_async_copy(x_ref.at[pl.ds(tok, 1), :], xs_ref.at[pl.ds(dest_ref[k, tok], 1), :], sem_ref.at[0])

    def issue(tok, carry):
        for k in range(TOP_K):
            row_copy(tok, k).start()
        return carry

    lax.fori_loop(0, tm, issue, 0)

    @pl.when(i == 0)
    def _():
        zero_ref[...] = jnp.zeros_like(zero_ref)

        def fill(e, carry):
            cnt = cnt_ref[e]
            padded = (cnt + block_rows - 1) // block_rows * block_rows
            base = start_ref[e] + cnt

            def zcopy(r):
                return pltpu.make_async_copy(zero_ref.at[pl.ds(0, 1), :], xs_ref.at[pl.ds(base + r, 1), :], sem_ref.at[1])

            def zstart(r, c2):
                zcopy(r).start()
                return c2

            def zwait(r, c2):
                zcopy(r).wait()
                return c2

            lax.fori_loop(0, padded - cnt, zstart, 0)
            lax.fori_loop(0, padded - cnt, zwait, 0)
            return carry

        lax.fori_loop(0, N_EXPERTS, fill, 0)

    def drain(tok, carry):
        for k in range(TOP_K):
            row_copy(tok, k).wait()
        return carry

    lax.fori_loop(0, tm, drain, 0)


def _slot_tiles(dest, tm):
    k, t = dest.shape
    return dest.reshape(k, t // tm, tm).transpose(1, 0, 2)


def _dispatch(hp2, dest, counts, starts, n_rows):
    t, w = hp2.shape
    tm = _pick(t, (256, 128))
    grid_spec = pltpu.PrefetchScalarGridSpec(
        num_scalar_prefetch=2,
        grid=(t // tm,),
        in_specs=[pl.BlockSpec(memory_space=pl.ANY),
                  pl.BlockSpec((tm, w), lambda i, c, s: (i, 0))],
        out_specs=pl.BlockSpec(memory_space=pl.ANY),
        scratch_shapes=[pltpu.SMEM((TOP_K, tm), I32), pltpu.VMEM((8, w), U32), pltpu.SemaphoreType.DMA((3,))],
    )
    return pl.pallas_call(
        functools.partial(_dispatch_body, block_rows=MOE_ROWS),
        grid_spec=grid_spec,
        out_shape=jax.ShapeDtypeStruct((n_rows, w), U32),
        compiler_params=pltpu.CompilerParams(dimension_semantics=("arbitrary",)),
        name="moe_dispatch",
    )(counts, starts, _slot_tiles(dest, tm), hp2)


def _expert_body(be_ref, nu_ref, x_ref, wg_ref, wu_ref, wd_ref, o_ref):
    j = pl.program_id(0)

    @pl.when(j < nu_ref[0])
    def _():
        half = x_ref.shape[1]
        lo, hi = _unpack_halves(x_ref[...])
        lo, hi = lo.astype(BF16), hi.astype(BF16)

        def proj(w_ref):
            return (jnp.dot(lo, w_ref[:half, :], preferred_element_type=F32)
                    + jnp.dot(hi, w_ref[half:, :], preferred_element_type=F32))

        hmid = (_silu(proj(wg_ref)) * proj(wu_ref)).astype(BF16)
        o_ref[...] = _pack_halves(jnp.dot(hmid, wd_ref[...], preferred_element_type=F32))


def _experts(xs, wg, wu, wd, block_e, n_used):
    rows, w = xs.shape
    _, d, ff = wg.shape
    bm = MOE_ROWS
    nb = rows // bm

    def row_idx(j, be, nu):
        return (jnp.minimum(j, nu[0] - 1), 0)

    def w_idx(j, be, nu):
        return (be[jnp.minimum(j, nu[0] - 1)], 0, 0)

    grid_spec = pltpu.PrefetchScalarGridSpec(
        num_scalar_prefetch=2,
        grid=(nb,),
        in_specs=[pl.BlockSpec((bm, w), row_idx),
                  pl.BlockSpec((None, d, ff), w_idx), pl.BlockSpec((None, d, ff), w_idx),
                  pl.BlockSpec((None, ff, d), w_idx)],
        out_specs=pl.BlockSpec((bm, w), row_idx),
    )
    return pl.pallas_call(
        _expert_body,
        grid_spec=grid_spec,
        out_shape=jax.ShapeDtypeStruct((rows, w), U32),
        compiler_params=_cparams(("arbitrary",), 60 * 1024 * 1024),
        name="moe_experts",
    )(block_e, n_used, xs, wg, wu, wd)


def _combine_body(dest_hbm, wt_ref, h_ref, ysh_ref, g_ref, b_ref, y_ref, o_ref, dest_ref, gbuf_ref, sem_ref, *, alpha):
    tm = h_ref.shape[0]
    _load_slots(dest_hbm, dest_ref, sem_ref.at[1])

    def row_copy(tok, k):
        return pltpu.make_async_copy(y_ref.at[pl.ds(dest_ref[k, tok], 1), :], gbuf_ref.at[k, pl.ds(tok, 1), :],
                                     sem_ref.at[0])

    def issue(tok, carry):
        for k in range(TOP_K):
            row_copy(tok, k).start()
        return carry

    def drain(tok, carry):
        for k in range(TOP_K):
            row_copy(tok, k).wait()
        return carry

    lax.fori_loop(0, tm, issue, 0)
    lax.fori_loop(0, tm, drain, 0)
    lo, hi = _unpack_halves(ysh_ref[...])
    wt = wt_ref[...]
    for k in range(TOP_K):
        glo, ghi = _unpack_halves(gbuf_ref[k])
        wk = wt[:, k:k + 1]
        lo = lo + glo * wk
        hi = hi + ghi * wk
    x = alpha * h_ref[...] + jnp.concatenate([lo, hi], axis=1)
    mu = jnp.mean(x, axis=-1, keepdims=True)
    xc = x - mu
    var = jnp.mean(xc * xc, axis=-1, keepdims=True)
    o_ref[...] = xc * lax.rsqrt(var + LN_EPS) * g_ref[...] + b_ref[...]


def _combine(h2, ysh, y_sorted, dest, wt, g, b, alpha):
    t, d = h2.shape
    w = d // 2
    tm = _pick(t, (128,))
    vec = pl.BlockSpec((1, d), lambda i: (0, 0))
    return pl.pallas_call(
        functools.partial(_combine_body, alpha=alpha),
        grid=(t // tm,),
        in_specs=[pl.BlockSpec(memory_space=pl.ANY),
                  pl.BlockSpec((tm, TOP_K), lambda i: (i, 0)),
                  pl.BlockSpec((tm, d), lambda i: (i, 0)),
                  pl.BlockSpec((tm, w), lambda i: (i, 0)),
                  vec, vec,
                  pl.BlockSpec(memory_space=pl.ANY)],
        out_specs=pl.BlockSpec((tm, d), lambda i: (i, 0)),
        out_shape=jax.ShapeDtypeStruct((t, d), F32),
        scratch_shapes=[pltpu.SMEM((TOP_K, tm), I32), pltpu.VMEM((TOP_K, tm, w), U32), pltpu.SemaphoreType.DMA((2,))],
        compiler_params=_cparams(("arbitrary",)),
        name="moe_combine",
    )(_slot_tiles(dest, tm), wt, h2, ysh, g.reshape(1, d), b.reshape(1, d), y_sorted)


def _rope_tables(pos):
    inv_freq = ROPE_THETA ** (-jnp.arange(0, MLA_ROPE, 2, dtype=F32) / MLA_ROPE)
    ang = pos.astype(F32)[..., None] * inv_freq
    return jnp.cos(ang), jnp.sin(ang)


def kernel(x, positions, meta_tokens, ln_in_g, ln_in_b, w_in, b_gate, conv_w, conv_b, dt_bias, a_log, d_skip,
           ssm_norm_g, w_ssm_proj, q_a_norm_g, w_q_b, kv_a_norm_g, w_kv_b, w_attn_proj, w_out, ln1_g, ln1_b,
           w_router, router_bias, w_exp_gate, w_exp_up, w_exp_down, w_sh_gate, w_sh_up, w_sh_down, ln2_g, ln2_b):
    bsz, seq, d = x.shape
    depth = w_in.shape[0]
    heads = dt_bias.shape[-1]
    d_inner = w_ssm_proj.shape[1]
    conv_dim = conv_w.shape[-1]
    q_rank = w_q_b.shape[1]
    kv_rank = w_kv_b.shape[1]
    hh = MLA_HEADS
    g = SSM_GROUPS
    hg = heads // g
    assert seq % CHUNK == 0 and d % (2 * LANES) == 0
    lp = seq + CHUNK
    t = bsz * seq
    alpha = (2.0 * depth) ** 0.25

    meta = jnp.broadcast_to(meta_tokens[None].astype(x.dtype), (bsz, N_META, d))
    hcat = jnp.concatenate([x, jnp.zeros((bsz, META_PAD, d), x.dtype), meta], axis=1)
    pos = jnp.concatenate([positions.astype(I32) + N_META, jnp.zeros((bsz, META_PAD), I32),
                           jnp.broadcast_to(jnp.arange(N_META, dtype=I32), (bsz, N_META))], axis=1)
    cos, sin = _rope_tables(pos)
    zr = jnp.zeros_like(cos)
    ones = jnp.ones((bsz, lp, MLA_NOPE), F32)
    zn = jnp.zeros((bsz, lp, MLA_NOPE), F32)
    rope_c = jnp.concatenate([ones, cos, cos, zr, zr], axis=-1)[:, :seq]
    rope_slo = jnp.concatenate([zn, -sin, zr, zr, zr], axis=-1)[:, :seq]
    rope_shi = jnp.concatenate([zn, zr, sin, zr, zr], axis=-1)[:, :seq]
    k_c = jnp.concatenate([cos, cos, zr, zr], axis=-1)
    k_s = jnp.concatenate([-sin, sin, zr, zr], axis=-1)

    h_f32, h_b16 = _layer_norm(hcat, ln_in_g, ln_in_b)
    tm_all = _pick(lp, (1408, 1152, 1024, 896, 768, 640, 512, 384, 256, 128))
    tm_seq = _pick(seq, (1024, 512, 256, 128))

    out = None
    for l in range(depth):
        offs = [0]
        for wdt in (d_inner, conv_dim, heads, q_rank, kv_rank, MLA_ROPE, 2 * d):
            offs.append(offs[-1] + wdt)
        wt = jnp.swapaxes(w_in[l], 0, 1)
        w_kr = wt[offs[5]:offs[6]]
        small_cols = q_rank + kv_rank + 2 * MLA_ROPE + heads
        small_n = small_cols + (-small_cols) % 256
        wt_small = jnp.concatenate([wt[offs[3]:offs[4]], wt[offs[4]:offs[5]], w_kr, w_kr, wt[offs[2]:offs[3]],
                                    jnp.zeros((small_n - small_cols, d), F32)], axis=0).astype(BF16)
        wt_g = wt[offs[6]:offs[7]].astype(BF16)

        n_exp, _, ff = w_exp_gate[l].shape
        z3 = _mm(h_b16, wt, w_rows=(offs[0], d_inner), out_dtype=BF16, tm=tm_all, tn=512, name="in_proj_z")
        xbc3 = _mm(h_b16, wt, w_rows=(offs[1], conv_dim), out_dtype=BF16, tm=tm_all, tn=512, name="in_proj_xbc")
        small = _mm(h_b16, wt_small, w_rows=(0, small_n), out_dtype=F32, tm=tm_all, tn=256, name="in_proj_small")
        gates = _mm(h_b16, wt_g, w_rows=(0, 2 * d), out_dtype=BF16, tm=tm_seq, tn=512, n_row_tiles=seq // tm_seq,
                    bias=b_gate[l], act="sigmoid", name="in_proj_gates")
        o = 0
        q_a = small[:, :, o:o + q_rank]; o += q_rank
        kv_a = small[:, :, o:o + kv_rank]; o += kv_rank
        kr2 = small[:, :, o:o + 2 * MLA_ROPE]; o += 2 * MLA_ROPE
        dt_raw = small[:, :, o:o + heads]

        dt_g = jnp.pad(dt_raw.reshape(bsz, lp, g, hg).transpose(0, 2, 1, 3), ((0, 0), (0, 0), (0, 0), (0, LANES - hg)))
        pad_h = lambda v: jnp.pad(v.astype(F32).reshape(g, 1, hg), ((0, 0), (0, 0), (0, LANES - hg)))
        y_ssm = _ssd(xbc3, z3, dt_g, conv_w[l].astype(F32), conv_b[l].reshape(1, conv_dim).astype(F32),
                     pad_h(dt_bias[l]), pad_h(a_log[l]),
                     jnp.repeat(d_skip[l].astype(F32), SSM_HEAD_DIM).reshape(1, d_inner),
                     ssm_norm_g[l].reshape(1, d_inner).astype(F32), d_inner, heads)

        qscale = (MLA_QK ** -0.5) * math.log2(math.e)
        head_pad = ((0, 0), (0, 0), (0, MLA_QKP - MLA_QK))
        w_q = jnp.pad(w_q_b[l].reshape(q_rank, hh, MLA_QK), head_pad).reshape(q_rank, hh * MLA_QKP).astype(BF16)
        q3 = _mm(q_a, w_q, out_dtype=BF16, tm=tm_seq, tn=4 * MLA_QKP, n_row_tiles=seq // tm_seq,
                 gain=q_a_norm_g[l], rope=(rope_c, rope_slo, rope_shi), scale=qscale, name="q_proj")
        wkv = w_kv_b[l].reshape(kv_rank, hh, MLA_NOPE + MLA_V)
        w_k = jnp.pad(wkv[:, :, :MLA_NOPE], ((0, 0), (0, 0), (0, MLA_QKP - MLA_NOPE)))
        w_k = w_k.reshape(kv_rank, hh * MLA_QKP).astype(BF16)
        w_vt = jnp.pad(wkv[:, :, MLA_NOPE:], ((0, 0), (0, 0), (0, MLA_VP - MLA_V)))
        w_vt = w_vt.reshape(kv_rank, hh * MLA_VP).T.astype(BF16)
        ones_rows = jnp.tile(jnp.concatenate([jnp.zeros((MLA_V,), F32), jnp.ones((MLA_VP - MLA_V,), F32)]), hh)
        k3 = _mm(kv_a, w_k, out_dtype=BF16, tm=tm_all, tn=4 * MLA_QKP, gain=kv_a_norm_g[l],
                 kadd=(kr2, k_c, k_s), name="k_proj")
        vt3 = _mm_t(kv_a, w_vt, kv_a_norm_g[l], ones_rows, out_dtype=BF16, tm=tm_all, tn=4 * MLA_VP,
                    name="v_proj")
        y_attn, wg_b, wu_b, wd_b = _attention(
            q3, k3, vt3, seq, [w_exp_gate[l].reshape(n_exp * d, ff), w_exp_up[l].reshape(n_exp * d, ff),
                               w_exp_down[l].reshape(n_exp * ff, d)])

        mixed = _mm_ktiled([y_ssm, y_attn], [w_ssm_proj[l].astype(BF16), w_attn_proj[l].astype(BF16)],
                           out_dtype=BF16, rows=seq, tm=tm_seq, tn=_pick(d, (1024, 512)), nk=4,
                           gates=gates, name="branch_proj")
        pre1 = _mm_ktiled([mixed], [w_out[l].astype(BF16)], out_dtype=F32, rows=seq, tm=tm_seq,
                          tn=_pick(d, (1024, 512)), nk=2, res=h_f32, res_scale=alpha, name="out_proj")
        h1, h1p = _layer_norm_pack(pre1.reshape(t, d), ln1_g[l], ln1_b[l])

        e_slot, pos_slot, w_slot, cnt = _router(h1, w_router[l], router_bias[l])
        counts = cnt[:, 0].astype(I32)
        padded = (counts + MOE_ROWS - 1) // MOE_ROWS * MOE_ROWS
        ends = jnp.cumsum(padded)
        starts = ends - padded
        onehot = e_slot[None] == jnp.arange(N_EXPERTS, dtype=I32)[:, None, None]
        dest = pos_slot + jnp.sum(jnp.where(onehot, starts[:, None, None], 0), axis=0)
        n_blocks = -(-(t * TOP_K) // MOE_ROWS) + N_EXPERTS
        blk0 = jnp.arange(n_blocks, dtype=I32) * MOE_ROWS
        block_e = jnp.minimum(jnp.sum((ends[None, :] <= blk0[:, None]).astype(I32), axis=1), N_EXPERTS - 1)
        n_used = (ends[-1] // MOE_ROWS).reshape(1)
        xs = _dispatch(h1p, dest, counts, starts, n_blocks * MOE_ROWS)
        y_sorted = _experts(xs, wg_b.reshape(n_exp, d, ff), wu_b.reshape(n_exp, d, ff), wd_b.reshape(n_exp, ff, d),
                            block_e, n_used)
        y_shared = _experts(h1p, w_sh_gate[l][None].astype(BF16), w_sh_up[l][None].astype(BF16),
                            w_sh_down[l][None].astype(BF16), jnp.zeros((t // MOE_ROWS,), I32),
                            jnp.full((1,), t // MOE_ROWS, I32))
        out = _combine(h1, y_shared, y_sorted, dest, w_slot.T, ln2_g[l], ln2_b[l], alpha)
        if l + 1 < depth:
            raise NotImplementedError("stacked layers need the meta rows carried through the channel mixer")
    return out.reshape(bsz, seq, d)
```

```python
import functools
import math

import jax
import jax.numpy as jnp
from jax import lax
from jax.experimental import pallas as pl
from jax.experimental.pallas import tpu as pltpu

F32 = jnp.float32
BF16 = jnp.bfloat16
U32 = jnp.uint32
I32 = jnp.int32

N_META = 16
CHUNK = 128
META_PAD = CHUNK - N_META
SSM_HEAD_DIM = 64
SSM_GROUPS = 8
SSM_STATE = 128
SSM_CONV = 4
MLA_HEADS = 64
MLA_NOPE = 128
MLA_ROPE = 64
MLA_V = 128
MLA_QK = MLA_NOPE + MLA_ROPE
ROPE_THETA = 10000.0
N_EXPERTS = 64
N_EXPERT_GROUPS = 8
TOPK_GROUPS = 4
TOP_K = 8
ROUTED_SCALE = 2.5
LN_EPS = 1e-5
RMS_EPS = 1e-6
NEG = -1e30
LANES = 128
MLA_QKP = 2 * LANES
MOE_ROWS = 256
MLA_VP = MLA_V + 16
VMEM_BIG = 56 * 1024 * 1024
VMEM_MID = 44 * 1024 * 1024


def _cparams(sem, vmem=VMEM_MID):
    return pltpu.CompilerParams(dimension_semantics=sem, vmem_limit_bytes=vmem)


def _pick(n, cands):
    for c in cands:
        if n % c == 0:
            return c
    raise ValueError(f"no tile for {n} in {cands}")


def _ln_body(x_ref, g_ref, b_ref, of_ref, ob_ref):
    x = x_ref[...]
    mu = jnp.mean(x, axis=-1, keepdims=True)
    xc = x - mu
    var = jnp.mean(xc * xc, axis=-1, keepdims=True)
    y = xc * lax.rsqrt(var + LN_EPS) * g_ref[...] + b_ref[...]
    of_ref[...] = y
    ob_ref[...] = y.astype(BF16)


def _layer_norm(x3, g, b):
    bsz, rows, d = x3.shape
    tm = _pick(rows, (256, 128))
    spec = pl.BlockSpec((None, tm, d), lambda bi, i: (bi, i, 0))
    vec = pl.BlockSpec((1, d), lambda bi, i: (0, 0))
    return pl.pallas_call(
        _ln_body,
        grid=(bsz, rows // tm),
        in_specs=[spec, vec, vec],
        out_specs=[spec, spec],
        out_shape=[jax.ShapeDtypeStruct(x3.shape, F32), jax.ShapeDtypeStruct(x3.shape, BF16)],
        compiler_params=_cparams(("parallel", "parallel")),
        name="layer_norm",
    )(x3, g.reshape(1, d), b.reshape(1, d))


def _pack_halves(y):
    n = y.shape[1] // 2
    lo = pltpu.bitcast(y[:, :n].astype(BF16).astype(F32), U32) >> 16
    hi = pltpu.bitcast(y[:, n:].astype(BF16).astype(F32), U32) & jnp.uint32(0xFFFF0000)
    return hi | lo


def _unpack_halves(w):
    lo = pltpu.bitcast(w << 16, F32)
    hi = pltpu.bitcast(w & jnp.uint32(0xFFFF0000), F32)
    return lo, hi


def _ln_pack_body(x_ref, g_ref, b_ref, of_ref, op_ref):
    x = x_ref[...]
    mu = jnp.mean(x, axis=-1, keepdims=True)
    xc = x - mu
    var = jnp.mean(xc * xc, axis=-1, keepdims=True)
    y = xc * lax.rsqrt(var + LN_EPS) * g_ref[...] + b_ref[...]
    of_ref[...] = y
    op_ref[...] = _pack_halves(y)


def _layer_norm_pack(x2, g, b):
    rows, d = x2.shape
    tm = _pick(rows, (256, 128))
    vec = pl.BlockSpec((1, d), lambda i: (0, 0))
    return pl.pallas_call(
        _ln_pack_body,
        grid=(rows // tm,),
        in_specs=[pl.BlockSpec((tm, d), lambda i: (i, 0)), vec, vec],
        out_specs=[pl.BlockSpec((tm, d), lambda i: (i, 0)), pl.BlockSpec((tm, d // 2), lambda i: (i, 0))],
        out_shape=[jax.ShapeDtypeStruct((rows, d), F32), jax.ShapeDtypeStruct((rows, d // 2), U32)],
        compiler_params=_cparams(("parallel",)),
        name="layer_norm_pack",
    )(x2, g.reshape(1, d), b.reshape(1, d))


def _mm_body(*refs, has_gain, has_bias, act, rope, kadd, scale, has_side, w_rows):
    it = iter(refs)
    a_ref, w_ref = next(it), next(it)
    gain_ref = next(it) if has_gain else None
    bias_ref = next(it) if has_bias else None
    rope_refs = [next(it) for _ in range(3)] if rope else None
    kadd_refs = [next(it) for _ in range(3)] if kadd else None
    side_in = next(it) if has_side else None
    o_ref = next(it)
    if has_side:
        next(it)[...] = side_in[...].astype(BF16)
    a = a_ref[...]
    if has_gain:
        af = a.astype(F32)
        a = af * lax.rsqrt(jnp.mean(af * af, axis=-1, keepdims=True) + RMS_EPS) * gain_ref[...]
    if w_rows:
        r = lax.dot_general(a.astype(BF16), w_ref[...].astype(BF16), (((1,), (1,)), ((), ())),
                            preferred_element_type=F32)
    else:
        r = jnp.dot(a.astype(BF16), w_ref[...].astype(BF16), preferred_element_type=F32)
    tn = r.shape[1]
    if has_bias:
        r = r + bias_ref[...]
    if act == "sigmoid":
        r = jax.nn.sigmoid(r)
    if rope:
        c, slo, shi = (t[...] for t in rope_refs)
        half = MLA_ROPE // 2
        pieces = []
        for h0 in range(0, tn, MLA_QKP):
            rp = r[:, h0 + MLA_NOPE:h0 + MLA_QKP]
            pieces += [r[:, h0:h0 + MLA_NOPE], rp * c + pltpu.roll(rp, LANES - half, 1) * slo + pltpu.roll(rp, half, 1) * shi]
        r = jnp.concatenate(pieces, axis=1)
    if kadd:
        kr_ref, kc, ks = kadd_refs
        kr = kr_ref[...]
        piece = jnp.concatenate([jnp.zeros_like(kr), kr * kc[...] + pltpu.roll(kr, MLA_ROPE // 2, 1) * ks[...]], axis=1)
        r = r + jnp.concatenate([piece] * (tn // piece.shape[1]), axis=1)
    if scale is not None:
        r = r * scale
    o_ref[...] = r.astype(o_ref.dtype)


def _side_specs(side, n_steps, lin):
    rows, cols = side.shape
    nb = max(c for c in range(1, n_steps + 1) if rows % c == 0 and (rows // c) % 16 == 0)
    spec = pl.BlockSpec((rows // nb, cols), lambda *ids: (jnp.minimum(lin(*ids), nb - 1), 0))
    return spec, jax.ShapeDtypeStruct(side.shape, BF16)


def _mm(a3, w, *, out_dtype, tm, tn, n_row_tiles=None, w_rows=None, gain=None, bias=None, act=None, rope=None,
        kadd=None, scale=None, side=None, name="mm"):
    bsz, rows, k = a3.shape
    row0, n = (0, w.shape[1]) if w_rows is None else w_rows
    ni = rows // tm if n_row_tiles is None else n_row_tiles
    nj = n // tn
    assert n % tn == 0 and row0 % tn == 0 and (n_row_tiles is not None or rows % tm == 0)
    if w_rows is None:
        w_spec = pl.BlockSpec((k, tn), lambda bi, i, j: (0, j))
    else:
        w_spec = pl.BlockSpec((tn, k), lambda bi, i, j: (row0 // tn + j, 0))
    in_specs = [pl.BlockSpec((None, tm, k), lambda bi, i, j: (bi, i, 0)), w_spec]
    args = [a3, w]
    if gain is not None:
        in_specs.append(pl.BlockSpec((1, k), lambda bi, i, j: (0, 0)))
        args.append(gain.reshape(1, k).astype(F32))
    if bias is not None:
        in_specs.append(pl.BlockSpec((1, tn), lambda bi, i, j: (0, j)))
        args.append(bias.reshape(1, n).astype(F32))
    for t in (rope or ()) + (kadd or ()):
        in_specs.append(pl.BlockSpec((None, tm, t.shape[2]), lambda bi, i, j: (bi, i, 0)))
        args.append(t)
    out_specs = [pl.BlockSpec((None, tm, tn), lambda bi, i, j: (bi, i, j))]
    out_shape = [jax.ShapeDtypeStruct((bsz, ni * tm, n), out_dtype)]
    if side is not None:
        spec, shape = _side_specs(side, bsz * ni * nj, lambda bi, i, j: (bi * ni + i) * nj + j)
        in_specs.append(spec)
        args.append(side)
        out_specs.append(spec)
        out_shape.append(shape)
    body = functools.partial(_mm_body, has_gain=gain is not None, has_bias=bias is not None, act=act,
                             rope=rope is not None, kadd=kadd is not None, scale=scale, has_side=side is not None,
                             w_rows=w_rows is not None)
    outs = pl.pallas_call(
        body,
        grid=(bsz, ni, nj),
        in_specs=in_specs,
        out_specs=out_specs,
        out_shape=out_shape,
        compiler_params=_cparams(("parallel", "parallel", "arbitrary"), VMEM_BIG),
        name=name,
    )(*args)
    return outs[0] if side is None else outs


def _mm_t_body(a_ref, wt_ref, gain_ref, bias_ref, o_ref):
    af = a_ref[...].astype(F32)
    a = af * lax.rsqrt(jnp.mean(af * af, axis=-1, keepdims=True) + RMS_EPS) * gain_ref[...]
    r = lax.dot_general(wt_ref[...], a.astype(BF16), (((1,), (1,)), ((), ())), preferred_element_type=F32)
    o_ref[...] = (r + bias_ref[...]).astype(o_ref.dtype)


def _mm_t(a3, wt, gain, bias_col, *, out_dtype, tm, tn, name):
    bsz, rows, k = a3.shape
    n = wt.shape[0]
    return pl.pallas_call(
        _mm_t_body,
        grid=(bsz, rows // tm, n // tn),
        in_specs=[pl.BlockSpec((None, tm, k), lambda bi, i, j: (bi, i, 0)),
                  pl.BlockSpec((tn, k), lambda bi, i, j: (j, 0)),
                  pl.BlockSpec((1, k), lambda bi, i, j: (0, 0)),
                  pl.BlockSpec((tn, 1), lambda bi, i, j: (j, 0))],
        out_specs=pl.BlockSpec((None, tn, tm), lambda bi, i, j: (bi, j, i)),
        out_shape=jax.ShapeDtypeStruct((bsz, n, rows), out_dtype),
        compiler_params=_cparams(("parallel", "parallel", "arbitrary"), VMEM_BIG),
        name=name,
    )(a3, wt, gain.reshape(1, k).astype(F32), bias_col.reshape(n, 1).astype(F32))


def _mmk_body(*refs, n_pairs, has_gate, has_res, res_scale, has_side):
    it = iter(refs)
    a_refs = [next(it) for _ in range(n_pairs)]
    w_refs = [next(it) for _ in range(n_pairs)]
    g_refs = [next(it) for _ in range(n_pairs)] if has_gate else None
    res_ref = next(it) if has_res else None
    side_in = next(it) if has_side else None
    o_ref = next(it)
    if has_side:
        next(it)[...] = side_in[...].astype(BF16)
    acc_refs = [next(it) for _ in range(n_pairs)]
    kk = pl.program_id(3)

    @pl.when(kk == 0)
    def _():
        for acc in acc_refs:
            acc[...] = jnp.zeros_like(acc)

    for a_ref, w_ref, acc in zip(a_refs, w_refs, acc_refs):
        acc[...] += jnp.dot(a_ref[...], w_ref[...], preferred_element_type=F32)

    @pl.when(kk == pl.num_programs(3) - 1)
    def _():
        r = None
        for p, acc in enumerate(acc_refs):
            t = acc[...]
            if has_gate:
                t = t * g_refs[p][...].astype(F32)
            r = t if r is None else r + t
        if has_res:
            r = r + res_scale * res_ref[...]
        o_ref[...] = r.astype(o_ref.dtype)


def _mm_ktiled(a_list, w_list, *, out_dtype, rows, tm, tn, nk, gates=None, res=None, res_scale=1.0, side=None,
               name="mmk"):
    bsz = a_list[0].shape[0]
    n = w_list[0].shape[1]
    npairs = len(a_list)
    tks = [a.shape[2] // nk for a in a_list]
    assert all(a.shape[2] == tk * nk and tk % LANES == 0 for a, tk in zip(a_list, tks))
    o_spec = pl.BlockSpec((None, tm, tn), lambda bi, i, j, kk: (bi, i, j))
    in_specs = ([pl.BlockSpec((None, tm, tk), lambda bi, i, j, kk: (bi, i, kk)) for tk in tks]
                + [pl.BlockSpec((tk, tn), lambda bi, i, j, kk: (kk, j)) for tk in tks])
    args = list(a_list) + list(w_list)
    if gates is not None:
        for p in range(npairs):
            in_specs.append(pl.BlockSpec((None, tm, tn), lambda bi, i, j, kk, p=p: (bi, i, p * (n // tn) + j)))
            args.append(gates)
    if res is not None:
        in_specs.append(o_spec)
        args.append(res)
    ni, nj = rows // tm, n // tn
    out_specs = [o_spec]
    out_shape = [jax.ShapeDtypeStruct((bsz, rows, n), out_dtype)]
    if side is not None:
        spec, shape = _side_specs(side, bsz * ni * nj * nk, lambda bi, i, j, kk: ((bi * ni + i) * nj + j) * nk + kk)
        in_specs.append(spec)
        args.append(side)
        out_specs.append(spec)
        out_shape.append(shape)
    body = functools.partial(_mmk_body, n_pairs=npairs, has_gate=gates is not None, has_res=res is not None,
                             res_scale=res_scale, has_side=side is not None)
    outs = pl.pallas_call(
        body,
        grid=(bsz, ni, nj, nk),
        in_specs=in_specs,
        out_specs=out_specs,
        out_shape=out_shape,
        scratch_shapes=[pltpu.VMEM((tm, tn), F32) for _ in range(npairs)],
        compiler_params=_cparams(("parallel", "parallel", "parallel", "arbitrary"), VMEM_BIG),
        name=name,
    )(*args)
    return outs[0] if side is None else outs


def _silu(x):
    return x * jax.nn.sigmoid(x)


def _ssd_body(xs_ref, b_ref, c_ref, z_ref, dt_ref, wx_ref, wb_ref, wc_ref, bx_ref, bb_ref, bc_ref,
              dtb_ref, alog_ref, dskip_ref, ng_ref, e_ref, o_ref,
              state_ref, extx_ref, extb_ref, extc_ref, y_ref, *, hg):
    c = pl.program_id(2)
    q = CHUNK
    p = SSM_HEAD_DIM
    first = c == 0
    row = lax.broadcasted_iota(I32, (q, 1), 0)
    live = jnp.logical_or(jnp.logical_not(first), row >= META_PAD)

    @pl.when(first)
    def _():
        state_ref[...] = jnp.zeros_like(state_ref)
        extx_ref[0:8, :] = jnp.zeros((8, extx_ref.shape[1]), F32)
        extb_ref[0:8, :] = jnp.zeros((8, extb_ref.shape[1]), F32)
        extc_ref[0:8, :] = jnp.zeros((8, extc_ref.shape[1]), F32)

    def conv(ext_ref, u_ref, w_ref, bias_ref):
        @pl.when(first)
        def _():
            ext_ref[8:8 + q, :] = jnp.where(live, u_ref[...].astype(F32), 0.0)

        @pl.when(jnp.logical_not(first))
        def _():
            ext_ref[8:8 + q, :] = u_ref[...].astype(F32)

        acc = bias_ref[...]
        for k in range(SSM_CONV):
            acc = acc + ext_ref[pl.ds(8 - (SSM_CONV - 1) + k, q), :] * w_ref[k:k + 1, :]
        ext_ref[0:8, :] = ext_ref[q:q + 8, :]
        return _silu(acc)

    xs = conv(extx_ref, xs_ref, wx_ref, bx_ref)
    bm = conv(extb_ref, b_ref, wb_ref, bb_ref)
    cm = conv(extc_ref, c_ref, wc_ref, bc_ref)

    x = dt_ref[...] + dtb_ref[...]
    dt = jnp.maximum(x, 0.0) + jnp.log1p(jnp.exp(-jnp.abs(x)))
    dt = jnp.where(live, dt, 0.0)
    a = -jnp.exp(alog_ref[...])
    r_i = lax.broadcasted_iota(I32, (q, q), 0)
    c_i = lax.broadcasted_iota(I32, (q, q), 1)
    causal = r_i >= c_i
    hi = lax.Precision.HIGHEST
    a_cs = jnp.dot(causal.astype(F32), dt * a, precision=hi, preferred_element_type=F32)
    a_cs_t = a_cs.T
    ea = jnp.exp(a_cs)
    decay_end = jnp.exp(a_cs[q - 1:q, :] - a_cs)
    e = e_ref[...]

    def spread(v):
        v_hi = v.astype(BF16)
        v_lo = (v - v_hi.astype(F32)).astype(BF16)
        return jnp.dot(v_hi, e, preferred_element_type=F32) + jnp.dot(v_lo, e, preferred_element_type=F32)

    dt_x = spread(dt)
    ea_x = spread(ea)
    de_x = spread(decay_end)

    xdt = xs * dt_x
    xdt_b = xdt.astype(BF16)
    cm_b = cm.astype(BF16)
    bm_b = bm.astype(BF16)
    cb = lax.dot_general(cm_b, bm_b, (((1,), (1,)), ((), ())), preferred_element_type=F32)
    prev = state_ref[...]
    y_off = jnp.dot(cm_b, prev.astype(BF16), preferred_element_type=F32) * ea_x
    def decay_scores(j):
        seg = a_cs[:, j:j + 1] - a_cs_t[j:j + 1, :]
        return (jnp.exp(jnp.where(causal, seg, NEG)) * cb).astype(BF16)

    m_next = decay_scores(0)
    for j in range(hg):
        m = m_next
        if j + 1 < hg:
            m_next = decay_scores(j + 1)
        y_ref[:, j * p:(j + 1) * p] = jnp.dot(m, xdt_b[:, j * p:(j + 1) * p], preferred_element_type=F32)
    y = y_ref[...] + y_off + xs * dskip_ref[...]
    state_ref[...] = prev * ea_x[q - 1:q, :] + jnp.dot(bm.T.astype(BF16), (xdt * de_x).astype(BF16),
                                                        preferred_element_type=F32)
    yz = y * _silu(z_ref[...].astype(F32))
    yn = yz * lax.rsqrt(jnp.mean(yz * yz, axis=-1, keepdims=True) + RMS_EPS) * ng_ref[...]
    o_ref[...] = yn.astype(o_ref.dtype)


def _ssd(xbc3, z3, dt_g, conv_w, conv_b, dtb_g, alog_g, dskip_x, norm_g, d_inner, heads):
    bsz, rows, conv_dim = xbc3.shape
    g, n, p, q = SSM_GROUPS, SSM_STATE, SSM_HEAD_DIM, CHUNK
    hg = heads // g
    gw = hg * p
    nc = rows // q
    assert gw % LANES == 0 and d_inner % n == 0 and hg <= LANES
    boff = d_inner // n
    expand = (jnp.arange(LANES)[:, None] == (jnp.arange(gw) // p)[None, :]).astype(BF16)

    def chunk(c):
        return (c + nc - 1) % nc

    in_specs = [
        pl.BlockSpec((None, q, gw), lambda b, gi, c: (b, chunk(c), gi)),
        pl.BlockSpec((None, q, n), lambda b, gi, c: (b, chunk(c), boff + gi)),
        pl.BlockSpec((None, q, n), lambda b, gi, c: (b, chunk(c), boff + g + gi)),
        pl.BlockSpec((None, q, gw), lambda b, gi, c: (b, chunk(c), gi)),
        pl.BlockSpec((None, None, q, LANES), lambda b, gi, c: (b, gi, chunk(c), 0)),
        pl.BlockSpec((SSM_CONV, gw), lambda b, gi, c: (0, gi)),
        pl.BlockSpec((SSM_CONV, n), lambda b, gi, c: (0, boff + gi)),
        pl.BlockSpec((SSM_CONV, n), lambda b, gi, c: (0, boff + g + gi)),
        pl.BlockSpec((1, gw), lambda b, gi, c: (0, gi)),
        pl.BlockSpec((1, n), lambda b, gi, c: (0, boff + gi)),
        pl.BlockSpec((1, n), lambda b, gi, c: (0, boff + g + gi)),
        pl.BlockSpec((None, 1, LANES), lambda b, gi, c: (gi, 0, 0)),
        pl.BlockSpec((None, 1, LANES), lambda b, gi, c: (gi, 0, 0)),
        pl.BlockSpec((1, gw), lambda b, gi, c: (0, gi)),
        pl.BlockSpec((1, gw), lambda b, gi, c: (0, gi)),
        pl.BlockSpec((LANES, gw), lambda b, gi, c: (0, 0)),
    ]
    return pl.pallas_call(
        functools.partial(_ssd_body, hg=hg),
        grid=(bsz, g, nc),
        in_specs=in_specs,
        out_specs=pl.BlockSpec((None, q, gw), lambda b, gi, c: (b, chunk(c), gi)),
        out_shape=jax.ShapeDtypeStruct((bsz, rows, d_inner), BF16),
        scratch_shapes=[pltpu.VMEM((n, gw), F32), pltpu.VMEM((q + 8, gw), F32), pltpu.VMEM((q + 8, n), F32),
                        pltpu.VMEM((q + 8, n), F32), pltpu.VMEM((q, gw), F32)],
        compiler_params=_cparams(("parallel", "parallel", "arbitrary")),
        name="ssd",
    )(xbc3, xbc3, xbc3, z3, dt_g, conv_w, conv_w, conv_w, conv_b, conv_b, conv_b, dtb_g, alog_g,
      dskip_x, norm_g, expand)


def _attn_body(qi_ref, ki_ref, q_ref, km_ref, vm_ref, k_ref, v_ref, *rest, heads, n_side):
    side_in, o_ref, side_out = rest[:n_side], rest[n_side], rest[n_side + 1:2 * n_side + 1]
    m_ref, acc_ref = rest[2 * n_side + 1:]
    for src, dst in zip(side_in, side_out):
        dst[...] = src[...].astype(BF16)
    step_id = pl.program_id(2)
    qi, ki = qi_ref[step_id], ki_ref[step_id]

    @pl.when(ki < 0)
    def _():
        m_ref[...] = jnp.full_like(m_ref, NEG)
        acc_ref[...] = jnp.zeros_like(acc_ref)

    def step(kr, vr, mask_fn):
        def scores(h):
            qh = q_ref[:, h * MLA_QKP:(h + 1) * MLA_QKP]
            kh = kr[:, h * MLA_QKP:(h + 1) * MLA_QKP]
            st = lax.dot_general(kh, qh, (((1,), (1,)), ((), ())), preferred_element_type=F32)
            return st if mask_fn is None else mask_fn(st)

        def probs(h, st):
            m_prev = m_ref[h]
            m_new = jnp.maximum(m_prev, jnp.max(st, axis=0, keepdims=True))
            m_ref[h] = m_new
            return jnp.exp2(m_prev - m_new), jnp.exp2((st - m_new).astype(BF16))

        def accumulate(h, alpha, pt):
            sl = slice(h * MLA_VP, (h + 1) * MLA_VP)
            acc_ref[sl, :] = acc_ref[sl, :] * alpha + jnp.dot(vr[sl, :], pt, preferred_element_type=F32)

        st_q, pr_q = {}, {}
        for stage in range(heads + 2):
            if stage < heads:
                st_q[stage] = scores(stage)
            if 0 <= stage - 1 < heads:
                pr_q[stage - 1] = probs(stage - 1, st_q.pop(stage - 1))
            if 0 <= stage - 2 < heads:
                accumulate(stage - 2, *pr_q.pop(stage - 2))

    def meta_mask(st):
        key = lax.broadcasted_iota(I32, st.shape, 0)
        return jnp.where(key >= META_PAD, st, NEG)

    def causal_mask(st):
        key = lax.broadcasted_iota(I32, st.shape, 0)
        qry = lax.broadcasted_iota(I32, st.shape, 1)
        return jnp.where(key <= qry, st, NEG)

    @pl.when(ki < 0)
    def _():
        step(km_ref, vm_ref, meta_mask)

    @pl.when(jnp.logical_and(ki >= 0, ki < qi))
    def _():
        step(k_ref, v_ref, None)

    @pl.when(ki == qi)
    def _():
        step(k_ref, v_ref, causal_mask)
        for h in range(heads):
            num = acc_ref[h * MLA_VP:h * MLA_VP + MLA_V, :]
            den = acc_ref[h * MLA_VP + MLA_V:h * MLA_VP + MLA_V + 1, :]
            o_ref[:, h * MLA_V:(h + 1) * MLA_V] = (num / den).T.astype(o_ref.dtype)


def _attention(q3, k3, vt3, seq, sides=()):
    bsz = q3.shape[0]
    hp = 8
    t = _pick(seq, (512, 256, 128))
    nq = seq // t
    meta_blk = seq // CHUNK
    qw, vw = hp * MLA_QKP, hp * MLA_VP
    qi_tab = jnp.asarray([qi for qi in range(nq) for _ in range(qi + 2)], I32)
    ki_tab = jnp.asarray([ki for qi in range(nq) for ki in range(-1, qi + 1)], I32)

    def q_idx(b, h, s, qt, kt):
        return (b, qt[s], h)

    def k_idx(b, h, s, qt, kt):
        return (b, jnp.maximum(kt[s], 0), h)

    def vt_idx(b, h, s, qt, kt):
        return (b, h, jnp.maximum(kt[s], 0))

    nh, npairs = MLA_HEADS // hp, int(qi_tab.shape[0])
    side_specs = [_side_specs(s_arr, bsz * nh * npairs, lambda b, h, s, qt, kt: (b * nh + h) * npairs + s)
                  for s_arr in sides]
    grid_spec = pltpu.PrefetchScalarGridSpec(
        num_scalar_prefetch=2,
        grid=(bsz, nh, npairs),
        in_specs=[
            pl.BlockSpec((None, t, qw), q_idx),
            pl.BlockSpec((None, CHUNK, qw), lambda b, h, s, qt, kt: (b, meta_blk, h)),
            pl.BlockSpec((None, vw, CHUNK), lambda b, h, s, qt, kt: (b, h, meta_blk)),
            pl.BlockSpec((None, t, qw), k_idx),
            pl.BlockSpec((None, vw, t), vt_idx),
        ] + [spec for spec, _ in side_specs],
        out_specs=[pl.BlockSpec((None, t, hp * MLA_V), q_idx)] + [spec for spec, _ in side_specs],
        scratch_shapes=[pltpu.VMEM((hp, 1, t), F32), pltpu.VMEM((vw, t), F32)],
    )
    return pl.pallas_call(
        functools.partial(_attn_body, heads=hp, n_side=len(sides)),
        grid_spec=grid_spec,
        out_shape=[jax.ShapeDtypeStruct((bsz, seq, MLA_HEADS * MLA_V), BF16)] + [shape for _, shape in side_specs],
        compiler_params=_cparams(("parallel", "parallel", "arbitrary")),
        name="mla_attention",
    )(qi_tab, ki_tab, q3, k3, vt3, k3, vt3, *sides)


def _router_body(h_ref, wr_ref, rb_ref, e_ref, pos_ref, w_ref, cnt_ref, carry_ref):
    i = pl.program_id(0)
    ne, ng = N_EXPERTS, N_EXPERT_GROUPS
    gs = ne // ng
    tm = h_ref.shape[0]

    @pl.when(i == 0)
    def _():
        carry_ref[...] = jnp.zeros_like(carry_ref)

    logits = lax.dot_general(wr_ref[...], h_ref[...], (((1,), (1,)), ((), ())), precision=lax.Precision.HIGHEST,
                             preferred_element_type=F32)
    scores = jax.nn.sigmoid(logits)
    choice = scores + rb_ref[...]
    sub = lax.broadcasted_iota(I32, (gs, tm), 0)
    grp_rows = []
    for g in range(ng):
        blk = choice[g * gs:(g + 1) * gs, :]
        m1 = jnp.max(blk, axis=0, keepdims=True)
        first = jnp.min(jnp.where(blk == m1, sub, gs), axis=0, keepdims=True)
        m2 = jnp.max(jnp.where(sub == first, -jnp.inf, blk), axis=0, keepdims=True)
        grp_rows.append(m1 + m2)
    grp = jnp.concatenate(grp_rows, axis=0)
    grank = jnp.zeros((ng, tm), I32)
    gidx = lax.broadcasted_iota(I32, (ng, tm), 0)
    for g in range(ng):
        rowv = grp[g:g + 1, :]
        beats = jnp.logical_or(rowv > grp, jnp.logical_and(rowv == grp, gidx > g))
        grank = grank + beats.astype(I32)
    gsel = (grank < TOPK_GROUPS).astype(F32)
    esel = jnp.concatenate([jnp.broadcast_to(gsel[g:g + 1, :], (gs, tm)) for g in range(ng)], axis=0)
    masked = jnp.where(esel > 0.0, choice, -jnp.inf)
    eidx = lax.broadcasted_iota(I32, (ne, tm), 0)
    rank = jnp.zeros((ne, tm), I32)
    for e in range(ne):
        rowv = masked[e:e + 1, :]
        beats = jnp.logical_or(rowv > masked, jnp.logical_and(rowv == masked, eidx > e))
        rank = rank + beats.astype(I32)
    top = jnp.logical_and(rank < TOP_K, esel > 0.0)
    topf = top.astype(F32)
    wsel = jnp.where(top, scores, 0.0)
    wn = wsel / jnp.sum(wsel, axis=0, keepdims=True) * ROUTED_SCALE
    r_i = lax.broadcasted_iota(I32, (tm, tm), 0)
    c_i = lax.broadcasted_iota(I32, (tm, tm), 1)
    before = (r_i < c_i).astype(BF16)
    pos = jnp.dot(topf.astype(BF16), before, preferred_element_type=F32) + carry_ref[:, :1]
    carry_ref[...] = carry_ref[...] + jnp.sum(topf, axis=1, keepdims=True)
    cnt_ref[...] = carry_ref[...]
    e_rows, p_rows, w_rows = [], [], []
    for k in range(TOP_K):
        hit = rank == k
        e_rows.append(jnp.sum(jnp.where(hit, eidx, 0), axis=0, keepdims=True))
        p_rows.append(jnp.sum(jnp.where(hit, pos, 0.0), axis=0, keepdims=True))
        w_rows.append(jnp.sum(jnp.where(hit, wn, 0.0), axis=0, keepdims=True))
    e_ref[...] = jnp.concatenate(e_rows, axis=0)
    pos_ref[...] = jnp.concatenate(p_rows, axis=0).astype(I32)
    w_ref[...] = jnp.concatenate(w_rows, axis=0)


def _router(h2, w_router, router_bias):
    t, d = h2.shape
    tm = _pick(t, (256, 128))
    slot = pl.BlockSpec((TOP_K, tm), lambda i: (0, i))
    rb = jnp.broadcast_to(router_bias.astype(F32)[:, None], (N_EXPERTS, tm))
    return pl.pallas_call(
        _router_body,
        grid=(t // tm,),
        in_specs=[pl.BlockSpec((tm, d), lambda i: (i, 0)), pl.BlockSpec((N_EXPERTS, d), lambda i: (0, 0)),
                  pl.BlockSpec((N_EXPERTS, tm), lambda i: (0, 0))],
        out_specs=[slot, slot, slot, pl.BlockSpec((N_EXPERTS, LANES), lambda i: (0, 0))],
        out_shape=[jax.ShapeDtypeStruct((TOP_K, t), I32), jax.ShapeDtypeStruct((TOP_K, t), I32),
                   jax.ShapeDtypeStruct((TOP_K, t), F32), jax.ShapeDtypeStruct((N_EXPERTS, LANES), F32)],
        scratch_shapes=[pltpu.VMEM((N_EXPERTS, LANES), F32)],
        compiler_params=_cparams(("arbitrary",)),
        name="moe_router",
    )(h2, w_router.T.astype(F32), rb)


def _load_slots(dest_hbm, dest_ref, sem):
    cp = pltpu.make_async_copy(dest_hbm.at[pl.program_id(0)], dest_ref, sem)
    cp.start()
    cp.wait()


def _dispatch_body(cnt_ref, start_ref, dest_hbm, x_ref, xs_ref, dest_ref, zero_ref, sem_ref, *, block_rows):
    i = pl.program_id(0)
    tm = x_ref.shape[0]
    _load_slots(dest_hbm, dest_ref, sem_ref.at[2])

    def row_copy(tok, k):
        return pltpu.make_async_copy(x_ref.at[pl.ds(tok, 1), :], xs_ref.at[pl.ds(dest_ref[k, tok], 1), :], sem_ref.at[0])

    def issue(tok, carry):
        for k in range(TOP_K):
            row_copy(tok, k).start(priority=k % 2)
        return carry

    lax.fori_loop(0, tm, issue, 0)

    @pl.when(i == 0)
    def _():
        zero_ref[...] = jnp.zeros_like(zero_ref)

        def fill(e, carry):
            cnt = cnt_ref[e]
            padded = (cnt + block_rows - 1) // block_rows * block_rows
            base = start_ref[e] + cnt

            def zcopy(r):
                return pltpu.make_async_copy(zero_ref.at[pl.ds(0, 1), :], xs_ref.at[pl.ds(base + r, 1), :], sem_ref.at[1])

            def zstart(r, c2):
                zcopy(r).start()
                return c2

            def zwait(r, c2):
                zcopy(r).wait()
                return c2

            lax.fori_loop(0, padded - cnt, zstart, 0)
            lax.fori_loop(0, padded - cnt, zwait, 0)
            return carry

        lax.fori_loop(0, N_EXPERTS, fill, 0)

    def drain(tok, carry):
        for k in range(TOP_K):
            row_copy(tok, k).wait()
        return carry

    lax.fori_loop(0, tm, drain, 0)


def _slot_tiles(dest, tm):
    k, t = dest.shape
    return dest.reshape(k, t // tm, tm).transpose(1, 0, 2)


def _dispatch(hp2, dest, counts, starts, n_rows):
    t, w = hp2.shape
    tm = _pick(t, (256, 128))
    grid_spec = pltpu.PrefetchScalarGridSpec(
        num_scalar_prefetch=2,
        grid=(t // tm,),
        in_specs=[pl.BlockSpec(memory_space=pl.ANY),
                  pl.BlockSpec((tm, w), lambda i, c, s: (i, 0))],
        out_specs=pl.BlockSpec(memory_space=pl.ANY),
        scratch_shapes=[pltpu.SMEM((TOP_K, tm), I32), pltpu.VMEM((8, w), U32), pltpu.SemaphoreType.DMA((3,))],
    )
    return pl.pallas_call(
        functools.partial(_dispatch_body, block_rows=MOE_ROWS),
        grid_spec=grid_spec,
        out_shape=jax.ShapeDtypeStruct((n_rows, w), U32),
        compiler_params=pltpu.CompilerParams(dimension_semantics=("arbitrary",)),
        name="moe_dispatch",
    )(counts, starts, _slot_tiles(dest, tm), hp2)


def _expert_body(be_ref, nu_ref, x_ref, wg_ref, wu_ref, wd_ref, o_ref):
    j = pl.program_id(0)

    @pl.when(j < nu_ref[0])
    def _():
        half = x_ref.shape[1]
        lo, hi = _unpack_halves(x_ref[...])
        lo, hi = lo.astype(BF16), hi.astype(BF16)

        def proj(w_ref):
            return (jnp.dot(lo, w_ref[:half, :], preferred_element_type=F32)
                    + jnp.dot(hi, w_ref[half:, :], preferred_element_type=F32))

        hmid = (_silu(proj(wg_ref)) * proj(wu_ref)).astype(BF16)
        o_ref[...] = _pack_halves(jnp.dot(hmid, wd_ref[...], preferred_element_type=F32))


def _experts(xs, wg, wu, wd, block_e, n_used):
    rows, w = xs.shape
    _, d, ff = wg.shape
    bm = MOE_ROWS
    nb = rows // bm

    def row_idx(j, be, nu):
        return (jnp.minimum(j, nu[0] - 1), 0)

    def w_idx(j, be, nu):
        return (be[jnp.minimum(j, nu[0] - 1)], 0, 0)

    grid_spec = pltpu.PrefetchScalarGridSpec(
        num_scalar_prefetch=2,
        grid=(nb,),
        in_specs=[pl.BlockSpec((bm, w), row_idx),
                  pl.BlockSpec((None, d, ff), w_idx), pl.BlockSpec((None, d, ff), w_idx),
                  pl.BlockSpec((None, ff, d), w_idx)],
        out_specs=pl.BlockSpec((bm, w), row_idx),
    )
    return pl.pallas_call(
        _expert_body,
        grid_spec=grid_spec,
        out_shape=jax.ShapeDtypeStruct((rows, w), U32),
        compiler_params=_cparams(("arbitrary",), 60 * 1024 * 1024),
        name="moe_experts",
    )(block_e, n_used, xs, wg, wu, wd)


def _combine_body(dest_hbm, wt_ref, h_ref, ysh_ref, g_ref, b_ref, y_ref, o_ref, dest_ref, gbuf_ref, sem_ref, *, alpha):
    tm = h_ref.shape[0]
    _load_slots(dest_hbm, dest_ref, sem_ref.at[1])

    def row_copy(tok, k):
        return pltpu.make_async_copy(y_ref.at[pl.ds(dest_ref[k, tok], 1), :], gbuf_ref.at[k, pl.ds(tok, 1), :],
                                     sem_ref.at[0])

    def issue(tok, carry):
        for k in range(TOP_K):
            row_copy(tok, k).start(priority=k % 2)
        return carry

    def drain(tok, carry):
        for k in range(TOP_K):
            row_copy(tok, k).wait()
        return carry

    lax.fori_loop(0, tm, issue, 0)
    lax.fori_loop(0, tm, drain, 0)
    lo, hi = _unpack_halves(ysh_ref[...])
    wt = wt_ref[...]
    for k in range(TOP_K):
        glo, ghi = _unpack_halves(gbuf_ref[k])
        wk = wt[:, k:k + 1]
        lo = lo + glo * wk
        hi = hi + ghi * wk
    x = alpha * h_ref[...] + jnp.concatenate([lo, hi], axis=1)
    mu = jnp.mean(x, axis=-1, keepdims=True)
    xc = x - mu
    var = jnp.mean(xc * xc, axis=-1, keepdims=True)
    o_ref[...] = xc * lax.rsqrt(var + LN_EPS) * g_ref[...] + b_ref[...]


def _combine(h2, ysh, y_sorted, dest, wt, g, b, alpha):
    t, d = h2.shape
    w = d // 2
    tm = _pick(t, (128,))
    vec = pl.BlockSpec((1, d), lambda i: (0, 0))
    return pl.pallas_call(
        functools.partial(_combine_body, alpha=alpha),
        grid=(t // tm,),
        in_specs=[pl.BlockSpec(memory_space=pl.ANY),
                  pl.BlockSpec((tm, TOP_K), lambda i: (i, 0)),
                  pl.BlockSpec((tm, d), lambda i: (i, 0)),
                  pl.BlockSpec((tm, w), lambda i: (i, 0)),
                  vec, vec,
                  pl.BlockSpec(memory_space=pl.ANY)],
        out_specs=pl.BlockSpec((tm, d), lambda i: (i, 0)),
        out_shape=jax.ShapeDtypeStruct((t, d), F32),
        scratch_shapes=[pltpu.SMEM((TOP_K, tm), I32), pltpu.VMEM((TOP_K, tm, w), U32), pltpu.SemaphoreType.DMA((2,))],
        compiler_params=_cparams(("arbitrary",)),
        name="moe_combine",
    )(_slot_tiles(dest, tm), wt, h2, ysh, g.reshape(1, d), b.reshape(1, d), y_sorted)


def _rope_tables(pos):
    inv_freq = ROPE_THETA ** (-jnp.arange(0, MLA_ROPE, 2, dtype=F32) / MLA_ROPE)
    ang = pos.astype(F32)[..., None] * inv_freq
    return jnp.cos(ang), jnp.sin(ang)


def kernel(x, positions, meta_tokens, ln_in_g, ln_in_b, w_in, b_gate, conv_w, conv_b, dt_bias, a_log, d_skip,
           ssm_norm_g, w_ssm_proj, q_a_norm_g, w_q_b, kv_a_norm_g, w_kv_b, w_attn_proj, w_out, ln1_g, ln1_b,
           w_router, router_bias, w_exp_gate, w_exp_up, w_exp_down, w_sh_gate, w_sh_up, w_sh_down, ln2_g, ln2_b):
    bsz, seq, d = x.shape
    depth = w_in.shape[0]
    heads = dt_bias.shape[-1]
    d_inner = w_ssm_proj.shape[1]
    conv_dim = conv_w.shape[-1]
    q_rank = w_q_b.shape[1]
    kv_rank = w_kv_b.shape[1]
    hh = MLA_HEADS
    g = SSM_GROUPS
    hg = heads // g
    assert seq % CHUNK == 0 and d % (2 * LANES) == 0
    lp = seq + CHUNK
    t = bsz * seq
    alpha = (2.0 * depth) ** 0.25

    meta = jnp.broadcast_to(meta_tokens[None].astype(x.dtype), (bsz, N_META, d))
    hcat = jnp.concatenate([x, jnp.zeros((bsz, META_PAD, d), x.dtype), meta], axis=1)
    pos = jnp.concatenate([positions.astype(I32) + N_META, jnp.zeros((bsz, META_PAD), I32),
                           jnp.broadcast_to(jnp.arange(N_META, dtype=I32), (bsz, N_META))], axis=1)
    cos, sin = _rope_tables(pos)
    zr = jnp.zeros_like(cos)
    rope_c = jnp.concatenate([cos, cos, zr, zr], axis=-1)[:, :seq]
    rope_slo = jnp.concatenate([-sin, zr, zr, zr], axis=-1)[:, :seq]
    rope_shi = jnp.concatenate([zr, sin, zr, zr], axis=-1)[:, :seq]
    k_c = jnp.concatenate([cos, cos, zr, zr], axis=-1)
    k_s = jnp.concatenate([-sin, sin, zr, zr], axis=-1)

    h_f32, h_b16 = _layer_norm(hcat, ln_in_g, ln_in_b)
    tm_all = _pick(lp, (1408, 1152, 1024, 896, 768, 640, 512, 384, 256, 128))
    tm_seq = _pick(seq, (1024, 512, 256, 128))

    out = None
    for l in range(depth):
        offs = [0]
        for wdt in (d_inner, conv_dim, heads, q_rank, kv_rank, MLA_ROPE, 2 * d):
            offs.append(offs[-1] + wdt)
        wt = jnp.swapaxes(w_in[l], 0, 1)
        w_kr = wt[offs[5]:offs[6]]
        small_cols = q_rank + kv_rank + 2 * MLA_ROPE + heads
        small_n = small_cols + (-small_cols) % 256
        wt_small = jnp.concatenate([wt[offs[3]:offs[4]], wt[offs[4]:offs[5]], w_kr, w_kr, wt[offs[2]:offs[3]],
                                    jnp.zeros((small_n - small_cols, d), F32)], axis=0)
        wt_g = wt[offs[6]:offs[7]]

        n_exp, _, ff = w_exp_gate[l].shape
        z3 = _mm(h_b16, wt, w_rows=(offs[0], d_inner), out_dtype=BF16, tm=tm_all, tn=512, name="in_proj_z")
        xbc3 = _mm(h_b16, wt, w_rows=(offs[1], conv_dim), out_dtype=BF16, tm=tm_all, tn=512, name="in_proj_xbc")
        small = _mm(h_b16, wt_small, w_rows=(0, small_n), out_dtype=F32, tm=tm_all, tn=256, name="in_proj_small")
        gates = _mm(h_b16, wt_g, w_rows=(0, 2 * d), out_dtype=BF16, tm=tm_seq, tn=512, n_row_tiles=seq // tm_seq,
                    bias=b_gate[l], act="sigmoid", name="in_proj_gates")
        o = 0
        q_a = small[:, :, o:o + q_rank]; o += q_rank
        kv_a = small[:, :, o:o + kv_rank]; o += kv_rank
        kr2 = small[:, :, o:o + 2 * MLA_ROPE]; o += 2 * MLA_ROPE
        dt_raw = small[:, :, o:o + heads]

        dt_g = jnp.pad(dt_raw.reshape(bsz, lp, g, hg).transpose(0, 2, 1, 3), ((0, 0), (0, 0), (0, 0), (0, LANES - hg)))
        pad_h = lambda v: jnp.pad(v.astype(F32).reshape(g, 1, hg), ((0, 0), (0, 0), (0, LANES - hg)))
        y_ssm = _ssd(xbc3, z3, dt_g, conv_w[l].astype(F32), conv_b[l].reshape(1, conv_dim).astype(F32),
                     pad_h(dt_bias[l]), pad_h(a_log[l]),
                     jnp.repeat(d_skip[l].astype(F32), SSM_HEAD_DIM).reshape(1, d_inner),
                     ssm_norm_g[l].reshape(1, d_inner).astype(F32), d_inner, heads)

        qscale = (MLA_QK ** -0.5) * math.log2(math.e)
        head_pad = ((0, 0), (0, 0), (0, MLA_QKP - MLA_QK))
        w_q = jnp.pad(w_q_b[l].reshape(q_rank, hh, MLA_QK), head_pad).reshape(q_rank, hh * MLA_QKP).astype(BF16)
        q3 = _mm(q_a, w_q, out_dtype=BF16, tm=tm_seq, tn=4 * MLA_QKP, n_row_tiles=seq // tm_seq,
                 gain=q_a_norm_g[l] * qscale, rope=(rope_c, rope_slo, rope_shi), name="q_proj")
        wkv = w_kv_b[l].reshape(kv_rank, hh, MLA_NOPE + MLA_V)
        w_k = jnp.pad(wkv[:, :, :MLA_NOPE], ((0, 0), (0, 0), (0, MLA_QKP - MLA_NOPE)))
        w_k = w_k.reshape(kv_rank, hh * MLA_QKP).astype(BF16)
        w_vt = jnp.pad(wkv[:, :, MLA_NOPE:], ((0, 0), (0, 0), (0, MLA_VP - MLA_V)))
        w_vt = w_vt.reshape(kv_rank, hh * MLA_VP).T.astype(BF16)
        ones_rows = jnp.tile(jnp.concatenate([jnp.zeros((MLA_V,), F32), jnp.ones((MLA_VP - MLA_V,), F32)]), hh)
        k3 = _mm(kv_a, w_k, out_dtype=BF16, tm=tm_all, tn=4 * MLA_QKP, gain=kv_a_norm_g[l],
                 kadd=(kr2, k_c, k_s), name="k_proj")
        vt3 = _mm_t(kv_a, w_vt, kv_a_norm_g[l], ones_rows, out_dtype=BF16, tm=tm_all, tn=4 * MLA_VP,
                    name="v_proj")
        y_attn, wg_b, wu_b, wd_b = _attention(
            q3, k3, vt3, seq, [w_exp_gate[l].reshape(n_exp * d, ff), w_exp_up[l].reshape(n_exp * d, ff),
                               w_exp_down[l].reshape(n_exp * ff, d)])

        mixed = _mm_ktiled([y_ssm, y_attn], [w_ssm_proj[l].astype(BF16), w_attn_proj[l].astype(BF16)],
                           out_dtype=BF16, rows=seq, tm=tm_seq, tn=_pick(d, (1024, 512)), nk=4,
                           gates=gates, name="branch_proj")
        pre1 = _mm_ktiled([mixed], [w_out[l].astype(BF16)], out_dtype=F32, rows=seq, tm=tm_seq,
                          tn=_pick(d, (1024, 512)), nk=2, res=h_f32, res_scale=alpha, name="out_proj")
        h1, h1p = _layer_norm_pack(pre1.reshape(t, d), ln1_g[l], ln1_b[l])

        e_slot, pos_slot, w_slot, cnt = _router(h1, w_router[l], router_bias[l])
        counts = cnt[:, 0].astype(I32)
        padded = (counts + MOE_ROWS - 1) // MOE_ROWS * MOE_ROWS
        ends = jnp.cumsum(padded)
        starts = ends - padded
        onehot = e_slot[None] == jnp.arange(N_EXPERTS, dtype=I32)[:, None, None]
        dest = pos_slot + jnp.sum(jnp.where(onehot, starts[:, None, None], 0), axis=0)
        n_blocks = -(-(t * TOP_K) // MOE_ROWS) + N_EXPERTS
        blk0 = jnp.arange(n_blocks, dtype=I32) * MOE_ROWS
        block_e = jnp.minimum(jnp.sum((ends[None, :] <= blk0[:, None]).astype(I32), axis=1), N_EXPERTS - 1)
        n_used = (ends[-1] // MOE_ROWS).reshape(1)
        xs = _dispatch(h1p, dest, counts, starts, n_blocks * MOE_ROWS)
        y_sorted = _experts(xs, wg_b.reshape(n_exp, d, ff), wu_b.reshape(n_exp, d, ff), wd_b.reshape(n_exp, ff, d),
                            block_e, n_used)
        y_shared = _experts(h1p, w_sh_gate[l][None].astype(BF16), w_sh_up[l][None].astype(BF16),
                            w_sh_down[l][None].astype(BF16), jnp.zeros((t // MOE_ROWS,), I32),
                            jnp.full((1,), t // MOE_ROWS, I32))
        out = _combine(h1, y_shared, y_sorted, dest, w_slot.T, ln2_g[l], ln2_b[l], alpha)
        if l + 1 < depth:
            raise NotImplementedError("stacked layers need the meta rows carried through the channel mixer")
    return out.reshape(bsz, seq, d)
```

```python
import functools
import math

import jax
import jax.numpy as jnp
from jax import lax
from jax.experimental import pallas as pl
from jax.experimental.pallas import tpu as pltpu

F32 = jnp.float32
BF16 = jnp.bfloat16
U32 = jnp.uint32
I32 = jnp.int32

N_META = 16
CHUNK = 128
META_PAD = CHUNK - N_META
SSM_HEAD_DIM = 64
SSM_GROUPS = 8
SSM_STATE = 128
SSM_CONV = 4
MLA_HEADS = 64
MLA_NOPE = 128
MLA_ROPE = 64
MLA_V = 128
MLA_QK = MLA_NOPE + MLA_ROPE
ROPE_THETA = 10000.0
N_EXPERTS = 64
N_EXPERT_GROUPS = 8
TOPK_GROUPS = 4
TOP_K = 8
ROUTED_SCALE = 2.5
LN_EPS = 1e-5
RMS_EPS = 1e-6
NEG = -1e30
LANES = 128
MLA_QKP = 2 * LANES
MOE_ROWS = 256
MLA_VP = MLA_V + 16
VMEM_BIG = 56 * 1024 * 1024
VMEM_MID = 44 * 1024 * 1024


def _cparams(sem, vmem=VMEM_MID):
    return pltpu.CompilerParams(dimension_semantics=sem, vmem_limit_bytes=vmem)


def _pick(n, cands):
    for c in cands:
        if n % c == 0:
            return c
    raise ValueError(f"no tile for {n} in {cands}")


def _ln_body(x_ref, g_ref, b_ref, of_ref, ob_ref):
    x = x_ref[...]
    mu = jnp.mean(x, axis=-1, keepdims=True)
    xc = x - mu
    var = jnp.mean(xc * xc, axis=-1, keepdims=True)
    y = xc * lax.rsqrt(var + LN_EPS) * g_ref[...] + b_ref[...]
    of_ref[...] = y
    ob_ref[...] = y.astype(BF16)


def _ln_in_body(x_ref, meta_ref, g_ref, b_ref, of_ref, ob_ref, *, n_real):
    i = pl.program_id(1)

    @pl.when(i < n_real)
    def _():
        _ln_body(x_ref, g_ref, b_ref, of_ref, ob_ref)

    @pl.when(i == n_real)
    def _():
        _ln_body(meta_ref, g_ref, b_ref, of_ref, ob_ref)


def _layer_norm_in(x3, meta_chunk, g, b):
    bsz, seq, d = x3.shape
    tm = CHUNK
    n_real = seq // tm
    spec = pl.BlockSpec((None, tm, d), lambda bi, i: (bi, i, 0))
    vec = pl.BlockSpec((1, d), lambda bi, i: (0, 0))
    shape = (bsz, seq + tm, d)
    return pl.pallas_call(
        functools.partial(_ln_in_body, n_real=n_real),
        grid=(bsz, n_real + 1),
        in_specs=[pl.BlockSpec((None, tm, d), lambda bi, i: (bi, jnp.minimum(i, n_real - 1), 0)),
                  pl.BlockSpec((tm, d), lambda bi, i: (0, 0)), vec, vec],
        out_specs=[spec, spec],
        out_shape=[jax.ShapeDtypeStruct(shape, F32), jax.ShapeDtypeStruct(shape, BF16)],
        compiler_params=_cparams(("parallel", "arbitrary")),
        name="layer_norm",
    )(x3, meta_chunk, g.reshape(1, d), b.reshape(1, d))


def _pack_halves(y):
    n = y.shape[1] // 2
    lo = pltpu.bitcast(y[:, :n].astype(BF16).astype(F32), U32) >> 16
    hi = pltpu.bitcast(y[:, n:].astype(BF16).astype(F32), U32) & jnp.uint32(0xFFFF0000)
    return hi | lo


def _unpack_halves(w):
    lo = pltpu.bitcast(w << 16, F32)
    hi = pltpu.bitcast(w & jnp.uint32(0xFFFF0000), F32)
    return lo, hi


def _ln_pack_body(x_ref, g_ref, b_ref, of_ref, op_ref):
    x = x_ref[...]
    mu = jnp.mean(x, axis=-1, keepdims=True)
    xc = x - mu
    var = jnp.mean(xc * xc, axis=-1, keepdims=True)
    y = xc * lax.rsqrt(var + LN_EPS) * g_ref[...] + b_ref[...]
    of_ref[...] = y
    op_ref[...] = _pack_halves(y)


def _layer_norm_pack(x2, g, b):
    rows, d = x2.shape
    tm = _pick(rows, (256, 128))
    vec = pl.BlockSpec((1, d), lambda i: (0, 0))
    return pl.pallas_call(
        _ln_pack_body,
        grid=(rows // tm,),
        in_specs=[pl.BlockSpec((tm, d), lambda i: (i, 0)), vec, vec],
        out_specs=[pl.BlockSpec((tm, d), lambda i: (i, 0)), pl.BlockSpec((tm, d // 2), lambda i: (i, 0))],
        out_shape=[jax.ShapeDtypeStruct((rows, d), F32), jax.ShapeDtypeStruct((rows, d // 2), U32)],
        compiler_params=_cparams(("parallel",)),
        name="layer_norm_pack",
    )(x2, g.reshape(1, d), b.reshape(1, d))


def _mm_body(*refs, has_gain, has_bias, act, rope, kadd, has_side, w_rows):
    it = iter(refs)
    a_ref, w_ref = next(it), next(it)
    gain_ref = next(it) if has_gain else None
    bias_ref = next(it) if has_bias else None
    rope_refs = [next(it) for _ in range(3)] if rope else None
    kadd_refs = [next(it) for _ in range(3)] if kadd else None
    side_in = next(it) if has_side else None
    o_ref = next(it)
    if has_side:
        next(it)[...] = side_in[...].astype(BF16)
    a = a_ref[...]
    if has_gain:
        af = a.astype(F32)
        a = af * lax.rsqrt(jnp.mean(af * af, axis=-1, keepdims=True) + RMS_EPS) * gain_ref[...]
    if w_rows:
        r = lax.dot_general(a.astype(BF16), w_ref[...].astype(BF16), (((1,), (1,)), ((), ())),
                            preferred_element_type=F32)
    else:
        r = jnp.dot(a.astype(BF16), w_ref[...].astype(BF16), preferred_element_type=F32)
    tn = r.shape[1]
    if has_bias:
        r = r + bias_ref[...]
    if act == "sigmoid":
        r = jax.nn.sigmoid(r)
    if rope:
        c, slo, shi = (t[...] for t in rope_refs)
        half = MLA_ROPE // 2
        pieces = []
        for h0 in range(0, tn, MLA_QKP):
            rp = r[:, h0 + MLA_NOPE:h0 + MLA_QKP]
            pieces += [r[:, h0:h0 + MLA_NOPE], rp * c + pltpu.roll(rp, LANES - half, 1) * slo + pltpu.roll(rp, half, 1) * shi]
        r = jnp.concatenate(pieces, axis=1)
    if kadd:
        kr_ref, kc, ks = kadd_refs
        kr = kr_ref[...]
        piece = jnp.concatenate([jnp.zeros_like(kr), kr * kc[...] + pltpu.roll(kr, MLA_ROPE // 2, 1) * ks[...]], axis=1)
        r = r + jnp.concatenate([piece] * (tn // piece.shape[1]), axis=1)
    o_ref[...] = r.astype(o_ref.dtype)


def _side_specs(side, n_steps, lin):
    rows, cols = side.shape
    nb = max(c for c in range(1, n_steps + 1) if rows % c == 0 and (rows // c) % 16 == 0)
    spec = pl.BlockSpec((rows // nb, cols), lambda *ids: (jnp.minimum(lin(*ids), nb - 1), 0))
    return spec, jax.ShapeDtypeStruct(side.shape, BF16)


def _mm(a3, w, *, out_dtype, tm, tn, n_row_tiles=None, w_rows=None, gain=None, bias=None, act=None, rope=None,
        kadd=None, side=None, name="mm"):
    bsz, rows, k = a3.shape
    row0, n = (0, w.shape[1]) if w_rows is None else w_rows
    ni = rows // tm if n_row_tiles is None else n_row_tiles
    nj = n // tn
    assert n % tn == 0 and row0 % 8 == 0 and (n_row_tiles is not None or rows % tm == 0)
    if w_rows is None:
        w_spec = pl.BlockSpec((k, tn), lambda bi, i, j: (0, j))
    elif row0 % tn == 0:
        w_spec = pl.BlockSpec((tn, k), lambda bi, i, j: (row0 // tn + j, 0))
    else:
        w_spec = pl.BlockSpec((pl.Element(tn), pl.Element(k)),
                              lambda bi, i, j: (pl.multiple_of(row0 + j * tn, 8), 0))
    in_specs = [pl.BlockSpec((None, tm, k), lambda bi, i, j: (bi, i, 0)), w_spec]
    args = [a3, w]
    if gain is not None:
        in_specs.append(pl.BlockSpec((1, k), lambda bi, i, j: (0, 0)))
        args.append(gain.reshape(1, k).astype(F32))
    if bias is not None:
        in_specs.append(pl.BlockSpec((1, tn), lambda bi, i, j: (0, j)))
        args.append(bias.reshape(1, n).astype(F32))
    for t in (rope or ()) + (kadd or ()):
        in_specs.append(pl.BlockSpec((None, tm, t.shape[2]), lambda bi, i, j: (bi, i, 0)))
        args.append(t)
    out_specs = [pl.BlockSpec((None, tm, tn), lambda bi, i, j: (bi, i, j))]
    out_shape = [jax.ShapeDtypeStruct((bsz, ni * tm, n), out_dtype)]
    if side is not None:
        spec, shape = _side_specs(side, bsz * ni * nj, lambda bi, i, j: (bi * ni + i) * nj + j)
        in_specs.append(spec)
        args.append(side)
        out_specs.append(spec)
        out_shape.append(shape)
    body = functools.partial(_mm_body, has_gain=gain is not None, has_bias=bias is not None, act=act,
                             rope=rope is not None, kadd=kadd is not None, has_side=side is not None,
                             w_rows=w_rows is not None)
    outs = pl.pallas_call(
        body,
        grid=(bsz, ni, nj),
        in_specs=in_specs,
        out_specs=out_specs,
        out_shape=out_shape,
        compiler_params=_cparams(("parallel", "parallel", "arbitrary"), VMEM_BIG),
        name=name,
    )(*args)
    return outs[0] if side is None else outs


def _mm_t_body(a_ref, wt_ref, gain_ref, bias_ref, o_ref):
    af = a_ref[...].astype(F32)
    a = af * lax.rsqrt(jnp.mean(af * af, axis=-1, keepdims=True) + RMS_EPS) * gain_ref[...]
    r = lax.dot_general(wt_ref[...], a.astype(BF16), (((1,), (1,)), ((), ())), preferred_element_type=F32)
    o_ref[...] = (r + bias_ref[...]).astype(o_ref.dtype)


def _mm_t(a3, wt, gain, bias_col, *, out_dtype, tm, tn, name):
    bsz, rows, k = a3.shape
    n = wt.shape[0]
    return pl.pallas_call(
        _mm_t_body,
        grid=(bsz, rows // tm, n // tn),
        in_specs=[pl.BlockSpec((None, tm, k), lambda bi, i, j: (bi, i, 0)),
                  pl.BlockSpec((tn, k), lambda bi, i, j: (j, 0)),
                  pl.BlockSpec((1, k), lambda bi, i, j: (0, 0)),
                  pl.BlockSpec((tn, 1), lambda bi, i, j: (j, 0))],
        out_specs=pl.BlockSpec((None, tn, tm), lambda bi, i, j: (bi, j, i)),
        out_shape=jax.ShapeDtypeStruct((bsz, n, rows), out_dtype),
        compiler_params=_cparams(("parallel", "parallel", "arbitrary"), VMEM_BIG),
        name=name,
    )(a3, wt, gain.reshape(1, k).astype(F32), bias_col.reshape(n, 1).astype(F32))


def _mmk_body(*refs, n_pairs, has_gate, has_res, res_scale, has_side):
    it = iter(refs)
    a_refs = [next(it) for _ in range(n_pairs)]
    w_refs = [next(it) for _ in range(n_pairs)]
    g_refs = [next(it) for _ in range(n_pairs)] if has_gate else None
    res_ref = next(it) if has_res else None
    side_in = next(it) if has_side else None
    o_ref = next(it)
    if has_side:
        next(it)[...] = side_in[...].astype(BF16)
    acc_refs = [next(it) for _ in range(n_pairs)]
    kk = pl.program_id(3)

    @pl.when(kk == 0)
    def _():
        for acc in acc_refs:
            acc[...] = jnp.zeros_like(acc)

    for a_ref, w_ref, acc in zip(a_refs, w_refs, acc_refs):
        acc[...] += jnp.dot(a_ref[...], w_ref[...], preferred_element_type=F32)

    @pl.when(kk == pl.num_programs(3) - 1)
    def _():
        r = None
        for p, acc in enumerate(acc_refs):
            t = acc[...]
            if has_gate:
                t = t * g_refs[p][...].astype(F32)
            r = t if r is None else r + t
        if has_res:
            r = r + res_scale * res_ref[...]
        o_ref[...] = r.astype(o_ref.dtype)


def _mm_ktiled(a_list, w_list, *, out_dtype, rows, tm, tn, nk, gates=None, res=None, res_scale=1.0, side=None,
               name="mmk"):
    bsz = a_list[0].shape[0]
    n = w_list[0].shape[1]
    npairs = len(a_list)
    tks = [a.shape[2] // nk for a in a_list]
    assert all(a.shape[2] == tk * nk and tk % LANES == 0 for a, tk in zip(a_list, tks))
    o_spec = pl.BlockSpec((None, tm, tn), lambda bi, i, j, kk: (bi, i, j))
    in_specs = ([pl.BlockSpec((None, tm, tk), lambda bi, i, j, kk: (bi, i, kk)) for tk in tks]
                + [pl.BlockSpec((tk, tn), lambda bi, i, j, kk: (kk, j)) for tk in tks])
    args = list(a_list) + list(w_list)
    if gates is not None:
        for p in range(npairs):
            in_specs.append(pl.BlockSpec((None, tm, tn), lambda bi, i, j, kk, p=p: (bi, i, p * (n // tn) + j)))
            args.append(gates)
    if res is not None:
        in_specs.append(o_spec)
        args.append(res)
    ni, nj = rows // tm, n // tn
    out_specs = [o_spec]
    out_shape = [jax.ShapeDtypeStruct((bsz, rows, n), out_dtype)]
    if side is not None:
        spec, shape = _side_specs(side, bsz * ni * nj * nk, lambda bi, i, j, kk: ((bi * ni + i) * nj + j) * nk + kk)
        in_specs.append(spec)
        args.append(side)
        out_specs.append(spec)
        out_shape.append(shape)
    body = functools.partial(_mmk_body, n_pairs=npairs, has_gate=gates is not None, has_res=res is not None,
                             res_scale=res_scale, has_side=side is not None)
    outs = pl.pallas_call(
        body,
        grid=(bsz, ni, nj, nk),
        in_specs=in_specs,
        out_specs=out_specs,
        out_shape=out_shape,
        scratch_shapes=[pltpu.VMEM((tm, tn), F32) for _ in range(npairs)],
        compiler_params=_cparams(("parallel", "parallel", "parallel", "arbitrary"), VMEM_BIG),
        name=name,
    )(*args)
    return outs[0] if side is None else outs


def _silu(x):
    return x * jax.nn.sigmoid(x)


def _ssd_body(xs_ref, b_ref, c_ref, z_ref, dt_ref, wx_ref, wb_ref, wc_ref, bx_ref, bb_ref, bc_ref,
              dtb_ref, alog_ref, dskip_ref, ng_ref, e_ref, o_ref,
              state_ref, extx_ref, extb_ref, extc_ref, y_ref, *, hg):
    c = pl.program_id(2)
    q = CHUNK
    p = SSM_HEAD_DIM
    first = c == 0
    row = lax.broadcasted_iota(I32, (q, 1), 0)
    live = jnp.logical_or(jnp.logical_not(first), row >= META_PAD)

    @pl.when(first)
    def _():
        state_ref[...] = jnp.zeros_like(state_ref)
        extx_ref[0:8, :] = jnp.zeros((8, extx_ref.shape[1]), F32)
        extb_ref[0:8, :] = jnp.zeros((8, extb_ref.shape[1]), F32)
        extc_ref[0:8, :] = jnp.zeros((8, extc_ref.shape[1]), F32)

    def conv(ext_ref, u_ref, w_ref, bias_ref):
        @pl.when(first)
        def _():
            ext_ref[8:8 + q, :] = jnp.where(live, u_ref[...].astype(F32), 0.0)

        @pl.when(jnp.logical_not(first))
        def _():
            ext_ref[8:8 + q, :] = u_ref[...].astype(F32)

        acc = bias_ref[...]
        for k in range(SSM_CONV):
            acc = acc + ext_ref[pl.ds(8 - (SSM_CONV - 1) + k, q), :] * w_ref[k:k + 1, :]
        ext_ref[0:8, :] = ext_ref[q:q + 8, :]
        return _silu(acc)

    xs = conv(extx_ref, xs_ref, wx_ref, bx_ref)
    bm = conv(extb_ref, b_ref, wb_ref, bb_ref)
    cm = conv(extc_ref, c_ref, wc_ref, bc_ref)

    x = dt_ref[...] + dtb_ref[...]
    dt = jnp.maximum(x, 0.0) + jnp.log1p(jnp.exp(-jnp.abs(x)))
    dt = jnp.where(live, dt, 0.0)
    a = -jnp.exp(alog_ref[...])
    r_i = lax.broadcasted_iota(I32, (q, q), 0)
    c_i = lax.broadcasted_iota(I32, (q, q), 1)
    causal = r_i >= c_i
    hi = lax.Precision.HIGHEST
    a_cs = jnp.dot(causal.astype(F32), dt * a, precision=hi, preferred_element_type=F32)
    a_cs_t = a_cs.T
    ea = jnp.exp(a_cs)
    decay_end = jnp.exp(a_cs[q - 1:q, :] - a_cs)
    e = e_ref[...]

    def spread(v):
        v_hi = v.astype(BF16)
        v_lo = (v - v_hi.astype(F32)).astype(BF16)
        return jnp.dot(v_hi, e, preferred_element_type=F32) + jnp.dot(v_lo, e, preferred_element_type=F32)

    dt_x = spread(dt)
    ea_x = spread(ea)
    de_x = spread(decay_end)

    xdt = xs * dt_x
    xdt_b = xdt.astype(BF16)
    cm_b = cm.astype(BF16)
    bm_b = bm.astype(BF16)
    cb = lax.dot_general(cm_b, bm_b, (((1,), (1,)), ((), ())), preferred_element_type=F32)
    prev = state_ref[...]
    y_off = jnp.dot(cm_b, prev.astype(BF16), preferred_element_type=F32) * ea_x
    def decay_scores(j):
        seg = a_cs[:, j:j + 1] - a_cs_t[j:j + 1, :]
        return (jnp.exp(jnp.where(causal, seg, NEG)) * cb).astype(BF16)

    m_next = decay_scores(0)
    for j in range(hg):
        m = m_next
        if j + 1 < hg:
            m_next = decay_scores(j + 1)
        y_ref[:, j * p:(j + 1) * p] = jnp.dot(m, xdt_b[:, j * p:(j + 1) * p], preferred_element_type=F32)
    y = y_ref[...] + y_off + xs * dskip_ref[...]
    state_ref[...] = prev * ea_x[q - 1:q, :] + jnp.dot(bm.T.astype(BF16), (xdt * de_x).astype(BF16),
                                                        preferred_element_type=F32)
    yz = y * _silu(z_ref[...].astype(F32))
    yn = yz * lax.rsqrt(jnp.mean(yz * yz, axis=-1, keepdims=True) + RMS_EPS) * ng_ref[...]
    o_ref[...] = yn.astype(o_ref.dtype)


def _ssd(xbc3, z3, dt_g, conv_w, conv_b, dtb_g, alog_g, dskip_x, norm_g, d_inner, heads):
    bsz, rows, conv_dim = xbc3.shape
    g, n, p, q = SSM_GROUPS, SSM_STATE, SSM_HEAD_DIM, CHUNK
    hg = heads // g
    gw = hg * p
    nc = rows // q
    assert gw % LANES == 0 and d_inner % n == 0 and hg <= LANES
    boff = d_inner // n
    expand = (jnp.arange(LANES)[:, None] == (jnp.arange(gw) // p)[None, :]).astype(BF16)

    def chunk(c):
        return (c + nc - 1) % nc

    in_specs = [
        pl.BlockSpec((None, q, gw), lambda b, gi, c: (b, chunk(c), gi)),
        pl.BlockSpec((None, q, n), lambda b, gi, c: (b, chunk(c), boff + gi)),
        pl.BlockSpec((None, q, n), lambda b, gi, c: (b, chunk(c), boff + g + gi)),
        pl.BlockSpec((None, q, gw), lambda b, gi, c: (b, chunk(c), gi)),
        pl.BlockSpec((None, None, q, LANES), lambda b, gi, c: (b, gi, chunk(c), 0)),
        pl.BlockSpec((SSM_CONV, gw), lambda b, gi, c: (0, gi)),
        pl.BlockSpec((SSM_CONV, n), lambda b, gi, c: (0, boff + gi)),
        pl.BlockSpec((SSM_CONV, n), lambda b, gi, c: (0, boff + g + gi)),
        pl.BlockSpec((1, gw), lambda b, gi, c: (0, gi)),
        pl.BlockSpec((1, n), lambda b, gi, c: (0, boff + gi)),
        pl.BlockSpec((1, n), lambda b, gi, c: (0, boff + g + gi)),
        pl.BlockSpec((None, 1, LANES), lambda b, gi, c: (gi, 0, 0)),
        pl.BlockSpec((None, 1, LANES), lambda b, gi, c: (gi, 0, 0)),
        pl.BlockSpec((1, gw), lambda b, gi, c: (0, gi)),
        pl.BlockSpec((1, gw), lambda b, gi, c: (0, gi)),
        pl.BlockSpec((LANES, gw), lambda b, gi, c: (0, 0)),
    ]
    return pl.pallas_call(
        functools.partial(_ssd_body, hg=hg),
        grid=(bsz, g, nc),
        in_specs=in_specs,
        out_specs=pl.BlockSpec((None, q, gw), lambda b, gi, c: (b, chunk(c), gi)),
        out_shape=jax.ShapeDtypeStruct((bsz, rows, d_inner), BF16),
        scratch_shapes=[pltpu.VMEM((n, gw), F32), pltpu.VMEM((q + 8, gw), F32), pltpu.VMEM((q + 8, n), F32),
                        pltpu.VMEM((q + 8, n), F32), pltpu.VMEM((q, gw), F32)],
        compiler_params=_cparams(("parallel", "parallel", "arbitrary")),
        name="ssd",
    )(xbc3, xbc3, xbc3, z3, dt_g, conv_w, conv_w, conv_w, conv_b, conv_b, conv_b, dtb_g, alog_g,
      dskip_x, norm_g, expand)


def _attn_body(qi_ref, ki_ref, q_ref, km_ref, vm_ref, k_ref, v_ref, *rest, heads, n_side):
    side_in, o_ref, side_out = rest[:n_side], rest[n_side], rest[n_side + 1:2 * n_side + 1]
    m_ref, acc_ref = rest[2 * n_side + 1:]
    for src, dst in zip(side_in, side_out):
        dst[...] = src[...].astype(BF16)
    step_id = pl.program_id(2)
    qi, ki = qi_ref[step_id], ki_ref[step_id]

    @pl.when(ki < 0)
    def _():
        m_ref[...] = jnp.full_like(m_ref, NEG)
        acc_ref[...] = jnp.zeros_like(acc_ref)

    def step(kr, vr, mask_fn):
        def scores(h):
            qh = q_ref[:, h * MLA_QKP:(h + 1) * MLA_QKP]
            kh = kr[:, h * MLA_QKP:(h + 1) * MLA_QKP]
            st = lax.dot_general(kh, qh, (((1,), (1,)), ((), ())), preferred_element_type=F32)
            return st if mask_fn is None else mask_fn(st)

        def probs(h, st):
            m_prev = m_ref[h]
            m_new = jnp.maximum(m_prev, jnp.max(st, axis=0, keepdims=True))
            m_ref[h] = m_new
            return jnp.exp2(m_prev - m_new), jnp.exp2((st - m_new).astype(BF16))

        def accumulate(h, alpha, pt):
            sl = slice(h * MLA_VP, (h + 1) * MLA_VP)
            acc_ref[sl, :] = acc_ref[sl, :] * alpha + jnp.dot(vr[sl, :], pt, preferred_element_type=F32)

        st_q, pr_q = {}, {}
        for stage in range(heads + 2):
            if stage < heads:
                st_q[stage] = scores(stage)
            if 0 <= stage - 1 < heads:
                pr_q[stage - 1] = probs(stage - 1, st_q.pop(stage - 1))
            if 0 <= stage - 2 < heads:
                accumulate(stage - 2, *pr_q.pop(stage - 2))

    def meta_mask(st):
        key = lax.broadcasted_iota(I32, st.shape, 0)
        return jnp.where(key >= META_PAD, st, NEG)

    def causal_mask(st):
        key = lax.broadcasted_iota(I32, st.shape, 0)
        qry = lax.broadcasted_iota(I32, st.shape, 1)
        return jnp.where(key <= qry, st, NEG)

    @pl.when(ki < 0)
    def _():
        step(km_ref, vm_ref, meta_mask)

    @pl.when(jnp.logical_and(ki >= 0, ki < qi))
    def _():
        step(k_ref, v_ref, None)

    @pl.when(ki == qi)
    def _():
        step(k_ref, v_ref, causal_mask)
        for h in range(heads):
            num = acc_ref[h * MLA_VP:h * MLA_VP + MLA_V, :]
            den = acc_ref[h * MLA_VP + MLA_V:h * MLA_VP + MLA_V + 1, :]
            o_ref[:, h * MLA_V:(h + 1) * MLA_V] = (num / den).T.astype(o_ref.dtype)


def _attention(q3, k3, vt3, seq, sides=()):
    bsz = q3.shape[0]
    hp = 8
    t = _pick(seq, (512, 256, 128))
    nq = seq // t
    meta_blk = seq // CHUNK
    qw, vw = hp * MLA_QKP, hp * MLA_VP
    qi_tab = jnp.asarray([qi for qi in range(nq) for _ in range(qi + 2)], I32)
    ki_tab = jnp.asarray([ki for qi in range(nq) for ki in range(-1, qi + 1)], I32)

    def q_idx(b, h, s, qt, kt):
        return (b, qt[s], h)

    def k_idx(b, h, s, qt, kt):
        return (b, jnp.maximum(kt[s], 0), h)

    def vt_idx(b, h, s, qt, kt):
        return (b, h, jnp.maximum(kt[s], 0))

    nh, npairs = MLA_HEADS // hp, int(qi_tab.shape[0])
    side_specs = [_side_specs(s_arr, bsz * nh * npairs, lambda b, h, s, qt, kt: (b * nh + h) * npairs + s)
                  for s_arr in sides]
    grid_spec = pltpu.PrefetchScalarGridSpec(
        num_scalar_prefetch=2,
        grid=(bsz, nh, npairs),
        in_specs=[
            pl.BlockSpec((None, t, qw), q_idx),
            pl.BlockSpec((None, CHUNK, qw), lambda b, h, s, qt, kt: (b, meta_blk, h)),
            pl.BlockSpec((None, vw, CHUNK), lambda b, h, s, qt, kt: (b, h, meta_blk)),
            pl.BlockSpec((None, t, qw), k_idx),
            pl.BlockSpec((None, vw, t), vt_idx),
        ] + [spec for spec, _ in side_specs],
        out_specs=[pl.BlockSpec((None, t, hp * MLA_V), q_idx)] + [spec for spec, _ in side_specs],
        scratch_shapes=[pltpu.VMEM((hp, 1, t), F32), pltpu.VMEM((vw, t), F32)],
    )
    return pl.pallas_call(
        functools.partial(_attn_body, heads=hp, n_side=len(sides)),
        grid_spec=grid_spec,
        out_shape=[jax.ShapeDtypeStruct((bsz, seq, MLA_HEADS * MLA_V), BF16)] + [shape for _, shape in side_specs],
        compiler_params=_cparams(("parallel", "parallel", "arbitrary")),
        name="mla_attention",
    )(qi_tab, ki_tab, q3, k3, vt3, k3, vt3, *sides)


def _router_body(h_ref, wr_ref, rb_ref, e_ref, pos_ref, w_ref, cnt_ref, carry_ref):
    i = pl.program_id(0)
    ne, ng = N_EXPERTS, N_EXPERT_GROUPS
    gs = ne // ng
    tm = h_ref.shape[0]

    @pl.when(i == 0)
    def _():
        carry_ref[...] = jnp.zeros_like(carry_ref)

    logits = lax.dot_general(wr_ref[...], h_ref[...], (((1,), (1,)), ((), ())), precision=lax.Precision.HIGHEST,
                             preferred_element_type=F32)
    scores = jax.nn.sigmoid(logits)
    choice = scores + rb_ref[...]
    sub = lax.broadcasted_iota(I32, (gs, tm), 0)
    grp_rows = []
    for g in range(ng):
        blk = choice[g * gs:(g + 1) * gs, :]
        m1 = jnp.max(blk, axis=0, keepdims=True)
        first = jnp.min(jnp.where(blk == m1, sub, gs), axis=0, keepdims=True)
        m2 = jnp.max(jnp.where(sub == first, -jnp.inf, blk), axis=0, keepdims=True)
        grp_rows.append(m1 + m2)
    grp = jnp.concatenate(grp_rows, axis=0)
    grank = jnp.zeros((ng, tm), I32)
    gidx = lax.broadcasted_iota(I32, (ng, tm), 0)
    for g in range(ng):
        rowv = grp[g:g + 1, :]
        beats = jnp.logical_or(rowv > grp, jnp.logical_and(rowv == grp, gidx > g))
        grank = grank + beats.astype(I32)
    gsel = (grank < TOPK_GROUPS).astype(F32)
    esel = jnp.concatenate([jnp.broadcast_to(gsel[g:g + 1, :], (gs, tm)) for g in range(ng)], axis=0)
    masked = jnp.where(esel > 0.0, choice, -jnp.inf)
    eidx = lax.broadcasted_iota(I32, (ne, tm), 0)
    rank = jnp.zeros((ne, tm), I32)
    for e in range(ne):
        rowv = masked[e:e + 1, :]
        beats = jnp.logical_or(rowv > masked, jnp.logical_and(rowv == masked, eidx > e))
        rank = rank + beats.astype(I32)
    top = jnp.logical_and(rank < TOP_K, esel > 0.0)
    topf = top.astype(F32)
    wsel = jnp.where(top, scores, 0.0)
    wn = wsel / jnp.sum(wsel, axis=0, keepdims=True) * ROUTED_SCALE
    r_i = lax.broadcasted_iota(I32, (tm, tm), 0)
    c_i = lax.broadcasted_iota(I32, (tm, tm), 1)
    before = (r_i < c_i).astype(BF16)
    pos = jnp.dot(topf.astype(BF16), before, preferred_element_type=F32) + carry_ref[:, :1]
    carry_ref[...] = carry_ref[...] + jnp.sum(topf, axis=1, keepdims=True)
    cnt_ref[...] = carry_ref[...]
    e_rows, p_rows, w_rows = [], [], []
    for k in range(TOP_K):
        hit = rank == k
        e_rows.append(jnp.sum(jnp.where(hit, eidx, 0), axis=0, keepdims=True))
        p_rows.append(jnp.sum(jnp.where(hit, pos, 0.0), axis=0, keepdims=True))
        w_rows.append(jnp.sum(jnp.where(hit, wn, 0.0), axis=0, keepdims=True))
    e_ref[...] = jnp.concatenate(e_rows, axis=0)
    pos_ref[...] = jnp.concatenate(p_rows, axis=0).astype(I32)
    w_ref[...] = jnp.concatenate(w_rows, axis=0)


def _router(h2, w_router, router_bias):
    t, d = h2.shape
    tm = _pick(t, (256, 128))
    slot = pl.BlockSpec((TOP_K, tm), lambda i: (0, i))
    rb = jnp.broadcast_to(router_bias.astype(F32)[:, None], (N_EXPERTS, tm))
    return pl.pallas_call(
        _router_body,
        grid=(t // tm,),
        in_specs=[pl.BlockSpec((tm, d), lambda i: (i, 0)), pl.BlockSpec((N_EXPERTS, d), lambda i: (0, 0)),
                  pl.BlockSpec((N_EXPERTS, tm), lambda i: (0, 0))],
        out_specs=[slot, slot, slot, pl.BlockSpec((N_EXPERTS, LANES), lambda i: (0, 0))],
        out_shape=[jax.ShapeDtypeStruct((TOP_K, t), I32), jax.ShapeDtypeStruct((TOP_K, t), I32),
                   jax.ShapeDtypeStruct((TOP_K, t), F32), jax.ShapeDtypeStruct((N_EXPERTS, LANES), F32)],
        scratch_shapes=[pltpu.VMEM((N_EXPERTS, LANES), F32)],
        compiler_params=_cparams(("arbitrary",)),
        name="moe_router",
    )(h2, w_router.T.astype(F32), rb)


def _load_slots(dest_hbm, dest_ref, sem):
    cp = pltpu.make_async_copy(dest_hbm.at[pl.program_id(0)], dest_ref, sem)
    cp.start()
    cp.wait()


def _dispatch_body(cnt_ref, start_ref, dest_hbm, x_ref, xs_ref, dest_ref, zero_ref, sem_ref, *, block_rows):
    i = pl.program_id(0)
    tm = x_ref.shape[0]
    _load_slots(dest_hbm, dest_ref, sem_ref.at[2])

    def row_copy(tok, k):
        return pltpu.make_async_copy(x_ref.at[pl.ds(tok, 1), :], xs_ref.at[pl.ds(dest_ref[k, tok], 1), :], sem_ref.at[0])

    def issue(tok, carry):
        for k in range(TOP_K):
            row_copy(tok, k).start(priority=k % 2)
        return carry

    lax.fori_loop(0, tm, issue, 0)

    @pl.when(i == 0)
    def _():
        zero_ref[...] = jnp.zeros_like(zero_ref)

        def fill(e, carry):
            cnt = cnt_ref[e]
            padded = (cnt + block_rows - 1) // block_rows * block_rows
            base = start_ref[e] + cnt

            def zcopy(r):
                return pltpu.make_async_copy(zero_ref.at[pl.ds(0, 1), :], xs_ref.at[pl.ds(base + r, 1), :], sem_ref.at[1])

            def zstart(r, c2):
                zcopy(r).start()
                return c2

            def zwait(r, c2):
                zcopy(r).wait()
                return c2

            lax.fori_loop(0, padded - cnt, zstart, 0)
            lax.fori_loop(0, padded - cnt, zwait, 0)
            return carry

        lax.fori_loop(0, N_EXPERTS, fill, 0)

    def drain(tok, carry):
        for k in range(TOP_K):
            row_copy(tok, k).wait()
        return carry

    lax.fori_loop(0, tm, drain, 0)


def _slot_tiles(dest, tm):
    k, t = dest.shape
    return dest.reshape(k, t // tm, tm).transpose(1, 0, 2)


def _dispatch(hp2, dest, counts, starts, n_rows):
    t, w = hp2.shape
    tm = _pick(t, (256, 128))
    grid_spec = pltpu.PrefetchScalarGridSpec(
        num_scalar_prefetch=2,
        grid=(t // tm,),
        in_specs=[pl.BlockSpec(memory_space=pl.ANY),
                  pl.BlockSpec((tm, w), lambda i, c, s: (i, 0))],
        out_specs=pl.BlockSpec(memory_space=pl.ANY),
        scratch_shapes=[pltpu.SMEM((TOP_K, tm), I32), pltpu.VMEM((8, w), U32), pltpu.SemaphoreType.DMA((3,))],
    )
    return pl.pallas_call(
        functools.partial(_dispatch_body, block_rows=MOE_ROWS),
        grid_spec=grid_spec,
        out_shape=jax.ShapeDtypeStruct((n_rows, w), U32),
        compiler_params=pltpu.CompilerParams(dimension_semantics=("arbitrary",)),
        name="moe_dispatch",
    )(counts, starts, _slot_tiles(dest, tm), hp2)


def _expert_body(be_ref, nu_ref, x_ref, wg_ref, wu_ref, wd_ref, o_ref):
    j = pl.program_id(0)

    @pl.when(j < nu_ref[0])
    def _():
        half = x_ref.shape[1]
        lo, hi = _unpack_halves(x_ref[...])
        lo, hi = lo.astype(BF16), hi.astype(BF16)

        def proj(w_ref):
            return (jnp.dot(lo, w_ref[:half, :], preferred_element_type=F32)
                    + jnp.dot(hi, w_ref[half:, :], preferred_element_type=F32))

        hmid = (_silu(proj(wg_ref)) * proj(wu_ref)).astype(BF16)
        o_ref[...] = _pack_halves(jnp.dot(hmid, wd_ref[...], preferred_element_type=F32))


def _experts(xs, wg, wu, wd, block_e, n_used):
    rows, w = xs.shape
    _, d, ff = wg.shape
    bm = MOE_ROWS
    nb = rows // bm

    def row_idx(j, be, nu):
        return (jnp.minimum(j, nu[0] - 1), 0)

    def w_idx(j, be, nu):
        return (be[jnp.minimum(j, nu[0] - 1)], 0, 0)

    grid_spec = pltpu.PrefetchScalarGridSpec(
        num_scalar_prefetch=2,
        grid=(nb,),
        in_specs=[pl.BlockSpec((bm, w), row_idx),
                  pl.BlockSpec((None, d, ff), w_idx), pl.BlockSpec((None, d, ff), w_idx),
                  pl.BlockSpec((None, ff, d), w_idx)],
        out_specs=pl.BlockSpec((bm, w), row_idx),
    )
    return pl.pallas_call(
        _expert_body,
        grid_spec=grid_spec,
        out_shape=jax.ShapeDtypeStruct((rows, w), U32),
        compiler_params=_cparams(("arbitrary",), 60 * 1024 * 1024),
        name="moe_experts",
    )(block_e, n_used, xs, wg, wu, wd)


def _combine_body(dest_hbm, wt_ref, h_ref, ysh_ref, g_ref, b_ref, y_ref, o_ref, dest_ref, gbuf_ref, sem_ref, *, alpha):
    tm = h_ref.shape[0]
    _load_slots(dest_hbm, dest_ref, sem_ref.at[1])

    def row_copy(tok, k):
        return pltpu.make_async_copy(y_ref.at[pl.ds(dest_ref[k, tok], 1), :], gbuf_ref.at[k, pl.ds(tok, 1), :],
                                     sem_ref.at[0])

    def issue(tok, carry):
        for k in range(TOP_K):
            row_copy(tok, k).start(priority=k % 2)
        return carry

    def drain(tok, carry):
        for k in range(TOP_K):
            row_copy(tok, k).wait()
        return carry

    lax.fori_loop(0, tm, issue, 0)
    lax.fori_loop(0, tm, drain, 0)
    lo, hi = _unpack_halves(ysh_ref[...])
    wt = wt_ref[...]
    for k in range(TOP_K):
        glo, ghi = _unpack_halves(gbuf_ref[k])
        wk = wt[:, k:k + 1]
        lo = lo + glo * wk
        hi = hi + ghi * wk
    x = alpha * h_ref[...] + jnp.concatenate([lo, hi], axis=1)
    mu = jnp.mean(x, axis=-1, keepdims=True)
    xc = x - mu
    var = jnp.mean(xc * xc, axis=-1, keepdims=True)
    o_ref[...] = xc * lax.rsqrt(var + LN_EPS) * g_ref[...] + b_ref[...]


def _combine(h2, ysh, y_sorted, dest, wt, g, b, alpha):
    t, d = h2.shape
    w = d // 2
    tm = _pick(t, (128,))
    vec = pl.BlockSpec((1, d), lambda i: (0, 0))
    return pl.pallas_call(
        functools.partial(_combine_body, alpha=alpha),
        grid=(t // tm,),
        in_specs=[pl.BlockSpec(memory_space=pl.ANY),
                  pl.BlockSpec((tm, TOP_K), lambda i: (i, 0)),
                  pl.BlockSpec((tm, d), lambda i: (i, 0)),
                  pl.BlockSpec((tm, w), lambda i: (i, 0)),
                  vec, vec,
                  pl.BlockSpec(memory_space=pl.ANY)],
        out_specs=pl.BlockSpec((tm, d), lambda i: (i, 0)),
        out_shape=jax.ShapeDtypeStruct((t, d), F32),
        scratch_shapes=[pltpu.SMEM((TOP_K, tm), I32), pltpu.VMEM((TOP_K, tm, w), U32), pltpu.SemaphoreType.DMA((2,))],
        compiler_params=_cparams(("arbitrary",)),
        name="moe_combine",
    )(_slot_tiles(dest, tm), wt, h2, ysh, g.reshape(1, d), b.reshape(1, d), y_sorted)


def _rope_tables(pos):
    inv_freq = ROPE_THETA ** (-jnp.arange(0, MLA_ROPE, 2, dtype=F32) / MLA_ROPE)
    ang = pos.astype(F32)[..., None] * inv_freq
    return jnp.cos(ang), jnp.sin(ang)


def kernel(x, positions, meta_tokens, ln_in_g, ln_in_b, w_in, b_gate, conv_w, conv_b, dt_bias, a_log, d_skip,
           ssm_norm_g, w_ssm_proj, q_a_norm_g, w_q_b, kv_a_norm_g, w_kv_b, w_attn_proj, w_out, ln1_g, ln1_b,
           w_router, router_bias, w_exp_gate, w_exp_up, w_exp_down, w_sh_gate, w_sh_up, w_sh_down, ln2_g, ln2_b):
    bsz, seq, d = x.shape
    depth = w_in.shape[0]
    heads = dt_bias.shape[-1]
    d_inner = w_ssm_proj.shape[1]
    conv_dim = conv_w.shape[-1]
    q_rank = w_q_b.shape[1]
    kv_rank = w_kv_b.shape[1]
    hh = MLA_HEADS
    g = SSM_GROUPS
    hg = heads // g
    assert seq % CHUNK == 0 and d % (2 * LANES) == 0
    lp = seq + CHUNK
    t = bsz * seq
    alpha = (2.0 * depth) ** 0.25

    meta_chunk = jnp.concatenate([jnp.zeros((META_PAD, d), x.dtype), meta_tokens.astype(x.dtype)], axis=0)
    pos =jnp.concatenate([positions.astype(I32) + N_META, jnp.zeros((bsz, META_PAD), I32),
                           jnp.broadcast_to(jnp.arange(N_META, dtype=I32), (bsz, N_META))], axis=1)
    cos, sin = _rope_tables(pos)
    zr = jnp.zeros_like(cos)
    rope_c = jnp.concatenate([cos, cos, zr, zr], axis=-1)[:, :seq]
    rope_slo = jnp.concatenate([-sin, zr, zr, zr], axis=-1)[:, :seq]
    rope_shi = jnp.concatenate([zr, sin, zr, zr], axis=-1)[:, :seq]
    k_c = jnp.concatenate([cos, cos, zr, zr], axis=-1)
    k_s = jnp.concatenate([-sin, sin, zr, zr], axis=-1)

    h_f32, h_b16 = _layer_norm_in(x, meta_chunk, ln_in_g, ln_in_b)
    tm_all = _pick(lp, (1408, 1152, 1024, 896, 768, 640, 512, 384, 256, 128))
    tm_seq = _pick(seq, (1024, 512, 256, 128))

    out = None
    for l in range(depth):
        offs = [0]
        for wdt in (d_inner, conv_dim, heads, q_rank, kv_rank, MLA_ROPE, 2 * d):
            offs.append(offs[-1] + wdt)
        wt = jnp.swapaxes(w_in[l], 0, 1)
        w_kr = wt[offs[5]:offs[6]]
        small_cols = q_rank + kv_rank + 2 * MLA_ROPE + heads
        small_n = small_cols + (-small_cols) % 256
        wt_small = jnp.concatenate([wt[offs[3]:offs[4]], wt[offs[4]:offs[5]], w_kr, w_kr, wt[offs[2]:offs[3]],
                                    jnp.zeros((small_n - small_cols, d), F32)], axis=0)

        n_exp, _, ff = w_exp_gate[l].shape
        z3 = _mm(h_b16, wt, w_rows=(offs[0], d_inner), out_dtype=BF16, tm=tm_all, tn=512, name="in_proj_z")
        xbc3 = _mm(h_b16, wt, w_rows=(offs[1], conv_dim), out_dtype=BF16, tm=tm_all, tn=512, name="in_proj_xbc")
        small = _mm(h_b16, wt_small, w_rows=(0, small_n), out_dtype=F32, tm=tm_all, tn=256, name="in_proj_small")
        gates = _mm(h_b16, wt, w_rows=(offs[6], 2 * d), out_dtype=BF16, tm=tm_seq, tn=512, n_row_tiles=seq // tm_seq,
                    bias=b_gate[l], act="sigmoid", name="in_proj_gates")
        o = 0
        q_a = small[:, :, o:o + q_rank]; o += q_rank
        kv_a = small[:, :, o:o + kv_rank]; o += kv_rank
        kr2 = small[:, :, o:o + 2 * MLA_ROPE]; o += 2 * MLA_ROPE
        dt_raw = small[:, :, o:o + heads]

        dt_g = jnp.pad(dt_raw.reshape(bsz, lp, g, hg).transpose(0, 2, 1, 3), ((0, 0), (0, 0), (0, 0), (0, LANES - hg)))
        pad_h = lambda v: jnp.pad(v.astype(F32).reshape(g, 1, hg), ((0, 0), (0, 0), (0, LANES - hg)))
        y_ssm = _ssd(xbc3, z3, dt_g, conv_w[l].astype(F32), conv_b[l].reshape(1, conv_dim).astype(F32),
                     pad_h(dt_bias[l]), pad_h(a_log[l]),
                     jnp.repeat(d_skip[l].astype(F32), SSM_HEAD_DIM).reshape(1, d_inner),
                     ssm_norm_g[l].reshape(1, d_inner).astype(F32), d_inner, heads)

        qscale = (MLA_QK ** -0.5) * math.log2(math.e)
        head_pad = ((0, 0), (0, 0), (0, MLA_QKP - MLA_QK))
        w_q = jnp.pad(w_q_b[l].reshape(q_rank, hh, MLA_QK), head_pad).reshape(q_rank, hh * MLA_QKP).astype(BF16)
        q3 = _mm(q_a, w_q, out_dtype=BF16, tm=tm_seq, tn=4 * MLA_QKP, n_row_tiles=seq // tm_seq,
                 gain=q_a_norm_g[l] * qscale, rope=(rope_c, rope_slo, rope_shi), name="q_proj")
        wkv = w_kv_b[l].reshape(kv_rank, hh, MLA_NOPE + MLA_V)
        w_k = jnp.pad(wkv[:, :, :MLA_NOPE], ((0, 0), (0, 0), (0, MLA_QKP - MLA_NOPE)))
        w_k = w_k.reshape(kv_rank, hh * MLA_QKP).astype(BF16)
        w_vt = jnp.pad(wkv[:, :, MLA_NOPE:], ((0, 0), (0, 0), (0, MLA_VP - MLA_V)))
        w_vt = w_vt.reshape(kv_rank, hh * MLA_VP).T.astype(BF16)
        ones_rows = jnp.tile(jnp.concatenate([jnp.zeros((MLA_V,), F32), jnp.ones((MLA_VP - MLA_V,), F32)]), hh)
        k3 = _mm(kv_a, w_k, out_dtype=BF16, tm=tm_all, tn=4 * MLA_QKP, gain=kv_a_norm_g[l],
                 kadd=(kr2, k_c, k_s), name="k_proj")
        vt3 = _mm_t(kv_a, w_vt, kv_a_norm_g[l], ones_rows, out_dtype=BF16, tm=tm_all, tn=4 * MLA_VP,
                    name="v_proj")
        y_attn, wg_b, wu_b, wd_b = _attention(
            q3, k3, vt3, seq, [w_exp_gate[l].reshape(n_exp * d, ff), w_exp_up[l].reshape(n_exp * d, ff),
                               w_exp_down[l].reshape(n_exp * ff, d)])

        mixed = _mm_ktiled([y_ssm, y_attn], [w_ssm_proj[l].astype(BF16), w_attn_proj[l].astype(BF16)],
                           out_dtype=BF16, rows=seq, tm=tm_seq, tn=_pick(d, (1024, 512)), nk=4,
                           gates=gates, name="branch_proj")
        pre1 = _mm_ktiled([mixed], [w_out[l].astype(BF16)], out_dtype=F32, rows=seq, tm=tm_seq,
                          tn=_pick(d, (1024, 512)), nk=2, res=h_f32, res_scale=alpha, name="out_proj")
        h1, h1p = _layer_norm_pack(pre1.reshape(t, d), ln1_g[l], ln1_b[l])

        e_slot, pos_slot, w_slot, cnt = _router(h1, w_router[l], router_bias[l])
        counts = cnt[:, 0].astype(I32)
        padded = (counts + MOE_ROWS - 1) // MOE_ROWS * MOE_ROWS
        ends = jnp.cumsum(padded)
        starts = ends - padded
        onehot = e_slot[None] == jnp.arange(N_EXPERTS, dtype=I32)[:, None, None]
        dest = pos_slot + jnp.sum(jnp.where(onehot, starts[:, None, None], 0), axis=0)
        n_blocks = -(-(t * TOP_K) // MOE_ROWS) + N_EXPERTS
        blk0 = jnp.arange(n_blocks, dtype=I32) * MOE_ROWS
        block_e = jnp.minimum(jnp.sum((ends[None, :] <= blk0[:, None]).astype(I32), axis=1), N_EXPERTS - 1)
        n_used = (ends[-1] // MOE_ROWS).reshape(1)
        xs = _dispatch(h1p, dest, counts, starts, n_blocks * MOE_ROWS)
        y_sorted = _experts(xs, wg_b.reshape(n_exp, d, ff), wu_b.reshape(n_exp, d, ff), wd_b.reshape(n_exp, ff, d),
                            block_e, n_used)
        y_shared = _experts(h1p, w_sh_gate[l][None].astype(BF16), w_sh_up[l][None].astype(BF16),
                            w_sh_down[l][None].astype(BF16), jnp.zeros((t // MOE_ROWS,), I32),
                            jnp.full((1,), t // MOE_ROWS, I32))
        out = _combine(h1, y_shared, y_sorted, dest, w_slot.T, ln2_g[l], ln2_b[l], alpha)
        if l + 1 < depth:
            raise NotImplementedError("stacked layers need the meta rows carried through the channel mixer")
    return out.reshape(bsz, seq, d)
```

```python
import functools
import math

import jax
import jax.numpy as jnp
from jax import lax
from jax.experimental import pallas as pl
from jax.experimental.pallas import tpu as pltpu

F32 = jnp.float32
BF16 = jnp.bfloat16
U32 = jnp.uint32
I32 = jnp.int32

N_META = 16
CHUNK = 128
META_PAD = CHUNK - N_META
SSM_HEAD_DIM = 64
SSM_GROUPS = 8
SSM_STATE = 128
SSM_CONV = 4
MLA_HEADS = 64
MLA_NOPE = 128
MLA_ROPE = 64
MLA_V = 128
MLA_QK = MLA_NOPE + MLA_ROPE
ROPE_THETA = 10000.0
N_EXPERTS = 64
N_EXPERT_GROUPS = 8
TOPK_GROUPS = 4
TOP_K = 8
ROUTED_SCALE = 2.5
LN_EPS = 1e-5
RMS_EPS = 1e-6
NEG = -1e30
LANES = 128
MLA_QKP = 2 * LANES
MOE_ROWS = 256
MLA_VP = MLA_V + 16
VMEM_BIG = 56 * 1024 * 1024
VMEM_MID = 44 * 1024 * 1024


def _cparams(sem, vmem=VMEM_MID):
    return pltpu.CompilerParams(dimension_semantics=sem, vmem_limit_bytes=vmem)


def _pick(n, cands):
    for c in cands:
        if n % c == 0:
            return c
    raise ValueError(f"no tile for {n} in {cands}")


def _ln_body(x_ref, g_ref, b_ref, of_ref, ob_ref):
    x = x_ref[...]
    mu = jnp.mean(x, axis=-1, keepdims=True)
    xc = x - mu
    var = jnp.mean(xc * xc, axis=-1, keepdims=True)
    y = xc * lax.rsqrt(var + LN_EPS) * g_ref[...] + b_ref[...]
    of_ref[...] = y
    ob_ref[...] = y.astype(BF16)


def _ln_in_body(x_ref, meta_ref, g_ref, b_ref, of_ref, ob_ref, *, n_real):
    i = pl.program_id(1)

    @pl.when(i < n_real)
    def _():
        _ln_body(x_ref, g_ref, b_ref, of_ref, ob_ref)

    @pl.when(i == n_real)
    def _():
        _ln_body(meta_ref, g_ref, b_ref, of_ref, ob_ref)


def _layer_norm_in(x3, meta_chunk, g, b):
    bsz, seq, d = x3.shape
    tm = CHUNK
    n_real = seq // tm
    spec = pl.BlockSpec((None, tm, d), lambda bi, i: (bi, i, 0))
    vec = pl.BlockSpec((1, d), lambda bi, i: (0, 0))
    shape = (bsz, seq + tm, d)
    return pl.pallas_call(
        functools.partial(_ln_in_body, n_real=n_real),
        grid=(bsz, n_real + 1),
        in_specs=[pl.BlockSpec((None, tm, d), lambda bi, i: (bi, jnp.minimum(i, n_real - 1), 0)),
                  pl.BlockSpec((tm, d), lambda bi, i: (0, 0)), vec, vec],
        out_specs=[spec, spec],
        out_shape=[jax.ShapeDtypeStruct(shape, F32), jax.ShapeDtypeStruct(shape, BF16)],
        compiler_params=_cparams(("parallel", "arbitrary")),
        name="layer_norm",
    )(x3, meta_chunk, g.reshape(1, d), b.reshape(1, d))


def _pack_halves(y):
    n = y.shape[1] // 2
    lo = pltpu.bitcast(y[:, :n].astype(BF16).astype(F32), U32) >> 16
    hi = pltpu.bitcast(y[:, n:].astype(BF16).astype(F32), U32) & jnp.uint32(0xFFFF0000)
    return hi | lo


def _unpack_halves(w):
    lo = pltpu.bitcast(w << 16, F32)
    hi = pltpu.bitcast(w & jnp.uint32(0xFFFF0000), F32)
    return lo, hi


def _ln_pack_body(x_ref, g_ref, b_ref, of_ref, op_ref):
    x = x_ref[...]
    mu = jnp.mean(x, axis=-1, keepdims=True)
    xc = x - mu
    var = jnp.mean(xc * xc, axis=-1, keepdims=True)
    y = xc * lax.rsqrt(var + LN_EPS) * g_ref[...] + b_ref[...]
    of_ref[...] = y
    op_ref[...] = _pack_halves(y)


def _layer_norm_pack(x2, g, b):
    rows, d = x2.shape
    tm = _pick(rows, (256, 128))
    vec = pl.BlockSpec((1, d), lambda i: (0, 0))
    return pl.pallas_call(
        _ln_pack_body,
        grid=(rows // tm,),
        in_specs=[pl.BlockSpec((tm, d), lambda i: (i, 0)), vec, vec],
        out_specs=[pl.BlockSpec((tm, d), lambda i: (i, 0)), pl.BlockSpec((tm, d // 2), lambda i: (i, 0))],
        out_shape=[jax.ShapeDtypeStruct((rows, d), F32), jax.ShapeDtypeStruct((rows, d // 2), U32)],
        compiler_params=_cparams(("parallel",)),
        name="layer_norm_pack",
    )(x2, g.reshape(1, d), b.reshape(1, d))


def _mm_body(*refs, has_gain, has_bias, act, rope, kadd, has_side, w_rows):
    it = iter(refs)
    a_ref, w_ref = next(it), next(it)
    gain_ref = next(it) if has_gain else None
    bias_ref = next(it) if has_bias else None
    rope_refs = [next(it) for _ in range(3)] if rope else None
    kadd_refs = [next(it) for _ in range(3)] if kadd else None
    side_in = next(it) if has_side else None
    o_ref = next(it)
    if has_side:
        next(it)[...] = side_in[...].astype(BF16)
    a = a_ref[...]
    if has_gain:
        af = a.astype(F32)
        a = af * lax.rsqrt(jnp.mean(af * af, axis=-1, keepdims=True) + RMS_EPS) * gain_ref[...]
    if w_rows:
        r = lax.dot_general(a.astype(BF16), w_ref[...].astype(BF16), (((1,), (1,)), ((), ())),
                            preferred_element_type=F32)
    else:
        r = jnp.dot(a.astype(BF16), w_ref[...].astype(BF16), preferred_element_type=F32)
    tn = r.shape[1]
    if has_bias:
        r = r + bias_ref[...]
    if act == "sigmoid":
        r = jax.nn.sigmoid(r)
    if rope:
        c, slo, shi = (t[...] for t in rope_refs)
        half = MLA_ROPE // 2
        pieces = []
        for h0 in range(0, tn, MLA_QKP):
            rp = r[:, h0 + MLA_NOPE:h0 + MLA_QKP]
            pieces += [r[:, h0:h0 + MLA_NOPE], rp * c + pltpu.roll(rp, LANES - half, 1) * slo + pltpu.roll(rp, half, 1) * shi]
        r = jnp.concatenate(pieces, axis=1)
    if kadd:
        kr_ref, kc, ks = kadd_refs
        kr = kr_ref[...]
        piece = jnp.concatenate([jnp.zeros_like(kr), kr * kc[...] + pltpu.roll(kr, MLA_ROPE // 2, 1) * ks[...]], axis=1)
        r = r + jnp.concatenate([piece] * (tn // piece.shape[1]), axis=1)
    o_ref[...] = r.astype(o_ref.dtype)


def _side_specs(side, n_steps, lin):
    rows, cols = side.shape
    nb = max(c for c in range(1, n_steps + 1) if rows % c == 0 and (rows // c) % 16 == 0)
    spec = pl.BlockSpec((rows // nb, cols), lambda *ids: (jnp.minimum(lin(*ids), nb - 1), 0))
    return spec, jax.ShapeDtypeStruct(side.shape, BF16)


def _mm(a3, w, *, out_dtype, tm, tn, n_row_tiles=None, w_rows=None, gain=None, bias=None, act=None, rope=None,
        kadd=None, side=None, name="mm"):
    bsz, rows, k = a3.shape
    row0, n = (0, w.shape[1]) if w_rows is None else w_rows
    ni = rows // tm if n_row_tiles is None else n_row_tiles
    nj = n // tn
    assert n % tn == 0 and row0 % 8 == 0 and (n_row_tiles is not None or rows % tm == 0)
    if w_rows is None:
        w_spec = pl.BlockSpec((k, tn), lambda bi, i, j: (0, j))
    elif row0 % tn == 0:
        w_spec = pl.BlockSpec((tn, k), lambda bi, i, j: (row0 // tn + j, 0))
    else:
        w_spec = pl.BlockSpec((pl.Element(tn), pl.Element(k)),
                              lambda bi, i, j: (pl.multiple_of(row0 + j * tn, 8), 0))
    in_specs = [pl.BlockSpec((None, tm, k), lambda bi, i, j: (bi, i, 0)), w_spec]
    args = [a3, w]
    if gain is not None:
        in_specs.append(pl.BlockSpec((1, k), lambda bi, i, j: (0, 0)))
        args.append(gain.reshape(1, k).astype(F32))
    if bias is not None:
        in_specs.append(pl.BlockSpec((1, tn), lambda bi, i, j: (0, j)))
        args.append(bias.reshape(1, n).astype(F32))
    for t in (rope or ()) + (kadd or ()):
        in_specs.append(pl.BlockSpec((None, tm, t.shape[2]), lambda bi, i, j: (bi, i, 0)))
        args.append(t)
    out_specs = [pl.BlockSpec((None, tm, tn), lambda bi, i, j: (bi, i, j))]
    out_shape = [jax.ShapeDtypeStruct((bsz, ni * tm, n), out_dtype)]
    if side is not None:
        spec, shape = _side_specs(side, bsz * ni * nj, lambda bi, i, j: (bi * ni + i) * nj + j)
        in_specs.append(spec)
        args.append(side)
        out_specs.append(spec)
        out_shape.append(shape)
    body = functools.partial(_mm_body, has_gain=gain is not None, has_bias=bias is not None, act=act,
                             rope=rope is not None, kadd=kadd is not None, has_side=side is not None,
                             w_rows=w_rows is not None)
    outs = pl.pallas_call(
        body,
        grid=(bsz, ni, nj),
        in_specs=in_specs,
        out_specs=out_specs,
        out_shape=out_shape,
        compiler_params=_cparams(("parallel", "parallel", "arbitrary") if side is None else ("arbitrary",) * 3,
                                 VMEM_BIG),
        name=name,
    )(*args)
    return outs[0] if side is None else outs


def _mm_t_body(a_ref, wt_ref, gain_ref, bias_ref, o_ref):
    af = a_ref[...].astype(F32)
    a = af * lax.rsqrt(jnp.mean(af * af, axis=-1, keepdims=True) + RMS_EPS) * gain_ref[...]
    r = lax.dot_general(wt_ref[...], a.astype(BF16), (((1,), (1,)), ((), ())), preferred_element_type=F32)
    o_ref[...] = (r + bias_ref[...]).astype(o_ref.dtype)


def _mm_t(a3, wt, gain, bias_col, *, out_dtype, tm, tn, name):
    bsz, rows, k = a3.shape
    n = wt.shape[0]
    return pl.pallas_call(
        _mm_t_body,
        grid=(bsz, rows // tm, n // tn),
        in_specs=[pl.BlockSpec((None, tm, k), lambda bi, i, j: (bi, i, 0)),
                  pl.BlockSpec((tn, k), lambda bi, i, j: (j, 0)),
                  pl.BlockSpec((1, k), lambda bi, i, j: (0, 0)),
                  pl.BlockSpec((tn, 1), lambda bi, i, j: (j, 0))],
        out_specs=pl.BlockSpec((None, tn, tm), lambda bi, i, j: (bi, j, i)),
        out_shape=jax.ShapeDtypeStruct((bsz, n, rows), out_dtype),
        compiler_params=_cparams(("parallel", "parallel", "arbitrary"), VMEM_BIG),
        name=name,
    )(a3, wt, gain.reshape(1, k).astype(F32), bias_col.reshape(n, 1).astype(F32))


def _mmk_body(*refs, n_pairs, has_gate, has_res, res_scale, has_side):
    it = iter(refs)
    a_refs = [next(it) for _ in range(n_pairs)]
    w_refs = [next(it) for _ in range(n_pairs)]
    g_refs = [next(it) for _ in range(n_pairs)] if has_gate else None
    res_ref = next(it) if has_res else None
    side_in = next(it) if has_side else None
    o_ref = next(it)
    if has_side:
        next(it)[...] = side_in[...].astype(BF16)
    acc_refs = [next(it) for _ in range(n_pairs)]
    kk = pl.program_id(3)

    @pl.when(kk == 0)
    def _():
        for acc in acc_refs:
            acc[...] = jnp.zeros_like(acc)

    for a_ref, w_ref, acc in zip(a_refs, w_refs, acc_refs):
        acc[...] += jnp.dot(a_ref[...], w_ref[...], preferred_element_type=F32)

    @pl.when(kk == pl.num_programs(3) - 1)
    def _():
        r = None
        for p, acc in enumerate(acc_refs):
            t = acc[...]
            if has_gate:
                t = t * g_refs[p][...].astype(F32)
            r = t if r is None else r + t
        if has_res:
            r = r + res_scale * res_ref[...]
        o_ref[...] = r.astype(o_ref.dtype)


def _mm_ktiled(a_list, w_list, *, out_dtype, rows, tm, tn, nk, gates=None, res=None, res_scale=1.0, side=None,
               name="mmk"):
    bsz = a_list[0].shape[0]
    n = w_list[0].shape[1]
    npairs = len(a_list)
    tks = [a.shape[2] // nk for a in a_list]
    assert all(a.shape[2] == tk * nk and tk % LANES == 0 for a, tk in zip(a_list, tks))
    o_spec = pl.BlockSpec((None, tm, tn), lambda bi, i, j, kk: (bi, i, j))
    in_specs = ([pl.BlockSpec((None, tm, tk), lambda bi, i, j, kk: (bi, i, kk)) for tk in tks]
                + [pl.BlockSpec((tk, tn), lambda bi, i, j, kk: (kk, j)) for tk in tks])
    args = list(a_list) + list(w_list)
    if gates is not None:
        for p in range(npairs):
            in_specs.append(pl.BlockSpec((None, tm, tn), lambda bi, i, j, kk, p=p: (bi, i, p * (n // tn) + j)))
            args.append(gates)
    if res is not None:
        in_specs.append(o_spec)
        args.append(res)
    ni, nj = rows // tm, n // tn
    out_specs = [o_spec]
    out_shape = [jax.ShapeDtypeStruct((bsz, rows, n), out_dtype)]
    if side is not None:
        spec, shape = _side_specs(side, bsz * ni * nj * nk, lambda bi, i, j, kk: ((bi * ni + i) * nj + j) * nk + kk)
        in_specs.append(spec)
        args.append(side)
        out_specs.append(spec)
        out_shape.append(shape)
    body = functools.partial(_mmk_body, n_pairs=npairs, has_gate=gates is not None, has_res=res is not None,
                             res_scale=res_scale, has_side=side is not None)
    outs = pl.pallas_call(
        body,
        grid=(bsz, ni, nj, nk),
        in_specs=in_specs,
        out_specs=out_specs,
        out_shape=out_shape,
        scratch_shapes=[pltpu.VMEM((tm, tn), F32) for _ in range(npairs)],
        compiler_params=_cparams(("parallel", "parallel", "parallel", "arbitrary") if side is None
                                 else ("arbitrary",) * 4, VMEM_BIG),
        name=name,
    )(*args)
    return outs[0] if side is None else outs


def _silu(x):
    return x * jax.nn.sigmoid(x)


def _ssd_body(xs_ref, b_ref, c_ref, z_ref, dt_ref, wx_ref, wb_ref, wc_ref, bx_ref, bb_ref, bc_ref,
              dtb_ref, alog_ref, dskip_ref, ng_ref, e_ref, o_ref,
              state_ref, extx_ref, extb_ref, extc_ref, y_ref, *, hg):
    c = pl.program_id(2)
    q = CHUNK
    p = SSM_HEAD_DIM
    first = c == 0
    row = lax.broadcasted_iota(I32, (q, 1), 0)
    live = jnp.logical_or(jnp.logical_not(first), row >= META_PAD)

    @pl.when(first)
    def _():
        state_ref[...] = jnp.zeros_like(state_ref)
        extx_ref[0:8, :] = jnp.zeros((8, extx_ref.shape[1]), F32)
        extb_ref[0:8, :] = jnp.zeros((8, extb_ref.shape[1]), F32)
        extc_ref[0:8, :] = jnp.zeros((8, extc_ref.shape[1]), F32)

    def conv(ext_ref, u_ref, w_ref, bias_ref):
        @pl.when(first)
        def _():
            ext_ref[8:8 + q, :] = jnp.where(live, u_ref[...].astype(F32), 0.0)

        @pl.when(jnp.logical_not(first))
        def _():
            ext_ref[8:8 + q, :] = u_ref[...].astype(F32)

        acc = bias_ref[...]
        for k in range(SSM_CONV):
            acc = acc + ext_ref[pl.ds(8 - (SSM_CONV - 1) + k, q), :] * w_ref[k:k + 1, :]
        ext_ref[0:8, :] = ext_ref[q:q + 8, :]
        return _silu(acc)

    xs = conv(extx_ref, xs_ref, wx_ref, bx_ref)
    bm = conv(extb_ref, b_ref, wb_ref, bb_ref)
    cm = conv(extc_ref, c_ref, wc_ref, bc_ref)

    x = dt_ref[...] + dtb_ref[...]
    dt = jnp.maximum(x, 0.0) + jnp.log1p(jnp.exp(-jnp.abs(x)))
    dt = jnp.where(live, dt, 0.0)
    a = -jnp.exp(alog_ref[...])
    r_i = lax.broadcasted_iota(I32, (q, q), 0)
    c_i = lax.broadcasted_iota(I32, (q, q), 1)
    causal = r_i >= c_i
    hi = lax.Precision.HIGHEST
    a_cs = jnp.dot(causal.astype(F32), dt * a, precision=hi, preferred_element_type=F32)
    a_cs_t = a_cs.T
    ea = jnp.exp(a_cs)
    decay_end = jnp.exp(a_cs[q - 1:q, :] - a_cs)
    e = e_ref[...]

    def spread(v):
        v_hi = v.astype(BF16)
        v_lo = (v - v_hi.astype(F32)).astype(BF16)
        return jnp.dot(v_hi, e, preferred_element_type=F32) + jnp.dot(v_lo, e, preferred_element_type=F32)

    dt_x = spread(dt)
    ea_x = spread(ea)
    de_x = spread(decay_end)

    xdt = xs * dt_x
    xdt_b = xdt.astype(BF16)
    cm_b = cm.astype(BF16)
    bm_b = bm.astype(BF16)
    cb = lax.dot_general(cm_b, bm_b, (((1,), (1,)), ((), ())), preferred_element_type=F32)
    prev = state_ref[...]
    y_off = jnp.dot(cm_b, prev.astype(BF16), preferred_element_type=F32) * ea_x
    def decay_scores(j):
        seg = a_cs[:, j:j + 1] - a_cs_t[j:j + 1, :]
        return (jnp.exp(jnp.where(causal, seg, NEG)) * cb).astype(BF16)

    m_next = decay_scores(0)
    for j in range(hg):
        m = m_next
        if j + 1 < hg:
            m_next = decay_scores(j + 1)
        y_ref[:, j * p:(j + 1) * p] = jnp.dot(m, xdt_b[:, j * p:(j + 1) * p], preferred_element_type=F32)
    y = y_ref[...] + y_off + xs * dskip_ref[...]
    state_ref[...] = prev * ea_x[q - 1:q, :] + jnp.dot(bm.T.astype(BF16), (xdt * de_x).astype(BF16),
                                                        preferred_element_type=F32)
    yz = y * _silu(z_ref[...].astype(F32))
    yn = yz * lax.rsqrt(jnp.mean(yz * yz, axis=-1, keepdims=True) + RMS_EPS) * ng_ref[...]
    o_ref[...] = yn.astype(o_ref.dtype)


def _ssd(xbc3, z3, dt_g, conv_w, conv_b, dtb_g, alog_g, dskip_x, norm_g, d_inner, heads):
    bsz, rows, conv_dim = xbc3.shape
    g, n, p, q = SSM_GROUPS, SSM_STATE, SSM_HEAD_DIM, CHUNK
    hg = heads // g
    gw = hg * p
    nc = rows // q
    assert gw % LANES == 0 and d_inner % n == 0 and hg <= LANES
    boff = d_inner // n
    expand = (jnp.arange(LANES)[:, None] == (jnp.arange(gw) // p)[None, :]).astype(BF16)

    def chunk(c):
        return (c + nc - 1) % nc

    in_specs = [
        pl.BlockSpec((None, q, gw), lambda b, gi, c: (b, chunk(c), gi)),
        pl.BlockSpec((None, q, n), lambda b, gi, c: (b, chunk(c), boff + gi)),
        pl.BlockSpec((None, q, n), lambda b, gi, c: (b, chunk(c), boff + g + gi)),
        pl.BlockSpec((None, q, gw), lambda b, gi, c: (b, chunk(c), gi)),
        pl.BlockSpec((None, None, q, LANES), lambda b, gi, c: (b, gi, chunk(c), 0)),
        pl.BlockSpec((SSM_CONV, gw), lambda b, gi, c: (0, gi)),
        pl.BlockSpec((SSM_CONV, n), lambda b, gi, c: (0, boff + gi)),
        pl.BlockSpec((SSM_CONV, n), lambda b, gi, c: (0, boff + g + gi)),
        pl.BlockSpec((1, gw), lambda b, gi, c: (0, gi)),
        pl.BlockSpec((1, n), lambda b, gi, c: (0, boff + gi)),
        pl.BlockSpec((1, n), lambda b, gi, c: (0, boff + g + gi)),
        pl.BlockSpec((None, 1, LANES), lambda b, gi, c: (gi, 0, 0)),
        pl.BlockSpec((None, 1, LANES), lambda b, gi, c: (gi, 0, 0)),
        pl.BlockSpec((1, gw), lambda b, gi, c: (0, gi)),
        pl.BlockSpec((1, gw), lambda b, gi, c: (0, gi)),
        pl.BlockSpec((LANES, gw), lambda b, gi, c: (0, 0)),
    ]
    return pl.pallas_call(
        functools.partial(_ssd_body, hg=hg),
        grid=(bsz, g, nc),
        in_specs=in_specs,
        out_specs=pl.BlockSpec((None, q, gw), lambda b, gi, c: (b, chunk(c), gi)),
        out_shape=jax.ShapeDtypeStruct((bsz, rows, d_inner), BF16),
        scratch_shapes=[pltpu.VMEM((n, gw), F32), pltpu.VMEM((q + 8, gw), F32), pltpu.VMEM((q + 8, n), F32),
                        pltpu.VMEM((q + 8, n), F32), pltpu.VMEM((q, gw), F32)],
        compiler_params=_cparams(("parallel", "parallel", "arbitrary")),
        name="ssd",
    )(xbc3, xbc3, xbc3, z3, dt_g, conv_w, conv_w, conv_w, conv_b, conv_b, conv_b, dtb_g, alog_g,
      dskip_x, norm_g, expand)


def _attn_body(qi_ref, ki_ref, q_ref, km_ref, vm_ref, k_ref, v_ref, *rest, heads, n_side):
    side_in, o_ref, side_out = rest[:n_side], rest[n_side], rest[n_side + 1:2 * n_side + 1]
    m_ref, acc_ref = rest[2 * n_side + 1:]
    for src, dst in zip(side_in, side_out):
        dst[...] = src[...].astype(BF16)
    step_id = pl.program_id(2)
    qi, ki = qi_ref[step_id], ki_ref[step_id]
    tq = q_ref.shape[0]

    def step(segments):
        def scores(h):
            out = []
            for kr, _, key0, nk, q0, mask in segments:
                qh = q_ref[q0:, h * MLA_QKP:(h + 1) * MLA_QKP]
                kh = kr[key0:key0 + nk, h * MLA_QKP:(h + 1) * MLA_QKP]
                st = lax.dot_general(kh, qh, (((1,), (1,)), ((), ())), preferred_element_type=F32)
                key = key0 + lax.broadcasted_iota(I32, st.shape, 0)
                if mask == "meta":
                    st = jnp.where(key >= META_PAD, st, NEG)
                elif mask == "causal":
                    st = jnp.where(key <= q0 + lax.broadcasted_iota(I32, st.shape, 1), st, NEG)
                out.append(st)
            return out

        def probs(h, sts):
            m_prev = m_ref[h]
            m_new = m_prev
            for (_, _, _, _, q0, _), st in zip(segments, sts):
                cm = jnp.max(st, axis=0, keepdims=True)
                if q0:
                    cm = jnp.concatenate([jnp.full((1, q0), NEG, F32), cm], axis=1)
                m_new = jnp.maximum(m_new, cm)
            m_ref[h] = m_new
            pts = [jnp.exp2((st - m_new[:, q0:]).astype(BF16)) for (_, _, _, _, q0, _), st in zip(segments, sts)]
            return jnp.exp2(m_prev - m_new), pts

        def accumulate(h, alpha, pts):
            sl = slice(h * MLA_VP, (h + 1) * MLA_VP)
            acc = acc_ref[sl, :] * alpha
            partial = []
            for (_, vr, key0, nk, q0, _), pt in zip(segments, pts):
                contrib = jnp.dot(vr[sl, key0:key0 + nk], pt, preferred_element_type=F32)
                if q0:
                    partial.append((q0, contrib))
                else:
                    acc = acc + contrib
            acc_ref[sl, :] = acc
            for q0, contrib in partial:
                acc_ref[sl, q0:] += contrib

        st_q, pr_q = {}, {}
        for stage in range(heads + 2):
            if stage < heads:
                st_q[stage] = scores(stage)
            if 0 <= stage - 1 < heads:
                pr_q[stage - 1] = probs(stage - 1, st_q.pop(stage - 1))
            if 0 <= stage - 2 < heads:
                accumulate(stage - 2, *pr_q.pop(stage - 2))

    def start():
        m_ref[...] = jnp.full_like(m_ref, NEG)
        acc_ref[...] = jnp.zeros_like(acc_ref)

    def finish():
        for h in range(heads):
            num = acc_ref[h * MLA_VP:h * MLA_VP + MLA_V, :]
            den = acc_ref[h * MLA_VP + MLA_V:h * MLA_VP + MLA_V + 1, :]
            o_ref[:, h * MLA_V:(h + 1) * MLA_V] = (num / den).T.astype(o_ref.dtype)

    tk = k_ref.shape[0]
    meta = (km_ref, vm_ref, 0, km_ref.shape[0], 0, "meta")
    full = (k_ref, v_ref, 0, tk, 0, None)
    half = tk // 2
    if half % (2 * LANES) == 0:
        diag = [(k_ref, v_ref, 0, half, 0, "causal"), (k_ref, v_ref, half, half, half, "causal")]
    else:
        diag = [(k_ref, v_ref, 0, tk, 0, "causal")]

    @pl.when(jnp.logical_and(ki == 0, qi == 0))
    def _():
        start()
        step([meta] + diag)
        finish()

    @pl.when(jnp.logical_and(ki == 0, qi > 0))
    def _():
        start()
        step([meta, full])

    @pl.when(jnp.logical_and(ki > 0, ki < qi))
    def _():
        step([full])

    @pl.when(jnp.logical_and(ki > 0, ki == qi))
    def _():
        step(diag)
        finish()


def _attention(q3, k3, vt3, seq, sides=()):
    bsz = q3.shape[0]
    hp = 8
    t = _pick(seq, (512, 256, 128))
    nq = seq // t
    meta_blk = seq // CHUNK
    qw, vw = hp * MLA_QKP, hp * MLA_VP
    qi_tab = jnp.asarray([qi for qi in range(nq) for _ in range(qi + 1)], I32)
    ki_tab = jnp.asarray([ki for qi in range(nq) for ki in range(qi + 1)], I32)

    def q_idx(b, h, s, qt, kt):
        return (b, qt[s], h)

    def k_idx(b, h, s, qt, kt):
        return (b, kt[s], h)

    def vt_idx(b, h, s, qt, kt):
        return (b, h, kt[s])

    nh, npairs = MLA_HEADS // hp, int(qi_tab.shape[0])
    side_specs = [_side_specs(s_arr, bsz * nh * npairs, lambda b, h, s, qt, kt: (b * nh + h) * npairs + s)
                  for s_arr in sides]
    grid_spec = pltpu.PrefetchScalarGridSpec(
        num_scalar_prefetch=2,
        grid=(bsz, nh, npairs),
        in_specs=[
            pl.BlockSpec((None, t, qw), q_idx),
            pl.BlockSpec((None, CHUNK, qw), lambda b, h, s, qt, kt: (b, meta_blk, h)),
            pl.BlockSpec((None, vw, CHUNK), lambda b, h, s, qt, kt: (b, h, meta_blk)),
            pl.BlockSpec((None, t, qw), k_idx),
            pl.BlockSpec((None, vw, t), vt_idx),
        ] + [spec for spec, _ in side_specs],
        out_specs=[pl.BlockSpec((None, t, hp * MLA_V), q_idx)] + [spec for spec, _ in side_specs],
        scratch_shapes=[pltpu.VMEM((hp, 1, t), F32), pltpu.VMEM((vw, t), F32)],
    )
    return pl.pallas_call(
        functools.partial(_attn_body, heads=hp, n_side=len(sides)),
        grid_spec=grid_spec,
        out_shape=[jax.ShapeDtypeStruct((bsz, seq, MLA_HEADS * MLA_V), BF16)] + [shape for _, shape in side_specs],
        compiler_params=_cparams(("arbitrary", "arbitrary", "arbitrary")),
        name="mla_attention",
    )(qi_tab, ki_tab, q3, k3, vt3, k3, vt3, *sides)


def _router_body(h_ref, wr_ref, rb_ref, e_ref, pos_ref, w_ref, cnt_ref, carry_ref):
    i = pl.program_id(0)
    ne, ng = N_EXPERTS, N_EXPERT_GROUPS
    gs = ne // ng
    tm = h_ref.shape[0]

    @pl.when(i == 0)
    def _():
        carry_ref[...] = jnp.zeros_like(carry_ref)

    logits = lax.dot_general(wr_ref[...], h_ref[...], (((1,), (1,)), ((), ())), precision=lax.Precision.HIGHEST,
                             preferred_element_type=F32)
    scores = jax.nn.sigmoid(logits)
    choice = scores + rb_ref[...]
    sub = lax.broadcasted_iota(I32, (gs, tm), 0)
    grp_rows = []
    for g in range(ng):
        blk = choice[g * gs:(g + 1) * gs, :]
        m1 = jnp.max(blk, axis=0, keepdims=True)
        first = jnp.min(jnp.where(blk == m1, sub, gs), axis=0, keepdims=True)
        m2 = jnp.max(jnp.where(sub == first, -jnp.inf, blk), axis=0, keepdims=True)
        grp_rows.append(m1 + m2)
    grp = jnp.concatenate(grp_rows, axis=0)
    grank = jnp.zeros((ng, tm), I32)
    gidx = lax.broadcasted_iota(I32, (ng, tm), 0)
    for g in range(ng):
        rowv = grp[g:g + 1, :]
        beats = jnp.logical_or(rowv > grp, jnp.logical_and(rowv == grp, gidx > g))
        grank = grank + beats.astype(I32)
    gsel = (grank < TOPK_GROUPS).astype(F32)
    esel = jnp.concatenate([jnp.broadcast_to(gsel[g:g + 1, :], (gs, tm)) for g in range(ng)], axis=0)
    masked = jnp.where(esel > 0.0, choice, -jnp.inf)
    eidx = lax.broadcasted_iota(I32, (ne, tm), 0)
    rank = jnp.zeros((ne, tm), I32)
    for e in range(ne):
        rowv = masked[e:e + 1, :]
        beats = jnp.logical_or(rowv > masked, jnp.logical_and(rowv == masked, eidx > e))
        rank = rank + beats.astype(I32)
    top = jnp.logical_and(rank < TOP_K, esel > 0.0)
    topf = top.astype(F32)
    wsel = jnp.where(top, scores, 0.0)
    wn = wsel / jnp.sum(wsel, axis=0, keepdims=True) * ROUTED_SCALE
    r_i = lax.broadcasted_iota(I32, (tm, tm), 0)
    c_i = lax.broadcasted_iota(I32, (tm, tm), 1)
    before = (r_i < c_i).astype(BF16)
    pos = jnp.dot(topf.astype(BF16), before, preferred_element_type=F32) + carry_ref[:, :1]
    carry_ref[...] = carry_ref[...] + jnp.sum(topf, axis=1, keepdims=True)
    cnt_ref[...] = carry_ref[...]
    e_rows, p_rows, w_rows = [], [], []
    for k in range(TOP_K):
        hit = rank == k
        e_rows.append(jnp.sum(jnp.where(hit, eidx, 0), axis=0, keepdims=True))
        p_rows.append(jnp.sum(jnp.where(hit, pos, 0.0), axis=0, keepdims=True))
        w_rows.append(jnp.sum(jnp.where(hit, wn, 0.0), axis=0, keepdims=True))
    e_ref[...] = jnp.concatenate(e_rows, axis=0)
    pos_ref[...] = jnp.concatenate(p_rows, axis=0).astype(I32)
    w_ref[...] = jnp.concatenate(w_rows, axis=0)


def _router(h2, w_router, router_bias):
    t, d = h2.shape
    tm = _pick(t, (256, 128))
    slot = pl.BlockSpec((TOP_K, tm), lambda i: (0, i))
    rb = jnp.broadcast_to(router_bias.astype(F32)[:, None], (N_EXPERTS, tm))
    return pl.pallas_call(
        _router_body,
        grid=(t // tm,),
        in_specs=[pl.BlockSpec((tm, d), lambda i: (i, 0)), pl.BlockSpec((N_EXPERTS, d), lambda i: (0, 0)),
                  pl.BlockSpec((N_EXPERTS, tm), lambda i: (0, 0))],
        out_specs=[slot, slot, slot, pl.BlockSpec((N_EXPERTS, LANES), lambda i: (0, 0))],
        out_shape=[jax.ShapeDtypeStruct((TOP_K, t), I32), jax.ShapeDtypeStruct((TOP_K, t), I32),
                   jax.ShapeDtypeStruct((TOP_K, t), F32), jax.ShapeDtypeStruct((N_EXPERTS, LANES), F32)],
        scratch_shapes=[pltpu.VMEM((N_EXPERTS, LANES), F32)],
        compiler_params=_cparams(("arbitrary",)),
        name="moe_router",
    )(h2, w_router.T.astype(F32), rb)


def _load_slots(dest_hbm, dest_ref, sem):
    cp = pltpu.make_async_copy(dest_hbm.at[pl.program_id(0)], dest_ref, sem)
    cp.start()
    cp.wait()


def _dispatch_body(cnt_ref, start_ref, dest_hbm, x_ref, xs_ref, dest_ref, zero_ref, sem_ref, *, block_rows):
    i = pl.program_id(0)
    tm = x_ref.shape[0]
    _load_slots(dest_hbm, dest_ref, sem_ref.at[2])

    def row_copy(tok, k):
        return pltpu.make_async_copy(x_ref.at[pl.ds(tok, 1), :], xs_ref.at[pl.ds(dest_ref[k, tok], 1), :], sem_ref.at[0])

    def issue(tok, carry):
        for k in range(TOP_K):
            row_copy(tok, k).start(priority=k % 2)
        return carry

    lax.fori_loop(0, tm, issue, 0)

    @pl.when(i == 0)
    def _():
        zero_ref[...] = jnp.zeros_like(zero_ref)

        def fill(e, carry):
            cnt = cnt_ref[e]
            padded = (cnt + block_rows - 1) // block_rows * block_rows
            base = start_ref[e] + cnt

            def zcopy(r):
                return pltpu.make_async_copy(zero_ref.at[pl.ds(0, 1), :], xs_ref.at[pl.ds(base + r, 1), :], sem_ref.at[1])

            def zstart(r, c2):
                zcopy(r).start()
                return c2

            def zwait(r, c2):
                zcopy(r).wait()
                return c2

            lax.fori_loop(0, padded - cnt, zstart, 0)
            lax.fori_loop(0, padded - cnt, zwait, 0)
            return carry

        lax.fori_loop(0, N_EXPERTS, fill, 0)

    def drain(tok, carry):
        for k in range(TOP_K):
            row_copy(tok, k).wait()
        return carry

    lax.fori_loop(0, tm, drain, 0)


def _slot_tiles(dest, tm):
    k, t = dest.shape
    return dest.reshape(k, t // tm, tm).transpose(1, 0, 2)


def _dispatch(hp2, dest, counts, starts, n_rows):
    t, w = hp2.shape
    tm = _pick(t, (256, 128))
    grid_spec = pltpu.PrefetchScalarGridSpec(
        num_scalar_prefetch=2,
        grid=(t // tm,),
        in_specs=[pl.BlockSpec(memory_space=pl.ANY),
                  pl.BlockSpec((tm, w), lambda i, c, s: (i, 0))],
        out_specs=pl.BlockSpec(memory_space=pl.ANY),
        scratch_shapes=[pltpu.SMEM((TOP_K, tm), I32), pltpu.VMEM((8, w), U32), pltpu.SemaphoreType.DMA((3,))],
    )
    return pl.pallas_call(
        functools.partial(_dispatch_body, block_rows=MOE_ROWS),
        grid_spec=grid_spec,
        out_shape=jax.ShapeDtypeStruct((n_rows, w), U32),
        compiler_params=pltpu.CompilerParams(dimension_semantics=("arbitrary",)),
        name="moe_dispatch",
    )(counts, starts, _slot_tiles(dest, tm), hp2)


def _expert_body(be_ref, nu_ref, x_ref, wg_ref, wu_ref, wd_ref, o_ref):
    j = pl.program_id(0)

    @pl.when(j < nu_ref[0])
    def _():
        half = x_ref.shape[1]
        lo, hi = _unpack_halves(x_ref[...])
        lo, hi = lo.astype(BF16), hi.astype(BF16)

        def proj(w_ref):
            return (jnp.dot(lo, w_ref[:half, :], preferred_element_type=F32)
                    + jnp.dot(hi, w_ref[half:, :], preferred_element_type=F32))

        hmid = (_silu(proj(wg_ref)) * proj(wu_ref)).astype(BF16)
        o_ref[...] = _pack_halves(jnp.dot(hmid, wd_ref[...], preferred_element_type=F32))


def _experts(xs, wg, wu, wd, block_e, n_used):
    rows, w = xs.shape
    _, d, ff = wg.shape
    bm = MOE_ROWS
    nb = rows // bm

    def row_idx(j, be, nu):
        return (jnp.minimum(j, nu[0] - 1), 0)

    def w_idx(j, be, nu):
        return (be[jnp.minimum(j, nu[0] - 1)], 0, 0)

    grid_spec = pltpu.PrefetchScalarGridSpec(
        num_scalar_prefetch=2,
        grid=(nb,),
        in_specs=[pl.BlockSpec((bm, w), row_idx),
                  pl.BlockSpec((None, d, ff), w_idx), pl.BlockSpec((None, d, ff), w_idx),
                  pl.BlockSpec((None, ff, d), w_idx)],
        out_specs=pl.BlockSpec((bm, w), row_idx),
    )
    return pl.pallas_call(
        _expert_body,
        grid_spec=grid_spec,
        out_shape=jax.ShapeDtypeStruct((rows, w), U32),
        compiler_params=_cparams(("arbitrary",), 60 * 1024 * 1024),
        name="moe_experts",
    )(block_e, n_used, xs, wg, wu, wd)


def _combine_body(dest_hbm, wt_ref, h_ref, ysh_ref, g_ref, b_ref, y_ref, o_ref, dest_ref, gbuf_ref, sem_ref, *, alpha):
    tm = h_ref.shape[0]
    _load_slots(dest_hbm, dest_ref, sem_ref.at[1])

    def row_copy(tok, k):
        return pltpu.make_async_copy(y_ref.at[pl.ds(dest_ref[k, tok], 1), :], gbuf_ref.at[k, pl.ds(tok, 1), :],
                                     sem_ref.at[0])

    def issue(tok, carry):
        for k in range(TOP_K):
            row_copy(tok, k).start(priority=k % 2)
        return carry

    def drain(tok, carry):
        for k in range(TOP_K):
            row_copy(tok, k).wait()
        return carry

    lax.fori_loop(0, tm, issue, 0)
    lax.fori_loop(0, tm, drain, 0)
    lo, hi = _unpack_halves(ysh_ref[...])
    wt = wt_ref[...]
    for k in range(TOP_K):
        glo, ghi = _unpack_halves(gbuf_ref[k])
        wk = wt[:, k:k + 1]
        lo = lo + glo * wk
        hi = hi + ghi * wk
    x = alpha * h_ref[...] + jnp.concatenate([lo, hi], axis=1)
    mu = jnp.mean(x, axis=-1, keepdims=True)
    xc = x - mu
    var = jnp.mean(xc * xc, axis=-1, keepdims=True)
    o_ref[...] = xc * lax.rsqrt(var + LN_EPS) * g_ref[...] + b_ref[...]


def _combine(h2, ysh, y_sorted, dest, wt, g, b, alpha):
    t, d = h2.shape
    w = d // 2
    tm = _pick(t, (128,))
    vec = pl.BlockSpec((1, d), lambda i: (0, 0))
    return pl.pallas_call(
        functools.partial(_combine_body, alpha=alpha),
        grid=(t // tm,),
        in_specs=[pl.BlockSpec(memory_space=pl.ANY),
                  pl.BlockSpec((tm, TOP_K), lambda i: (i, 0)),
                  pl.BlockSpec((tm, d), lambda i: (i, 0)),
                  pl.BlockSpec((tm, w), lambda i: (i, 0)),
                  vec, vec,
                  pl.BlockSpec(memory_space=pl.ANY)],
        out_specs=pl.BlockSpec((tm, d), lambda i: (i, 0)),
        out_shape=jax.ShapeDtypeStruct((t, d), F32),
        scratch_shapes=[pltpu.SMEM((TOP_K, tm), I32), pltpu.VMEM((TOP_K, tm, w), U32), pltpu.SemaphoreType.DMA((2,))],
        compiler_params=_cparams(("arbitrary",)),
        name="moe_combine",
    )(_slot_tiles(dest, tm), wt, h2, ysh, g.reshape(1, d), b.reshape(1, d), y_sorted)


def _rope_tables(pos):
    inv_freq = ROPE_THETA ** (-jnp.arange(0, MLA_ROPE, 2, dtype=F32) / MLA_ROPE)
    ang = pos.astype(F32)[..., None] * inv_freq
    return jnp.cos(ang), jnp.sin(ang)


def kernel(x, positions, meta_tokens, ln_in_g, ln_in_b, w_in, b_gate, conv_w, conv_b, dt_bias, a_log, d_skip,
           ssm_norm_g, w_ssm_proj, q_a_norm_g, w_q_b, kv_a_norm_g, w_kv_b, w_attn_proj, w_out, ln1_g, ln1_b,
           w_router, router_bias, w_exp_gate, w_exp_up, w_exp_down, w_sh_gate, w_sh_up, w_sh_down, ln2_g, ln2_b):
    bsz, seq, d = x.shape
    depth = w_in.shape[0]
    heads = dt_bias.shape[-1]
    d_inner = w_ssm_proj.shape[1]
    conv_dim = conv_w.shape[-1]
    q_rank = w_q_b.shape[1]
    kv_rank = w_kv_b.shape[1]
    hh = MLA_HEADS
    g = SSM_GROUPS
    hg = heads // g
    assert seq % CHUNK == 0 and d % (2 * LANES) == 0
    lp = seq + CHUNK
    t = bsz * seq
    alpha = (2.0 * depth) ** 0.25

    meta_chunk = jnp.concatenate([jnp.zeros((META_PAD, d), x.dtype), meta_tokens.astype(x.dtype)], axis=0)
    pos =jnp.concatenate([positions.astype(I32) + N_META, jnp.zeros((bsz, META_PAD), I32),
                           jnp.broadcast_to(jnp.arange(N_META, dtype=I32), (bsz, N_META))], axis=1)
    cos, sin = _rope_tables(pos)
    zr = jnp.zeros_like(cos)
    rope_c = jnp.concatenate([cos, cos, zr, zr], axis=-1)[:, :seq]
    rope_slo = jnp.concatenate([-sin, zr, zr, zr], axis=-1)[:, :seq]
    rope_shi = jnp.concatenate([zr, sin, zr, zr], axis=-1)[:, :seq]
    k_c = jnp.concatenate([cos, cos, zr, zr], axis=-1)
    k_s = jnp.concatenate([-sin, sin, zr, zr], axis=-1)

    h_f32, h_b16 = _layer_norm_in(x, meta_chunk, ln_in_g, ln_in_b)
    tm_all = _pick(lp, (1408, 1152, 1024, 896, 768, 640, 512, 384, 256, 128))
    tm_seq = _pick(seq, (1024, 512, 256, 128))

    out = None
    for l in range(depth):
        offs = [0]
        for wdt in (d_inner, conv_dim, heads, q_rank, kv_rank, MLA_ROPE, 2 * d):
            offs.append(offs[-1] + wdt)
        wt = jnp.swapaxes(w_in[l], 0, 1)
        w_kr = wt[offs[5]:offs[6]]
        small_cols = q_rank + kv_rank + 2 * MLA_ROPE + heads
        small_n = small_cols + (-small_cols) % 256
        wt_small = jnp.concatenate([wt[offs[3]:offs[4]], wt[offs[4]:offs[5]], w_kr, w_kr, wt[offs[2]:offs[3]],
                                    jnp.zeros((small_n - small_cols, d), F32)], axis=0)

        n_exp, _, ff = w_exp_gate[l].shape
        z3 = _mm(h_b16, wt, w_rows=(offs[0], d_inner), out_dtype=BF16, tm=tm_all, tn=512, name="in_proj_z")
        xbc3 = _mm(h_b16, wt, w_rows=(offs[1], conv_dim), out_dtype=BF16, tm=tm_all, tn=512, name="in_proj_xbc")
        small = _mm(h_b16, wt_small, w_rows=(0, small_n), out_dtype=F32, tm=tm_all, tn=256, name="in_proj_small")
        gates = _mm(h_b16, wt, w_rows=(offs[6], 2 * d), out_dtype=BF16, tm=tm_seq, tn=512, n_row_tiles=seq // tm_seq,
                    bias=b_gate[l], act="sigmoid", name="in_proj_gates")
        o = 0
        q_a = small[:, :, o:o + q_rank]; o += q_rank
        kv_a = small[:, :, o:o + kv_rank]; o += kv_rank
        kr2 = small[:, :, o:o + 2 * MLA_ROPE]; o += 2 * MLA_ROPE
        dt_raw = small[:, :, o:o + heads]

        dt_g = jnp.pad(dt_raw.reshape(bsz, lp, g, hg).transpose(0, 2, 1, 3), ((0, 0), (0, 0), (0, 0), (0, LANES - hg)))
        pad_h = lambda v: jnp.pad(v.astype(F32).reshape(g, 1, hg), ((0, 0), (0, 0), (0, LANES - hg)))
        y_ssm = _ssd(xbc3, z3, dt_g, conv_w[l].astype(F32), conv_b[l].reshape(1, conv_dim).astype(F32),
                     pad_h(dt_bias[l]), pad_h(a_log[l]),
                     jnp.repeat(d_skip[l].astype(F32), SSM_HEAD_DIM).reshape(1, d_inner),
                     ssm_norm_g[l].reshape(1, d_inner).astype(F32), d_inner, heads)

        qscale = (MLA_QK ** -0.5) * math.log2(math.e)
        head_pad = ((0, 0), (0, 0), (0, MLA_QKP - MLA_QK))
        w_q = jnp.pad(w_q_b[l].reshape(q_rank, hh, MLA_QK), head_pad).reshape(q_rank, hh * MLA_QKP).astype(BF16)
        q3 = _mm(q_a, w_q, out_dtype=BF16, tm=tm_seq, tn=4 * MLA_QKP, n_row_tiles=seq // tm_seq,
                 gain=q_a_norm_g[l] * qscale, rope=(rope_c, rope_slo, rope_shi), name="q_proj")
        wkv = w_kv_b[l].reshape(kv_rank, hh, MLA_NOPE + MLA_V)
        w_k = jnp.pad(wkv[:, :, :MLA_NOPE], ((0, 0), (0, 0), (0, MLA_QKP - MLA_NOPE)))
        w_k = w_k.reshape(kv_rank, hh * MLA_QKP).astype(BF16)
        w_vt = jnp.pad(wkv[:, :, MLA_NOPE:], ((0, 0), (0, 0), (0, MLA_VP - MLA_V)))
        w_vt = w_vt.reshape(kv_rank, hh * MLA_VP).T.astype(BF16)
        ones_rows = jnp.tile(jnp.concatenate([jnp.zeros((MLA_V,), F32), jnp.ones((MLA_VP - MLA_V,), F32)]), hh)
        k3 = _mm(kv_a, w_k, out_dtype=BF16, tm=tm_all, tn=4 * MLA_QKP, gain=kv_a_norm_g[l],
                 kadd=(kr2, k_c, k_s), name="k_proj")
        vt3 = _mm_t(kv_a, w_vt, kv_a_norm_g[l], ones_rows, out_dtype=BF16, tm=tm_all, tn=4 * MLA_VP,
                    name="v_proj")
        y_attn, wg_b, wu_b, wd_b = _attention(
            q3, k3, vt3, seq, [w_exp_gate[l].reshape(n_exp * d, ff), w_exp_up[l].reshape(n_exp * d, ff),
                               w_exp_down[l].reshape(n_exp * ff, d)])

        mixed = _mm_ktiled([y_ssm, y_attn], [w_ssm_proj[l].astype(BF16), w_attn_proj[l].astype(BF16)],
                           out_dtype=BF16, rows=seq, tm=tm_seq, tn=_pick(d, (1024, 512)), nk=4,
                           gates=gates, name="branch_proj")
        pre1 = _mm_ktiled([mixed], [w_out[l].astype(BF16)], out_dtype=F32, rows=seq, tm=tm_seq,
                          tn=_pick(d, (1024, 512)), nk=2, res=h_f32, res_scale=alpha, name="out_proj")
        h1, h1p = _layer_norm_pack(pre1.reshape(t, d), ln1_g[l], ln1_b[l])

        e_slot, pos_slot, w_slot, cnt = _router(h1, w_router[l], router_bias[l])
        counts = cnt[:, 0].astype(I32)
        padded = (counts + MOE_ROWS - 1) // MOE_ROWS * MOE_ROWS
        ends = jnp.cumsum(padded)
        starts = ends - padded
        onehot = e_slot[None] == jnp.arange(N_EXPERTS, dtype=I32)[:, None, None]
        dest = pos_slot + jnp.sum(jnp.where(onehot, starts[:, None, None], 0), axis=0)
        n_blocks = -(-(t * TOP_K) // MOE_ROWS) + N_EXPERTS
        blk0 = jnp.arange(n_blocks, dtype=I32) * MOE_ROWS
        block_e = jnp.minimum(jnp.sum((ends[None, :] <= blk0[:, None]).astype(I32), axis=1), N_EXPERTS - 1)
        n_used = (ends[-1] // MOE_ROWS).reshape(1)
        xs = _dispatch(h1p, dest, counts, starts, n_blocks * MOE_ROWS)
        y_sorted = _experts(xs, wg_b.reshape(n_exp, d, ff), wu_b.reshape(n_exp, d, ff), wd_b.reshape(n_exp, ff, d),
                            block_e, n_used)
        y_shared = _experts(h1p, w_sh_gate[l][None].astype(BF16), w_sh_up[l][None].astype(BF16),
                            w_sh_down[l][None].astype(BF16), jnp.zeros((t // MOE_ROWS,), I32),
                            jnp.full((1,), t // MOE_ROWS, I32))
        out = _combine(h1, y_shared, y_sorted, dest, w_slot.T, ln2_g[l], ln2_b[l], alpha)
        if l + 1 < depth:
            raise NotImplementedError("stacked layers need the meta rows carried through the channel mixer")
    return out.reshape(bsz, seq, d)
```

```python
import functools
import math

import jax
import jax.numpy as jnp
from jax import lax
from jax.experimental import pallas as pl
from jax.experimental.pallas import tpu as pltpu

F32 = jnp.float32
BF16 = jnp.bfloat16
U32 = jnp.uint32
I32 = jnp.int32

N_META = 16
CHUNK = 128
META_PAD = CHUNK - N_META
SSM_HEAD_DIM = 64
SSM_GROUPS = 8
SSM_STATE = 128
SSM_CONV = 4
MLA_HEADS = 64
MLA_NOPE = 128
MLA_ROPE = 64
MLA_V = 128
MLA_QK = MLA_NOPE + MLA_ROPE
ROPE_THETA = 10000.0
N_EXPERTS = 64
N_EXPERT_GROUPS = 8
TOPK_GROUPS = 4
TOP_K = 8
ROUTED_SCALE = 2.5
LN_EPS = 1e-5
RMS_EPS = 1e-6
NEG = -1e30
LANES = 128
MLA_QKP = 2 * LANES
MOE_ROWS = 256
MLA_VP = MLA_V + 16
VMEM_BIG = 56 * 1024 * 1024
VMEM_MID = 44 * 1024 * 1024


def _cparams(sem, vmem=VMEM_MID):
    return pltpu.CompilerParams(dimension_semantics=sem, vmem_limit_bytes=vmem)


def _pick(n, cands):
    for c in cands:
        if n % c == 0:
            return c
    raise ValueError(f"no tile for {n} in {cands}")


def _ln_body(x_ref, g_ref, b_ref, of_ref, ob_ref):
    x = x_ref[...]
    mu = jnp.mean(x, axis=-1, keepdims=True)
    xc = x - mu
    var = jnp.mean(xc * xc, axis=-1, keepdims=True)
    y = xc * lax.rsqrt(var + LN_EPS) * g_ref[...] + b_ref[...]
    of_ref[...] = y
    ob_ref[...] = y.astype(BF16)


def _ln_in_body(x_ref, meta_ref, g_ref, b_ref, of_ref, ob_ref, *, n_real):
    i = pl.program_id(1)

    @pl.when(i < n_real)
    def _():
        _ln_body(x_ref, g_ref, b_ref, of_ref, ob_ref)

    @pl.when(i == n_real)
    def _():
        _ln_body(meta_ref, g_ref, b_ref, of_ref, ob_ref)


def _layer_norm_in(x3, meta_chunk, g, b):
    bsz, seq, d = x3.shape
    tm = CHUNK
    n_real = seq // tm
    spec = pl.BlockSpec((None, tm, d), lambda bi, i: (bi, i, 0))
    vec = pl.BlockSpec((1, d), lambda bi, i: (0, 0))
    shape = (bsz, seq + tm, d)
    return pl.pallas_call(
        functools.partial(_ln_in_body, n_real=n_real),
        grid=(bsz, n_real + 1),
        in_specs=[pl.BlockSpec((None, tm, d), lambda bi, i: (bi, jnp.minimum(i, n_real - 1), 0)),
                  pl.BlockSpec((tm, d), lambda bi, i: (0, 0)), vec, vec],
        out_specs=[spec, spec],
        out_shape=[jax.ShapeDtypeStruct(shape, F32), jax.ShapeDtypeStruct(shape, BF16)],
        compiler_params=_cparams(("parallel", "arbitrary")),
        name="layer_norm",
    )(x3, meta_chunk, g.reshape(1, d), b.reshape(1, d))


def _pack_halves(y):
    n = y.shape[1] // 2
    lo = pltpu.bitcast(y[:, :n].astype(BF16).astype(F32), U32) >> 16
    hi = pltpu.bitcast(y[:, n:].astype(BF16).astype(F32), U32) & jnp.uint32(0xFFFF0000)
    return hi | lo


def _unpack_halves(w):
    lo = pltpu.bitcast(w << 16, F32)
    hi = pltpu.bitcast(w & jnp.uint32(0xFFFF0000), F32)
    return lo, hi


def _ln_pack_body(x_ref, g_ref, b_ref, of_ref, op_ref):
    x = x_ref[...]
    mu = jnp.mean(x, axis=-1, keepdims=True)
    xc = x - mu
    var = jnp.mean(xc * xc, axis=-1, keepdims=True)
    y = xc * lax.rsqrt(var + LN_EPS) * g_ref[...] + b_ref[...]
    of_ref[...] = y
    op_ref[...] = _pack_halves(y)


def _layer_norm_pack(x2, g, b):
    rows, d = x2.shape
    tm = _pick(rows, (256, 128))
    vec = pl.BlockSpec((1, d), lambda i: (0, 0))
    return pl.pallas_call(
        _ln_pack_body,
        grid=(rows // tm,),
        in_specs=[pl.BlockSpec((tm, d), lambda i: (i, 0)), vec, vec],
        out_specs=[pl.BlockSpec((tm, d), lambda i: (i, 0)), pl.BlockSpec((tm, d // 2), lambda i: (i, 0))],
        out_shape=[jax.ShapeDtypeStruct((rows, d), F32), jax.ShapeDtypeStruct((rows, d // 2), U32)],
        compiler_params=_cparams(("parallel",)),
        name="layer_norm_pack",
    )(x2, g.reshape(1, d), b.reshape(1, d))


def _mm_body(*refs, has_gain, has_bias, act, rope, kadd, has_side, w_rows):
    it = iter(refs)
    a_ref, w_ref = next(it), next(it)
    gain_ref = next(it) if has_gain else None
    bias_ref = next(it) if has_bias else None
    rope_refs = [next(it) for _ in range(3)] if rope else None
    kadd_refs = [next(it) for _ in range(3)] if kadd else None
    side_in = next(it) if has_side else None
    o_ref = next(it)
    if has_side:
        next(it)[...] = side_in[...].astype(BF16)
    a = a_ref[...]
    if has_gain:
        af = a.astype(F32)
        a = af * lax.rsqrt(jnp.mean(af * af, axis=-1, keepdims=True) + RMS_EPS) * gain_ref[...]
    if w_rows:
        r = lax.dot_general(a.astype(BF16), w_ref[...].astype(BF16), (((1,), (1,)), ((), ())),
                            preferred_element_type=F32)
    else:
        r = jnp.dot(a.astype(BF16), w_ref[...].astype(BF16), preferred_element_type=F32)
    tn = r.shape[1]
    if has_bias:
        r = r + bias_ref[...]
    if act == "sigmoid":
        r = jax.nn.sigmoid(r)
    if rope:
        c, slo, shi = (t[...] for t in rope_refs)
        half = MLA_ROPE // 2
        pieces = []
        for h0 in range(0, tn, MLA_QKP):
            rp = r[:, h0 + MLA_NOPE:h0 + MLA_QKP]
            pieces += [r[:, h0:h0 + MLA_NOPE], rp * c + pltpu.roll(rp, LANES - half, 1) * slo + pltpu.roll(rp, half, 1) * shi]
        r = jnp.concatenate(pieces, axis=1)
    if kadd:
        kr_ref, kc, ks = kadd_refs
        kr = kr_ref[...]
        piece = jnp.concatenate([jnp.zeros_like(kr), kr * kc[...] + pltpu.roll(kr, MLA_ROPE // 2, 1) * ks[...]], axis=1)
        r = r + jnp.concatenate([piece] * (tn // piece.shape[1]), axis=1)
    o_ref[...] = r.astype(o_ref.dtype)


def _side_specs(side, n_steps, lin):
    rows, cols = side.shape
    nb = max(c for c in range(1, n_steps + 1) if rows % c == 0 and (rows // c) % 16 == 0)
    spec = pl.BlockSpec((rows // nb, cols), lambda *ids: (jnp.minimum(lin(*ids), nb - 1), 0))
    return spec, jax.ShapeDtypeStruct(side.shape, BF16)


def _mm(a3, w, *, out_dtype, tm, tn, n_row_tiles=None, w_rows=None, gain=None, bias=None, act=None, rope=None,
        kadd=None, side=None, name="mm"):
    bsz, rows, k = a3.shape
    row0, n = (0, w.shape[1]) if w_rows is None else w_rows
    ni = rows // tm if n_row_tiles is None else n_row_tiles
    nj = n // tn
    assert n % tn == 0 and row0 % 8 == 0 and (n_row_tiles is not None or rows % tm == 0)
    if w_rows is None:
        w_spec = pl.BlockSpec((k, tn), lambda bi, i, j: (0, j))
    elif row0 % tn == 0:
        w_spec = pl.BlockSpec((tn, k), lambda bi, i, j: (row0 // tn + j, 0))
    else:
        w_spec = pl.BlockSpec((pl.Element(tn), pl.Element(k)),
                              lambda bi, i, j: (pl.multiple_of(row0 + j * tn, 8), 0))
    in_specs = [pl.BlockSpec((None, tm, k), lambda bi, i, j: (bi, i, 0)), w_spec]
    args = [a3, w]
    if gain is not None:
        in_specs.append(pl.BlockSpec((1, k), lambda bi, i, j: (0, 0)))
        args.append(gain.reshape(1, k).astype(F32))
    if bias is not None:
        in_specs.append(pl.BlockSpec((1, tn), lambda bi, i, j: (0, j)))
        args.append(bias.reshape(1, n).astype(F32))
    for t in (rope or ()) + (kadd or ()):
        in_specs.append(pl.BlockSpec((None, tm, t.shape[2]), lambda bi, i, j: (bi, i, 0)))
        args.append(t)
    out_specs = [pl.BlockSpec((None, tm, tn), lambda bi, i, j: (bi, i, j))]
    out_shape = [jax.ShapeDtypeStruct((bsz, ni * tm, n), out_dtype)]
    if side is not None:
        spec, shape = _side_specs(side, bsz * ni * nj, lambda bi, i, j: (bi * ni + i) * nj + j)
        in_specs.append(spec)
        args.append(side)
        out_specs.append(spec)
        out_shape.append(shape)
    body = functools.partial(_mm_body, has_gain=gain is not None, has_bias=bias is not None, act=act,
                             rope=rope is not None, kadd=kadd is not None, has_side=side is not None,
                             w_rows=w_rows is not None)
    outs = pl.pallas_call(
        body,
        grid=(bsz, ni, nj),
        in_specs=in_specs,
        out_specs=out_specs,
        out_shape=out_shape,
        compiler_params=_cparams(("parallel", "parallel", "arbitrary") if side is None else ("arbitrary",) * 3,
                                 VMEM_BIG),
        name=name,
    )(*args)
    return outs[0] if side is None else outs


def _mm_t_body(a_ref, wt_ref, gain_ref, bias_ref, o_ref):
    af = a_ref[...].astype(F32)
    a = af * lax.rsqrt(jnp.mean(af * af, axis=-1, keepdims=True) + RMS_EPS) * gain_ref[...]
    r = lax.dot_general(wt_ref[...], a.astype(BF16), (((1,), (1,)), ((), ())), preferred_element_type=F32)
    o_ref[...] = (r + bias_ref[...]).astype(o_ref.dtype)


def _mm_t(a3, wt, gain, bias_col, *, out_dtype, tm, tn, name):
    bsz, rows, k = a3.shape
    n = wt.shape[0]
    return pl.pallas_call(
        _mm_t_body,
        grid=(bsz, rows // tm, n // tn),
        in_specs=[pl.BlockSpec((None, tm, k), lambda bi, i, j: (bi, i, 0)),
                  pl.BlockSpec((tn, k), lambda bi, i, j: (j, 0)),
                  pl.BlockSpec((1, k), lambda bi, i, j: (0, 0)),
                  pl.BlockSpec((tn, 1), lambda bi, i, j: (j, 0))],
        out_specs=pl.BlockSpec((None, tn, tm), lambda bi, i, j: (bi, j, i)),
        out_shape=jax.ShapeDtypeStruct((bsz, n, rows), out_dtype),
        compiler_params=_cparams(("parallel", "parallel", "arbitrary"), VMEM_BIG),
        name=name,
    )(a3, wt, gain.reshape(1, k).astype(F32), bias_col.reshape(n, 1).astype(F32))


def _mmk_body(*refs, n_pairs, has_gate, has_res, res_scale, has_side):
    it = iter(refs)
    a_refs = [next(it) for _ in range(n_pairs)]
    w_refs = [next(it) for _ in range(n_pairs)]
    g_refs = [next(it) for _ in range(n_pairs)] if has_gate else None
    res_ref = next(it) if has_res else None
    side_in = next(it) if has_side else None
    o_ref = next(it)
    if has_side:
        next(it)[...] = side_in[...].astype(BF16)
    acc_refs = [next(it) for _ in range(n_pairs)]
    kk = pl.program_id(3)

    @pl.when(kk == 0)
    def _():
        for acc in acc_refs:
            acc[...] = jnp.zeros_like(acc)

    for a_ref, w_ref, acc in zip(a_refs, w_refs, acc_refs):
        acc[...] += jnp.dot(a_ref[...], w_ref[...], preferred_element_type=F32)

    @pl.when(kk == pl.num_programs(3) - 1)
    def _():
        r = None
        for p, acc in enumerate(acc_refs):
            t = acc[...]
            if has_gate:
                t = t * g_refs[p][...].astype(F32)
            r = t if r is None else r + t
        if has_res:
            r = r + res_scale * res_ref[...]
        o_ref[...] = r.astype(o_ref.dtype)


def _mm_ktiled(a_list, w_list, *, out_dtype, rows, tm, tn, nk, gates=None, res=None, res_scale=1.0, side=None,
               name="mmk"):
    bsz = a_list[0].shape[0]
    n = w_list[0].shape[1]
    npairs = len(a_list)
    tks = [a.shape[2] // nk for a in a_list]
    assert all(a.shape[2] == tk * nk and tk % LANES == 0 for a, tk in zip(a_list, tks))
    o_spec = pl.BlockSpec((None, tm, tn), lambda bi, i, j, kk: (bi, i, j))
    in_specs = ([pl.BlockSpec((None, tm, tk), lambda bi, i, j, kk: (bi, i, kk)) for tk in tks]
                + [pl.BlockSpec((tk, tn), lambda bi, i, j, kk: (kk, j)) for tk in tks])
    args = list(a_list) + list(w_list)
    if gates is not None:
        for p in range(npairs):
            in_specs.append(pl.BlockSpec((None, tm, tn), lambda bi, i, j, kk, p=p: (bi, i, p * (n // tn) + j)))
            args.append(gates)
    if res is not None:
        in_specs.append(o_spec)
        args.append(res)
    ni, nj = rows // tm, n // tn
    out_specs = [o_spec]
    out_shape = [jax.ShapeDtypeStruct((bsz, rows, n), out_dtype)]
    if side is not None:
        spec, shape = _side_specs(side, bsz * ni * nj * nk, lambda bi, i, j, kk: ((bi * ni + i) * nj + j) * nk + kk)
        in_specs.append(spec)
        args.append(side)
        out_specs.append(spec)
        out_shape.append(shape)
    body = functools.partial(_mmk_body, n_pairs=npairs, has_gate=gates is not None, has_res=res is not None,
                             res_scale=res_scale, has_side=side is not None)
    outs = pl.pallas_call(
        body,
        grid=(bsz, ni, nj, nk),
        in_specs=in_specs,
        out_specs=out_specs,
        out_shape=out_shape,
        scratch_shapes=[pltpu.VMEM((tm, tn), F32) for _ in range(npairs)],
        compiler_params=_cparams(("parallel", "parallel", "parallel", "arbitrary") if side is None
                                 else ("arbitrary",) * 4, VMEM_BIG),
        name=name,
    )(*args)
    return outs[0] if side is None else outs


def _silu(x):
    return x * jax.nn.sigmoid(x)


def _ssd_body(xs_ref, b_ref, c_ref, z_ref, dt_ref, wx_ref, wb_ref, wc_ref, bx_ref, bb_ref, bc_ref,
              dtb_ref, alog_ref, dskip_ref, ng_ref, e_ref, o_ref,
              state_ref, extx_ref, extb_ref, extc_ref, y_ref, *, hg, gp):
    c = pl.program_id(2)
    q = CHUNK
    p = SSM_HEAD_DIM
    n = SSM_STATE
    gw = hg * p
    first = c == 0
    row = lax.broadcasted_iota(I32, (q, 1), 0)
    live = jnp.logical_or(jnp.logical_not(first), row >= META_PAD)

    @pl.when(first)
    def _():
        state_ref[...] = jnp.zeros_like(state_ref)
        extx_ref[0:8, :] = jnp.zeros((8, extx_ref.shape[1]), F32)
        extb_ref[0:8, :] = jnp.zeros((8, extb_ref.shape[1]), F32)
        extc_ref[0:8, :] = jnp.zeros((8, extc_ref.shape[1]), F32)

    def conv(ext_ref, u_ref, w_ref, bias_ref):
        @pl.when(first)
        def _():
            ext_ref[8:8 + q, :] = jnp.where(live, u_ref[...].astype(F32), 0.0)

        @pl.when(jnp.logical_not(first))
        def _():
            ext_ref[8:8 + q, :] = u_ref[...].astype(F32)

        acc = bias_ref[...]
        for k in range(SSM_CONV):
            acc = acc + ext_ref[pl.ds(8 - (SSM_CONV - 1) + k, q), :] * w_ref[k:k + 1, :]
        ext_ref[0:8, :] = ext_ref[q:q + 8, :]
        return _silu(acc)

    xs_w = conv(extx_ref, xs_ref, wx_ref, bx_ref)
    bm_w = conv(extb_ref, b_ref, wb_ref, bb_ref)
    cm_w = conv(extc_ref, c_ref, wc_ref, bc_ref)

    r_i = lax.broadcasted_iota(I32, (q, q), 0)
    c_i = lax.broadcasted_iota(I32, (q, q), 1)
    causal = r_i >= c_i
    tri = causal.astype(F32)
    e = e_ref[...]

    def spread(v):
        v_hi = v.astype(BF16)
        v_lo = (v - v_hi.astype(F32)).astype(BF16)
        return jnp.dot(v_hi, e, preferred_element_type=F32) + jnp.dot(v_lo, e, preferred_element_type=F32)

    groups = [dict() for _ in range(gp)]
    for gg, s in enumerate(groups):
        x = dt_ref[gg] + dtb_ref[gg]
        dt = jnp.maximum(x, 0.0) + jnp.log1p(jnp.exp(-jnp.abs(x)))
        dt = jnp.where(live, dt, 0.0)
        a = -jnp.exp(alog_ref[gg])
        a_cs = jnp.dot(tri, dt * a, precision=lax.Precision.HIGHEST, preferred_element_type=F32)
        s.update(dt=dt, a_cs=a_cs, a_cs_t=a_cs.T, ea=jnp.exp(a_cs), de=jnp.exp(a_cs[q - 1:q, :] - a_cs))

    for gg, s in enumerate(groups):
        xs = xs_w[:, gg * gw:(gg + 1) * gw]
        bm = bm_w[:, gg * n:(gg + 1) * n]
        cm_b = cm_w[:, gg * n:(gg + 1) * n].astype(BF16)
        ea_x = spread(s["ea"])
        xdt = xs * spread(s["dt"])
        prev = state_ref[gg]
        s.update(xs=xs, ea_x=ea_x, prev=prev, xdt_b=xdt.astype(BF16),
                 cb=lax.dot_general(cm_b, bm.astype(BF16), (((1,), (1,)), ((), ())), preferred_element_type=F32),
                 y_off=jnp.dot(cm_b, prev.astype(BF16), preferred_element_type=F32) * ea_x,
                 upd=jnp.dot(bm.T.astype(BF16), (xdt * spread(s["de"])).astype(BF16), preferred_element_type=F32))

    def decay_scores(s, j):
        seg = s["a_cs"][:, j:j + 1] - s["a_cs_t"][j:j + 1, :]
        return (jnp.exp(jnp.where(causal, seg, NEG)) * s["cb"]).astype(BF16)

    nxt = [decay_scores(s, 0) for s in groups]
    for j in range(hg):
        for gg, s in enumerate(groups):
            m = nxt[gg]
            if j + 1 < hg:
                nxt[gg] = decay_scores(s, j + 1)
            lo = gg * gw + j * p
            y_ref[:, lo:lo + p] = jnp.dot(m, s["xdt_b"][:, j * p:(j + 1) * p], preferred_element_type=F32)

    for gg, s in enumerate(groups):
        sl = slice(gg * gw, (gg + 1) * gw)
        y = y_ref[:, sl] + s["y_off"] + s["xs"] * dskip_ref[:, sl]
        state_ref[gg] = s["prev"] * s["ea_x"][q - 1:q, :] + s["upd"]
        yz = y * _silu(z_ref[:, sl].astype(F32))
        yn = yz * lax.rsqrt(jnp.mean(yz * yz, axis=-1, keepdims=True) + RMS_EPS) * ng_ref[:, sl]
        o_ref[:, sl] = yn.astype(o_ref.dtype)


def _ssd(xbc3, z3, dt_g, conv_w, conv_b, dtb_g, alog_g, dskip_x, norm_g, d_inner, heads):
    bsz, rows, conv_dim = xbc3.shape
    g, n, p, q = SSM_GROUPS, SSM_STATE, SSM_HEAD_DIM, CHUNK
    gp = 2
    hg = heads // g
    gw = hg * p
    nc = rows // q
    assert gw % LANES == 0 and d_inner % (gp * n) == 0 and hg <= LANES and g % gp == 0
    boff = d_inner // (gp * n)
    coff = boff + g // gp
    expand = (jnp.arange(LANES)[:, None] == (jnp.arange(gw) // p)[None, :]).astype(BF16)

    def chunk(c):
        return (c + nc - 1) % nc

    in_specs = [
        pl.BlockSpec((None, q, gp * gw), lambda b, gi, c: (b, chunk(c), gi)),
        pl.BlockSpec((None, q, gp * n), lambda b, gi, c: (b, chunk(c), boff + gi)),
        pl.BlockSpec((None, q, gp * n), lambda b, gi, c: (b, chunk(c), coff + gi)),
        pl.BlockSpec((None, q, gp * gw), lambda b, gi, c: (b, chunk(c), gi)),
        pl.BlockSpec((None, gp, q, LANES), lambda b, gi, c: (b, gi, chunk(c), 0)),
        pl.BlockSpec((SSM_CONV, gp * gw), lambda b, gi, c: (0, gi)),
        pl.BlockSpec((SSM_CONV, gp * n), lambda b, gi, c: (0, boff + gi)),
        pl.BlockSpec((SSM_CONV, gp * n), lambda b, gi, c: (0, coff + gi)),
        pl.BlockSpec((1, gp * gw), lambda b, gi, c: (0, gi)),
        pl.BlockSpec((1, gp * n), lambda b, gi, c: (0, boff + gi)),
        pl.BlockSpec((1, gp * n), lambda b, gi, c: (0, coff + gi)),
        pl.BlockSpec((gp, 1, LANES), lambda b, gi, c: (gi, 0, 0)),
        pl.BlockSpec((gp, 1, LANES), lambda b, gi, c: (gi, 0, 0)),
        pl.BlockSpec((1, gp * gw), lambda b, gi, c: (0, gi)),
        pl.BlockSpec((1, gp * gw), lambda b, gi, c: (0, gi)),
        pl.BlockSpec((LANES, gw), lambda b, gi, c: (0, 0)),
    ]
    return pl.pallas_call(
        functools.partial(_ssd_body, hg=hg, gp=gp),
        grid=(bsz, g // gp, nc),
        in_specs=in_specs,
        out_specs=pl.BlockSpec((None, q, gp * gw), lambda b, gi, c: (b, chunk(c), gi)),
        out_shape=jax.ShapeDtypeStruct((bsz, rows, d_inner), BF16),
        scratch_shapes=[pltpu.VMEM((gp, n, gw), F32), pltpu.VMEM((q + 8, gp * gw), F32),
                        pltpu.VMEM((q + 8, gp * n), F32), pltpu.VMEM((q + 8, gp * n), F32),
                        pltpu.VMEM((q, gp * gw), F32)],
        compiler_params=_cparams(("parallel", "parallel", "arbitrary")),
        name="ssd",
    )(xbc3, xbc3, xbc3, z3, dt_g, conv_w, conv_w, conv_w, conv_b, conv_b, conv_b, dtb_g, alog_g,
      dskip_x, norm_g, expand)


def _attn_body(qi_ref, ki_ref, q_ref, km_ref, vm_ref, k_ref, v_ref, *rest, heads, n_side):
    side_in, o_ref, side_out = rest[:n_side], rest[n_side], rest[n_side + 1:2 * n_side + 1]
    m_ref, acc_ref = rest[2 * n_side + 1:]
    for src, dst in zip(side_in, side_out):
        dst[...] = src[...].astype(BF16)
    step_id = pl.program_id(2)
    qi, ki = qi_ref[step_id], ki_ref[step_id]
    tq = q_ref.shape[0]

    def step(segments):
        def scores(h):
            out = []
            for kr, _, key0, nk, q0, mask in segments:
                qh = q_ref[q0:, h * MLA_QKP:(h + 1) * MLA_QKP]
                kh = kr[key0:key0 + nk, h * MLA_QKP:(h + 1) * MLA_QKP]
                st = lax.dot_general(kh, qh, (((1,), (1,)), ((), ())), preferred_element_type=F32)
                key = key0 + lax.broadcasted_iota(I32, st.shape, 0)
                if mask == "meta":
                    st = jnp.where(key >= META_PAD, st, NEG)
                elif mask == "causal":
                    st = jnp.where(key <= q0 + lax.broadcasted_iota(I32, st.shape, 1), st, NEG)
                out.append(st)
            return out

        def probs(h, sts):
            m_prev = m_ref[h]
            m_new = m_prev
            for (_, _, _, _, q0, _), st in zip(segments, sts):
                cm = jnp.max(st, axis=0, keepdims=True)
                if q0:
                    cm = jnp.concatenate([jnp.full((1, q0), NEG, F32), cm], axis=1)
                m_new = jnp.maximum(m_new, cm)
            m_ref[h] = m_new
            pts = [jnp.exp2((st - m_new[:, q0:]).astype(BF16)) for (_, _, _, _, q0, _), st in zip(segments, sts)]
            return jnp.exp2(m_prev - m_new), pts

        def accumulate(h, alpha, pts):
            sl = slice(h * MLA_VP, (h + 1) * MLA_VP)
            acc = acc_ref[sl, :] * alpha
            partial = []
            for (_, vr, key0, nk, q0, _), pt in zip(segments, pts):
                contrib = jnp.dot(vr[sl, key0:key0 + nk], pt, preferred_element_type=F32)
                if q0:
                    partial.append((q0, contrib))
                else:
                    acc = acc + contrib
            acc_ref[sl, :] = acc
            for q0, contrib in partial:
                acc_ref[sl, q0:] += contrib

        st_q, pr_q = {}, {}
        for stage in range(heads + 2):
            if stage < heads:
                st_q[stage] = scores(stage)
            if 0 <= stage - 1 < heads:
                pr_q[stage - 1] = probs(stage - 1, st_q.pop(stage - 1))
            if 0 <= stage - 2 < heads:
                accumulate(stage - 2, *pr_q.pop(stage - 2))

    def start():
        m_ref[...] = jnp.full_like(m_ref, NEG)
        acc_ref[...] = jnp.zeros_like(acc_ref)

    def finish():
        for h in range(heads):
            num = acc_ref[h * MLA_VP:h * MLA_VP + MLA_V, :]
            den = acc_ref[h * MLA_VP + MLA_V:h * MLA_VP + MLA_V + 1, :]
            o_ref[:, h * MLA_V:(h + 1) * MLA_V] = (num / den).T.astype(o_ref.dtype)

    tk = k_ref.shape[0]
    meta = (km_ref, vm_ref, 0, km_ref.shape[0], 0, "meta")
    full = (k_ref, v_ref, 0, tk, 0, None)
    half = tk // 2
    if half % (2 * LANES) == 0:
        diag = [(k_ref, v_ref, 0, half, 0, "causal"), (k_ref, v_ref, half, half, half, "causal")]
    else:
        diag = [(k_ref, v_ref, 0, tk, 0, "causal")]

    @pl.when(jnp.logical_and(ki == 0, qi == 0))
    def _():
        start()
        step([meta] + diag)
        finish()

    @pl.when(jnp.logical_and(ki == 0, qi > 0))
    def _():
        start()
        step([meta, full])

    @pl.when(jnp.logical_and(ki > 0, ki < qi))
    def _():
        step([full])

    @pl.when(jnp.logical_and(ki > 0, ki == qi))
    def _():
        step(diag)
        finish()


def _attention(q3, k3, vt3, seq, sides=()):
    bsz = q3.shape[0]
    hp = 8
    t = _pick(seq, (512, 256, 128))
    nq = seq // t
    meta_blk = seq // CHUNK
    qw, vw = hp * MLA_QKP, hp * MLA_VP
    qi_tab = jnp.asarray([qi for qi in range(nq) for _ in range(qi + 1)], I32)
    ki_tab = jnp.asarray([ki for qi in range(nq) for ki in range(qi + 1)], I32)

    def q_idx(b, h, s, qt, kt):
        return (b, qt[s], h)

    def k_idx(b, h, s, qt, kt):
        return (b, kt[s], h)

    def vt_idx(b, h, s, qt, kt):
        return (b, h, kt[s])

    nh, npairs = MLA_HEADS // hp, int(qi_tab.shape[0])
    side_specs = [_side_specs(s_arr, bsz * nh * npairs, lambda b, h, s, qt, kt: (b * nh + h) * npairs + s)
                  for s_arr in sides]
    grid_spec = pltpu.PrefetchScalarGridSpec(
        num_scalar_prefetch=2,
        grid=(bsz, nh, npairs),
        in_specs=[
            pl.BlockSpec((None, t, qw), q_idx),
            pl.BlockSpec((None, CHUNK, qw), lambda b, h, s, qt, kt: (b, meta_blk, h)),
            pl.BlockSpec((None, vw, CHUNK), lambda b, h, s, qt, kt: (b, h, meta_blk)),
            pl.BlockSpec((None, t, qw), k_idx),
            pl.BlockSpec((None, vw, t), vt_idx),
        ] + [spec for spec, _ in side_specs],
        out_specs=[pl.BlockSpec((None, t, hp * MLA_V), q_idx)] + [spec for spec, _ in side_specs],
        scratch_shapes=[pltpu.VMEM((hp, 1, t), F32), pltpu.VMEM((vw, t), F32)],
    )
    return pl.pallas_call(
        functools.partial(_attn_body, heads=hp, n_side=len(sides)),
        grid_spec=grid_spec,
        out_shape=[jax.ShapeDtypeStruct((bsz, seq, MLA_HEADS * MLA_V), BF16)] + [shape for _, shape in side_specs],
        compiler_params=_cparams(("arbitrary", "arbitrary", "arbitrary")),
        name="mla_attention",
    )(qi_tab, ki_tab, q3, k3, vt3, k3, vt3, *sides)


def _router_body(h_ref, wr_ref, rb_ref, e_ref, pos_ref, w_ref, cnt_ref, carry_ref):
    i = pl.program_id(0)
    ne, ng = N_EXPERTS, N_EXPERT_GROUPS
    gs = ne // ng
    tm = h_ref.shape[0]

    @pl.when(i == 0)
    def _():
        carry_ref[...] = jnp.zeros_like(carry_ref)

    logits = lax.dot_general(wr_ref[...], h_ref[...], (((1,), (1,)), ((), ())), precision=lax.Precision.HIGHEST,
                             preferred_element_type=F32)
    scores = jax.nn.sigmoid(logits)
    choice = scores + rb_ref[...]
    sub = lax.broadcasted_iota(I32, (gs, tm), 0)
    grp_rows = []
    for g in range(ng):
        blk = choice[g * gs:(g + 1) * gs, :]
        m1 = jnp.max(blk, axis=0, keepdims=True)
        first = jnp.min(jnp.where(blk == m1, sub, gs), axis=0, keepdims=True)
        m2 = jnp.max(jnp.where(sub == first, -jnp.inf, blk), axis=0, keepdims=True)
        grp_rows.append(m1 + m2)
    grp = jnp.concatenate(grp_rows, axis=0)
    grank = jnp.zeros((ng, tm), I32)
    gidx = lax.broadcasted_iota(I32, (ng, tm), 0)
    for g in range(ng):
        rowv = grp[g:g + 1, :]
        beats = jnp.logical_or(rowv > grp, jnp.logical_and(rowv == grp, gidx > g))
        grank = grank + beats.astype(I32)
    gsel = (grank < TOPK_GROUPS).astype(F32)
    esel = jnp.concatenate([jnp.broadcast_to(gsel[g:g + 1, :], (gs, tm)) for g in range(ng)], axis=0)
    masked = jnp.where(esel > 0.0, choice, -jnp.inf)
    eidx = lax.broadcasted_iota(I32, (ne, tm), 0)
    rank = jnp.zeros((ne, tm), I32)
    for e in range(ne):
        rowv = masked[e:e + 1, :]
        beats = jnp.logical_or(rowv > masked, jnp.logical_and(rowv == masked, eidx > e))
        rank = rank + beats.astype(I32)
    top = jnp.logical_and(rank < TOP_K, esel > 0.0)
    topf = top.astype(F32)
    wsel = jnp.where(top, scores, 0.0)
    wn = wsel / jnp.sum(wsel, axis=0, keepdims=True) * ROUTED_SCALE
    r_i = lax.broadcasted_iota(I32, (tm, tm), 0)
    c_i = lax.broadcasted_iota(I32, (tm, tm), 1)
    before = (r_i < c_i).astype(BF16)
    pos = jnp.dot(topf.astype(BF16), before, preferred_element_type=F32) + carry_ref[:, :1]
    carry_ref[...] = carry_ref[...] + jnp.sum(topf, axis=1, keepdims=True)
    cnt_ref[...] = carry_ref[...]
    e_rows, p_rows, w_rows = [], [], []
    for k in range(TOP_K):
        hit = rank == k
        e_rows.append(jnp.sum(jnp.where(hit, eidx, 0), axis=0, keepdims=True))
        p_rows.append(jnp.sum(jnp.where(hit, pos, 0.0), axis=0, keepdims=True))
        w_rows.append(jnp.sum(jnp.where(hit, wn, 0.0), axis=0, keepdims=True))
    e_ref[...] = jnp.concatenate(e_rows, axis=0)
    pos_ref[...] = jnp.concatenate(p_rows, axis=0).astype(I32)
    w_ref[...] = jnp.concatenate(w_rows, axis=0)


def _router(h2, w_router, router_bias):
    t, d = h2.shape
    tm = _pick(t, (256, 128))
    slot = pl.BlockSpec((TOP_K, tm), lambda i: (0, i))
    rb = jnp.broadcast_to(router_bias.astype(F32)[:, None], (N_EXPERTS, tm))
    return pl.pallas_call(
        _router_body,
        grid=(t // tm,),
        in_specs=[pl.BlockSpec((tm, d), lambda i: (i, 0)), pl.BlockSpec((N_EXPERTS, d), lambda i: (0, 0)),
                  pl.BlockSpec((N_EXPERTS, tm), lambda i: (0, 0))],
        out_specs=[slot, slot, slot, pl.BlockSpec((N_EXPERTS, LANES), lambda i: (0, 0))],
        out_shape=[jax.ShapeDtypeStruct((TOP_K, t), I32), jax.ShapeDtypeStruct((TOP_K, t), I32),
                   jax.ShapeDtypeStruct((TOP_K, t), F32), jax.ShapeDtypeStruct((N_EXPERTS, LANES), F32)],
        scratch_shapes=[pltpu.VMEM((N_EXPERTS, LANES), F32)],
        compiler_params=_cparams(("arbitrary",)),
        name="moe_router",
    )(h2, w_router.T.astype(F32), rb)


def _load_slots(dest_hbm, dest_ref, sem):
    cp = pltpu.make_async_copy(dest_hbm.at[pl.program_id(0)], dest_ref, sem)
    cp.start()
    cp.wait()


def _dispatch_body(cnt_ref, start_ref, dest_hbm, x_ref, xs_ref, dest_ref, zero_ref, sem_ref, *, block_rows):
    i = pl.program_id(0)
    tm = x_ref.shape[0]
    _load_slots(dest_hbm, dest_ref, sem_ref.at[2])

    def row_copy(tok, k):
        return pltpu.make_async_copy(x_ref.at[pl.ds(tok, 1), :], xs_ref.at[pl.ds(dest_ref[k, tok], 1), :], sem_ref.at[0])

    def issue(tok, carry):
        for k in range(TOP_K):
            row_copy(tok, k).start(priority=k % 2)
        return carry

    lax.fori_loop(0, tm, issue, 0)

    @pl.when(i == 0)
    def _():
        zero_ref[...] = jnp.zeros_like(zero_ref)

        def fill(e, carry):
            cnt = cnt_ref[e]
            padded = (cnt + block_rows - 1) // block_rows * block_rows
            base = start_ref[e] + cnt

            def zcopy(r):
                return pltpu.make_async_copy(zero_ref.at[pl.ds(0, 1), :], xs_ref.at[pl.ds(base + r, 1), :], sem_ref.at[1])

            def zstart(r, c2):
                zcopy(r).start()
                return c2

            def zwait(r, c2):
                zcopy(r).wait()
                return c2

            lax.fori_loop(0, padded - cnt, zstart, 0)
            lax.fori_loop(0, padded - cnt, zwait, 0)
            return carry

        lax.fori_loop(0, N_EXPERTS, fill, 0)

    def drain(tok, carry):
        for k in range(TOP_K):
            row_copy(tok, k).wait()
        return carry

    lax.fori_loop(0, tm, drain, 0)


def _slot_tiles(dest, tm):
    k, t = dest.shape
    return dest.reshape(k, t // tm, tm).transpose(1, 0, 2)


def _dispatch(hp2, dest, counts, starts, n_rows):
    t, w = hp2.shape
    tm = _pick(t, (256, 128))
    grid_spec = pltpu.PrefetchScalarGridSpec(
        num_scalar_prefetch=2,
        grid=(t // tm,),
        in_specs=[pl.BlockSpec(memory_space=pl.ANY),
                  pl.BlockSpec((tm, w), lambda i, c, s: (i, 0))],
        out_specs=pl.BlockSpec(memory_space=pl.ANY),
        scratch_shapes=[pltpu.SMEM((TOP_K, tm), I32), pltpu.VMEM((8, w), U32), pltpu.SemaphoreType.DMA((3,))],
    )
    return pl.pallas_call(
        functools.partial(_dispatch_body, block_rows=MOE_ROWS),
        grid_spec=grid_spec,
        out_shape=jax.ShapeDtypeStruct((n_rows, w), U32),
        compiler_params=pltpu.CompilerParams(dimension_semantics=("arbitrary",)),
        name="moe_dispatch",
    )(counts, starts, _slot_tiles(dest, tm), hp2)


def _expert_body(be_ref, nu_ref, x_ref, wg_ref, wu_ref, wd_ref, o_ref):
    j = pl.program_id(0)

    @pl.when(j < nu_ref[0])
    def _():
        half = x_ref.shape[1]
        lo, hi = _unpack_halves(x_ref[...])
        lo, hi = lo.astype(BF16), hi.astype(BF16)

        def proj(w_ref):
            return (jnp.dot(lo, w_ref[:half, :], preferred_element_type=F32)
                    + jnp.dot(hi, w_ref[half:, :], preferred_element_type=F32))

        hmid = (_silu(proj(wg_ref)) * proj(wu_ref)).astype(BF16)
        o_ref[...] = _pack_halves(jnp.dot(hmid, wd_ref[...], preferred_element_type=F32))


def _experts(xs, wg, wu, wd, block_e, n_used):
    rows, w = xs.shape
    _, d, ff = wg.shape
    bm = MOE_ROWS
    nb = rows // bm

    def row_idx(j, be, nu):
        return (jnp.minimum(j, nu[0] - 1), 0)

    def w_idx(j, be, nu):
        return (be[jnp.minimum(j, nu[0] - 1)], 0, 0)

    grid_spec = pltpu.PrefetchScalarGridSpec(
        num_scalar_prefetch=2,
        grid=(nb,),
        in_specs=[pl.BlockSpec((bm, w), row_idx),
                  pl.BlockSpec((None, d, ff), w_idx), pl.BlockSpec((None, d, ff), w_idx),
                  pl.BlockSpec((None, ff, d), w_idx)],
        out_specs=pl.BlockSpec((bm, w), row_idx),
    )
    return pl.pallas_call(
        _expert_body,
        grid_spec=grid_spec,
        out_shape=jax.ShapeDtypeStruct((rows, w), U32),
        compiler_params=_cparams(("arbitrary",), 60 * 1024 * 1024),
        name="moe_experts",
    )(block_e, n_used, xs, wg, wu, wd)


def _combine_body(dest_hbm, wt_ref, h_ref, ysh_ref, g_ref, b_ref, y_ref, o_ref, dest_ref, gbuf_ref, sem_ref, *, alpha):
    tm = h_ref.shape[0]
    _load_slots(dest_hbm, dest_ref, sem_ref.at[1])

    def row_copy(tok, k):
        return pltpu.make_async_copy(y_ref.at[pl.ds(dest_ref[k, tok], 1), :], gbuf_ref.at[k, pl.ds(tok, 1), :],
                                     sem_ref.at[0])

    def issue(tok, carry):
        for k in range(TOP_K):
            row_copy(tok, k).start(priority=k % 2)
        return carry

    def drain(tok, carry):
        for k in range(TOP_K):
            row_copy(tok, k).wait()
        return carry

    lax.fori_loop(0, tm, issue, 0)
    lax.fori_loop(0, tm, drain, 0)
    lo, hi = _unpack_halves(ysh_ref[...])
    wt = wt_ref[...]
    for k in range(TOP_K):
        glo, ghi = _unpack_halves(gbuf_ref[k])
        wk = wt[:, k:k + 1]
        lo = lo + glo * wk
        hi = hi + ghi * wk
    x = alpha * h_ref[...] + jnp.concatenate([lo, hi], axis=1)
    mu = jnp.mean(x, axis=-1, keepdims=True)
    xc = x - mu
    var = jnp.mean(xc * xc, axis=-1, keepdims=True)
    o_ref[...] = xc * lax.rsqrt(var + LN_EPS) * g_ref[...] + b_ref[...]


def _combine(h2, ysh, y_sorted, dest, wt, g, b, alpha):
    t, d = h2.shape
    w = d // 2
    tm = _pick(t, (128,))
    vec = pl.BlockSpec((1, d), lambda i: (0, 0))
    return pl.pallas_call(
        functools.partial(_combine_body, alpha=alpha),
        grid=(t // tm,),
        in_specs=[pl.BlockSpec(memory_space=pl.ANY),
                  pl.BlockSpec((tm, TOP_K), lambda i: (i, 0)),
                  pl.BlockSpec((tm, d), lambda i: (i, 0)),
                  pl.BlockSpec((tm, w), lambda i: (i, 0)),
                  vec, vec,
                  pl.BlockSpec(memory_space=pl.ANY)],
        out_specs=pl.BlockSpec((tm, d), lambda i: (i, 0)),
        out_shape=jax.ShapeDtypeStruct((t, d), F32),
        scratch_shapes=[pltpu.SMEM((TOP_K, tm), I32), pltpu.VMEM((TOP_K, tm, w), U32), pltpu.SemaphoreType.DMA((2,))],
        compiler_params=_cparams(("arbitrary",)),
        name="moe_combine",
    )(_slot_tiles(dest, tm), wt, h2, ysh, g.reshape(1, d), b.reshape(1, d), y_sorted)


def _rope_tables(pos):
    inv_freq = ROPE_THETA ** (-jnp.arange(0, MLA_ROPE, 2, dtype=F32) / MLA_ROPE)
    ang = pos.astype(F32)[..., None] * inv_freq
    return jnp.cos(ang), jnp.sin(ang)


def kernel(x, positions, meta_tokens, ln_in_g, ln_in_b, w_in, b_gate, conv_w, conv_b, dt_bias, a_log, d_skip,
           ssm_norm_g, w_ssm_proj, q_a_norm_g, w_q_b, kv_a_norm_g, w_kv_b, w_attn_proj, w_out, ln1_g, ln1_b,
           w_router, router_bias, w_exp_gate, w_exp_up, w_exp_down, w_sh_gate, w_sh_up, w_sh_down, ln2_g, ln2_b):
    bsz, seq, d = x.shape
    depth = w_in.shape[0]
    heads = dt_bias.shape[-1]
    d_inner = w_ssm_proj.shape[1]
    conv_dim = conv_w.shape[-1]
    q_rank = w_q_b.shape[1]
    kv_rank = w_kv_b.shape[1]
    hh = MLA_HEADS
    g = SSM_GROUPS
    hg = heads // g
    assert seq % CHUNK == 0 and d % (2 * LANES) == 0
    lp = seq + CHUNK
    t = bsz * seq
    alpha = (2.0 * depth) ** 0.25

    meta_chunk = jnp.concatenate([jnp.zeros((META_PAD, d), x.dtype), meta_tokens.astype(x.dtype)], axis=0)
    pos =jnp.concatenate([positions.astype(I32) + N_META, jnp.zeros((bsz, META_PAD), I32),
                           jnp.broadcast_to(jnp.arange(N_META, dtype=I32), (bsz, N_META))], axis=1)
    cos, sin = _rope_tables(pos)
    zr = jnp.zeros_like(cos)
    rope_c = jnp.concatenate([cos, cos, zr, zr], axis=-1)[:, :seq]
    rope_slo = jnp.concatenate([-sin, zr, zr, zr], axis=-1)[:, :seq]
    rope_shi = jnp.concatenate([zr, sin, zr, zr], axis=-1)[:, :seq]
    k_c = jnp.concatenate([cos, cos, zr, zr], axis=-1)
    k_s = jnp.concatenate([-sin, sin, zr, zr], axis=-1)

    h_f32, h_b16 = _layer_norm_in(x, meta_chunk, ln_in_g, ln_in_b)
    tm_all = _pick(lp, (1408, 1152, 1024, 896, 768, 640, 512, 384, 256, 128))
    tm_seq = _pick(seq, (1024, 512, 256, 128))

    out = None
    for l in range(depth):
        offs = [0]
        for wdt in (d_inner, conv_dim, heads, q_rank, kv_rank, MLA_ROPE, 2 * d):
            offs.append(offs[-1] + wdt)
        wt = jnp.swapaxes(w_in[l], 0, 1)
        w_kr = wt[offs[5]:offs[6]]
        small_cols = q_rank + kv_rank + 2 * MLA_ROPE + heads
        small_n = small_cols + (-small_cols) % 256
        wt_small = jnp.concatenate([wt[offs[3]:offs[4]], wt[offs[4]:offs[5]], w_kr, w_kr, wt[offs[2]:offs[3]],
                                    jnp.zeros((small_n - small_cols, d), F32)], axis=0)

        n_exp, _, ff = w_exp_gate[l].shape
        z3 = _mm(h_b16, wt, w_rows=(offs[0], d_inner), out_dtype=BF16, tm=tm_all, tn=512, name="in_proj_z")
        xbc3 = _mm(h_b16, wt, w_rows=(offs[1], conv_dim), out_dtype=BF16, tm=tm_all, tn=512, name="in_proj_xbc")
        small = _mm(h_b16, wt_small, w_rows=(0, small_n), out_dtype=F32, tm=tm_all, tn=256, name="in_proj_small")
        gates = _mm(h_b16, wt, w_rows=(offs[6], 2 * d), out_dtype=BF16, tm=tm_seq, tn=512, n_row_tiles=seq // tm_seq,
                    bias=b_gate[l], act="sigmoid", name="in_proj_gates")
        o = 0
        q_a = small[:, :, o:o + q_rank]; o += q_rank
        kv_a = small[:, :, o:o + kv_rank]; o += kv_rank
        kr2 = small[:, :, o:o + 2 * MLA_ROPE]; o += 2 * MLA_ROPE
        dt_raw = small[:, :, o:o + heads]

        dt_g = jnp.pad(dt_raw.reshape(bsz, lp, g, hg).transpose(0, 2, 1, 3), ((0, 0), (0, 0), (0, 0), (0, LANES - hg)))
        pad_h = lambda v: jnp.pad(v.astype(F32).reshape(g, 1, hg), ((0, 0), (0, 0), (0, LANES - hg)))
        y_ssm = _ssd(xbc3, z3, dt_g, conv_w[l].astype(F32), conv_b[l].reshape(1, conv_dim).astype(F32),
                     pad_h(dt_bias[l]), pad_h(a_log[l]),
                     jnp.repeat(d_skip[l].astype(F32), SSM_HEAD_DIM).reshape(1, d_inner),
                     ssm_norm_g[l].reshape(1, d_inner).astype(F32), d_inner, heads)

        qscale = (MLA_QK ** -0.5) * math.log2(math.e)
        head_pad = ((0, 0), (0, 0), (0, MLA_QKP - MLA_QK))
        w_q = jnp.pad(w_q_b[l].reshape(q_rank, hh, MLA_QK), head_pad).reshape(q_rank, hh * MLA_QKP).astype(BF16)
        q3 = _mm(q_a, w_q, out_dtype=BF16, tm=tm_seq, tn=4 * MLA_QKP, n_row_tiles=seq // tm_seq,
                 gain=q_a_norm_g[l] * qscale, rope=(rope_c, rope_slo, rope_shi), name="q_proj")
        wkv = w_kv_b[l].reshape(kv_rank, hh, MLA_NOPE + MLA_V)
        w_k = jnp.pad(wkv[:, :, :MLA_NOPE], ((0, 0), (0, 0), (0, MLA_QKP - MLA_NOPE)))
        w_k = w_k.reshape(kv_rank, hh * MLA_QKP).astype(BF16)
        w_vt = jnp.pad(wkv[:, :, MLA_NOPE:], ((0, 0), (0, 0), (0, MLA_VP - MLA_V)))
        w_vt = w_vt.reshape(kv_rank, hh * MLA_VP).T.astype(BF16)
        ones_rows = jnp.tile(jnp.concatenate([jnp.zeros((MLA_V,), F32), jnp.ones((MLA_VP - MLA_V,), F32)]), hh)
        k3 = _mm(kv_a, w_k, out_dtype=BF16, tm=tm_all, tn=4 * MLA_QKP, gain=kv_a_norm_g[l],
                 kadd=(kr2, k_c, k_s), name="k_proj")
        vt3 = _mm_t(kv_a, w_vt, kv_a_norm_g[l], ones_rows, out_dtype=BF16, tm=tm_all, tn=4 * MLA_VP,
                    name="v_proj")
        y_attn, wg_b, wu_b, wd_b = _attention(
            q3, k3, vt3, seq, [w_exp_gate[l].reshape(n_exp * d, ff), w_exp_up[l].reshape(n_exp * d, ff),
                               w_exp_down[l].reshape(n_exp * ff, d)])

        mixed = _mm_ktiled([y_ssm, y_attn], [w_ssm_proj[l].astype(BF16), w_attn_proj[l].astype(BF16)],
                           out_dtype=BF16, rows=seq, tm=tm_seq, tn=_pick(d, (1024, 512)), nk=4,
                           gates=gates, name="branch_proj")
        pre1 = _mm_ktiled([mixed], [w_out[l].astype(BF16)], out_dtype=F32, rows=seq, tm=tm_seq,
                          tn=_pick(d, (1024, 512)), nk=2, res=h_f32, res_scale=alpha, name="out_proj")
        h1, h1p = _layer_norm_pack(pre1.reshape(t, d), ln1_g[l], ln1_b[l])

        e_slot, pos_slot, w_slot, cnt = _router(h1, w_router[l], router_bias[l])
        counts = cnt[:, 0].astype(I32)
        padded = (counts + MOE_ROWS - 1) // MOE_ROWS * MOE_ROWS
        ends = jnp.cumsum(padded)
        starts = ends - padded
        onehot = e_slot[None] == jnp.arange(N_EXPERTS, dtype=I32)[:, None, None]
        dest = pos_slot + jnp.sum(jnp.where(onehot, starts[:, None, None], 0), axis=0)
        n_blocks = -(-(t * TOP_K) // MOE_ROWS) + N_EXPERTS
        blk0 = jnp.arange(n_blocks, dtype=I32) * MOE_ROWS
        block_e = jnp.minimum(jnp.sum((ends[None, :] <= blk0[:, None]).astype(I32), axis=1), N_EXPERTS - 1)
        n_used = (ends[-1] // MOE_ROWS).reshape(1)
        xs = _dispatch(h1p, dest, counts, starts, n_blocks * MOE_ROWS)
        y_sorted = _experts(xs, wg_b.reshape(n_exp, d, ff), wu_b.reshape(n_exp, d, ff), wd_b.reshape(n_exp, ff, d),
                            block_e, n_used)
        y_shared = _experts(h1p, w_sh_gate[l][None].astype(BF16), w_sh_up[l][None].astype(BF16),
                            w_sh_down[l][None].astype(BF16), jnp.zeros((t // MOE_ROWS,), I32),
                            jnp.full((1,), t // MOE_ROWS, I32))
        out = _combine(h1, y_shared, y_sorted, dest, w_slot.T, ln2_g[l], ln2_b[l], alpha)
        if l + 1 < depth:
            raise NotImplementedError("stacked layers need the meta rows carried through the channel mixer")
    return out.reshape(bsz, seq, d)
```

```python
import functools
import math

import jax
import jax.numpy as jnp
from jax import lax
from jax.experimental import pallas as pl
from jax.experimental.pallas import tpu as pltpu

F32 = jnp.float32
BF16 = jnp.bfloat16
U32 = jnp.uint32
I32 = jnp.int32

N_META = 16
CHUNK = 128
META_PAD = CHUNK - N_META
SSM_HEAD_DIM = 64
SSM_GROUPS = 8
SSM_STATE = 128
SSM_CONV = 4
MLA_HEADS = 64
MLA_NOPE = 128
MLA_ROPE = 64
MLA_V = 128
MLA_QK = MLA_NOPE + MLA_ROPE
ROPE_THETA = 10000.0
N_EXPERTS = 64
N_EXPERT_GROUPS = 8
TOPK_GROUPS = 4
TOP_K = 8
ROUTED_SCALE = 2.5
LN_EPS = 1e-5
RMS_EPS = 1e-6
NEG = -1e30
LANES = 128
MLA_QKP = 2 * LANES
MOE_ROWS = 256
MLA_VP = MLA_V + 16
VMEM_BIG = 56 * 1024 * 1024
VMEM_MID = 44 * 1024 * 1024


def _cparams(sem, vmem=VMEM_MID):
    return pltpu.CompilerParams(dimension_semantics=sem, vmem_limit_bytes=vmem)


def _pick(n, cands):
    for c in cands:
        if n % c == 0:
            return c
    raise ValueError(f"no tile for {n} in {cands}")


def _ln_body(x_ref, g_ref, b_ref, of_ref, ob_ref):
    x = x_ref[...]
    mu = jnp.mean(x, axis=-1, keepdims=True)
    xc = x - mu
    var = jnp.mean(xc * xc, axis=-1, keepdims=True)
    y = xc * lax.rsqrt(var + LN_EPS) * g_ref[...] + b_ref[...]
    of_ref[...] = y
    ob_ref[...] = y.astype(BF16)


def _ln_in_body(x_ref, meta_ref, g_ref, b_ref, of_ref, ob_ref, *, n_real):
    i = pl.program_id(1)

    @pl.when(i < n_real)
    def _():
        _ln_body(x_ref, g_ref, b_ref, of_ref, ob_ref)

    @pl.when(i == n_real)
    def _():
        _ln_body(meta_ref, g_ref, b_ref, of_ref, ob_ref)


def _layer_norm_in(x3, meta_chunk, g, b):
    bsz, seq, d = x3.shape
    tm = CHUNK
    n_real = seq // tm
    spec = pl.BlockSpec((None, tm, d), lambda bi, i: (bi, i, 0))
    vec = pl.BlockSpec((1, d), lambda bi, i: (0, 0))
    shape = (bsz, seq + tm, d)
    return pl.pallas_call(
        functools.partial(_ln_in_body, n_real=n_real),
        grid=(bsz, n_real + 1),
        in_specs=[pl.BlockSpec((None, tm, d), lambda bi, i: (bi, jnp.minimum(i, n_real - 1), 0)),
                  pl.BlockSpec((tm, d), lambda bi, i: (0, 0)), vec, vec],
        out_specs=[spec, spec],
        out_shape=[jax.ShapeDtypeStruct(shape, F32), jax.ShapeDtypeStruct(shape, BF16)],
        compiler_params=_cparams(("parallel", "arbitrary")),
        name="layer_norm",
    )(x3, meta_chunk, g.reshape(1, d), b.reshape(1, d))


def _pack_halves(y):
    n = y.shape[1] // 2
    lo = pltpu.bitcast(y[:, :n].astype(BF16).astype(F32), U32) >> 16
    hi = pltpu.bitcast(y[:, n:].astype(BF16).astype(F32), U32) & jnp.uint32(0xFFFF0000)
    return hi | lo


def _unpack_halves(w):
    lo = pltpu.bitcast(w << 16, F32)
    hi = pltpu.bitcast(w & jnp.uint32(0xFFFF0000), F32)
    return lo, hi


def _ln_pack_body(x_ref, g_ref, b_ref, of_ref, op_ref):
    x = x_ref[...]
    mu = jnp.mean(x, axis=-1, keepdims=True)
    xc = x - mu
    var = jnp.mean(xc * xc, axis=-1, keepdims=True)
    y = xc * lax.rsqrt(var + LN_EPS) * g_ref[...] + b_ref[...]
    of_ref[...] = y
    op_ref[...] = _pack_halves(y)


def _layer_norm_pack(x2, g, b):
    rows, d = x2.shape
    tm = _pick(rows, (256, 128))
    vec = pl.BlockSpec((1, d), lambda i: (0, 0))
    return pl.pallas_call(
        _ln_pack_body,
        grid=(rows // tm,),
        in_specs=[pl.BlockSpec((tm, d), lambda i: (i, 0)), vec, vec],
        out_specs=[pl.BlockSpec((tm, d), lambda i: (i, 0)), pl.BlockSpec((tm, d // 2), lambda i: (i, 0))],
        out_shape=[jax.ShapeDtypeStruct((rows, d), F32), jax.ShapeDtypeStruct((rows, d // 2), U32)],
        compiler_params=_cparams(("parallel",)),
        name="layer_norm_pack",
    )(x2, g.reshape(1, d), b.reshape(1, d))


def _mm_body(*refs, has_gain, has_bias, act, rope, kadd, has_side, w_rows):
    it = iter(refs)
    a_ref, w_ref = next(it), next(it)
    gain_ref = next(it) if has_gain else None
    bias_ref = next(it) if has_bias else None
    rope_refs = [next(it) for _ in range(3)] if rope else None
    kadd_refs = [next(it) for _ in range(3)] if kadd else None
    side_in = next(it) if has_side else None
    o_ref = next(it)
    if has_side:
        next(it)[...] = side_in[...].astype(BF16)
    a = a_ref[...]
    if has_gain:
        af = a.astype(F32)
        a = af * lax.rsqrt(jnp.mean(af * af, axis=-1, keepdims=True) + RMS_EPS) * gain_ref[...]
    if w_rows:
        r = lax.dot_general(a.astype(BF16), w_ref[...].astype(BF16), (((1,), (1,)), ((), ())),
                            preferred_element_type=F32)
    else:
        r = jnp.dot(a.astype(BF16), w_ref[...].astype(BF16), preferred_element_type=F32)
    tn = r.shape[1]
    if has_bias:
        r = r + bias_ref[...]
    if act == "sigmoid":
        r = jax.nn.sigmoid(r)
    if rope:
        c, slo, shi = (t[...] for t in rope_refs)
        half = MLA_ROPE // 2
        pieces = []
        for h0 in range(0, tn, MLA_QKP):
            rp = r[:, h0 + MLA_NOPE:h0 + MLA_QKP]
            pieces += [r[:, h0:h0 + MLA_NOPE], rp * c + pltpu.roll(rp, LANES - half, 1) * slo + pltpu.roll(rp, half, 1) * shi]
        r = jnp.concatenate(pieces, axis=1)
    if kadd:
        kr_ref, kc, ks = kadd_refs
        kr = kr_ref[...]
        piece = jnp.concatenate([jnp.zeros_like(kr), kr * kc[...] + pltpu.roll(kr, MLA_ROPE // 2, 1) * ks[...]], axis=1)
        r = r + jnp.concatenate([piece] * (tn // piece.shape[1]), axis=1)
    o_ref[...] = r.astype(o_ref.dtype)


def _side_specs(side, n_steps, lin):
    rows, cols = side.shape
    nb = max(c for c in range(1, n_steps + 1) if rows % c == 0 and (rows // c) % 16 == 0)
    spec = pl.BlockSpec((rows // nb, cols), lambda *ids: (jnp.minimum(lin(*ids), nb - 1), 0))
    return spec, jax.ShapeDtypeStruct(side.shape, BF16)


def _mm(a3, w, *, out_dtype, tm, tn, n_row_tiles=None, w_rows=None, gain=None, bias=None, act=None, rope=None,
        kadd=None, side=None, name="mm"):
    bsz, rows, k = a3.shape
    row0, n = (0, w.shape[1]) if w_rows is None else w_rows
    ni = rows // tm if n_row_tiles is None else n_row_tiles
    nj = n // tn
    assert n % tn == 0 and row0 % 8 == 0 and (n_row_tiles is not None or rows % tm == 0)
    if w_rows is None:
        w_spec = pl.BlockSpec((k, tn), lambda bi, i, j: (0, j))
    elif row0 % tn == 0:
        w_spec = pl.BlockSpec((tn, k), lambda bi, i, j: (row0 // tn + j, 0))
    else:
        w_spec = pl.BlockSpec((pl.Element(tn), pl.Element(k)),
                              lambda bi, i, j: (pl.multiple_of(row0 + j * tn, 8), 0))
    in_specs = [pl.BlockSpec((None, tm, k), lambda bi, i, j: (bi, i, 0)), w_spec]
    args = [a3, w]
    if gain is not None:
        in_specs.append(pl.BlockSpec((1, k), lambda bi, i, j: (0, 0)))
        args.append(gain.reshape(1, k).astype(F32))
    if bias is not None:
        in_specs.append(pl.BlockSpec((1, tn), lambda bi, i, j: (0, j)))
        args.append(bias.reshape(1, n).astype(F32))
    for t in (rope or ()) + (kadd or ()):
        in_specs.append(pl.BlockSpec((None, tm, t.shape[2]), lambda bi, i, j: (bi, i, 0)))
        args.append(t)
    out_specs = [pl.BlockSpec((None, tm, tn), lambda bi, i, j: (bi, i, j))]
    out_shape = [jax.ShapeDtypeStruct((bsz, ni * tm, n), out_dtype)]
    if side is not None:
        spec, shape = _side_specs(side, bsz * ni * nj, lambda bi, i, j: (bi * ni + i) * nj + j)
        in_specs.append(spec)
        args.append(side)
        out_specs.append(spec)
        out_shape.append(shape)
    body = functools.partial(_mm_body, has_gain=gain is not None, has_bias=bias is not None, act=act,
                             rope=rope is not None, kadd=kadd is not None, has_side=side is not None,
                             w_rows=w_rows is not None)
    outs = pl.pallas_call(
        body,
        grid=(bsz, ni, nj),
        in_specs=in_specs,
        out_specs=out_specs,
        out_shape=out_shape,
        compiler_params=_cparams(("parallel", "parallel", "arbitrary") if side is None else ("arbitrary",) * 3,
                                 VMEM_BIG),
        name=name,
    )(*args)
    return outs[0] if side is None else outs


def _mm_t_body(a_ref, wt_ref, gain_ref, bias_ref, o_ref):
    af = a_ref[...].astype(F32)
    a = af * lax.rsqrt(jnp.mean(af * af, axis=-1, keepdims=True) + RMS_EPS) * gain_ref[...]
    r = lax.dot_general(wt_ref[...], a.astype(BF16), (((1,), (1,)), ((), ())), preferred_element_type=F32)
    o_ref[...] = (r + bias_ref[...]).astype(o_ref.dtype)


def _mm_t(a3, wt, gain, bias_col, *, out_dtype, tm, tn, name):
    bsz, rows, k = a3.shape
    n = wt.shape[0]
    return pl.pallas_call(
        _mm_t_body,
        grid=(bsz, rows // tm, n // tn),
        in_specs=[pl.BlockSpec((None, tm, k), lambda bi, i, j: (bi, i, 0)),
                  pl.BlockSpec((tn, k), lambda bi, i, j: (j, 0)),
                  pl.BlockSpec((1, k), lambda bi, i, j: (0, 0)),
                  pl.BlockSpec((tn, 1), lambda bi, i, j: (j, 0))],
        out_specs=pl.BlockSpec((None, tn, tm), lambda bi, i, j: (bi, j, i)),
        out_shape=jax.ShapeDtypeStruct((bsz, n, rows), out_dtype),
        compiler_params=_cparams(("parallel", "parallel", "arbitrary"), VMEM_BIG),
        name=name,
    )(a3, wt, gain.reshape(1, k).astype(F32), bias_col.reshape(n, 1).astype(F32))


def _mmk_body(*refs, n_pairs, has_gate, has_res, res_scale, has_side):
    it = iter(refs)
    a_refs = [next(it) for _ in range(n_pairs)]
    w_refs = [next(it) for _ in range(n_pairs)]
    g_refs = [next(it) for _ in range(n_pairs)] if has_gate else None
    res_ref = next(it) if has_res else None
    side_in = next(it) if has_side else None
    o_ref = next(it)
    if has_side:
        next(it)[...] = side_in[...].astype(BF16)
    acc_refs = [next(it) for _ in range(n_pairs)]
    kk = pl.program_id(3)

    @pl.when(kk == 0)
    def _():
        for acc in acc_refs:
            acc[...] = jnp.zeros_like(acc)

    for a_ref, w_ref, acc in zip(a_refs, w_refs, acc_refs):
        acc[...] += jnp.dot(a_ref[...], w_ref[...], preferred_element_type=F32)

    @pl.when(kk == pl.num_programs(3) - 1)
    def _():
        r = None
        for p, acc in enumerate(acc_refs):
            t = acc[...]
            if has_gate:
                t = t * g_refs[p][...].astype(F32)
            r = t if r is None else r + t
        if has_res:
            r = r + res_scale * res_ref[...]
        o_ref[...] = r.astype(o_ref.dtype)


def _mm_ktiled(a_list, w_list, *, out_dtype, rows, tm, tn, nk, gates=None, res=None, res_scale=1.0, side=None,
               name="mmk"):
    bsz = a_list[0].shape[0]
    n = w_list[0].shape[1]
    npairs = len(a_list)
    tks = [a.shape[2] // nk for a in a_list]
    assert all(a.shape[2] == tk * nk and tk % LANES == 0 for a, tk in zip(a_list, tks))
    o_spec = pl.BlockSpec((None, tm, tn), lambda bi, i, j, kk: (bi, i, j))
    in_specs = ([pl.BlockSpec((None, tm, tk), lambda bi, i, j, kk: (bi, i, kk)) for tk in tks]
                + [pl.BlockSpec((tk, tn), lambda bi, i, j, kk: (kk, j)) for tk in tks])
    args = list(a_list) + list(w_list)
    if gates is not None:
        for p in range(npairs):
            in_specs.append(pl.BlockSpec((None, tm, tn), lambda bi, i, j, kk, p=p: (bi, i, p * (n // tn) + j)))
            args.append(gates)
    if res is not None:
        in_specs.append(o_spec)
        args.append(res)
    ni, nj = rows // tm, n // tn
    out_specs = [o_spec]
    out_shape = [jax.ShapeDtypeStruct((bsz, rows, n), out_dtype)]
    if side is not None:
        spec, shape = _side_specs(side, bsz * ni * nj * nk, lambda bi, i, j, kk: ((bi * ni + i) * nj + j) * nk + kk)
        in_specs.append(spec)
        args.append(side)
        out_specs.append(spec)
        out_shape.append(shape)
    body = functools.partial(_mmk_body, n_pairs=npairs, has_gate=gates is not None, has_res=res is not None,
                             res_scale=res_scale, has_side=side is not None)
    outs = pl.pallas_call(
        body,
        grid=(bsz, ni, nj, nk),
        in_specs=in_specs,
        out_specs=out_specs,
        out_shape=out_shape,
        scratch_shapes=[pltpu.VMEM((tm, tn), F32) for _ in range(npairs)],
        compiler_params=_cparams(("parallel", "parallel", "parallel", "arbitrary") if side is None
                                 else ("arbitrary",) * 4, VMEM_BIG),
        name=name,
    )(*args)
    return outs[0] if side is None else outs


def _silu(x):
    return x * jax.nn.sigmoid(x)


def _ssd_body(xs_ref, b_ref, c_ref, z_ref, dt_ref, wx_ref, wb_ref, wc_ref, bx_ref, bb_ref, bc_ref,
              dtb_ref, alog_ref, dskip_ref, ng_ref, e_ref, o_ref,
              state_ref, extx_ref, extb_ref, extc_ref, y_ref, *, hg, gp):
    c = pl.program_id(2)
    q = CHUNK
    p = SSM_HEAD_DIM
    n = SSM_STATE
    gw = hg * p
    first = c == 0
    row = lax.broadcasted_iota(I32, (q, 1), 0)
    live = jnp.logical_or(jnp.logical_not(first), row >= META_PAD)

    @pl.when(first)
    def _():
        state_ref[...] = jnp.zeros_like(state_ref)
        extx_ref[0:8, :] = jnp.zeros((8, extx_ref.shape[1]), F32)
        extb_ref[0:8, :] = jnp.zeros((8, extb_ref.shape[1]), F32)
        extc_ref[0:8, :] = jnp.zeros((8, extc_ref.shape[1]), F32)

    def conv(ext_ref, u_ref, w_ref, bias_ref):
        @pl.when(first)
        def _():
            ext_ref[8:8 + q, :] = jnp.where(live, u_ref[...].astype(F32), 0.0)

        @pl.when(jnp.logical_not(first))
        def _():
            ext_ref[8:8 + q, :] = u_ref[...].astype(F32)

        acc = bias_ref[...]
        for k in range(SSM_CONV):
            acc = acc + ext_ref[pl.ds(8 - (SSM_CONV - 1) + k, q), :] * w_ref[k:k + 1, :]
        ext_ref[0:8, :] = ext_ref[q:q + 8, :]
        return _silu(acc)

    xs_w = conv(extx_ref, xs_ref, wx_ref, bx_ref)
    bm_w = conv(extb_ref, b_ref, wb_ref, bb_ref)
    cm_w = conv(extc_ref, c_ref, wc_ref, bc_ref)

    r_i = lax.broadcasted_iota(I32, (q, q), 0)
    c_i = lax.broadcasted_iota(I32, (q, q), 1)
    causal = r_i >= c_i
    tri = causal.astype(F32)
    e = e_ref[...]

    def spread(v):
        v_hi = v.astype(BF16)
        v_lo = (v - v_hi.astype(F32)).astype(BF16)
        return jnp.dot(v_hi, e, preferred_element_type=F32) + jnp.dot(v_lo, e, preferred_element_type=F32)

    groups = [dict() for _ in range(gp)]
    for gg, s in enumerate(groups):
        x = dt_ref[gg] + dtb_ref[gg]
        dt = jnp.maximum(x, 0.0) + jnp.log1p(jnp.exp(-jnp.abs(x)))
        dt = jnp.where(live, dt, 0.0)
        a = -jnp.exp(alog_ref[gg])
        a_cs = jnp.dot(tri, dt * a, precision=lax.Precision.HIGHEST, preferred_element_type=F32)
        s.update(dt=dt, a_cs=a_cs, a_cs_t=a_cs.T, ea=jnp.exp(a_cs), de=jnp.exp(a_cs[q - 1:q, :] - a_cs))

    for gg, s in enumerate(groups):
        xs = xs_w[:, gg * gw:(gg + 1) * gw]
        bm = bm_w[:, gg * n:(gg + 1) * n]
        cm_b = cm_w[:, gg * n:(gg + 1) * n].astype(BF16)
        ea_x = spread(s["ea"])
        xdt = xs * spread(s["dt"])
        prev = state_ref[gg]
        s.update(xs=xs, ea_x=ea_x, prev=prev, xdt_b=xdt.astype(BF16),
                 cb=lax.dot_general(cm_b, bm.astype(BF16), (((1,), (1,)), ((), ())), preferred_element_type=F32),
                 y_off=jnp.dot(cm_b, prev.astype(BF16), preferred_element_type=F32) * ea_x,
                 upd=jnp.dot(bm.T.astype(BF16), (xdt * spread(s["de"])).astype(BF16), preferred_element_type=F32))

    def decay_scores(s, j):
        seg = s["a_cs"][:, j:j + 1] - s["a_cs_t"][j:j + 1, :]
        return (jnp.exp(jnp.where(causal, seg, NEG)) * s["cb"]).astype(BF16)

    nxt = [decay_scores(s, 0) for s in groups]
    for j in range(hg):
        for gg, s in enumerate(groups):
            m = nxt[gg]
            if j + 1 < hg:
                nxt[gg] = decay_scores(s, j + 1)
            lo = gg * gw + j * p
            y_ref[:, lo:lo + p] = jnp.dot(m, s["xdt_b"][:, j * p:(j + 1) * p], preferred_element_type=F32)

    for gg, s in enumerate(groups):
        sl = slice(gg * gw, (gg + 1) * gw)
        y = y_ref[:, sl] + s["y_off"] + s["xs"] * dskip_ref[:, sl]
        state_ref[gg] = s["prev"] * s["ea_x"][q - 1:q, :] + s["upd"]
        yz = y * _silu(z_ref[:, sl].astype(F32))
        yn = yz * lax.rsqrt(jnp.mean(yz * yz, axis=-1, keepdims=True) + RMS_EPS) * ng_ref[:, sl]
        o_ref[:, sl] = yn.astype(o_ref.dtype)


def _ssd(xbc3, z3, dt_g, conv_w, conv_b, dtb_g, alog_g, dskip_x, norm_g, d_inner, heads):
    bsz, rows, conv_dim = xbc3.shape
    g, n, p, q = SSM_GROUPS, SSM_STATE, SSM_HEAD_DIM, CHUNK
    gp = 2
    hg = heads // g
    gw = hg * p
    nc = rows // q
    assert gw % LANES == 0 and d_inner % (gp * n) == 0 and hg <= LANES and g % gp == 0
    boff = d_inner // (gp * n)
    coff = boff + g // gp
    expand = (jnp.arange(LANES)[:, None] == (jnp.arange(gw) // p)[None, :]).astype(BF16)

    def chunk(c):
        return (c + nc - 1) % nc

    in_specs = [
        pl.BlockSpec((None, q, gp * gw), lambda b, gi, c: (b, chunk(c), gi)),
        pl.BlockSpec((None, q, gp * n), lambda b, gi, c: (b, chunk(c), boff + gi)),
        pl.BlockSpec((None, q, gp * n), lambda b, gi, c: (b, chunk(c), coff + gi)),
        pl.BlockSpec((None, q, gp * gw), lambda b, gi, c: (b, chunk(c), gi)),
        pl.BlockSpec((None, gp, q, LANES), lambda b, gi, c: (b, gi, chunk(c), 0)),
        pl.BlockSpec((SSM_CONV, gp * gw), lambda b, gi, c: (0, gi)),
        pl.BlockSpec((SSM_CONV, gp * n), lambda b, gi, c: (0, boff + gi)),
        pl.BlockSpec((SSM_CONV, gp * n), lambda b, gi, c: (0, coff + gi)),
        pl.BlockSpec((1, gp * gw), lambda b, gi, c: (0, gi)),
        pl.BlockSpec((1, gp * n), lambda b, gi, c: (0, boff + gi)),
        pl.BlockSpec((1, gp * n), lambda b, gi, c: (0, coff + gi)),
        pl.BlockSpec((gp, 1, LANES), lambda b, gi, c: (gi, 0, 0)),
        pl.BlockSpec((gp, 1, LANES), lambda b, gi, c: (gi, 0, 0)),
        pl.BlockSpec((1, gp * gw), lambda b, gi, c: (0, gi)),
        pl.BlockSpec((1, gp * gw), lambda b, gi, c: (0, gi)),
        pl.BlockSpec((LANES, gw), lambda b, gi, c: (0, 0)),
    ]
    return pl.pallas_call(
        functools.partial(_ssd_body, hg=hg, gp=gp),
        grid=(bsz, g // gp, nc),
        in_specs=in_specs,
        out_specs=pl.BlockSpec((None, q, gp * gw), lambda b, gi, c: (b, chunk(c), gi)),
        out_shape=jax.ShapeDtypeStruct((bsz, rows, d_inner), BF16),
        scratch_shapes=[pltpu.VMEM((gp, n, gw), F32), pltpu.VMEM((q + 8, gp * gw), F32),
                        pltpu.VMEM((q + 8, gp * n), F32), pltpu.VMEM((q + 8, gp * n), F32),
                        pltpu.VMEM((q, gp * gw), F32)],
        compiler_params=_cparams(("parallel", "parallel", "arbitrary")),
        name="ssd",
    )(xbc3, xbc3, xbc3, z3, dt_g, conv_w, conv_w, conv_w, conv_b, conv_b, conv_b, dtb_g, alog_g,
      dskip_x, norm_g, expand)


def _attn_body(qi_ref, ki_ref, q_ref, km_ref, vm_ref, k_ref, v_ref, *rest, heads, n_side):
    side_in, o_ref, side_out = rest[:n_side], rest[n_side], rest[n_side + 1:2 * n_side + 1]
    m_ref, acc_ref = rest[2 * n_side + 1:]
    for src, dst in zip(side_in, side_out):
        dst[...] = src[...].astype(BF16)
    step_id = pl.program_id(2)
    qi, ki = qi_ref[step_id], ki_ref[step_id]
    tq = q_ref.shape[0]

    def step(segments):
        def scores(h):
            out = []
            for kr, _, key0, nk, q0, mask in segments:
                qh = q_ref[q0:, h * MLA_QKP:(h + 1) * MLA_QKP]
                kh = kr[key0:key0 + nk, h * MLA_QKP:(h + 1) * MLA_QKP]
                st = lax.dot_general(kh, qh, (((1,), (1,)), ((), ())), preferred_element_type=F32)
                key = key0 + lax.broadcasted_iota(I32, st.shape, 0)
                if mask == "meta":
                    st = jnp.where(key >= META_PAD, st, NEG)
                elif mask == "causal":
                    st = jnp.where(key <= q0 + lax.broadcasted_iota(I32, st.shape, 1), st, NEG)
                out.append(st)
            return out

        def probs(h, sts):
            m_prev = m_ref[h]
            m_new = m_prev
            for (_, _, _, _, q0, _), st in zip(segments, sts):
                cm = jnp.max(st, axis=0, keepdims=True)
                if q0:
                    cm = jnp.concatenate([jnp.full((1, q0), NEG, F32), cm], axis=1)
                m_new = jnp.maximum(m_new, cm)
            m_ref[h] = m_new
            pts = [jnp.exp2((st - m_new[:, q0:]).astype(BF16)) for (_, _, _, _, q0, _), st in zip(segments, sts)]
            return jnp.exp2(m_prev - m_new), pts

        def accumulate(h, alpha, pts):
            sl = slice(h * MLA_VP, (h + 1) * MLA_VP)
            acc = acc_ref[sl, :] * alpha
            partial = []
            for (_, vr, key0, nk, q0, _), pt in zip(segments, pts):
                contrib = jnp.dot(vr[sl, key0:key0 + nk], pt, preferred_element_type=F32)
                if q0:
                    partial.append((q0, contrib))
                else:
                    acc = acc + contrib
            acc_ref[sl, :] = acc
            for q0, contrib in partial:
                acc_ref[sl, q0:] += contrib

        st_q, pr_q = {}, {}
        for stage in range(heads + 2):
            if stage < heads:
                st_q[stage] = scores(stage)
            if 0 <= stage - 1 < heads:
                pr_q[stage - 1] = probs(stage - 1, st_q.pop(stage - 1))
            if 0 <= stage - 2 < heads:
                accumulate(stage - 2, *pr_q.pop(stage - 2))

    def start():
        m_ref[...] = jnp.full_like(m_ref, NEG)
        acc_ref[...] = jnp.zeros_like(acc_ref)

    def finish():
        for h in range(heads):
            num = acc_ref[h * MLA_VP:h * MLA_VP + MLA_V, :]
            den = acc_ref[h * MLA_VP + MLA_V:h * MLA_VP + MLA_V + 1, :]
            o_ref[:, h * MLA_V:(h + 1) * MLA_V] = (num / den).T.astype(o_ref.dtype)

    tk = k_ref.shape[0]
    meta = (km_ref, vm_ref, 0, km_ref.shape[0], 0, "meta")
    full = (k_ref, v_ref, 0, tk, 0, None)
    half = tk // 2
    if half % (2 * LANES) == 0:
        diag = [(k_ref, v_ref, 0, half, 0, "causal"), (k_ref, v_ref, half, half, half, "causal")]
    else:
        diag = [(k_ref, v_ref, 0, tk, 0, "causal")]

    @pl.when(jnp.logical_and(ki == 0, qi == 0))
    def _():
        start()
        step([meta] + diag)
        finish()

    @pl.when(jnp.logical_and(ki == 0, qi > 0))
    def _():
        start()
        step([meta, full])

    @pl.when(jnp.logical_and(ki > 0, ki < qi))
    def _():
        step([full])

    @pl.when(jnp.logical_and(ki > 0, ki == qi))
    def _():
        step(diag)
        finish()


def _attention(q3, k3, vt3, seq, sides=()):
    bsz = q3.shape[0]
    hp = 8
    t = _pick(seq, (512, 256, 128))
    nq = seq // t
    meta_blk = seq // CHUNK
    qw, vw = hp * MLA_QKP, hp * MLA_VP
    qi_tab = jnp.asarray([qi for qi in range(nq) for _ in range(qi + 1)], I32)
    ki_tab = jnp.asarray([ki for qi in range(nq) for ki in range(qi + 1)], I32)

    def q_idx(b, h, s, qt, kt):
        return (b, qt[s], h)

    def k_idx(b, h, s, qt, kt):
        return (b, kt[s], h)

    def vt_idx(b, h, s, qt, kt):
        return (b, h, kt[s])

    nh, npairs = MLA_HEADS // hp, int(qi_tab.shape[0])
    side_specs = [_side_specs(s_arr, bsz * nh * npairs, lambda b, h, s, qt, kt: (b * nh + h) * npairs + s)
                  for s_arr in sides]
    grid_spec = pltpu.PrefetchScalarGridSpec(
        num_scalar_prefetch=2,
        grid=(bsz, nh, npairs),
        in_specs=[
            pl.BlockSpec((None, t, qw), q_idx),
            pl.BlockSpec((None, CHUNK, qw), lambda b, h, s, qt, kt: (b, meta_blk, h)),
            pl.BlockSpec((None, vw, CHUNK), lambda b, h, s, qt, kt: (b, h, meta_blk)),
            pl.BlockSpec((None, t, qw), k_idx),
            pl.BlockSpec((None, vw, t), vt_idx),
        ] + [spec for spec, _ in side_specs],
        out_specs=[pl.BlockSpec((None, t, hp * MLA_V), q_idx)] + [spec for spec, _ in side_specs],
        scratch_shapes=[pltpu.VMEM((hp, 1, t), F32), pltpu.VMEM((vw, t), F32)],
    )
    return pl.pallas_call(
        functools.partial(_attn_body, heads=hp, n_side=len(sides)),
        grid_spec=grid_spec,
        out_shape=[jax.ShapeDtypeStruct((bsz, seq, MLA_HEADS * MLA_V), BF16)] + [shape for _, shape in side_specs],
        compiler_params=_cparams(("arbitrary", "arbitrary", "arbitrary")),
        name="mla_attention",
    )(qi_tab, ki_tab, q3, k3, vt3, k3, vt3, *sides)


def _router_body(h_ref, wr_ref, rb_ref, e_ref, pos_ref, w_ref, cnt_ref, carry_ref):
    i = pl.program_id(0)
    ne, ng = N_EXPERTS, N_EXPERT_GROUPS
    gs = ne // ng
    tm = h_ref.shape[0]

    @pl.when(i == 0)
    def _():
        carry_ref[...] = jnp.zeros_like(carry_ref)

    logits = lax.dot_general(wr_ref[...], h_ref[...], (((1,), (1,)), ((), ())), precision=lax.Precision.HIGHEST,
                             preferred_element_type=F32)
    scores = jax.nn.sigmoid(logits)
    choice = scores + rb_ref[...]
    sub = lax.broadcasted_iota(I32, (gs, tm), 0)
    grp_rows = []
    for g in range(ng):
        blk = choice[g * gs:(g + 1) * gs, :]
        m1 = jnp.max(blk, axis=0, keepdims=True)
        first = jnp.min(jnp.where(blk == m1, sub, gs), axis=0, keepdims=True)
        m2 = jnp.max(jnp.where(sub == first, -jnp.inf, blk), axis=0, keepdims=True)
        grp_rows.append(m1 + m2)
    grp = jnp.concatenate(grp_rows, axis=0)
    grank = jnp.zeros((ng, tm), I32)
    gidx = lax.broadcasted_iota(I32, (ng, tm), 0)
    for g in range(ng):
        rowv = grp[g:g + 1, :]
        beats = jnp.logical_or(rowv > grp, jnp.logical_and(rowv == grp, gidx > g))
        grank = grank + beats.astype(I32)
    gsel = (grank < TOPK_GROUPS).astype(F32)
    esel = jnp.concatenate([jnp.broadcast_to(gsel[g:g + 1, :], (gs, tm)) for g in range(ng)], axis=0)
    masked = jnp.where(esel > 0.0, choice, -jnp.inf)
    eidx = lax.broadcasted_iota(I32, (ne, tm), 0)
    rank = jnp.zeros((ne, tm), I32)
    for e in range(ne):
        rowv = masked[e:e + 1, :]
        beats = jnp.logical_or(rowv > masked, jnp.logical_and(rowv == masked, eidx > e))
        rank = rank + beats.astype(I32)
    top = jnp.logical_and(rank < TOP_K, esel > 0.0)
    topf = top.astype(F32)
    wsel = jnp.where(top, scores, 0.0)
    wn = wsel / jnp.sum(wsel, axis=0, keepdims=True) * ROUTED_SCALE
    r_i = lax.broadcasted_iota(I32, (tm, tm), 0)
    c_i = lax.broadcasted_iota(I32, (tm, tm), 1)
    before = (r_i < c_i).astype(BF16)
    pos = jnp.dot(topf.astype(BF16), before, preferred_element_type=F32) + carry_ref[:, :1]
    carry_ref[...] = carry_ref[...] + jnp.sum(topf, axis=1, keepdims=True)
    cnt_ref[...] = carry_ref[...]
    e_rows, p_rows, w_rows = [], [], []
    for k in range(TOP_K):
        hit = rank == k
        e_rows.append(jnp.sum(jnp.where(hit, eidx, 0), axis=0, keepdims=True))
        p_rows.append(jnp.sum(jnp.where(hit, pos, 0.0), axis=0, keepdims=True))
        w_rows.append(jnp.sum(jnp.where(hit, wn, 0.0), axis=0, keepdims=True))
    e_ref[...] = jnp.concatenate(e_rows, axis=0)
    pos_ref[...] = jnp.concatenate(p_rows, axis=0).astype(I32)
    w_ref[...] = jnp.concatenate(w_rows, axis=0)


def _router(h2, w_router, router_bias):
    t, d = h2.shape
    tm = _pick(t, (256, 128))
    slot = pl.BlockSpec((TOP_K, tm), lambda i: (0, i))
    rb = jnp.broadcast_to(router_bias.astype(F32)[:, None], (N_EXPERTS, tm))
    return pl.pallas_call(
        _router_body,
        grid=(t // tm,),
        in_specs=[pl.BlockSpec((tm, d), lambda i: (i, 0)), pl.BlockSpec((N_EXPERTS, d), lambda i: (0, 0)),
                  pl.BlockSpec((N_EXPERTS, tm), lambda i: (0, 0))],
        out_specs=[slot, slot, slot, pl.BlockSpec((N_EXPERTS, LANES), lambda i: (0, 0))],
        out_shape=[jax.ShapeDtypeStruct((TOP_K, t), I32), jax.ShapeDtypeStruct((TOP_K, t), I32),
                   jax.ShapeDtypeStruct((TOP_K, t), F32), jax.ShapeDtypeStruct((N_EXPERTS, LANES), F32)],
        scratch_shapes=[pltpu.VMEM((N_EXPERTS, LANES), F32)],
        compiler_params=_cparams(("arbitrary",)),
        name="moe_router",
    )(h2, w_router.T.astype(F32), rb)


def _load_slots(dest_hbm, dest_ref, sem):
    cp = pltpu.make_async_copy(dest_hbm.at[pl.program_id(0)], dest_ref, sem)
    cp.start()
    cp.wait()


def _dispatch_body(cnt_ref, start_ref, dest_hbm, x_ref, xs_ref, dest_ref, zero_ref, sem_ref, *, block_rows):
    i = pl.program_id(0)
    tm = x_ref.shape[0]
    _load_slots(dest_hbm, dest_ref, sem_ref.at[2])

    def row_copy(tok, k):
        return pltpu.make_async_copy(x_ref.at[pl.ds(tok, 1), :], xs_ref.at[pl.ds(dest_ref[k, tok], 1), :], sem_ref.at[0])

    def issue(tok, carry):
        for k in range(TOP_K):
            row_copy(tok, k).start()
        return carry

    lax.fori_loop(0, tm, issue, 0)

    @pl.when(i == 0)
    def _():
        zero_ref[...] = jnp.zeros_like(zero_ref)

        def fill(e, carry):
            cnt = cnt_ref[e]
            padded = (cnt + block_rows - 1) // block_rows * block_rows
            base = start_ref[e] + cnt

            def zcopy(r):
                return pltpu.make_async_copy(zero_ref.at[pl.ds(0, 1), :], xs_ref.at[pl.ds(base + r, 1), :], sem_ref.at[1])

            def zstart(r, c2):
                zcopy(r).start()
                return c2

            def zwait(r, c2):
                zcopy(r).wait()
                return c2

            lax.fori_loop(0, padded - cnt, zstart, 0)
            lax.fori_loop(0, padded - cnt, zwait, 0)
            return carry

        lax.fori_loop(0, N_EXPERTS, fill, 0)

    for _ in range(TOP_K):
        pltpu.make_async_copy(x_ref, xs_ref.at[pl.ds(0, tm), :], sem_ref.at[0]).wait()


def _slot_tiles(dest, tm):
    k, t = dest.shape
    return dest.reshape(k, t // tm, tm).transpose(1, 0, 2)


def _dispatch(hp2, dest, counts, starts, n_rows):
    t, w = hp2.shape
    tm = _pick(t, (256, 128))
    grid_spec = pltpu.PrefetchScalarGridSpec(
        num_scalar_prefetch=2,
        grid=(t // tm,),
        in_specs=[pl.BlockSpec(memory_space=pl.ANY),
                  pl.BlockSpec((tm, w), lambda i, c, s: (i, 0))],
        out_specs=pl.BlockSpec(memory_space=pl.ANY),
        scratch_shapes=[pltpu.SMEM((TOP_K, tm), I32), pltpu.VMEM((8, w), U32), pltpu.SemaphoreType.DMA((3,))],
    )
    return pl.pallas_call(
        functools.partial(_dispatch_body, block_rows=MOE_ROWS),
        grid_spec=grid_spec,
        out_shape=jax.ShapeDtypeStruct((n_rows, w), U32),
        compiler_params=pltpu.CompilerParams(dimension_semantics=("arbitrary",)),
        name="moe_dispatch",
    )(counts, starts, _slot_tiles(dest, tm), hp2)


def _expert_body(be_ref, nu_ref, x_ref, wg_ref, wu_ref, wd_ref, o_ref):
    j = pl.program_id(0)

    @pl.when(j < nu_ref[0])
    def _():
        half = x_ref.shape[1]
        lo, hi = _unpack_halves(x_ref[...])
        lo, hi = lo.astype(BF16), hi.astype(BF16)

        def proj(w_ref):
            return (jnp.dot(lo, w_ref[:half, :], preferred_element_type=F32)
                    + jnp.dot(hi, w_ref[half:, :], preferred_element_type=F32))

        hmid = (_silu(proj(wg_ref)) * proj(wu_ref)).astype(BF16)
        o_ref[...] = _pack_halves(jnp.dot(hmid, wd_ref[...], preferred_element_type=F32))


def _experts(xs, wg, wu, wd, block_e, n_used):
    rows, w = xs.shape
    _, d, ff = wg.shape
    bm = MOE_ROWS
    nb = rows // bm

    def row_idx(j, be, nu):
        return (jnp.minimum(j, nu[0] - 1), 0)

    def w_idx(j, be, nu):
        return (be[jnp.minimum(j, nu[0] - 1)], 0, 0)

    grid_spec = pltpu.PrefetchScalarGridSpec(
        num_scalar_prefetch=2,
        grid=(nb,),
        in_specs=[pl.BlockSpec((bm, w), row_idx),
                  pl.BlockSpec((None, d, ff), w_idx), pl.BlockSpec((None, d, ff), w_idx),
                  pl.BlockSpec((None, ff, d), w_idx)],
        out_specs=pl.BlockSpec((bm, w), row_idx),
    )
    return pl.pallas_call(
        _expert_body,
        grid_spec=grid_spec,
        out_shape=jax.ShapeDtypeStruct((rows, w), U32),
        compiler_params=_cparams(("arbitrary",), 60 * 1024 * 1024),
        name="moe_experts",
    )(block_e, n_used, xs, wg, wu, wd)


def _combine_body(dest_hbm, wt_ref, h_ref, ysh_ref, g_ref, b_ref, y_ref, o_ref, dest_ref, gbuf_ref, sem_ref, *, alpha):
    i, n_steps = pl.program_id(0), pl.num_programs(0)
    tm = h_ref.shape[0]

    def fetch(step, slot):
        cp = pltpu.make_async_copy(dest_hbm.at[step], dest_ref.at[slot], sem_ref.at[2])
        cp.start()
        cp.wait()

        def issue(tok, carry):
            for k in range(TOP_K):
                pltpu.make_async_copy(y_ref.at[pl.ds(dest_ref[slot, k, tok], 1), :],
                                      gbuf_ref.at[slot, k, pl.ds(tok, 1), :], sem_ref.at[slot]).start()
            return carry

        lax.fori_loop(0, tm, issue, 0)

    @pl.when(i == 0)
    def _():
        fetch(0, 0)

    @pl.when(i + 1 < n_steps)
    def _():
        fetch(i + 1, (i + 1) % 2)

    slot = i % 2
    for k in range(TOP_K):
        pltpu.make_async_copy(y_ref.at[pl.ds(0, tm), :], gbuf_ref.at[slot, k], sem_ref.at[slot]).wait()
    lo, hi = _unpack_halves(ysh_ref[...])
    wt = wt_ref[...]
    for k in range(TOP_K):
        glo, ghi = _unpack_halves(gbuf_ref[slot, k])
        wk = wt[:, k:k + 1]
        lo = lo + glo * wk
        hi = hi + ghi * wk
    x = alpha * h_ref[...] + jnp.concatenate([lo, hi], axis=1)
    mu = jnp.mean(x, axis=-1, keepdims=True)
    xc = x - mu
    var = jnp.mean(xc * xc, axis=-1, keepdims=True)
    o_ref[...] = xc * lax.rsqrt(var + LN_EPS) * g_ref[...] + b_ref[...]


def _combine(h2, ysh, y_sorted, dest, wt, g, b, alpha):
    t, d = h2.shape
    w = d // 2
    tm = _pick(t, (128,))
    vec = pl.BlockSpec((1, d), lambda i: (0, 0))
    return pl.pallas_call(
        functools.partial(_combine_body, alpha=alpha),
        grid=(t // tm,),
        in_specs=[pl.BlockSpec(memory_space=pl.ANY),
                  pl.BlockSpec((tm, TOP_K), lambda i: (i, 0)),
                  pl.BlockSpec((tm, d), lambda i: (i, 0)),
                  pl.BlockSpec((tm, w), lambda i: (i, 0)),
                  vec, vec,
                  pl.BlockSpec(memory_space=pl.ANY)],
        out_specs=pl.BlockSpec((tm, d), lambda i: (i, 0)),
        out_shape=jax.ShapeDtypeStruct((t, d), F32),
        scratch_shapes=[pltpu.SMEM((2, TOP_K, tm), I32), pltpu.VMEM((2, TOP_K, tm, w), U32),
                        pltpu.SemaphoreType.DMA((3,))],
        compiler_params=_cparams(("arbitrary",)),
        name="moe_combine",
    )(_slot_tiles(dest, tm), wt, h2, ysh, g.reshape(1, d), b.reshape(1, d), y_sorted)


def _rope_tables(pos):
    inv_freq = ROPE_THETA ** (-jnp.arange(0, MLA_ROPE, 2, dtype=F32) / MLA_ROPE)
    ang = pos.astype(F32)[..., None] * inv_freq
    return jnp.cos(ang), jnp.sin(ang)


def kernel(x, positions, meta_tokens, ln_in_g, ln_in_b, w_in, b_gate, conv_w, conv_b, dt_bias, a_log, d_skip,
           ssm_norm_g, w_ssm_proj, q_a_norm_g, w_q_b, kv_a_norm_g, w_kv_b, w_attn_proj, w_out, ln1_g, ln1_b,
           w_router, router_bias, w_exp_gate, w_exp_up, w_exp_down, w_sh_gate, w_sh_up, w_sh_down, ln2_g, ln2_b):
    bsz, seq, d = x.shape
    depth = w_in.shape[0]
    heads = dt_bias.shape[-1]
    d_inner = w_ssm_proj.shape[1]
    conv_dim = conv_w.shape[-1]
    q_rank = w_q_b.shape[1]
    kv_rank = w_kv_b.shape[1]
    hh = MLA_HEADS
    g = SSM_GROUPS
    hg = heads // g
    assert seq % CHUNK == 0 and d % (2 * LANES) == 0
    lp = seq + CHUNK
    t = bsz * seq
    alpha = (2.0 * depth) ** 0.25

    meta_chunk = jnp.concatenate([jnp.zeros((META_PAD, d), x.dtype), meta_tokens.astype(x.dtype)], axis=0)
    pos =jnp.concatenate([positions.astype(I32) + N_META, jnp.zeros((bsz, META_PAD), I32),
                           jnp.broadcast_to(jnp.arange(N_META, dtype=I32), (bsz, N_META))], axis=1)
    cos, sin = _rope_tables(pos)
    zr = jnp.zeros_like(cos)
    rope_c = jnp.concatenate([cos, cos, zr, zr], axis=-1)[:, :seq]
    rope_slo = jnp.concatenate([-sin, zr, zr, zr], axis=-1)[:, :seq]
    rope_shi = jnp.concatenate([zr, sin, zr, zr], axis=-1)[:, :seq]
    k_c = jnp.concatenate([cos, cos, zr, zr], axis=-1)
    k_s = jnp.concatenate([-sin, sin, zr, zr], axis=-1)

    h_f32, h_b16 = _layer_norm_in(x, meta_chunk, ln_in_g, ln_in_b)
    tm_all = _pick(lp, (1408, 1152, 1024, 896, 768, 640, 512, 384, 256, 128))
    tm_seq = _pick(seq, (1024, 512, 256, 128))

    out = None
    for l in range(depth):
        offs = [0]
        for wdt in (d_inner, conv_dim, heads, q_rank, kv_rank, MLA_ROPE, 2 * d):
            offs.append(offs[-1] + wdt)
        wt = jnp.swapaxes(w_in[l], 0, 1)
        w_kr = wt[offs[5]:offs[6]]
        small_cols = q_rank + kv_rank + 2 * MLA_ROPE + heads
        small_n = small_cols + (-small_cols) % 256
        wt_small = jnp.concatenate([wt[offs[3]:offs[4]], wt[offs[4]:offs[5]], w_kr, w_kr, wt[offs[2]:offs[3]],
                                    jnp.zeros((small_n - small_cols, d), F32)], axis=0)

        n_exp, _, ff = w_exp_gate[l].shape
        z3 = _mm(h_b16, wt, w_rows=(offs[0], d_inner), out_dtype=BF16, tm=tm_all, tn=512, name="in_proj_z")
        xbc3 = _mm(h_b16, wt, w_rows=(offs[1], conv_dim), out_dtype=BF16, tm=tm_all, tn=512, name="in_proj_xbc")
        small = _mm(h_b16, wt_small, w_rows=(0, small_n), out_dtype=F32, tm=tm_all, tn=256, name="in_proj_small")
        gates = _mm(h_b16, wt, w_rows=(offs[6], 2 * d), out_dtype=BF16, tm=tm_seq, tn=512, n_row_tiles=seq // tm_seq,
                    bias=b_gate[l], act="sigmoid", name="in_proj_gates")
        o = 0
        q_a = small[:, :, o:o + q_rank]; o += q_rank
        kv_a = small[:, :, o:o + kv_rank]; o += kv_rank
        kr2 = small[:, :, o:o + 2 * MLA_ROPE]; o += 2 * MLA_ROPE
        dt_raw = small[:, :, o:o + heads]

        dt_g = jnp.pad(dt_raw.reshape(bsz, lp, g, hg).transpose(0, 2, 1, 3), ((0, 0), (0, 0), (0, 0), (0, LANES - hg)))
        pad_h = lambda v: jnp.pad(v.astype(F32).reshape(g, 1, hg), ((0, 0), (0, 0), (0, LANES - hg)))
        y_ssm = _ssd(xbc3, z3, dt_g, conv_w[l].astype(F32), conv_b[l].reshape(1, conv_dim).astype(F32),
                     pad_h(dt_bias[l]), pad_h(a_log[l]),
                     jnp.repeat(d_skip[l].astype(F32), SSM_HEAD_DIM).reshape(1, d_inner),
                     ssm_norm_g[l].reshape(1, d_inner).astype(F32), d_inner, heads)

        qscale = (MLA_QK ** -0.5) * math.log2(math.e)
        head_pad = ((0, 0), (0, 0), (0, MLA_QKP - MLA_QK))
        w_q = jnp.pad(w_q_b[l].reshape(q_rank, hh, MLA_QK), head_pad).reshape(q_rank, hh * MLA_QKP).astype(BF16)
        q3 = _mm(q_a, w_q, out_dtype=BF16, tm=tm_seq, tn=4 * MLA_QKP, n_row_tiles=seq // tm_seq,
                 gain=q_a_norm_g[l] * qscale, rope=(rope_c, rope_slo, rope_shi), name="q_proj")
        wkv = w_kv_b[l].reshape(kv_rank, hh, MLA_NOPE + MLA_V)
        w_k = jnp.pad(wkv[:, :, :MLA_NOPE], ((0, 0), (0, 0), (0, MLA_QKP - MLA_NOPE)))
        w_k = w_k.reshape(kv_rank, hh * MLA_QKP).astype(BF16)
        w_vt = jnp.pad(wkv[:, :, MLA_NOPE:], ((0, 0), (0, 0), (0, MLA_VP - MLA_V)))
        w_vt = w_vt.reshape(kv_rank, hh * MLA_VP).T.astype(BF16)
        ones_rows = jnp.tile(jnp.concatenate([jnp.zeros((MLA_V,), F32), jnp.ones((MLA_VP - MLA_V,), F32)]), hh)
        k3 = _mm(kv_a, w_k, out_dtype=BF16, tm=tm_all, tn=4 * MLA_QKP, gain=kv_a_norm_g[l],
                 kadd=(kr2, k_c, k_s), name="k_proj")
        vt3 = _mm_t(kv_a, w_vt, kv_a_norm_g[l], ones_rows, out_dtype=BF16, tm=tm_all, tn=4 * MLA_VP,
                    name="v_proj")
        y_attn, wg_b, wu_b, wd_b = _attention(
            q3, k3, vt3, seq, [w_exp_gate[l].reshape(n_exp * d, ff), w_exp_up[l].reshape(n_exp * d, ff),
                               w_exp_down[l].reshape(n_exp * ff, d)])

        mixed = _mm_ktiled([y_ssm, y_attn], [w_ssm_proj[l].astype(BF16), w_attn_proj[l].astype(BF16)],
                           out_dtype=BF16, rows=seq, tm=tm_seq, tn=_pick(d, (1024, 512)), nk=4,
                           gates=gates, name="branch_proj")
        pre1 = _mm_ktiled([mixed], [w_out[l].astype(BF16)], out_dtype=F32, rows=seq, tm=tm_seq,
                          tn=_pick(d, (1024, 512)), nk=2, res=h_f32, res_scale=alpha, name="out_proj")
        h1, h1p = _layer_norm_pack(pre1.reshape(t, d), ln1_g[l], ln1_b[l])

        e_slot, pos_slot, w_slot, cnt = _router(h1, w_router[l], router_bias[l])
        counts = cnt[:, 0].astype(I32)
        padded = (counts + MOE_ROWS - 1) // MOE_ROWS * MOE_ROWS
        ends = jnp.cumsum(padded)
        starts = ends - padded
        onehot = e_slot[None] == jnp.arange(N_EXPERTS, dtype=I32)[:, None, None]
        dest = pos_slot + jnp.sum(jnp.where(onehot, starts[:, None, None], 0), axis=0)
        n_blocks = -(-(t * TOP_K) // MOE_ROWS) + N_EXPERTS
        blk0 = jnp.arange(n_blocks, dtype=I32) * MOE_ROWS
        block_e = jnp.minimum(jnp.sum((ends[None, :] <= blk0[:, None]).astype(I32), axis=1), N_EXPERTS - 1)
        n_used = (ends[-1] // MOE_ROWS).reshape(1)
        xs = _dispatch(h1p, dest, counts, starts, n_blocks * MOE_ROWS)
        y_sorted = _experts(xs, wg_b.reshape(n_exp, d, ff), wu_b.reshape(n_exp, d, ff), wd_b.reshape(n_exp, ff, d),
                            block_e, n_used)
        y_shared = _experts(h1p, w_sh_gate[l][None].astype(BF16), w_sh_up[l][None].astype(BF16),
                            w_sh_down[l][None].astype(BF16), jnp.zeros((t // MOE_ROWS,), I32),
                            jnp.full((1,), t // MOE_ROWS, I32))
        out = _combine(h1, y_shared, y_sorted, dest, w_slot.T, ln2_g[l], ln2_b[l], alpha)
        if l + 1 < depth:
            raise NotImplementedError("stacked layers need the meta rows carried through the channel mixer")
    return out.reshape(bsz, seq, d)
```

```python
import functools
import math

import jax
import jax.numpy as jnp
from jax import lax
from jax.experimental import pallas as pl
from jax.experimental.pallas import tpu as pltpu

F32 = jnp.float32
BF16 = jnp.bfloat16
U32 = jnp.uint32
I32 = jnp.int32

N_META = 16
CHUNK = 128
META_PAD = CHUNK - N_META
SSM_HEAD_DIM = 64
SSM_GROUPS = 8
SSM_STATE = 128
SSM_CONV = 4
MLA_HEADS = 64
MLA_NOPE = 128
MLA_ROPE = 64
MLA_V = 128
MLA_QK = MLA_NOPE + MLA_ROPE
ROPE_THETA = 10000.0
N_EXPERTS = 64
N_EXPERT_GROUPS = 8
TOPK_GROUPS = 4
TOP_K = 8
ROUTED_SCALE = 2.5
LN_EPS = 1e-5
RMS_EPS = 1e-6
NEG = -1e30
LANES = 128
MLA_QKP = 2 * LANES
MOE_ROWS = 256
MLA_VP = MLA_V + 16
VMEM_BIG = 56 * 1024 * 1024
VMEM_MID = 44 * 1024 * 1024


def _cparams(sem, vmem=VMEM_MID):
    return pltpu.CompilerParams(dimension_semantics=sem, vmem_limit_bytes=vmem)


def _pick(n, cands):
    for c in cands:
        if n % c == 0:
            return c
    raise ValueError(f"no tile for {n} in {cands}")


def _ln_body(x_ref, g_ref, b_ref, of_ref, ob_ref):
    x = x_ref[...]
    mu = jnp.mean(x, axis=-1, keepdims=True)
    xc = x - mu
    var = jnp.mean(xc * xc, axis=-1, keepdims=True)
    y = xc * lax.rsqrt(var + LN_EPS) * g_ref[...] + b_ref[...]
    of_ref[...] = y
    ob_ref[...] = y.astype(BF16)


def _ln_in_body(x_ref, meta_ref, g_ref, b_ref, of_ref, ob_ref, *, n_real):
    i = pl.program_id(1)

    @pl.when(i < n_real)
    def _():
        _ln_body(x_ref, g_ref, b_ref, of_ref, ob_ref)

    @pl.when(i == n_real)
    def _():
        _ln_body(meta_ref, g_ref, b_ref, of_ref, ob_ref)


def _layer_norm_in(x3, meta_chunk, g, b):
    bsz, seq, d = x3.shape
    tm = CHUNK
    n_real = seq // tm
    spec = pl.BlockSpec((None, tm, d), lambda bi, i: (bi, i, 0))
    vec = pl.BlockSpec((1, d), lambda bi, i: (0, 0))
    shape = (bsz, seq + tm, d)
    return pl.pallas_call(
        functools.partial(_ln_in_body, n_real=n_real),
        grid=(bsz, n_real + 1),
        in_specs=[pl.BlockSpec((None, tm, d), lambda bi, i: (bi, jnp.minimum(i, n_real - 1), 0)),
                  pl.BlockSpec((tm, d), lambda bi, i: (0, 0)), vec, vec],
        out_specs=[spec, spec],
        out_shape=[jax.ShapeDtypeStruct(shape, F32), jax.ShapeDtypeStruct(shape, BF16)],
        compiler_params=_cparams(("parallel", "arbitrary")),
        name="layer_norm",
    )(x3, meta_chunk, g.reshape(1, d), b.reshape(1, d))


def _pack_halves(y):
    n = y.shape[1] // 2
    lo = pltpu.bitcast(y[:, :n].astype(BF16).astype(F32), U32) >> 16
    hi = pltpu.bitcast(y[:, n:].astype(BF16).astype(F32), U32) & jnp.uint32(0xFFFF0000)
    return hi | lo


def _unpack_halves(w):
    lo = pltpu.bitcast(w << 16, F32)
    hi = pltpu.bitcast(w & jnp.uint32(0xFFFF0000), F32)
    return lo, hi


def _ln_pack_body(x_ref, g_ref, b_ref, of_ref, op_ref):
    x = x_ref[...]
    mu = jnp.mean(x, axis=-1, keepdims=True)
    xc = x - mu
    var = jnp.mean(xc * xc, axis=-1, keepdims=True)
    y = xc * lax.rsqrt(var + LN_EPS) * g_ref[...] + b_ref[...]
    of_ref[...] = y
    op_ref[...] = _pack_halves(y)


def _layer_norm_pack(x2, g, b):
    rows, d = x2.shape
    tm = _pick(rows, (256, 128))
    vec = pl.BlockSpec((1, d), lambda i: (0, 0))
    return pl.pallas_call(
        _ln_pack_body,
        grid=(rows // tm,),
        in_specs=[pl.BlockSpec((tm, d), lambda i: (i, 0)), vec, vec],
        out_specs=[pl.BlockSpec((tm, d), lambda i: (i, 0)), pl.BlockSpec((tm, d // 2), lambda i: (i, 0))],
        out_shape=[jax.ShapeDtypeStruct((rows, d), F32), jax.ShapeDtypeStruct((rows, d // 2), U32)],
        compiler_params=_cparams(("parallel",)),
        name="layer_norm_pack",
    )(x2, g.reshape(1, d), b.reshape(1, d))


def _rms_rows(a_ref, gain_ref):
    af = a_ref[...].astype(F32)
    return (af * lax.rsqrt(jnp.mean(af * af, axis=-1, keepdims=True) + RMS_EPS) * gain_ref[...]).astype(BF16)


def _mm_body(*refs, has_gain, has_bias, act, rope, kadd, has_side, w_rows):
    it = iter(refs)
    a_ref, w_ref = next(it), next(it)
    gain_ref = next(it) if has_gain else None
    bias_ref = next(it) if has_bias else None
    rope_refs = [next(it) for _ in range(3)] if rope else None
    kadd_refs = [next(it) for _ in range(3)] if kadd else None
    side_in = next(it) if has_side else None
    o_ref = next(it)
    if has_side:
        next(it)[...] = side_in[...].astype(BF16)
    if has_gain:
        an_ref = next(it)

        @pl.when(pl.program_id(2) == 0)
        def _():
            an_ref[...] = _rms_rows(a_ref, gain_ref)

        a = an_ref[...]
    else:
        a = a_ref[...]
    if w_rows:
        r = lax.dot_general(a.astype(BF16), w_ref[...].astype(BF16), (((1,), (1,)), ((), ())),
                            preferred_element_type=F32)
    else:
        r = jnp.dot(a.astype(BF16), w_ref[...].astype(BF16), preferred_element_type=F32)
    tn = r.shape[1]
    if has_bias:
        r = r + bias_ref[...]
    if act == "sigmoid":
        r = jax.nn.sigmoid(r)
    if rope:
        c, slo, shi = (t[...] for t in rope_refs)
        half = MLA_ROPE // 2
        pieces = []
        for h0 in range(0, tn, MLA_QKP):
            rp = r[:, h0 + MLA_NOPE:h0 + MLA_QKP]
            pieces += [r[:, h0:h0 + MLA_NOPE], rp * c + pltpu.roll(rp, LANES - half, 1) * slo + pltpu.roll(rp, half, 1) * shi]
        r = jnp.concatenate(pieces, axis=1)
    if kadd:
        kr_ref, kc, ks = kadd_refs
        kr = kr_ref[...]
        piece = jnp.concatenate([jnp.zeros_like(kr), kr * kc[...] + pltpu.roll(kr, MLA_ROPE // 2, 1) * ks[...]], axis=1)
        r = r + jnp.concatenate([piece] * (tn // piece.shape[1]), axis=1)
    o_ref[...] = r.astype(o_ref.dtype)


def _side_specs(side, n_steps, lin):
    rows, cols = side.shape
    nb = max(c for c in range(1, n_steps + 1) if rows % c == 0 and (rows // c) % 16 == 0)
    spec = pl.BlockSpec((rows // nb, cols), lambda *ids: (jnp.minimum(lin(*ids), nb - 1), 0))
    return spec, jax.ShapeDtypeStruct(side.shape, BF16)


def _mm(a3, w, *, out_dtype, tm, tn, n_row_tiles=None, w_rows=None, gain=None, bias=None, act=None, rope=None,
        kadd=None, side=None, name="mm"):
    bsz, rows, k = a3.shape
    row0, n = (0, w.shape[1]) if w_rows is None else w_rows
    ni = rows // tm if n_row_tiles is None else n_row_tiles
    nj = n // tn
    assert n % tn == 0 and row0 % 8 == 0 and (n_row_tiles is not None or rows % tm == 0)
    if w_rows is None:
        w_spec = pl.BlockSpec((k, tn), lambda bi, i, j: (0, j))
    elif row0 % tn == 0:
        w_spec = pl.BlockSpec((tn, k), lambda bi, i, j: (row0 // tn + j, 0))
    else:
        w_spec = pl.BlockSpec((pl.Element(tn), pl.Element(k)),
                              lambda bi, i, j: (pl.multiple_of(row0 + j * tn, 8), 0))
    in_specs = [pl.BlockSpec((None, tm, k), lambda bi, i, j: (bi, i, 0)), w_spec]
    args = [a3, w]
    if gain is not None:
        in_specs.append(pl.BlockSpec((1, k), lambda bi, i, j: (0, 0)))
        args.append(gain.reshape(1, k).astype(F32))
    if bias is not None:
        in_specs.append(pl.BlockSpec((1, tn), lambda bi, i, j: (0, j)))
        args.append(bias.reshape(1, n).astype(F32))
    for t in (rope or ()) + (kadd or ()):
        in_specs.append(pl.BlockSpec((None, tm, t.shape[2]), lambda bi, i, j: (bi, i, 0)))
        args.append(t)
    out_specs = [pl.BlockSpec((None, tm, tn), lambda bi, i, j: (bi, i, j))]
    out_shape = [jax.ShapeDtypeStruct((bsz, ni * tm, n), out_dtype)]
    if side is not None:
        spec, shape = _side_specs(side, bsz * ni * nj, lambda bi, i, j: (bi * ni + i) * nj + j)
        in_specs.append(spec)
        args.append(side)
        out_specs.append(spec)
        out_shape.append(shape)
    body = functools.partial(_mm_body, has_gain=gain is not None, has_bias=bias is not None, act=act,
                             rope=rope is not None, kadd=kadd is not None, has_side=side is not None,
                             w_rows=w_rows is not None)
    outs = pl.pallas_call(
        body,
        grid=(bsz, ni, nj),
        in_specs=in_specs,
        out_specs=out_specs,
        out_shape=out_shape,
        scratch_shapes=[pltpu.VMEM((tm, k), BF16)] if gain is not None else [],
        compiler_params=_cparams(("parallel", "parallel", "arbitrary") if side is None else ("arbitrary",) * 3,
                                 VMEM_BIG),
        name=name,
    )(*args)
    return outs[0] if side is None else outs


def _mm_t_body(a_ref, wt_ref, gain_ref, bias_ref, o_ref, an_ref):
    @pl.when(pl.program_id(2) == 0)
    def _():
        an_ref[...] = _rms_rows(a_ref, gain_ref)

    r = lax.dot_general(wt_ref[...], an_ref[...], (((1,), (1,)), ((), ())), preferred_element_type=F32)
    o_ref[...] = (r + bias_ref[...]).astype(o_ref.dtype)


def _mm_t(a3, wt, gain, bias_col, *, out_dtype, tm, tn, name):
    bsz, rows, k = a3.shape
    n = wt.shape[0]
    return pl.pallas_call(
        _mm_t_body,
        grid=(bsz, rows // tm, n // tn),
        in_specs=[pl.BlockSpec((None, tm, k), lambda bi, i, j: (bi, i, 0)),
                  pl.BlockSpec((tn, k), lambda bi, i, j: (j, 0)),
                  pl.BlockSpec((1, k), lambda bi, i, j: (0, 0)),
                  pl.BlockSpec((tn, 1), lambda bi, i, j: (j, 0))],
        out_specs=pl.BlockSpec((None, tn, tm), lambda bi, i, j: (bi, j, i)),
        out_shape=jax.ShapeDtypeStruct((bsz, n, rows), out_dtype),
        scratch_shapes=[pltpu.VMEM((tm, k), BF16)],
        compiler_params=_cparams(("parallel", "parallel", "arbitrary"), VMEM_BIG),
        name=name,
    )(a3, wt, gain.reshape(1, k).astype(F32), bias_col.reshape(n, 1).astype(F32))


def _mmk_body(*refs, n_pairs, has_gate, has_res, res_scale, has_side):
    it = iter(refs)
    a_refs = [next(it) for _ in range(n_pairs)]
    w_refs = [next(it) for _ in range(n_pairs)]
    g_refs = [next(it) for _ in range(n_pairs)] if has_gate else None
    res_ref = next(it) if has_res else None
    side_in = next(it) if has_side else None
    o_ref = next(it)
    if has_side:
        next(it)[...] = side_in[...].astype(BF16)
    acc_refs = [next(it) for _ in range(n_pairs)]
    kk = pl.program_id(3)

    @pl.when(kk == 0)
    def _():
        for acc in acc_refs:
            acc[...] = jnp.zeros_like(acc)

    for a_ref, w_ref, acc in zip(a_refs, w_refs, acc_refs):
        acc[...] += jnp.dot(a_ref[...], w_ref[...], preferred_element_type=F32)

    @pl.when(kk == pl.num_programs(3) - 1)
    def _():
        r = None
        for p, acc in enumerate(acc_refs):
            t = acc[...]
            if has_gate:
                t = t * g_refs[p][...].astype(F32)
            r = t if r is None else r + t
        if has_res:
            r = r + res_scale * res_ref[...]
        o_ref[...] = r.astype(o_ref.dtype)


def _mm_ktiled(a_list, w_list, *, out_dtype, rows, tm, tn, nk, gates=None, res=None, res_scale=1.0, side=None,
               name="mmk"):
    bsz = a_list[0].shape[0]
    n = w_list[0].shape[1]
    npairs = len(a_list)
    tks = [a.shape[2] // nk for a in a_list]
    assert all(a.shape[2] == tk * nk and tk % LANES == 0 for a, tk in zip(a_list, tks))
    o_spec = pl.BlockSpec((None, tm, tn), lambda bi, i, j, kk: (bi, i, j))
    in_specs = ([pl.BlockSpec((None, tm, tk), lambda bi, i, j, kk: (bi, i, kk)) for tk in tks]
                + [pl.BlockSpec((tk, tn), lambda bi, i, j, kk: (kk, j)) for tk in tks])
    args = list(a_list) + list(w_list)
    if gates is not None:
        for p in range(npairs):
            in_specs.append(pl.BlockSpec((None, tm, tn), lambda bi, i, j, kk, p=p: (bi, i, p * (n // tn) + j)))
            args.append(gates)
    if res is not None:
        in_specs.append(o_spec)
        args.append(res)
    ni, nj = rows // tm, n // tn
    out_specs = [o_spec]
    out_shape = [jax.ShapeDtypeStruct((bsz, rows, n), out_dtype)]
    if side is not None:
        spec, shape = _side_specs(side, bsz * ni * nj * nk, lambda bi, i, j, kk: ((bi * ni + i) * nj + j) * nk + kk)
        in_specs.append(spec)
        args.append(side)
        out_specs.append(spec)
        out_shape.append(shape)
    body = functools.partial(_mmk_body, n_pairs=npairs, has_gate=gates is not None, has_res=res is not None,
                             res_scale=res_scale, has_side=side is not None)
    outs = pl.pallas_call(
        body,
        grid=(bsz, ni, nj, nk),
        in_specs=in_specs,
        out_specs=out_specs,
        out_shape=out_shape,
        scratch_shapes=[pltpu.VMEM((tm, tn), F32) for _ in range(npairs)],
        compiler_params=_cparams(("parallel", "parallel", "parallel", "arbitrary") if side is None
                                 else ("arbitrary",) * 4, VMEM_BIG),
        name=name,
    )(*args)
    return outs[0] if side is None else outs


def _silu(x):
    return x * jax.nn.sigmoid(x)


def _ssd_body(xs_ref, b_ref, c_ref, z_ref, dt_ref, wx_ref, wb_ref, wc_ref, bx_ref, bb_ref, bc_ref,
              dtb_ref, alog_ref, dskip_ref, ng_ref, e_ref, o_ref,
              state_ref, extx_ref, extb_ref, extc_ref, y_ref, *, hg, gp):
    c = pl.program_id(2)
    q = CHUNK
    p = SSM_HEAD_DIM
    n = SSM_STATE
    gw = hg * p
    first = c == 0
    row = lax.broadcasted_iota(I32, (q, 1), 0)
    live = jnp.logical_or(jnp.logical_not(first), row >= META_PAD)

    @pl.when(first)
    def _():
        state_ref[...] = jnp.zeros_like(state_ref)
        extx_ref[0:8, :] = jnp.zeros((8, extx_ref.shape[1]), F32)
        extb_ref[0:8, :] = jnp.zeros((8, extb_ref.shape[1]), F32)
        extc_ref[0:8, :] = jnp.zeros((8, extc_ref.shape[1]), F32)

    def conv(ext_ref, u_ref, w_ref, bias_ref):
        @pl.when(first)
        def _():
            ext_ref[8:8 + q, :] = jnp.where(live, u_ref[...].astype(F32), 0.0)

        @pl.when(jnp.logical_not(first))
        def _():
            ext_ref[8:8 + q, :] = u_ref[...].astype(F32)

        acc = bias_ref[...]
        for k in range(SSM_CONV):
            acc = acc + ext_ref[pl.ds(8 - (SSM_CONV - 1) + k, q), :] * w_ref[k:k + 1, :]
        ext_ref[0:8, :] = ext_ref[q:q + 8, :]
        return _silu(acc)

    xs_w = conv(extx_ref, xs_ref, wx_ref, bx_ref)
    bm_w = conv(extb_ref, b_ref, wb_ref, bb_ref)
    cm_w = conv(extc_ref, c_ref, wc_ref, bc_ref)

    r_i = lax.broadcasted_iota(I32, (q, q), 0)
    c_i = lax.broadcasted_iota(I32, (q, q), 1)
    causal = r_i >= c_i
    tri = causal.astype(F32)
    e = e_ref[...]

    def spread(v):
        v_hi = v.astype(BF16)
        v_lo = (v - v_hi.astype(F32)).astype(BF16)
        return jnp.dot(v_hi, e, preferred_element_type=F32) + jnp.dot(v_lo, e, preferred_element_type=F32)

    groups = [dict() for _ in range(gp)]
    for gg, s in enumerate(groups):
        x = dt_ref[gg] + dtb_ref[gg]
        dt = jnp.maximum(x, 0.0) + jnp.log1p(jnp.exp(-jnp.abs(x)))
        dt = jnp.where(live, dt, 0.0)
        a = -jnp.exp(alog_ref[gg])
        a_cs = jnp.dot(tri, dt * a, precision=lax.Precision.HIGHEST, preferred_element_type=F32)
        s.update(dt=dt, a_cs=a_cs, a_cs_t=a_cs.T, ea=jnp.exp(a_cs), de=jnp.exp(a_cs[q - 1:q, :] - a_cs))

    for gg, s in enumerate(groups):
        xs = xs_w[:, gg * gw:(gg + 1) * gw]
        bm = bm_w[:, gg * n:(gg + 1) * n]
        cm_b = cm_w[:, gg * n:(gg + 1) * n].astype(BF16)
        ea_x = spread(s["ea"])
        xdt = xs * spread(s["dt"])
        prev = state_ref[gg]
        s.update(xs=xs, ea_x=ea_x, prev=prev, xdt_b=xdt.astype(BF16),
                 cb=lax.dot_general(cm_b, bm.astype(BF16), (((1,), (1,)), ((), ())), preferred_element_type=F32),
                 y_off=jnp.dot(cm_b, prev.astype(BF16), preferred_element_type=F32) * ea_x,
                 upd=jnp.dot(bm.T.astype(BF16), (xdt * spread(s["de"])).astype(BF16), preferred_element_type=F32))

    def decay_scores(s, j):
        seg = s["a_cs"][:, j:j + 1] - s["a_cs_t"][j:j + 1, :]
        return (jnp.exp(jnp.where(causal, seg, NEG)) * s["cb"]).astype(BF16)

    nxt = [decay_scores(s, 0) for s in groups]
    for j in range(hg):
        for gg, s in enumerate(groups):
            m = nxt[gg]
            if j + 1 < hg:
                nxt[gg] = decay_scores(s, j + 1)
            lo = gg * gw + j * p
            y_ref[:, lo:lo + p] = jnp.dot(m, s["xdt_b"][:, j * p:(j + 1) * p], preferred_element_type=F32)

    for gg, s in enumerate(groups):
        sl = slice(gg * gw, (gg + 1) * gw)
        y = y_ref[:, sl] + s["y_off"] + s["xs"] * dskip_ref[:, sl]
        state_ref[gg] = s["prev"] * s["ea_x"][q - 1:q, :] + s["upd"]
        yz = y * _silu(z_ref[:, sl].astype(F32))
        yn = yz * lax.rsqrt(jnp.mean(yz * yz, axis=-1, keepdims=True) + RMS_EPS) * ng_ref[:, sl]
        o_ref[:, sl] = yn.astype(o_ref.dtype)


def _ssd(xbc3, z3, dt_g, conv_w, conv_b, dtb_g, alog_g, dskip_x, norm_g, d_inner, heads):
    bsz, rows, conv_dim = xbc3.shape
    g, n, p, q = SSM_GROUPS, SSM_STATE, SSM_HEAD_DIM, CHUNK
    gp = 2
    hg = heads // g
    gw = hg * p
    nc = rows // q
    assert gw % LANES == 0 and d_inner % (gp * n) == 0 and hg <= LANES and g % gp == 0
    boff = d_inner // (gp * n)
    coff = boff + g // gp
    expand = (jnp.arange(LANES)[:, None] == (jnp.arange(gw) // p)[None, :]).astype(BF16)

    def chunk(c):
        return (c + nc - 1) % nc

    in_specs = [
        pl.BlockSpec((None, q, gp * gw), lambda b, gi, c: (b, chunk(c), gi)),
        pl.BlockSpec((None, q, gp * n), lambda b, gi, c: (b, chunk(c), boff + gi)),
        pl.BlockSpec((None, q, gp * n), lambda b, gi, c: (b, chunk(c), coff + gi)),
        pl.BlockSpec((None, q, gp * gw), lambda b, gi, c: (b, chunk(c), gi)),
        pl.BlockSpec((None, gp, q, LANES), lambda b, gi, c: (b, gi, chunk(c), 0)),
        pl.BlockSpec((SSM_CONV, gp * gw), lambda b, gi, c: (0, gi)),
        pl.BlockSpec((SSM_CONV, gp * n), lambda b, gi, c: (0, boff + gi)),
        pl.BlockSpec((SSM_CONV, gp * n), lambda b, gi, c: (0, coff + gi)),
        pl.BlockSpec((1, gp * gw), lambda b, gi, c: (0, gi)),
        pl.BlockSpec((1, gp * n), lambda b, gi, c: (0, boff + gi)),
        pl.BlockSpec((1, gp * n), lambda b, gi, c: (0, coff + gi)),
        pl.BlockSpec((gp, 1, LANES), lambda b, gi, c: (gi, 0, 0)),
        pl.BlockSpec((gp, 1, LANES), lambda b, gi, c: (gi, 0, 0)),
        pl.BlockSpec((1, gp * gw), lambda b, gi, c: (0, gi)),
        pl.BlockSpec((1, gp * gw), lambda b, gi, c: (0, gi)),
        pl.BlockSpec((LANES, gw), lambda b, gi, c: (0, 0)),
    ]
    return pl.pallas_call(
        functools.partial(_ssd_body, hg=hg, gp=gp),
        grid=(bsz, g // gp, nc),
        in_specs=in_specs,
        out_specs=pl.BlockSpec((None, q, gp * gw), lambda b, gi, c: (b, chunk(c), gi)),
        out_shape=jax.ShapeDtypeStruct((bsz, rows, d_inner), BF16),
        scratch_shapes=[pltpu.VMEM((gp, n, gw), F32), pltpu.VMEM((q + 8, gp * gw), F32),
                        pltpu.VMEM((q + 8, gp * n), F32), pltpu.VMEM((q + 8, gp * n), F32),
                        pltpu.VMEM((q, gp * gw), F32)],
        compiler_params=_cparams(("parallel", "parallel", "arbitrary")),
        name="ssd",
    )(xbc3, xbc3, xbc3, z3, dt_g, conv_w, conv_w, conv_w, conv_b, conv_b, conv_b, dtb_g, alog_g,
      dskip_x, norm_g, expand)


def _attn_body(qi_ref, ki_ref, q_ref, km_ref, vm_ref, k_ref, v_ref, *rest, heads, n_side):
    side_in, o_ref, side_out = rest[:n_side], rest[n_side], rest[n_side + 1:2 * n_side + 1]
    m_ref, acc_ref = rest[2 * n_side + 1:]
    for src, dst in zip(side_in, side_out):
        dst[...] = src[...].astype(BF16)
    step_id = pl.program_id(2)
    qi, ki = qi_ref[step_id], ki_ref[step_id]
    tq = q_ref.shape[0]

    def step(segments):
        def scores(h):
            out = []
            for kr, _, key0, nk, q0, mask in segments:
                qh = q_ref[q0:, h * MLA_QKP:(h + 1) * MLA_QKP]
                kh = kr[key0:key0 + nk, h * MLA_QKP:(h + 1) * MLA_QKP]
                st = lax.dot_general(kh, qh, (((1,), (1,)), ((), ())), preferred_element_type=F32)
                key = key0 + lax.broadcasted_iota(I32, st.shape, 0)
                if mask == "meta":
                    st = jnp.where(key >= META_PAD, st, NEG)
                elif mask == "causal":
                    st = jnp.where(key <= q0 + lax.broadcasted_iota(I32, st.shape, 1), st, NEG)
                out.append(st)
            return out

        def probs(h, sts):
            m_prev = m_ref[h]
            m_new = m_prev
            for (_, _, _, _, q0, _), st in zip(segments, sts):
                cm = jnp.max(st, axis=0, keepdims=True)
                if q0:
                    cm = jnp.concatenate([jnp.full((1, q0), NEG, F32), cm], axis=1)
                m_new = jnp.maximum(m_new, cm)
            m_ref[h] = m_new
            pts = [jnp.exp2((st - m_new[:, q0:]).astype(BF16)) for (_, _, _, _, q0, _), st in zip(segments, sts)]
            return jnp.exp2(m_prev - m_new), pts

        def accumulate(h, alpha, pts):
            sl = slice(h * MLA_VP, (h + 1) * MLA_VP)
            acc = acc_ref[sl, :] * alpha
            partial = []
            for (_, vr, key0, nk, q0, _), pt in zip(segments, pts):
                contrib = jnp.dot(vr[sl, key0:key0 + nk], pt, preferred_element_type=F32)
                if q0:
                    partial.append((q0, contrib))
                else:
                    acc = acc + contrib
            acc_ref[sl, :] = acc
            for q0, contrib in partial:
                acc_ref[sl, q0:] += contrib

        st_q, pr_q = {}, {}
        for stage in range(heads + 2):
            if stage < heads:
                st_q[stage] = scores(stage)
            if 0 <= stage - 1 < heads:
                pr_q[stage - 1] = probs(stage - 1, st_q.pop(stage - 1))
            if 0 <= stage - 2 < heads:
                accumulate(stage - 2, *pr_q.pop(stage - 2))

    def start():
        m_ref[...] = jnp.full_like(m_ref, NEG)
        acc_ref[...] = jnp.zeros_like(acc_ref)

    def finish():
        for h in range(heads):
            num = acc_ref[h * MLA_VP:h * MLA_VP + MLA_V, :]
            den = acc_ref[h * MLA_VP + MLA_V:h * MLA_VP + MLA_V + 1, :]
            o_ref[:, h * MLA_V:(h + 1) * MLA_V] = (num / den).T.astype(o_ref.dtype)

    tk = k_ref.shape[0]
    meta = (km_ref, vm_ref, 0, km_ref.shape[0], 0, "meta")
    full = (k_ref, v_ref, 0, tk, 0, None)
    half = tk // 2
    if half % (2 * LANES) == 0:
        diag = [(k_ref, v_ref, 0, half, 0, "causal"), (k_ref, v_ref, half, half, half, "causal")]
    else:
        diag = [(k_ref, v_ref, 0, tk, 0, "causal")]

    @pl.when(jnp.logical_and(ki == 0, qi == 0))
    def _():
        start()
        step([meta] + diag)
        finish()

    @pl.when(jnp.logical_and(ki == 0, qi > 0))
    def _():
        start()
        step([meta, full])

    @pl.when(jnp.logical_and(ki > 0, ki < qi))
    def _():
        step([full])

    @pl.when(jnp.logical_and(ki > 0, ki == qi))
    def _():
        step(diag)
        finish()


def _attention(q3, k3, vt3, seq, sides=()):
    bsz = q3.shape[0]
    hp = 8
    t = _pick(seq, (512, 256, 128))
    nq = seq // t
    meta_blk = seq // CHUNK
    qw, vw = hp * MLA_QKP, hp * MLA_VP
    qi_tab = jnp.asarray([qi for qi in range(nq) for _ in range(qi + 1)], I32)
    ki_tab = jnp.asarray([ki for qi in range(nq) for ki in range(qi + 1)], I32)

    def q_idx(b, h, s, qt, kt):
        return (b, qt[s], h)

    def k_idx(b, h, s, qt, kt):
        return (b, kt[s], h)

    def vt_idx(b, h, s, qt, kt):
        return (b, h, kt[s])

    nh, npairs = MLA_HEADS // hp, int(qi_tab.shape[0])
    side_specs = [_side_specs(s_arr, bsz * nh * npairs, lambda b, h, s, qt, kt: (b * nh + h) * npairs + s)
                  for s_arr in sides]
    grid_spec = pltpu.PrefetchScalarGridSpec(
        num_scalar_prefetch=2,
        grid=(bsz, nh, npairs),
        in_specs=[
            pl.BlockSpec((None, t, qw), q_idx),
            pl.BlockSpec((None, CHUNK, qw), lambda b, h, s, qt, kt: (b, meta_blk, h)),
            pl.BlockSpec((None, vw, CHUNK), lambda b, h, s, qt, kt: (b, h, meta_blk)),
            pl.BlockSpec((None, t, qw), k_idx),
            pl.BlockSpec((None, vw, t), vt_idx),
        ] + [spec for spec, _ in side_specs],
        out_specs=[pl.BlockSpec((None, t, hp * MLA_V), q_idx)] + [spec for spec, _ in side_specs],
        scratch_shapes=[pltpu.VMEM((hp, 1, t), F32), pltpu.VMEM((vw, t), F32)],
    )
    return pl.pallas_call(
        functools.partial(_attn_body, heads=hp, n_side=len(sides)),
        grid_spec=grid_spec,
        out_shape=[jax.ShapeDtypeStruct((bsz, seq, MLA_HEADS * MLA_V), BF16)] + [shape for _, shape in side_specs],
        compiler_params=_cparams(("arbitrary", "arbitrary", "arbitrary")),
        name="mla_attention",
    )(qi_tab, ki_tab, q3, k3, vt3, k3, vt3, *sides)


def _router_body(h_ref, wr_ref, rb_ref, e_ref, pos_ref, w_ref, cnt_ref, carry_ref):
    i = pl.program_id(0)
    ne, ng = N_EXPERTS, N_EXPERT_GROUPS
    gs = ne // ng
    tm = h_ref.shape[0]

    @pl.when(i == 0)
    def _():
        carry_ref[...] = jnp.zeros_like(carry_ref)

    logits = lax.dot_general(wr_ref[...], h_ref[...], (((1,), (1,)), ((), ())), precision=lax.Precision.HIGHEST,
                             preferred_element_type=F32)
    scores = jax.nn.sigmoid(logits)
    choice = scores + rb_ref[...]
    sub = lax.broadcasted_iota(I32, (gs, tm), 0)
    grp_rows = []
    for g in range(ng):
        blk = choice[g * gs:(g + 1) * gs, :]
        m1 = jnp.max(blk, axis=0, keepdims=True)
        first = jnp.min(jnp.where(blk == m1, sub, gs), axis=0, keepdims=True)
        m2 = jnp.max(jnp.where(sub == first, -jnp.inf, blk), axis=0, keepdims=True)
        grp_rows.append(m1 + m2)
    grp = jnp.concatenate(grp_rows, axis=0)
    grank = jnp.zeros((ng, tm), I32)
    gidx = lax.broadcasted_iota(I32, (ng, tm), 0)
    for g in range(ng):
        rowv = grp[g:g + 1, :]
        beats = jnp.logical_or(rowv > grp, jnp.logical_and(rowv == grp, gidx > g))
        grank = grank + beats.astype(I32)
    gsel = (grank < TOPK_GROUPS).astype(F32)
    esel = jnp.concatenate([jnp.broadcast_to(gsel[g:g + 1, :], (gs, tm)) for g in range(ng)], axis=0)
    masked = jnp.where(esel > 0.0, choice, -jnp.inf)
    eidx = lax.broadcasted_iota(I32, (ne, tm), 0)
    rank = jnp.zeros((ne, tm), I32)
    for e in range(ne):
        rowv = masked[e:e + 1, :]
        beats = jnp.logical_or(rowv > masked, jnp.logical_and(rowv == masked, eidx > e))
        rank = rank + beats.astype(I32)
    top = jnp.logical_and(rank < TOP_K, esel > 0.0)
    topf = top.astype(F32)
    wsel = jnp.where(top, scores, 0.0)
    wn = wsel / jnp.sum(wsel, axis=0, keepdims=True) * ROUTED_SCALE
    r_i = lax.broadcasted_iota(I32, (tm, tm), 0)
    c_i = lax.broadcasted_iota(I32, (tm, tm), 1)
    before = (r_i < c_i).astype(BF16)
    pos = jnp.dot(topf.astype(BF16), before, preferred_element_type=F32) + carry_ref[:, :1]
    carry_ref[...] = carry_ref[...] + jnp.sum(topf, axis=1, keepdims=True)
    cnt_ref[...] = carry_ref[...]
    e_rows, p_rows, w_rows = [], [], []
    for k in range(TOP_K):
        hit = rank == k
        e_rows.append(jnp.sum(jnp.where(hit, eidx, 0), axis=0, keepdims=True))
        p_rows.append(jnp.sum(jnp.where(hit, pos, 0.0), axis=0, keepdims=True))
        w_rows.append(jnp.sum(jnp.where(hit, wn, 0.0), axis=0, keepdims=True))
    e_ref[...] = jnp.concatenate(e_rows, axis=0)
    pos_ref[...] = jnp.concatenate(p_rows, axis=0).astype(I32)
    w_ref[...] = jnp.concatenate(w_rows, axis=0)


def _router(h2, w_router, router_bias):
    t, d = h2.shape
    tm = _pick(t, (256, 128))
    slot = pl.BlockSpec((TOP_K, tm), lambda i: (0, i))
    rb = jnp.broadcast_to(router_bias.astype(F32)[:, None], (N_EXPERTS, tm))
    return pl.pallas_call(
        _router_body,
        grid=(t // tm,),
        in_specs=[pl.BlockSpec((tm, d), lambda i: (i, 0)), pl.BlockSpec((N_EXPERTS, d), lambda i: (0, 0)),
                  pl.BlockSpec((N_EXPERTS, tm), lambda i: (0, 0))],
        out_specs=[slot, slot, slot, pl.BlockSpec((N_EXPERTS, LANES), lambda i: (0, 0))],
        out_shape=[jax.ShapeDtypeStruct((TOP_K, t), I32), jax.ShapeDtypeStruct((TOP_K, t), I32),
                   jax.ShapeDtypeStruct((TOP_K, t), F32), jax.ShapeDtypeStruct((N_EXPERTS, LANES), F32)],
        scratch_shapes=[pltpu.VMEM((N_EXPERTS, LANES), F32)],
        compiler_params=_cparams(("arbitrary",)),
        name="moe_router",
    )(h2, w_router.T.astype(F32), rb)


def _load_slots(dest_hbm, dest_ref, sem):
    cp = pltpu.make_async_copy(dest_hbm.at[pl.program_id(0)], dest_ref, sem)
    cp.start()
    cp.wait()


def _dispatch_body(cnt_ref, start_ref, dest_hbm, x_ref, xs_ref, dest_ref, zero_ref, sem_ref, *, block_rows):
    i = pl.program_id(0)
    tm = x_ref.shape[0]
    _load_slots(dest_hbm, dest_ref, sem_ref.at[2])

    def row_copy(tok, k):
        return pltpu.make_async_copy(x_ref.at[pl.ds(tok, 1), :], xs_ref.at[pl.ds(dest_ref[k, tok], 1), :], sem_ref.at[0])

    def issue(tok, carry):
        for k in range(TOP_K):
            row_copy(tok, k).start()
        return carry

    lax.fori_loop(0, tm, issue, 0)

    @pl.when(i == 0)
    def _():
        zero_ref[...] = jnp.zeros_like(zero_ref)

        def fill(e, carry):
            cnt = cnt_ref[e]
            padded = (cnt + block_rows - 1) // block_rows * block_rows
            base = start_ref[e] + cnt

            def zcopy(r):
                return pltpu.make_async_copy(zero_ref.at[pl.ds(0, 1), :], xs_ref.at[pl.ds(base + r, 1), :], sem_ref.at[1])

            def zstart(r, c2):
                zcopy(r).start()
                return c2

            def zwait(r, c2):
                zcopy(r).wait()
                return c2

            lax.fori_loop(0, padded - cnt, zstart, 0)
            lax.fori_loop(0, padded - cnt, zwait, 0)
            return carry

        lax.fori_loop(0, N_EXPERTS, fill, 0)

    for _ in range(TOP_K):
        pltpu.make_async_copy(x_ref, xs_ref.at[pl.ds(0, tm), :], sem_ref.at[0]).wait()


def _slot_tiles(dest, tm):
    k, t = dest.shape
    return dest.reshape(k, t // tm, tm).transpose(1, 0, 2)


def _dispatch(hp2, dest, counts, starts, n_rows):
    t, w = hp2.shape
    tm = _pick(t, (256, 128))
    grid_spec = pltpu.PrefetchScalarGridSpec(
        num_scalar_prefetch=2,
        grid=(t // tm,),
        in_specs=[pl.BlockSpec(memory_space=pl.ANY),
                  pl.BlockSpec((tm, w), lambda i, c, s: (i, 0))],
        out_specs=pl.BlockSpec(memory_space=pl.ANY),
        scratch_shapes=[pltpu.SMEM((TOP_K, tm), I32), pltpu.VMEM((8, w), U32), pltpu.SemaphoreType.DMA((3,))],
    )
    return pl.pallas_call(
        functools.partial(_dispatch_body, block_rows=MOE_ROWS),
        grid_spec=grid_spec,
        out_shape=jax.ShapeDtypeStruct((n_rows, w), U32),
        compiler_params=pltpu.CompilerParams(dimension_semantics=("arbitrary",)),
        name="moe_dispatch",
    )(counts, starts, _slot_tiles(dest, tm), hp2)


def _expert_body(be_ref, nu_ref, x_ref, wg_ref, wu_ref, wd_ref, o_ref):
    j = pl.program_id(0)

    @pl.when(j < nu_ref[0])
    def _():
        half = x_ref.shape[1]
        lo, hi = _unpack_halves(x_ref[...])
        lo, hi = lo.astype(BF16), hi.astype(BF16)

        def proj(w_ref):
            return (jnp.dot(lo, w_ref[:half, :], preferred_element_type=F32)
                    + jnp.dot(hi, w_ref[half:, :], preferred_element_type=F32))

        hmid = (_silu(proj(wg_ref)) * proj(wu_ref)).astype(BF16)
        o_ref[...] = _pack_halves(jnp.dot(hmid, wd_ref[...], preferred_element_type=F32))


def _experts(xs, wg, wu, wd, block_e, n_used):
    rows, w = xs.shape
    _, d, ff = wg.shape
    bm = MOE_ROWS
    nb = rows // bm

    def row_idx(j, be, nu):
        return (jnp.minimum(j, nu[0] - 1), 0)

    def w_idx(j, be, nu):
        return (be[jnp.minimum(j, nu[0] - 1)], 0, 0)

    grid_spec = pltpu.PrefetchScalarGridSpec(
        num_scalar_prefetch=2,
        grid=(nb,),
        in_specs=[pl.BlockSpec((bm, w), row_idx),
                  pl.BlockSpec((None, d, ff), w_idx), pl.BlockSpec((None, d, ff), w_idx),
                  pl.BlockSpec((None, ff, d), w_idx)],
        out_specs=pl.BlockSpec((bm, w), row_idx),
    )
    return pl.pallas_call(
        _expert_body,
        grid_spec=grid_spec,
        out_shape=jax.ShapeDtypeStruct((rows, w), U32),
        compiler_params=_cparams(("arbitrary",), 60 * 1024 * 1024),
        name="moe_experts",
    )(block_e, n_used, xs, wg, wu, wd)


def _combine_body(dest_hbm, wt_ref, h_ref, ysh_ref, g_ref, b_ref, y_ref, o_ref, dest_ref, gbuf_ref, sem_ref, *, alpha):
    i, n_steps = pl.program_id(0), pl.num_programs(0)
    tm = h_ref.shape[0]

    def fetch(step, slot):
        cp = pltpu.make_async_copy(dest_hbm.at[step], dest_ref.at[slot], sem_ref.at[2])
        cp.start()
        cp.wait()

        def issue(tok, carry):
            for k in range(TOP_K):
                pltpu.make_async_copy(y_ref.at[pl.ds(dest_ref[slot, k, tok], 1), :],
                                      gbuf_ref.at[slot, k, pl.ds(tok, 1), :], sem_ref.at[slot]).start()
            return carry

        lax.fori_loop(0, tm, issue, 0)

    @pl.when(i == 0)
    def _():
        fetch(0, 0)

    @pl.when(i + 1 < n_steps)
    def _():
        fetch(i + 1, (i + 1) % 2)

    slot = i % 2
    for k in range(TOP_K):
        pltpu.make_async_copy(y_ref.at[pl.ds(0, tm), :], gbuf_ref.at[slot, k], sem_ref.at[slot]).wait()
    lo, hi = _unpack_halves(ysh_ref[...])
    wt = wt_ref[...]
    for k in range(TOP_K):
        glo, ghi = _unpack_halves(gbuf_ref[slot, k])
        wk = wt[:, k:k + 1]
        lo = lo + glo * wk
        hi = hi + ghi * wk
    x = alpha * h_ref[...] + jnp.concatenate([lo, hi], axis=1)
    mu = jnp.mean(x, axis=-1, keepdims=True)
    xc = x - mu
    var = jnp.mean(xc * xc, axis=-1, keepdims=True)
    o_ref[...] = xc * lax.rsqrt(var + LN_EPS) * g_ref[...] + b_ref[...]


def _combine(h2, ysh, y_sorted, dest, wt, g, b, alpha):
    t, d = h2.shape
    w = d // 2
    tm = _pick(t, (128,))
    vec = pl.BlockSpec((1, d), lambda i: (0, 0))
    return pl.pallas_call(
        functools.partial(_combine_body, alpha=alpha),
        grid=(t // tm,),
        in_specs=[pl.BlockSpec(memory_space=pl.ANY),
                  pl.BlockSpec((tm, TOP_K), lambda i: (i, 0)),
                  pl.BlockSpec((tm, d), lambda i: (i, 0)),
                  pl.BlockSpec((tm, w), lambda i: (i, 0)),
                  vec, vec,
                  pl.BlockSpec(memory_space=pl.ANY)],
        out_specs=pl.BlockSpec((tm, d), lambda i: (i, 0)),
        out_shape=jax.ShapeDtypeStruct((t, d), F32),
        scratch_shapes=[pltpu.SMEM((2, TOP_K, tm), I32), pltpu.VMEM((2, TOP_K, tm, w), U32),
                        pltpu.SemaphoreType.DMA((3,))],
        compiler_params=_cparams(("arbitrary",)),
        name="moe_combine",
    )(_slot_tiles(dest, tm), wt, h2, ysh, g.reshape(1, d), b.reshape(1, d), y_sorted)


def _rope_tables(pos):
    inv_freq = ROPE_THETA ** (-jnp.arange(0, MLA_ROPE, 2, dtype=F32) / MLA_ROPE)
    ang = pos.astype(F32)[..., None] * inv_freq
    return jnp.cos(ang), jnp.sin(ang)


def kernel(x, positions, meta_tokens, ln_in_g, ln_in_b, w_in, b_gate, conv_w, conv_b, dt_bias, a_log, d_skip,
           ssm_norm_g, w_ssm_proj, q_a_norm_g, w_q_b, kv_a_norm_g, w_kv_b, w_attn_proj, w_out, ln1_g, ln1_b,
           w_router, router_bias, w_exp_gate, w_exp_up, w_exp_down, w_sh_gate, w_sh_up, w_sh_down, ln2_g, ln2_b):
    bsz, seq, d = x.shape
    depth = w_in.shape[0]
    heads = dt_bias.shape[-1]
    d_inner = w_ssm_proj.shape[1]
    conv_dim = conv_w.shape[-1]
    q_rank = w_q_b.shape[1]
    kv_rank = w_kv_b.shape[1]
    hh = MLA_HEADS
    g = SSM_GROUPS
    hg = heads // g
    assert seq % CHUNK == 0 and d % (2 * LANES) == 0
    lp = seq + CHUNK
    t = bsz * seq
    alpha = (2.0 * depth) ** 0.25

    meta_chunk = jnp.concatenate([jnp.zeros((META_PAD, d), x.dtype), meta_tokens.astype(x.dtype)], axis=0)
    pos =jnp.concatenate([positions.astype(I32) + N_META, jnp.zeros((bsz, META_PAD), I32),
                           jnp.broadcast_to(jnp.arange(N_META, dtype=I32), (bsz, N_META))], axis=1)
    cos, sin = _rope_tables(pos)
    zr = jnp.zeros_like(cos)
    rope_c = jnp.concatenate([cos, cos, zr, zr], axis=-1)[:, :seq]
    rope_slo = jnp.concatenate([-sin, zr, zr, zr], axis=-1)[:, :seq]
    rope_shi = jnp.concatenate([zr, sin, zr, zr], axis=-1)[:, :seq]
    k_c = jnp.concatenate([cos, cos, zr, zr], axis=-1)
    k_s = jnp.concatenate([-sin, sin, zr, zr], axis=-1)

    h_f32, h_b16 = _layer_norm_in(x, meta_chunk, ln_in_g, ln_in_b)
    tm_all = _pick(lp, (1408, 1152, 1024, 896, 768, 640, 512, 384, 256, 128))
    tm_seq = _pick(seq, (1024, 512, 256, 128))

    out = None
    for l in range(depth):
        offs = [0]
        for wdt in (d_inner, conv_dim, heads, q_rank, kv_rank, MLA_ROPE, 2 * d):
            offs.append(offs[-1] + wdt)
        wt = jnp.swapaxes(w_in[l], 0, 1)
        w_kr = wt[offs[5]:offs[6]]
        small_cols = q_rank + kv_rank + 2 * MLA_ROPE + heads
        small_n = small_cols + (-small_cols) % 256
        wt_small = jnp.concatenate([wt[offs[3]:offs[4]], wt[offs[4]:offs[5]], w_kr, w_kr, wt[offs[2]:offs[3]],
                                    jnp.zeros((small_n - small_cols, d), F32)], axis=0)

        n_exp, _, ff = w_exp_gate[l].shape
        z3 = _mm(h_b16, wt, w_rows=(offs[0], d_inner), out_dtype=BF16, tm=tm_all, tn=512, name="in_proj_z")
        xbc3 = _mm(h_b16, wt, w_rows=(offs[1], conv_dim), out_dtype=BF16, tm=tm_all, tn=512, name="in_proj_xbc")
        small = _mm(h_b16, wt_small, w_rows=(0, small_n), out_dtype=F32, tm=tm_all, tn=256, name="in_proj_small")
        gates = _mm(h_b16, wt, w_rows=(offs[6], 2 * d), out_dtype=BF16, tm=tm_seq, tn=512, n_row_tiles=seq // tm_seq,
                    bias=b_gate[l], act="sigmoid", name="in_proj_gates")
        o = 0
        q_a = small[:, :, o:o + q_rank]; o += q_rank
        kv_a = small[:, :, o:o + kv_rank]; o += kv_rank
        kr2 = small[:, :, o:o + 2 * MLA_ROPE]; o += 2 * MLA_ROPE
        dt_raw = small[:, :, o:o + heads]

        dt_g = jnp.pad(dt_raw.reshape(bsz, lp, g, hg).transpose(0, 2, 1, 3), ((0, 0), (0, 0), (0, 0), (0, LANES - hg)))
        pad_h = lambda v: jnp.pad(v.astype(F32).reshape(g, 1, hg), ((0, 0), (0, 0), (0, LANES - hg)))
        y_ssm = _ssd(xbc3, z3, dt_g, conv_w[l].astype(F32), conv_b[l].reshape(1, conv_dim).astype(F32),
                     pad_h(dt_bias[l]), pad_h(a_log[l]),
                     jnp.repeat(d_skip[l].astype(F32), SSM_HEAD_DIM).reshape(1, d_inner),
                     ssm_norm_g[l].reshape(1, d_inner).astype(F32), d_inner, heads)

        qscale = (MLA_QK ** -0.5) * math.log2(math.e)
        head_pad = ((0, 0), (0, 0), (0, MLA_QKP - MLA_QK))
        w_q = jnp.pad(w_q_b[l].reshape(q_rank, hh, MLA_QK), head_pad).reshape(q_rank, hh * MLA_QKP).astype(BF16)
        q3 = _mm(q_a, w_q, out_dtype=BF16, tm=tm_seq, tn=4 * MLA_QKP, n_row_tiles=seq // tm_seq,
                 gain=q_a_norm_g[l] * qscale, rope=(rope_c, rope_slo, rope_shi), name="q_proj")
        wkv = w_kv_b[l].reshape(kv_rank, hh, MLA_NOPE + MLA_V)
        w_k = jnp.pad(wkv[:, :, :MLA_NOPE], ((0, 0), (0, 0), (0, MLA_QKP - MLA_NOPE)))
        w_k = w_k.reshape(kv_rank, hh * MLA_QKP).astype(BF16)
        w_vt = jnp.pad(wkv[:, :, MLA_NOPE:], ((0, 0), (0, 0), (0, MLA_VP - MLA_V)))
        w_vt = w_vt.reshape(kv_rank, hh * MLA_VP).T.astype(BF16)
        ones_rows = jnp.tile(jnp.concatenate([jnp.zeros((MLA_V,), F32), jnp.ones((MLA_VP - MLA_V,), F32)]), hh)
        k3 = _mm(kv_a, w_k, out_dtype=BF16, tm=tm_all, tn=4 * MLA_QKP, gain=kv_a_norm_g[l],
                 kadd=(kr2, k_c, k_s), name="k_proj")
        vt3 = _mm_t(kv_a, w_vt, kv_a_norm_g[l], ones_rows, out_dtype=BF16, tm=tm_all, tn=4 * MLA_VP,
                    name="v_proj")
        y_attn, wg_b, wu_b, wd_b = _attention(
            q3, k3, vt3, seq, [w_exp_gate[l].reshape(n_exp * d, ff), w_exp_up[l].reshape(n_exp * d, ff),
                               w_exp_down[l].reshape(n_exp * ff, d)])

        mixed = _mm_ktiled([y_ssm, y_attn], [w_ssm_proj[l].astype(BF16), w_attn_proj[l].astype(BF16)],
                           out_dtype=BF16, rows=seq, tm=tm_seq, tn=_pick(d, (1024, 512)), nk=4,
                           gates=gates, name="branch_proj")
        pre1 = _mm_ktiled([mixed], [w_out[l].astype(BF16)], out_dtype=F32, rows=seq, tm=tm_seq,
                          tn=_pick(d, (1024, 512)), nk=2, res=h_f32, res_scale=alpha, name="out_proj")
        h1, h1p = _layer_norm_pack(pre1.reshape(t, d), ln1_g[l], ln1_b[l])

        e_slot, pos_slot, w_slot, cnt = _router(h1, w_router[l], router_bias[l])
        counts = cnt[:, 0].astype(I32)
        padded = (counts + MOE_ROWS - 1) // MOE_ROWS * MOE_ROWS
        ends = jnp.cumsum(padded)
        starts = ends - padded
        onehot = e_slot[None] == jnp.arange(N_EXPERTS, dtype=I32)[:, None, None]
        dest = pos_slot + jnp.sum(jnp.where(onehot, starts[:, None, None], 0), axis=0)
        n_blocks = -(-(t * TOP_K) // MOE_ROWS) + N_EXPERTS
        blk0 = jnp.arange(n_blocks, dtype=I32) * MOE_ROWS
        block_e = jnp.minimum(jnp.sum((ends[None, :] <= blk0[:, None]).astype(I32), axis=1), N_EXPERTS - 1)
        n_used = (ends[-1] // MOE_ROWS).reshape(1)
        xs = _dispatch(h1p, dest, counts, starts, n_blocks * MOE_ROWS)
        y_sorted = _experts(xs, wg_b.reshape(n_exp, d, ff), wu_b.reshape(n_exp, d, ff), wd_b.reshape(n_exp, ff, d),
                            block_e, n_used)
        y_shared = _experts(h1p, w_sh_gate[l][None].astype(BF16), w_sh_up[l][None].astype(BF16),
                            w_sh_down[l][None].astype(BF16), jnp.zeros((t // MOE_ROWS,), I32),
                            jnp.full((1,), t // MOE_ROWS, I32))
        out = _combine(h1, y_shared, y_sorted, dest, w_slot.T, ln2_g[l], ln2_b[l], alpha)
        if l + 1 < depth:
            raise NotImplementedError("stacked layers need the meta rows carried through the channel mixer")
    return out.reshape(bsz, seq, d)
```

```python
import functools
import math

import jax
import jax.numpy as jnp
from jax import lax
from jax.experimental import pallas as pl
from jax.experimental.pallas import tpu as pltpu

F32 = jnp.float32
BF16 = jnp.bfloat16
U32 = jnp.uint32
I32 = jnp.int32

N_META = 16
CHUNK = 128
META_PAD = CHUNK - N_META
SSM_HEAD_DIM = 64
SSM_GROUPS = 8
SSM_STATE = 128
SSM_CONV = 4
MLA_HEADS = 64
MLA_NOPE = 128
MLA_ROPE = 64
MLA_V = 128
MLA_QK = MLA_NOPE + MLA_ROPE
ROPE_THETA = 10000.0
N_EXPERTS = 64
N_EXPERT_GROUPS = 8
TOPK_GROUPS = 4
TOP_K = 8
ROUTED_SCALE = 2.5
LN_EPS = 1e-5
RMS_EPS = 1e-6
NEG = -1e30
LANES = 128
MLA_QKP = 2 * LANES
MOE_ROWS = 256
MLA_VP = MLA_V + 16
MIB = 1024 * 1024
VMEM_EXPERTS = 60 * MIB
VMEM_BIG = 56 * MIB
VMEM_MID = 44 * MIB


def _cparams(sem, vmem=VMEM_MID):
    return pltpu.CompilerParams(dimension_semantics=sem, vmem_limit_bytes=vmem)


def _pick(n, cands):
    for c in cands:
        if n % c == 0:
            return c
    raise ValueError(f"no tile for {n} in {cands}")


def _ln_body(x_ref, g_ref, b_ref, of_ref, ob_ref):
    x = x_ref[...]
    mu = jnp.mean(x, axis=-1, keepdims=True)
    xc = x - mu
    var = jnp.mean(xc * xc, axis=-1, keepdims=True)
    y = xc * lax.rsqrt(var + LN_EPS) * g_ref[...] + b_ref[...]
    of_ref[...] = y
    ob_ref[...] = y.astype(BF16)


def _ln_in_body(x_ref, meta_ref, g_ref, b_ref, of_ref, ob_ref, *, n_real):
    i = pl.program_id(1)

    @pl.when(i < n_real)
    def _():
        _ln_body(x_ref, g_ref, b_ref, of_ref, ob_ref)

    @pl.when(i == n_real)
    def _():
        _ln_body(meta_ref, g_ref, b_ref, of_ref, ob_ref)


def _layer_norm_in(x3, meta_chunk, g, b):
    bsz, seq, d = x3.shape
    tm = CHUNK
    n_real = seq // tm
    spec = pl.BlockSpec((None, tm, d), lambda bi, i: (bi, i, 0))
    vec = pl.BlockSpec((1, d), lambda bi, i: (0, 0))
    shape = (bsz, seq + tm, d)
    return pl.pallas_call(
        functools.partial(_ln_in_body, n_real=n_real),
        grid=(bsz, n_real + 1),
        in_specs=[pl.BlockSpec((None, tm, d), lambda bi, i: (bi, jnp.minimum(i, n_real - 1), 0)),
                  pl.BlockSpec((tm, d), lambda bi, i: (0, 0)), vec, vec],
        out_specs=[spec, spec],
        out_shape=[jax.ShapeDtypeStruct(shape, F32), jax.ShapeDtypeStruct(shape, BF16)],
        compiler_params=_cparams(("parallel", "arbitrary")),
        name="layer_norm",
    )(x3, meta_chunk, g.reshape(1, d), b.reshape(1, d))


def _pack_halves(y):
    n = y.shape[1] // 2
    lo = pltpu.bitcast(y[:, :n].astype(BF16).astype(F32), U32) >> 16
    hi = pltpu.bitcast(y[:, n:].astype(BF16).astype(F32), U32) & jnp.uint32(0xFFFF0000)
    return hi | lo


def _unpack_halves(w):
    lo = pltpu.bitcast(w << 16, F32)
    hi = pltpu.bitcast(w & jnp.uint32(0xFFFF0000), F32)
    return lo, hi


def _ln_pack_body(x_ref, g_ref, b_ref, of_ref, op_ref):
    x = x_ref[...]
    mu = jnp.mean(x, axis=-1, keepdims=True)
    xc = x - mu
    var = jnp.mean(xc * xc, axis=-1, keepdims=True)
    y = xc * lax.rsqrt(var + LN_EPS) * g_ref[...] + b_ref[...]
    of_ref[...] = y
    op_ref[...] = _pack_halves(y)


def _layer_norm_pack(x2, g, b):
    rows, d = x2.shape
    tm = _pick(rows, (256, 128))
    vec = pl.BlockSpec((1, d), lambda i: (0, 0))
    return pl.pallas_call(
        _ln_pack_body,
        grid=(rows // tm,),
        in_specs=[pl.BlockSpec((tm, d), lambda i: (i, 0)), vec, vec],
        out_specs=[pl.BlockSpec((tm, d), lambda i: (i, 0)), pl.BlockSpec((tm, d // 2), lambda i: (i, 0))],
        out_shape=[jax.ShapeDtypeStruct((rows, d), F32), jax.ShapeDtypeStruct((rows, d // 2), U32)],
        compiler_params=_cparams(("parallel",)),
        name="layer_norm_pack",
    )(x2, g.reshape(1, d), b.reshape(1, d))


def _rms_rows(a_ref, gain_ref):
    af = a_ref[...].astype(F32)
    return (af * lax.rsqrt(jnp.mean(af * af, axis=-1, keepdims=True) + RMS_EPS) * gain_ref[...]).astype(BF16)


def _mm_body(*refs, has_gain, has_bias, act, rope, kadd, has_side, w_rows):
    it = iter(refs)
    a_ref, w_ref = next(it), next(it)
    gain_ref = next(it) if has_gain else None
    bias_ref = next(it) if has_bias else None
    rope_refs = [next(it) for _ in range(3)] if rope else None
    kadd_refs = [next(it) for _ in range(3)] if kadd else None
    side_in = next(it) if has_side else None
    o_ref = next(it)
    if has_side:
        next(it)[...] = side_in[...].astype(BF16)
    if has_gain:
        an_ref = next(it)

        @pl.when(pl.program_id(2) == 0)
        def _():
            an_ref[...] = _rms_rows(a_ref, gain_ref)

        a = an_ref[...]
    else:
        a = a_ref[...]
    if w_rows:
        r = lax.dot_general(a.astype(BF16), w_ref[...].astype(BF16), (((1,), (1,)), ((), ())),
                            preferred_element_type=F32)
    else:
        r = jnp.dot(a.astype(BF16), w_ref[...].astype(BF16), preferred_element_type=F32)
    tn = r.shape[1]
    if has_bias:
        r = r + bias_ref[...]
    if act == "sigmoid":
        r = jax.nn.sigmoid(r)
    if rope:
        c, slo, shi = (t[...] for t in rope_refs)
        half = MLA_ROPE // 2
        pieces = []
        for h0 in range(0, tn, MLA_QKP):
            rp = r[:, h0 + MLA_NOPE:h0 + MLA_QKP]
            pieces += [r[:, h0:h0 + MLA_NOPE], rp * c + pltpu.roll(rp, LANES - half, 1) * slo + pltpu.roll(rp, half, 1) * shi]
        r = jnp.concatenate(pieces, axis=1)
    if kadd:
        kr_ref, kc, ks = kadd_refs
        kr = kr_ref[...]
        piece = jnp.concatenate([jnp.zeros_like(kr), kr * kc[...] + pltpu.roll(kr, MLA_ROPE // 2, 1) * ks[...]], axis=1)
        r = r + jnp.concatenate([piece] * (tn // piece.shape[1]), axis=1)
    o_ref[...] = r.astype(o_ref.dtype)


def _side_specs(side, n_steps, lin):
    rows, cols = side.shape
    nb = max(c for c in range(1, n_steps + 1) if rows % c == 0 and (rows // c) % 16 == 0)
    spec = pl.BlockSpec((rows // nb, cols), lambda *ids: (jnp.minimum(lin(*ids), nb - 1), 0))
    return spec, jax.ShapeDtypeStruct(side.shape, BF16)


def _column_window(a3, first, width):
    if first % width == 0 and width % LANES == 0:
        return a3, first // width
    return a3[:, :, first:first + width], 0


def _mm(a3, w, *, out_dtype, tm, tn, n_row_tiles=None, w_rows=None, a_cols=None, gain=None, bias=None, act=None,
        rope=None, kadd=None, side=None, name="mm"):
    a3, a_blk = (a3, 0) if a_cols is None else _column_window(a3, *a_cols)
    bsz, rows = a3.shape[:2]
    k = a3.shape[2] if a_cols is None else a_cols[1]
    row0, n = (0, w.shape[1]) if w_rows is None else w_rows
    ni = rows // tm if n_row_tiles is None else n_row_tiles
    nj = n // tn
    assert n % tn == 0 and row0 % 8 == 0 and (n_row_tiles is not None or rows % tm == 0)
    if w_rows is None:
        w_spec = pl.BlockSpec((k, tn), lambda bi, i, j: (0, j))
    elif row0 % tn == 0:
        w_spec = pl.BlockSpec((tn, k), lambda bi, i, j: (row0 // tn + j, 0))
    else:
        w_spec = pl.BlockSpec((pl.Element(tn), pl.Element(k)),
                              lambda bi, i, j: (pl.multiple_of(row0 + j * tn, 8), 0))
    in_specs = [pl.BlockSpec((None, tm, k), lambda bi, i, j: (bi, i, a_blk)), w_spec]
    args = [a3, w]
    if gain is not None:
        in_specs.append(pl.BlockSpec((1, k), lambda bi, i, j: (0, 0)))
        args.append(gain.reshape(1, k).astype(F32))
    if bias is not None:
        in_specs.append(pl.BlockSpec((1, tn), lambda bi, i, j: (0, j)))
        args.append(bias.reshape(1, n).astype(F32))
    for t in (rope or ()) + (kadd or ()):
        in_specs.append(pl.BlockSpec((None, tm, t.shape[2]), lambda bi, i, j: (bi, i, 0)))
        args.append(t)
    out_specs = [pl.BlockSpec((None, tm, tn), lambda bi, i, j: (bi, i, j))]
    out_shape = [jax.ShapeDtypeStruct((bsz, ni * tm, n), out_dtype)]
    if side is not None:
        spec, shape = _side_specs(side, bsz * ni * nj, lambda bi, i, j: (bi * ni + i) * nj + j)
        in_specs.append(spec)
        args.append(side)
        out_specs.append(spec)
        out_shape.append(shape)
    body = functools.partial(_mm_body, has_gain=gain is not None, has_bias=bias is not None, act=act,
                             rope=rope is not None, kadd=kadd is not None, has_side=side is not None,
                             w_rows=w_rows is not None)
    outs = pl.pallas_call(
        body,
        grid=(bsz, ni, nj),
        in_specs=in_specs,
        out_specs=out_specs,
        out_shape=out_shape,
        scratch_shapes=[pltpu.VMEM((tm, k), BF16)] if gain is not None else [],
        compiler_params=_cparams(("parallel", "parallel", "arbitrary") if side is None else ("arbitrary",) * 3,
                                 VMEM_BIG),
        name=name,
    )(*args)
    return outs[0] if side is None else outs


def _mm_t_body(a_ref, wt_ref, gain_ref, bias_ref, o_ref, an_ref):
    @pl.when(pl.program_id(2) == 0)
    def _():
        an_ref[...] = _rms_rows(a_ref, gain_ref)

    r = lax.dot_general(wt_ref[...], an_ref[...], (((1,), (1,)), ((), ())), preferred_element_type=F32)
    o_ref[...] = (r + bias_ref[...]).astype(o_ref.dtype)


def _mm_t(a3, wt, gain, bias_col, *, a_cols, out_dtype, tm, tn, name):
    a3, a_blk = _column_window(a3, *a_cols)
    bsz, rows = a3.shape[:2]
    k = a_cols[1]
    n = wt.shape[0]
    return pl.pallas_call(
        _mm_t_body,
        grid=(bsz, rows // tm, n // tn),
        in_specs=[pl.BlockSpec((None, tm, k), lambda bi, i, j: (bi, i, a_blk)),
                  pl.BlockSpec((tn, k), lambda bi, i, j: (j, 0)),
                  pl.BlockSpec((1, k), lambda bi, i, j: (0, 0)),
                  pl.BlockSpec((tn, 1), lambda bi, i, j: (j, 0))],
        out_specs=pl.BlockSpec((None, tn, tm), lambda bi, i, j: (bi, j, i)),
        out_shape=jax.ShapeDtypeStruct((bsz, n, rows), out_dtype),
        scratch_shapes=[pltpu.VMEM((tm, k), BF16)],
        compiler_params=_cparams(("parallel", "parallel", "arbitrary"), VMEM_BIG),
        name=name,
    )(a3, wt, gain.reshape(1, k).astype(F32), bias_col.reshape(n, 1).astype(F32))


def _mmk_body(*refs, n_pairs, has_gate, has_res, res_scale, has_side):
    it = iter(refs)
    a_refs = [next(it) for _ in range(n_pairs)]
    w_refs = [next(it) for _ in range(n_pairs)]
    g_refs = [next(it) for _ in range(n_pairs)] if has_gate else None
    res_ref = next(it) if has_res else None
    side_in = next(it) if has_side else None
    o_ref = next(it)
    if has_side:
        next(it)[...] = side_in[...].astype(BF16)
    acc_refs = [next(it) for _ in range(n_pairs)]
    kk = pl.program_id(3)

    @pl.when(kk == 0)
    def _():
        for acc in acc_refs:
            acc[...] = jnp.zeros_like(acc)

    for a_ref, w_ref, acc in zip(a_refs, w_refs, acc_refs):
        acc[...] += jnp.dot(a_ref[...], w_ref[...], preferred_element_type=F32)

    @pl.when(kk == pl.num_programs(3) - 1)
    def _():
        r = None
        for p, acc in enumerate(acc_refs):
            t = acc[...]
            if has_gate:
                t = t * g_refs[p][...].astype(F32)
            r = t if r is None else r + t
        if has_res:
            r = r + res_scale * res_ref[...]
        o_ref[...] = r.astype(o_ref.dtype)


def _mm_ktiled(a_list, w_list, *, out_dtype, rows, tm, tn, nk, gates=None, res=None, res_scale=1.0, side=None,
               name="mmk"):
    bsz = a_list[0].shape[0]
    n = w_list[0].shape[1]
    npairs = len(a_list)
    tks = [a.shape[2] // nk for a in a_list]
    assert all(a.shape[2] == tk * nk and tk % LANES == 0 for a, tk in zip(a_list, tks))
    o_spec = pl.BlockSpec((None, tm, tn), lambda bi, i, j, kk: (bi, i, j))
    in_specs = ([pl.BlockSpec((None, tm, tk), lambda bi, i, j, kk: (bi, i, kk)) for tk in tks]
                + [pl.BlockSpec((tk, tn), lambda bi, i, j, kk: (kk, j)) for tk in tks])
    args = list(a_list) + list(w_list)
    if gates is not None:
        for p in range(npairs):
            in_specs.append(pl.BlockSpec((None, tm, tn), lambda bi, i, j, kk, p=p: (bi, i, p * (n // tn) + j)))
            args.append(gates)
    if res is not None:
        in_specs.append(o_spec)
        args.append(res)
    ni, nj = rows // tm, n // tn
    out_specs = [o_spec]
    out_shape = [jax.ShapeDtypeStruct((bsz, rows, n), out_dtype)]
    if side is not None:
        spec, shape = _side_specs(side, bsz * ni * nj * nk, lambda bi, i, j, kk: ((bi * ni + i) * nj + j) * nk + kk)
        in_specs.append(spec)
        args.append(side)
        out_specs.append(spec)
        out_shape.append(shape)
    body = functools.partial(_mmk_body, n_pairs=npairs, has_gate=gates is not None, has_res=res is not None,
                             res_scale=res_scale, has_side=side is not None)
    outs = pl.pallas_call(
        body,
        grid=(bsz, ni, nj, nk),
        in_specs=in_specs,
        out_specs=out_specs,
        out_shape=out_shape,
        scratch_shapes=[pltpu.VMEM((tm, tn), F32) for _ in range(npairs)],
        compiler_params=_cparams(("parallel", "parallel", "parallel", "arbitrary") if side is None
                                 else ("arbitrary",) * 4, VMEM_BIG),
        name=name,
    )(*args)
    return outs[0] if side is None else outs


def _silu(x):
    return x * jax.nn.sigmoid(x)


def _ssd_body(xs_ref, b_ref, c_ref, z_ref, dt_ref, wx_ref, wb_ref, wc_ref, bx_ref, bb_ref, bc_ref,
              dtb_ref, alog_ref, dskip_ref, ng_ref, e_ref, o_ref,
              state_ref, extx_ref, extb_ref, extc_ref, y_ref, *, hg, gp):
    c = pl.program_id(2)
    q = CHUNK
    p = SSM_HEAD_DIM
    n = SSM_STATE
    gw = hg * p
    first = c == 0
    row = lax.broadcasted_iota(I32, (q, 1), 0)
    live = jnp.logical_or(jnp.logical_not(first), row >= META_PAD)

    @pl.when(first)
    def _():
        state_ref[...] = jnp.zeros_like(state_ref)
        extx_ref[0:8, :] = jnp.zeros((8, extx_ref.shape[1]), F32)
        extb_ref[0:8, :] = jnp.zeros((8, extb_ref.shape[1]), F32)
        extc_ref[0:8, :] = jnp.zeros((8, extc_ref.shape[1]), F32)

    def conv(ext_ref, u_ref, w_ref, bias_ref):
        @pl.when(first)
        def _():
            ext_ref[8:8 + q, :] = jnp.where(live, u_ref[...].astype(F32), 0.0)

        @pl.when(jnp.logical_not(first))
        def _():
            ext_ref[8:8 + q, :] = u_ref[...].astype(F32)

        acc = bias_ref[...]
        for k in range(SSM_CONV):
            acc = acc + ext_ref[pl.ds(8 - (SSM_CONV - 1) + k, q), :] * w_ref[k:k + 1, :]
        ext_ref[0:8, :] = ext_ref[q:q + 8, :]
        return _silu(acc)

    xs_w = conv(extx_ref, xs_ref, wx_ref, bx_ref)
    bm_w = conv(extb_ref, b_ref, wb_ref, bb_ref)
    cm_w = conv(extc_ref, c_ref, wc_ref, bc_ref)

    r_i = lax.broadcasted_iota(I32, (q, q), 0)
    c_i = lax.broadcasted_iota(I32, (q, q), 1)
    causal = r_i >= c_i
    tri = causal.astype(F32)
    e = e_ref[...]

    def spread(v):
        v_hi = v.astype(BF16)
        v_lo = (v - v_hi.astype(F32)).astype(BF16)
        return jnp.dot(v_hi, e, preferred_element_type=F32) + jnp.dot(v_lo, e, preferred_element_type=F32)

    groups = [dict() for _ in range(gp)]
    for gg, s in enumerate(groups):
        x = dt_ref[gg] + dtb_ref[gg]
        dt = jnp.maximum(x, 0.0) + jnp.log1p(jnp.exp(-jnp.abs(x)))
        dt = jnp.where(live, dt, 0.0)
        a = -jnp.exp(alog_ref[gg])
        a_cs = jnp.dot(tri, dt * a, precision=lax.Precision.HIGHEST, preferred_element_type=F32)
        s.update(dt=dt, a_cs=a_cs, a_cs_t=a_cs.T, ea=jnp.exp(a_cs), de=jnp.exp(a_cs[q - 1:q, :] - a_cs))

    for gg, s in enumerate(groups):
        xs = xs_w[:, gg * gw:(gg + 1) * gw]
        bm = bm_w[:, gg * n:(gg + 1) * n]
        cm_b = cm_w[:, gg * n:(gg + 1) * n].astype(BF16)
        ea_x = spread(s["ea"])
        xdt = xs * spread(s["dt"])
        prev = state_ref[gg]
        s.update(xs=xs, ea_x=ea_x, prev=prev, xdt_b=xdt.astype(BF16),
                 cb=lax.dot_general(cm_b, bm.astype(BF16), (((1,), (1,)), ((), ())), preferred_element_type=F32),
                 y_off=jnp.dot(cm_b, prev.astype(BF16), preferred_element_type=F32) * ea_x,
                 upd=jnp.dot(bm.T.astype(BF16), (xdt * spread(s["de"])).astype(BF16), preferred_element_type=F32))

    def decay_scores(s, j):
        seg = s["a_cs"][:, j:j + 1] - s["a_cs_t"][j:j + 1, :]
        return (jnp.exp(jnp.where(causal, seg, NEG)) * s["cb"]).astype(BF16)

    nxt = [decay_scores(s, 0) for s in groups]
    for j in range(hg):
        for gg, s in enumerate(groups):
            m = nxt[gg]
            if j + 1 < hg:
                nxt[gg] = decay_scores(s, j + 1)
            lo = gg * gw + j * p
            y_ref[:, lo:lo + p] = jnp.dot(m, s["xdt_b"][:, j * p:(j + 1) * p], preferred_element_type=F32)

    for gg, s in enumerate(groups):
        sl = slice(gg * gw, (gg + 1) * gw)
        y = y_ref[:, sl] + s["y_off"] + s["xs"] * dskip_ref[:, sl]
        state_ref[gg] = s["prev"] * s["ea_x"][q - 1:q, :] + s["upd"]
        yz = y * _silu(z_ref[:, sl].astype(F32))
        yn = yz * lax.rsqrt(jnp.mean(yz * yz, axis=-1, keepdims=True) + RMS_EPS) * ng_ref[:, sl]
        o_ref[:, sl] = yn.astype(o_ref.dtype)


def _ssd(xbc3, z3, dt_g, conv_w, conv_b, dtb_g, alog_g, dskip_x, norm_g, d_inner, heads):
    bsz, rows, conv_dim = xbc3.shape
    g, n, p, q = SSM_GROUPS, SSM_STATE, SSM_HEAD_DIM, CHUNK
    gp = 2
    hg = heads // g
    gw = hg * p
    nc = rows // q
    assert gw % LANES == 0 and d_inner % (gp * n) == 0 and hg <= LANES and g % gp == 0
    boff = d_inner // (gp * n)
    coff = boff + g // gp
    expand = (jnp.arange(LANES)[:, None] == (jnp.arange(gw) // p)[None, :]).astype(BF16)

    def chunk(c):
        return (c + nc - 1) % nc

    in_specs = [
        pl.BlockSpec((None, q, gp * gw), lambda b, gi, c: (b, chunk(c), gi)),
        pl.BlockSpec((None, q, gp * n), lambda b, gi, c: (b, chunk(c), boff + gi)),
        pl.BlockSpec((None, q, gp * n), lambda b, gi, c: (b, chunk(c), coff + gi)),
        pl.BlockSpec((None, q, gp * gw), lambda b, gi, c: (b, chunk(c), gi)),
        pl.BlockSpec((None, gp, q, LANES), lambda b, gi, c: (b, gi, chunk(c), 0)),
        pl.BlockSpec((SSM_CONV, gp * gw), lambda b, gi, c: (0, gi)),
        pl.BlockSpec((SSM_CONV, gp * n), lambda b, gi, c: (0, boff + gi)),
        pl.BlockSpec((SSM_CONV, gp * n), lambda b, gi, c: (0, coff + gi)),
        pl.BlockSpec((1, gp * gw), lambda b, gi, c: (0, gi)),
        pl.BlockSpec((1, gp * n), lambda b, gi, c: (0, boff + gi)),
        pl.BlockSpec((1, gp * n), lambda b, gi, c: (0, coff + gi)),
        pl.BlockSpec((gp, 1, LANES), lambda b, gi, c: (gi, 0, 0)),
        pl.BlockSpec((gp, 1, LANES), lambda b, gi, c: (gi, 0, 0)),
        pl.BlockSpec((1, gp * gw), lambda b, gi, c: (0, gi)),
        pl.BlockSpec((1, gp * gw), lambda b, gi, c: (0, gi)),
        pl.BlockSpec((LANES, gw), lambda b, gi, c: (0, 0)),
    ]
    return pl.pallas_call(
        functools.partial(_ssd_body, hg=hg, gp=gp),
        grid=(bsz, g // gp, nc),
        in_specs=in_specs,
        out_specs=pl.BlockSpec((None, q, gp * gw), lambda b, gi, c: (b, chunk(c), gi)),
        out_shape=jax.ShapeDtypeStruct((bsz, rows, d_inner), BF16),
        scratch_shapes=[pltpu.VMEM((gp, n, gw), F32), pltpu.VMEM((q + 8, gp * gw), F32),
                        pltpu.VMEM((q + 8, gp * n), F32), pltpu.VMEM((q + 8, gp * n), F32),
                        pltpu.VMEM((q, gp * gw), F32)],
        compiler_params=_cparams(("parallel", "parallel", "arbitrary")),
        name="ssd",
    )(xbc3, xbc3, xbc3, z3, dt_g, conv_w, conv_w, conv_w, conv_b, conv_b, conv_b, dtb_g, alog_g,
      dskip_x, norm_g, expand)


def _attn_body(qi_ref, ki_ref, q_ref, km_ref, vm_ref, k_ref, v_ref, *rest, heads, n_side):
    side_in, o_ref, side_out = rest[:n_side], rest[n_side], rest[n_side + 1:2 * n_side + 1]
    m_ref, acc_ref = rest[2 * n_side + 1:]
    for src, dst in zip(side_in, side_out):
        dst[...] = src[...].astype(BF16)
    step_id = pl.program_id(2)
    qi, ki = qi_ref[step_id], ki_ref[step_id]
    tq = q_ref.shape[0]

    def step(segments):
        def scores(h):
            out = []
            for kr, _, key0, nk, q0, mask in segments:
                qh = q_ref[q0:, h * MLA_QKP:(h + 1) * MLA_QKP]
                kh = kr[key0:key0 + nk, h * MLA_QKP:(h + 1) * MLA_QKP]
                st = lax.dot_general(kh, qh, (((1,), (1,)), ((), ())), preferred_element_type=F32)
                key = key0 + lax.broadcasted_iota(I32, st.shape, 0)
                if mask == "meta":
                    st = jnp.where(key >= META_PAD, st, NEG)
                elif mask == "causal":
                    st = jnp.where(key <= q0 + lax.broadcasted_iota(I32, st.shape, 1), st, NEG)
                out.append(st)
            return out

        def probs(h, sts):
            m_prev = m_ref[h]
            m_new = m_prev
            for (_, _, _, _, q0, _), st in zip(segments, sts):
                cm = jnp.max(st, axis=0, keepdims=True)
                if q0:
                    cm = jnp.concatenate([jnp.full((1, q0), NEG, F32), cm], axis=1)
                m_new = jnp.maximum(m_new, cm)
            m_ref[h] = m_new
            pts = [jnp.exp2((st - m_new[:, q0:]).astype(BF16)) for (_, _, _, _, q0, _), st in zip(segments, sts)]
            return jnp.exp2(m_prev - m_new), pts

        def accumulate(h, alpha, pts):
            sl = slice(h * MLA_VP, (h + 1) * MLA_VP)
            acc = acc_ref[sl, :] * alpha
            partial = []
            for (_, vr, key0, nk, q0, _), pt in zip(segments, pts):
                contrib = jnp.dot(vr[sl, key0:key0 + nk], pt, preferred_element_type=F32)
                if q0:
                    partial.append((q0, contrib))
                else:
                    acc = acc + contrib
            acc_ref[sl, :] = acc
            for q0, contrib in partial:
                acc_ref[sl, q0:] += contrib

        st_q, pr_q = {}, {}
        for stage in range(heads + 2):
            if stage < heads:
                st_q[stage] = scores(stage)
            if 0 <= stage - 1 < heads:
                pr_q[stage - 1] = probs(stage - 1, st_q.pop(stage - 1))
            if 0 <= stage - 2 < heads:
                accumulate(stage - 2, *pr_q.pop(stage - 2))

    def start():
        m_ref[...] = jnp.full_like(m_ref, NEG)
        acc_ref[...] = jnp.zeros_like(acc_ref)

    def finish():
        for h in range(heads):
            num = acc_ref[h * MLA_VP:h * MLA_VP + MLA_V, :]
            den = acc_ref[h * MLA_VP + MLA_V:h * MLA_VP + MLA_V + 1, :]
            o_ref[:, h * MLA_V:(h + 1) * MLA_V] = (num / den).T.astype(o_ref.dtype)

    tk = k_ref.shape[0]
    meta = (km_ref, vm_ref, 0, km_ref.shape[0], 0, "meta")
    full = (k_ref, v_ref, 0, tk, 0, None)
    half = tk // 2
    if half % (2 * LANES) == 0:
        diag = [(k_ref, v_ref, 0, half, 0, "causal"), (k_ref, v_ref, half, half, half, "causal")]
    else:
        diag = [(k_ref, v_ref, 0, tk, 0, "causal")]

    @pl.when(jnp.logical_and(ki == 0, qi == 0))
    def _():
        start()
        step([meta] + diag)
        finish()

    @pl.when(jnp.logical_and(ki == 0, qi > 0))
    def _():
        start()
        step([meta, full])

    @pl.when(jnp.logical_and(ki > 0, ki < qi))
    def _():
        step([full])

    @pl.when(jnp.logical_and(ki > 0, ki == qi))
    def _():
        step(diag)
        finish()


def _attention(q3, k3, vt3, seq, sides=()):
    bsz = q3.shape[0]
    hp = 8
    t = _pick(seq, (512, 256, 128))
    nq = seq // t
    meta_blk = seq // CHUNK
    qw, vw = hp * MLA_QKP, hp * MLA_VP
    qi_tab = jnp.asarray([qi for qi in range(nq) for _ in range(qi + 1)], I32)
    ki_tab = jnp.asarray([ki for qi in range(nq) for ki in range(qi + 1)], I32)

    def q_idx(b, h, s, qt, kt):
        return (b, qt[s], h)

    def k_idx(b, h, s, qt, kt):
        return (b, kt[s], h)

    def vt_idx(b, h, s, qt, kt):
        return (b, h, kt[s])

    nh, npairs = MLA_HEADS // hp, int(qi_tab.shape[0])
    side_specs = [_side_specs(s_arr, bsz * nh * npairs, lambda b, h, s, qt, kt: (b * nh + h) * npairs + s)
                  for s_arr in sides]
    grid_spec = pltpu.PrefetchScalarGridSpec(
        num_scalar_prefetch=2,
        grid=(bsz, nh, npairs),
        in_specs=[
            pl.BlockSpec((None, t, qw), q_idx),
            pl.BlockSpec((None, CHUNK, qw), lambda b, h, s, qt, kt: (b, meta_blk, h)),
            pl.BlockSpec((None, vw, CHUNK), lambda b, h, s, qt, kt: (b, h, meta_blk)),
            pl.BlockSpec((None, t, qw), k_idx),
            pl.BlockSpec((None, vw, t), vt_idx),
        ] + [spec for spec, _ in side_specs],
        out_specs=[pl.BlockSpec((None, t, hp * MLA_V), q_idx)] + [spec for spec, _ in side_specs],
        scratch_shapes=[pltpu.VMEM((hp, 1, t), F32), pltpu.VMEM((vw, t), F32)],
    )
    return pl.pallas_call(
        functools.partial(_attn_body, heads=hp, n_side=len(sides)),
        grid_spec=grid_spec,
        out_shape=[jax.ShapeDtypeStruct((bsz, seq, MLA_HEADS * MLA_V), BF16)] + [shape for _, shape in side_specs],
        compiler_params=_cparams(("arbitrary", "arbitrary", "arbitrary")),
        name="mla_attention",
    )(qi_tab, ki_tab, q3, k3, vt3, k3, vt3, *sides)


def _router_body(h_ref, wr_ref, rb_ref, e_ref, pos_ref, w_ref, cnt_ref, carry_ref):
    i = pl.program_id(0)
    ne, ng = N_EXPERTS, N_EXPERT_GROUPS
    gs = ne // ng
    tm = h_ref.shape[0]

    @pl.when(i == 0)
    def _():
        carry_ref[...] = jnp.zeros_like(carry_ref)

    logits = lax.dot_general(wr_ref[...], h_ref[...], (((1,), (1,)), ((), ())), precision=lax.Precision.HIGHEST,
                             preferred_element_type=F32)
    scores = jax.nn.sigmoid(logits)
    choice = scores + rb_ref[...]
    sub = lax.broadcasted_iota(I32, (gs, tm), 0)
    grp_rows = []
    for g in range(ng):
        blk = choice[g * gs:(g + 1) * gs, :]
        m1 = jnp.max(blk, axis=0, keepdims=True)
        first = jnp.min(jnp.where(blk == m1, sub, gs), axis=0, keepdims=True)
        m2 = jnp.max(jnp.where(sub == first, -jnp.inf, blk), axis=0, keepdims=True)
        grp_rows.append(m1 + m2)
    grp = jnp.concatenate(grp_rows, axis=0)
    grank = jnp.zeros((ng, tm), I32)
    gidx = lax.broadcasted_iota(I32, (ng, tm), 0)
    for g in range(ng):
        rowv = grp[g:g + 1, :]
        beats = jnp.logical_or(rowv > grp, jnp.logical_and(rowv == grp, gidx > g))
        grank = grank + beats.astype(I32)
    gsel = (grank < TOPK_GROUPS).astype(F32)
    esel = jnp.concatenate([jnp.broadcast_to(gsel[g:g + 1, :], (gs, tm)) for g in range(ng)], axis=0)
    masked = jnp.where(esel > 0.0, choice, -jnp.inf)
    eidx = lax.broadcasted_iota(I32, (ne, tm), 0)
    rank = jnp.zeros((ne, tm), I32)
    for e in range(ne):
        rowv = masked[e:e + 1, :]
        beats = jnp.logical_or(rowv > masked, jnp.logical_and(rowv == masked, eidx > e))
        rank = rank + beats.astype(I32)
    top = jnp.logical_and(rank < TOP_K, esel > 0.0)
    topf = top.astype(F32)
    wsel = jnp.where(top, scores, 0.0)
    wn = wsel / jnp.sum(wsel, axis=0, keepdims=True) * ROUTED_SCALE
    r_i = lax.broadcasted_iota(I32, (tm, tm), 0)
    c_i = lax.broadcasted_iota(I32, (tm, tm), 1)
    before = (r_i < c_i).astype(BF16)
    pos = jnp.dot(topf.astype(BF16), before, preferred_element_type=F32) + carry_ref[:, :1]
    carry_ref[...] = carry_ref[...] + jnp.sum(topf, axis=1, keepdims=True)
    cnt_ref[...] = carry_ref[...]
    e_rows, p_rows, w_rows = [], [], []
    for k in range(TOP_K):
        hit = rank == k
        e_rows.append(jnp.sum(jnp.where(hit, eidx, 0), axis=0, keepdims=True))
        p_rows.append(jnp.sum(jnp.where(hit, pos, 0.0), axis=0, keepdims=True))
        w_rows.append(jnp.sum(jnp.where(hit, wn, 0.0), axis=0, keepdims=True))
    e_ref[...] = jnp.concatenate(e_rows, axis=0)
    pos_ref[...] = jnp.concatenate(p_rows, axis=0).astype(I32)
    w_ref[...] = jnp.concatenate(w_rows, axis=0)


def _router(h2, w_router, router_bias):
    t, d = h2.shape
    tm = _pick(t, (256, 128))
    slot = pl.BlockSpec((TOP_K, tm), lambda i: (0, i))
    rb = jnp.broadcast_to(router_bias.astype(F32)[:, None], (N_EXPERTS, tm))
    return pl.pallas_call(
        _router_body,
        grid=(t // tm,),
        in_specs=[pl.BlockSpec((tm, d), lambda i: (i, 0)), pl.BlockSpec((N_EXPERTS, d), lambda i: (0, 0)),
                  pl.BlockSpec((N_EXPERTS, tm), lambda i: (0, 0))],
        out_specs=[slot, slot, slot, pl.BlockSpec((N_EXPERTS, LANES), lambda i: (0, 0))],
        out_shape=[jax.ShapeDtypeStruct((TOP_K, t), I32), jax.ShapeDtypeStruct((TOP_K, t), I32),
                   jax.ShapeDtypeStruct((TOP_K, t), F32), jax.ShapeDtypeStruct((N_EXPERTS, LANES), F32)],
        scratch_shapes=[pltpu.VMEM((N_EXPERTS, LANES), F32)],
        compiler_params=_cparams(("arbitrary",)),
        name="moe_router",
    )(h2, w_router.T.astype(F32), rb)


def _load_slots(dest_hbm, dest_ref, sem):
    cp = pltpu.make_async_copy(dest_hbm.at[pl.program_id(0)], dest_ref, sem)
    cp.start()
    cp.wait()


def _dispatch_body(cnt_ref, start_ref, dest_hbm, x_ref, xs_ref, dest_ref, zero_ref, sem_ref, *, block_rows):
    i = pl.program_id(0)
    tm = x_ref.shape[0]
    _load_slots(dest_hbm, dest_ref, sem_ref.at[2])

    def row_copy(tok, k):
        return pltpu.make_async_copy(x_ref.at[pl.ds(tok, 1), :], xs_ref.at[pl.ds(dest_ref[k, tok], 1), :], sem_ref.at[0])

    def issue(tok, carry):
        for k in range(TOP_K):
            row_copy(tok, k).start()
        return carry

    lax.fori_loop(0, tm, issue, 0)

    @pl.when(i == 0)
    def _():
        zero_ref[...] = jnp.zeros_like(zero_ref)

        def fill(e, carry):
            cnt = cnt_ref[e]
            padded = (cnt + block_rows - 1) // block_rows * block_rows
            base = start_ref[e] + cnt

            def zcopy(r):
                return pltpu.make_async_copy(zero_ref.at[pl.ds(0, 1), :], xs_ref.at[pl.ds(base + r, 1), :], sem_ref.at[1])

            def zstart(r, c2):
                zcopy(r).start()
                return c2

            def zwait(r, c2):
                zcopy(r).wait()
                return c2

            lax.fori_loop(0, padded - cnt, zstart, 0)
            lax.fori_loop(0, padded - cnt, zwait, 0)
            return carry

        lax.fori_loop(0, N_EXPERTS, fill, 0)

    for _ in range(TOP_K):
        pltpu.make_async_copy(x_ref, xs_ref.at[pl.ds(0, tm), :], sem_ref.at[0]).wait()


def _slot_tiles(dest, tm):
    k, t = dest.shape
    return dest.reshape(k, t // tm, tm).transpose(1, 0, 2)


def _dispatch(hp2, dest, counts, starts, n_rows):
    t, w = hp2.shape
    tm = _pick(t, (256, 128))
    grid_spec = pltpu.PrefetchScalarGridSpec(
        num_scalar_prefetch=2,
        grid=(t // tm,),
        in_specs=[pl.BlockSpec(memory_space=pl.ANY),
                  pl.BlockSpec((tm, w), lambda i, c, s: (i, 0))],
        out_specs=pl.BlockSpec(memory_space=pl.ANY),
        scratch_shapes=[pltpu.SMEM((TOP_K, tm), I32), pltpu.VMEM((8, w), U32), pltpu.SemaphoreType.DMA((3,))],
    )
    return pl.pallas_call(
        functools.partial(_dispatch_body, block_rows=MOE_ROWS),
        grid_spec=grid_spec,
        out_shape=jax.ShapeDtypeStruct((n_rows, w), U32),
        compiler_params=pltpu.CompilerParams(dimension_semantics=("arbitrary",)),
        name="moe_dispatch",
    )(counts, starts, _slot_tiles(dest, tm), hp2)


def _expert_body(be_ref, nu_ref, x_ref, wg_ref, wu_ref, wd_ref, o_ref):
    j = pl.program_id(0)

    @pl.when(j < nu_ref[0])
    def _():
        half = x_ref.shape[1]
        lo, hi = _unpack_halves(x_ref[...])
        lo, hi = lo.astype(BF16), hi.astype(BF16)

        def proj(w_ref):
            return (jnp.dot(lo, w_ref[:half, :], preferred_element_type=F32)
                    + jnp.dot(hi, w_ref[half:, :], preferred_element_type=F32))

        hmid = (_silu(proj(wg_ref)) * proj(wu_ref)).astype(BF16)
        o_ref[...] = _pack_halves(jnp.dot(hmid, wd_ref[...], preferred_element_type=F32))


def _experts(xs, wg, wu, wd, block_e, n_used):
    rows, w = xs.shape
    _, d, ff = wg.shape
    bm = MOE_ROWS
    nb = rows // bm

    def row_idx(j, be, nu):
        return (jnp.minimum(j, nu[0] - 1), 0)

    def w_idx(j, be, nu):
        return (be[jnp.minimum(j, nu[0] - 1)], 0, 0)

    grid_spec = pltpu.PrefetchScalarGridSpec(
        num_scalar_prefetch=2,
        grid=(nb,),
        in_specs=[pl.BlockSpec((bm, w), row_idx),
                  pl.BlockSpec((None, d, ff), w_idx), pl.BlockSpec((None, d, ff), w_idx),
                  pl.BlockSpec((None, ff, d), w_idx)],
        out_specs=pl.BlockSpec((bm, w), row_idx),
    )
    return pl.pallas_call(
        _expert_body,
        grid_spec=grid_spec,
        out_shape=jax.ShapeDtypeStruct((rows, w), U32),
        compiler_params=_cparams(("arbitrary",), VMEM_EXPERTS),
        name="moe_experts",
    )(block_e, n_used, xs, wg, wu, wd)


def _combine_body(dest_hbm, wt_ref, h_ref, ysh_ref, g_ref, b_ref, y_ref, o_ref, dest_ref, gbuf_ref, sem_ref, *, alpha):
    i, n_steps = pl.program_id(0), pl.num_programs(0)
    tm = h_ref.shape[0]

    def fetch(step, slot):
        cp = pltpu.make_async_copy(dest_hbm.at[step], dest_ref.at[slot], sem_ref.at[2])
        cp.start()
        cp.wait()

        def issue(tok, carry):
            for k in range(TOP_K):
                pltpu.make_async_copy(y_ref.at[pl.ds(dest_ref[slot, k, tok], 1), :],
                                      gbuf_ref.at[slot, k, pl.ds(tok, 1), :], sem_ref.at[slot]).start()
            return carry

        lax.fori_loop(0, tm, issue, 0)

    @pl.when(i == 0)
    def _():
        fetch(0, 0)

    @pl.when(i + 1 < n_steps)
    def _():
        fetch(i + 1, (i + 1) % 2)

    slot = i % 2
    for k in range(TOP_K):
        pltpu.make_async_copy(y_ref.at[pl.ds(0, tm), :], gbuf_ref.at[slot, k], sem_ref.at[slot]).wait()
    lo, hi = _unpack_halves(ysh_ref[...])
    wt = wt_ref[...]
    for k in range(TOP_K):
        glo, ghi = _unpack_halves(gbuf_ref[slot, k])
        wk = wt[:, k:k + 1]
        lo = lo + glo * wk
        hi = hi + ghi * wk
    x = alpha * h_ref[...] + jnp.concatenate([lo, hi], axis=1)
    mu = jnp.mean(x, axis=-1, keepdims=True)
    xc = x - mu
    var = jnp.mean(xc * xc, axis=-1, keepdims=True)
    o_ref[...] = xc * lax.rsqrt(var + LN_EPS) * g_ref[...] + b_ref[...]


def _combine(h2, ysh, y_sorted, dest, wt, g, b, alpha):
    t, d = h2.shape
    w = d // 2
    tm = _pick(t, (128,))
    vec = pl.BlockSpec((1, d), lambda i: (0, 0))
    return pl.pallas_call(
        functools.partial(_combine_body, alpha=alpha),
        grid=(t // tm,),
        in_specs=[pl.BlockSpec(memory_space=pl.ANY),
                  pl.BlockSpec((tm, TOP_K), lambda i: (i, 0)),
                  pl.BlockSpec((tm, d), lambda i: (i, 0)),
                  pl.BlockSpec((tm, w), lambda i: (i, 0)),
                  vec, vec,
                  pl.BlockSpec(memory_space=pl.ANY)],
        out_specs=pl.BlockSpec((tm, d), lambda i: (i, 0)),
        out_shape=jax.ShapeDtypeStruct((t, d), F32),
        scratch_shapes=[pltpu.SMEM((2, TOP_K, tm), I32), pltpu.VMEM((2, TOP_K, tm, w), U32),
                        pltpu.SemaphoreType.DMA((3,))],
        compiler_params=_cparams(("arbitrary",)),
        name="moe_combine",
    )(_slot_tiles(dest, tm), wt, h2, ysh, g.reshape(1, d), b.reshape(1, d), y_sorted)


def _rope_tables(pos):
    inv_freq = ROPE_THETA ** (-jnp.arange(0, MLA_ROPE, 2, dtype=F32) / MLA_ROPE)
    ang = pos.astype(F32)[..., None] * inv_freq
    return jnp.cos(ang), jnp.sin(ang)


def kernel(x, positions, meta_tokens, ln_in_g, ln_in_b, w_in, b_gate, conv_w, conv_b, dt_bias, a_log, d_skip,
           ssm_norm_g, w_ssm_proj, q_a_norm_g, w_q_b, kv_a_norm_g, w_kv_b, w_attn_proj, w_out, ln1_g, ln1_b,
           w_router, router_bias, w_exp_gate, w_exp_up, w_exp_down, w_sh_gate, w_sh_up, w_sh_down, ln2_g, ln2_b):
    bsz, seq, d = x.shape
    depth = w_in.shape[0]
    heads = dt_bias.shape[-1]
    d_inner = w_ssm_proj.shape[1]
    conv_dim = conv_w.shape[-1]
    q_rank = w_q_b.shape[1]
    kv_rank = w_kv_b.shape[1]
    hh = MLA_HEADS
    g = SSM_GROUPS
    hg = heads // g
    assert seq % CHUNK == 0 and d % (2 * LANES) == 0
    lp = seq + CHUNK
    t = bsz * seq
    alpha = (2.0 * depth) ** 0.25

    meta_chunk = jnp.concatenate([jnp.zeros((META_PAD, d), x.dtype), meta_tokens.astype(x.dtype)], axis=0)
    pos =jnp.concatenate([positions.astype(I32) + N_META, jnp.zeros((bsz, META_PAD), I32),
                           jnp.broadcast_to(jnp.arange(N_META, dtype=I32), (bsz, N_META))], axis=1)
    cos, sin = _rope_tables(pos)
    zr = jnp.zeros_like(cos)
    rope_c = jnp.concatenate([cos, cos, zr, zr], axis=-1)[:, :seq]
    rope_slo = jnp.concatenate([-sin, zr, zr, zr], axis=-1)[:, :seq]
    rope_shi = jnp.concatenate([zr, sin, zr, zr], axis=-1)[:, :seq]
    k_c = jnp.concatenate([cos, cos, zr, zr], axis=-1)
    k_s = jnp.concatenate([-sin, sin, zr, zr], axis=-1)

    h_f32, h_b16 = _layer_norm_in(x, meta_chunk, ln_in_g, ln_in_b)
    tm_all = _pick(lp, (1408, 1152, 1024, 896, 768, 640, 512, 384, 256, 128))
    tm_seq = _pick(seq, (1024, 512, 256, 128))

    out = None
    for l in range(depth):
        offs = [0]
        for wdt in (d_inner, conv_dim, heads, q_rank, kv_rank, MLA_ROPE, 2 * d):
            offs.append(offs[-1] + wdt)
        wt = jnp.swapaxes(w_in[l], 0, 1)
        w_kr = wt[offs[5]:offs[6]]
        small_cols = q_rank + kv_rank + 2 * MLA_ROPE + heads
        small_n = small_cols + (-small_cols) % 256
        wt_small = jnp.concatenate([wt[offs[3]:offs[4]], wt[offs[4]:offs[5]], w_kr, w_kr, wt[offs[2]:offs[3]],
                                    jnp.zeros((small_n - small_cols, d), F32)], axis=0)

        n_exp, _, ff = w_exp_gate[l].shape
        z3 = _mm(h_b16, wt, w_rows=(offs[0], d_inner), out_dtype=BF16, tm=tm_all, tn=512, name="in_proj_z")
        xbc3 = _mm(h_b16, wt, w_rows=(offs[1], conv_dim), out_dtype=BF16, tm=tm_all, tn=512, name="in_proj_xbc")
        small = _mm(h_b16, wt_small, w_rows=(0, small_n), out_dtype=F32, tm=tm_all, tn=256, name="in_proj_small")
        gates = _mm(h_b16, wt, w_rows=(offs[6], 2 * d), out_dtype=BF16, tm=tm_seq, tn=512, n_row_tiles=seq // tm_seq,
                    bias=b_gate[l], act="sigmoid", name="in_proj_gates")
        qa_cols, kva_cols = (0, q_rank), (q_rank, kv_rank)
        o = q_rank + kv_rank
        kr2 = small[:, :, o:o + 2 * MLA_ROPE]; o += 2 * MLA_ROPE
        dt_raw = small[:, :, o:o + heads]

        dt_g = jnp.pad(dt_raw.reshape(bsz, lp, g, hg).transpose(0, 2, 1, 3), ((0, 0), (0, 0), (0, 0), (0, LANES - hg)))
        pad_h = lambda v: jnp.pad(v.astype(F32).reshape(g, 1, hg), ((0, 0), (0, 0), (0, LANES - hg)))
        y_ssm = _ssd(xbc3, z3, dt_g, conv_w[l].astype(F32), conv_b[l].reshape(1, conv_dim).astype(F32),
                     pad_h(dt_bias[l]), pad_h(a_log[l]),
                     jnp.repeat(d_skip[l].astype(F32), SSM_HEAD_DIM).reshape(1, d_inner),
                     ssm_norm_g[l].reshape(1, d_inner).astype(F32), d_inner, heads)

        qscale = (MLA_QK ** -0.5) * math.log2(math.e)
        head_pad = ((0, 0), (0, 0), (0, MLA_QKP - MLA_QK))
        w_q = jnp.pad(w_q_b[l].reshape(q_rank, hh, MLA_QK), head_pad).reshape(q_rank, hh * MLA_QKP).astype(BF16)
        q3 = _mm(small, w_q, a_cols=qa_cols, out_dtype=BF16, tm=tm_seq, tn=4 * MLA_QKP, n_row_tiles=seq // tm_seq,
                 gain=q_a_norm_g[l] * qscale, rope=(rope_c, rope_slo, rope_shi), name="q_proj")
        wkv = w_kv_b[l].reshape(kv_rank, hh, MLA_NOPE + MLA_V)
        w_k = jnp.pad(wkv[:, :, :MLA_NOPE], ((0, 0), (0, 0), (0, MLA_QKP - MLA_NOPE)))
        w_k = w_k.reshape(kv_rank, hh * MLA_QKP).astype(BF16)
        w_vt = jnp.pad(wkv[:, :, MLA_NOPE:], ((0, 0), (0, 0), (0, MLA_VP - MLA_V)))
        w_vt = w_vt.reshape(kv_rank, hh * MLA_VP).T.astype(BF16)
        ones_rows = jnp.tile(jnp.concatenate([jnp.zeros((MLA_V,), F32), jnp.ones((MLA_VP - MLA_V,), F32)]), hh)
        k3 = _mm(small, w_k, a_cols=kva_cols, out_dtype=BF16, tm=tm_all, tn=4 * MLA_QKP, gain=kv_a_norm_g[l],
                 kadd=(kr2, k_c, k_s), name="k_proj")
        vt3 = _mm_t(small, w_vt, kv_a_norm_g[l], ones_rows, a_cols=kva_cols, out_dtype=BF16, tm=tm_all, tn=4 * MLA_VP,
                    name="v_proj")
        y_attn, wg_b, wu_b, wd_b = _attention(
            q3, k3, vt3, seq, [w_exp_gate[l].reshape(n_exp * d, ff), w_exp_up[l].reshape(n_exp * d, ff),
                               w_exp_down[l].reshape(n_exp * ff, d)])

        mixed = _mm_ktiled([y_ssm, y_attn], [w_ssm_proj[l].astype(BF16), w_attn_proj[l].astype(BF16)],
                           out_dtype=BF16, rows=seq, tm=tm_seq, tn=_pick(d, (1024, 512)), nk=4,
                           gates=gates, name="branch_proj")
        pre1 = _mm_ktiled([mixed], [w_out[l].astype(BF16)], out_dtype=F32, rows=seq, tm=tm_seq,
                          tn=_pick(d, (1024, 512)), nk=2, res=h_f32, res_scale=alpha, name="out_proj")
        h1, h1p = _layer_norm_pack(pre1.reshape(t, d), ln1_g[l], ln1_b[l])

        e_slot, pos_slot, w_slot, cnt = _router(h1, w_router[l], router_bias[l])
        counts = cnt[:, 0].astype(I32)
        padded = (counts + MOE_ROWS - 1) // MOE_ROWS * MOE_ROWS
        ends = jnp.cumsum(padded)
        starts = ends - padded
        onehot = e_slot[None] == jnp.arange(N_EXPERTS, dtype=I32)[:, None, None]
        dest = pos_slot + jnp.sum(jnp.where(onehot, starts[:, None, None], 0), axis=0)
        n_blocks = -(-(t * TOP_K) // MOE_ROWS) + N_EXPERTS
        blk0 = jnp.arange(n_blocks, dtype=I32) * MOE_ROWS
        block_e = jnp.minimum(jnp.sum((ends[None, :] <= blk0[:, None]).astype(I32), axis=1), N_EXPERTS - 1)
        n_used = (ends[-1] // MOE_ROWS).reshape(1)
        xs = _dispatch(h1p, dest, counts, starts, n_blocks * MOE_ROWS)
        y_sorted = _experts(xs, wg_b.reshape(n_exp, d, ff), wu_b.reshape(n_exp, d, ff), wd_b.reshape(n_exp, ff, d),
                            block_e, n_used)
        y_shared = _experts(h1p, w_sh_gate[l][None].astype(BF16), w_sh_up[l][None].astype(BF16),
                            w_sh_down[l][None].astype(BF16), jnp.zeros((t // MOE_ROWS,), I32),
                            jnp.full((1,), t // MOE_ROWS, I32))
        out = _combine(h1, y_shared, y_sorted, dest, w_slot.T, ln2_g[l], ln2_b[l], alpha)
        if l + 1 < depth:
            raise NotImplementedError("stacked layers need the meta rows carried through the channel mixer")
    return out.reshape(bsz, seq, d)
```

```python
import functools
import math

import jax
import jax.numpy as jnp
from jax import lax
from jax.experimental import pallas as pl
from jax.experimental.pallas import tpu as pltpu

F32 = jnp.float32
BF16 = jnp.bfloat16
U32 = jnp.uint32
I32 = jnp.int32

N_META = 16
CHUNK = 128
META_PAD = CHUNK - N_META
SSM_HEAD_DIM = 64
SSM_GROUPS = 8
SSM_STATE = 128
SSM_CONV = 4
MLA_HEADS = 64
MLA_NOPE = 128
MLA_ROPE = 64
MLA_V = 128
MLA_QK = MLA_NOPE + MLA_ROPE
ROPE_THETA = 10000.0
N_EXPERTS = 64
N_EXPERT_GROUPS = 8
TOPK_GROUPS = 4
TOP_K = 8
ROUTED_SCALE = 2.5
LN_EPS = 1e-5
RMS_EPS = 1e-6
NEG = -1e30
LANES = 128
MLA_QKP = 2 * LANES
MOE_ROWS = 256
MLA_VP = MLA_V + 16
MIB = 1024 * 1024
VMEM_EXPERTS = 60 * MIB
VMEM_BIG = 56 * MIB
VMEM_MID = 44 * MIB


def _cparams(sem, vmem=VMEM_MID):
    return pltpu.CompilerParams(dimension_semantics=sem, vmem_limit_bytes=vmem)


def _pick(n, cands):
    for c in cands:
        if n % c == 0:
            return c
    raise ValueError(f"no tile for {n} in {cands}")


def _ln_body(x_ref, g_ref, b_ref, of_ref, ob_ref):
    x = x_ref[...]
    mu = jnp.mean(x, axis=-1, keepdims=True)
    xc = x - mu
    var = jnp.mean(xc * xc, axis=-1, keepdims=True)
    y = xc * lax.rsqrt(var + LN_EPS) * g_ref[...] + b_ref[...]
    of_ref[...] = y
    ob_ref[...] = y.astype(BF16)


def _ln_in_body(x_ref, meta_ref, g_ref, b_ref, of_ref, ob_ref, *, n_real):
    i = pl.program_id(1)

    @pl.when(i < n_real)
    def _():
        _ln_body(x_ref, g_ref, b_ref, of_ref, ob_ref)

    @pl.when(i == n_real)
    def _():
        _ln_body(meta_ref, g_ref, b_ref, of_ref, ob_ref)


def _layer_norm_in(x3, meta_chunk, g, b):
    bsz, seq, d = x3.shape
    tm = CHUNK
    n_real = seq // tm
    spec = pl.BlockSpec((None, tm, d), lambda bi, i: (bi, i, 0))
    vec = pl.BlockSpec((1, d), lambda bi, i: (0, 0))
    shape = (bsz, seq + tm, d)
    return pl.pallas_call(
        functools.partial(_ln_in_body, n_real=n_real),
        grid=(bsz, n_real + 1),
        in_specs=[pl.BlockSpec((None, tm, d), lambda bi, i: (bi, jnp.minimum(i, n_real - 1), 0)),
                  pl.BlockSpec((tm, d), lambda bi, i: (0, 0)), vec, vec],
        out_specs=[spec, spec],
        out_shape=[jax.ShapeDtypeStruct(shape, F32), jax.ShapeDtypeStruct(shape, BF16)],
        compiler_params=_cparams(("parallel", "arbitrary")),
        name="layer_norm",
    )(x3, meta_chunk, g.reshape(1, d), b.reshape(1, d))


def _pack_halves(y):
    n = y.shape[1] // 2
    lo = pltpu.bitcast(y[:, :n].astype(BF16).astype(F32), U32) >> 16
    hi = pltpu.bitcast(y[:, n:].astype(BF16).astype(F32), U32) & jnp.uint32(0xFFFF0000)
    return hi | lo


def _unpack_halves(w):
    lo = pltpu.bitcast(w << 16, F32)
    hi = pltpu.bitcast(w & jnp.uint32(0xFFFF0000), F32)
    return lo, hi


def _ln_pack_body(x_ref, g_ref, b_ref, of_ref, op_ref):
    x = x_ref[...]
    mu = jnp.mean(x, axis=-1, keepdims=True)
    xc = x - mu
    var = jnp.mean(xc * xc, axis=-1, keepdims=True)
    y = xc * lax.rsqrt(var + LN_EPS) * g_ref[...] + b_ref[...]
    of_ref[...] = y
    op_ref[...] = _pack_halves(y)


def _layer_norm_pack(x2, g, b):
    rows, d = x2.shape
    tm = _pick(rows, (256, 128))
    vec = pl.BlockSpec((1, d), lambda i: (0, 0))
    return pl.pallas_call(
        _ln_pack_body,
        grid=(rows // tm,),
        in_specs=[pl.BlockSpec((tm, d), lambda i: (i, 0)), vec, vec],
        out_specs=[pl.BlockSpec((tm, d), lambda i: (i, 0)), pl.BlockSpec((tm, d // 2), lambda i: (i, 0))],
        out_shape=[jax.ShapeDtypeStruct((rows, d), F32), jax.ShapeDtypeStruct((rows, d // 2), U32)],
        compiler_params=_cparams(("parallel",)),
        name="layer_norm_pack",
    )(x2, g.reshape(1, d), b.reshape(1, d))


def _rms_rows(a_ref, gain_ref):
    af = a_ref[...].astype(F32)
    return (af * lax.rsqrt(jnp.mean(af * af, axis=-1, keepdims=True) + RMS_EPS) * gain_ref[...]).astype(BF16)


def _mm_body(*refs, has_gain, has_bias, act, rope, kadd, has_side, w_rows):
    it = iter(refs)
    a_ref, w_ref = next(it), next(it)
    gain_ref = next(it) if has_gain else None
    bias_ref = next(it) if has_bias else None
    rope_refs = [next(it) for _ in range(3)] if rope else None
    kadd_refs = [next(it) for _ in range(3)] if kadd else None
    side_in = next(it) if has_side else None
    o_ref = next(it)
    if has_side:
        next(it)[...] = side_in[...].astype(BF16)
    if has_gain:
        an_ref = next(it)

        @pl.when(pl.program_id(2) == 0)
        def _():
            an_ref[...] = _rms_rows(a_ref, gain_ref)

        a = an_ref[...]
    else:
        a = a_ref[...]
    if w_rows:
        r = lax.dot_general(a.astype(BF16), w_ref[...].astype(BF16), (((1,), (1,)), ((), ())),
                            preferred_element_type=F32)
    else:
        r = jnp.dot(a.astype(BF16), w_ref[...].astype(BF16), preferred_element_type=F32)
    tn = r.shape[1]
    if has_bias:
        r = r + bias_ref[...]
    if act == "sigmoid":
        r = jax.nn.sigmoid(r)
    if rope:
        c, slo, shi = (t[...] for t in rope_refs)
        half = MLA_ROPE // 2
        pieces = []
        for h0 in range(0, tn, MLA_QKP):
            rp = r[:, h0 + MLA_NOPE:h0 + MLA_QKP]
            pieces += [r[:, h0:h0 + MLA_NOPE], rp * c + pltpu.roll(rp, LANES - half, 1) * slo + pltpu.roll(rp, half, 1) * shi]
        r = jnp.concatenate(pieces, axis=1)
    if kadd:
        kr_ref, kc, ks = kadd_refs
        kr = kr_ref[...]
        piece = jnp.concatenate([jnp.zeros_like(kr), kr * kc[...] + pltpu.roll(kr, MLA_ROPE // 2, 1) * ks[...]], axis=1)
        r = r + jnp.concatenate([piece] * (tn // piece.shape[1]), axis=1)
    o_ref[...] = r.astype(o_ref.dtype)


def _side_specs(side, n_steps, lin):
    rows, cols = side.shape
    nb = max(c for c in range(1, n_steps + 1) if rows % c == 0 and (rows // c) % 16 == 0)
    spec = pl.BlockSpec((rows // nb, cols), lambda *ids: (jnp.minimum(lin(*ids), nb - 1), 0))
    return spec, jax.ShapeDtypeStruct(side.shape, BF16)


def _column_window(a3, first, width):
    if first % width == 0 and width % LANES == 0:
        return a3, first // width
    return a3[:, :, first:first + width], 0


def _mm(a3, w, *, out_dtype, tm, tn, n_row_tiles=None, w_rows=None, a_cols=None, gain=None, bias=None, act=None,
        rope=None, kadd=None, side=None, name="mm"):
    a3, a_blk = (a3, 0) if a_cols is None else _column_window(a3, *a_cols)
    bsz, rows = a3.shape[:2]
    k = a3.shape[2] if a_cols is None else a_cols[1]
    row0, n = (0, w.shape[1]) if w_rows is None else w_rows
    ni = rows // tm if n_row_tiles is None else n_row_tiles
    nj = n // tn
    assert n % tn == 0 and row0 % 8 == 0 and (n_row_tiles is not None or rows % tm == 0)
    if w_rows is None:
        w_spec = pl.BlockSpec((k, tn), lambda bi, i, j: (0, j))
    elif row0 % tn == 0:
        w_spec = pl.BlockSpec((tn, k), lambda bi, i, j: (row0 // tn + j, 0))
    else:
        w_spec = pl.BlockSpec((pl.Element(tn), pl.Element(k)),
                              lambda bi, i, j: (pl.multiple_of(row0 + j * tn, 8), 0))
    in_specs = [pl.BlockSpec((None, tm, k), lambda bi, i, j: (bi, i, a_blk)), w_spec]
    args = [a3, w]
    if gain is not None:
        in_specs.append(pl.BlockSpec((1, k), lambda bi, i, j: (0, 0)))
        args.append(gain.reshape(1, k).astype(F32))
    if bias is not None:
        in_specs.append(pl.BlockSpec((1, tn), lambda bi, i, j: (0, j)))
        args.append(bias.reshape(1, n).astype(F32))
    for t in (rope or ()) + (kadd or ()):
        in_specs.append(pl.BlockSpec((None, tm, t.shape[2]), lambda bi, i, j: (bi, i, 0)))
        args.append(t)
    out_specs = [pl.BlockSpec((None, tm, tn), lambda bi, i, j: (bi, i, j))]
    out_shape = [jax.ShapeDtypeStruct((bsz, ni * tm, n), out_dtype)]
    if side is not None:
        spec, shape = _side_specs(side, bsz * ni * nj, lambda bi, i, j: (bi * ni + i) * nj + j)
        in_specs.append(spec)
        args.append(side)
        out_specs.append(spec)
        out_shape.append(shape)
    body = functools.partial(_mm_body, has_gain=gain is not None, has_bias=bias is not None, act=act,
                             rope=rope is not None, kadd=kadd is not None, has_side=side is not None,
                             w_rows=w_rows is not None)
    outs = pl.pallas_call(
        body,
        grid=(bsz, ni, nj),
        in_specs=in_specs,
        out_specs=out_specs,
        out_shape=out_shape,
        scratch_shapes=[pltpu.VMEM((tm, k), BF16)] if gain is not None else [],
        compiler_params=_cparams(("parallel", "parallel", "arbitrary") if side is None else ("arbitrary",) * 3,
                                 VMEM_BIG),
        name=name,
    )(*args)
    return outs[0] if side is None else outs


def _mm_t_body(a_ref, wt_ref, gain_ref, bias_ref, o_ref, an_ref):
    @pl.when(pl.program_id(2) == 0)
    def _():
        an_ref[...] = _rms_rows(a_ref, gain_ref)

    r = lax.dot_general(wt_ref[...], an_ref[...], (((1,), (1,)), ((), ())), preferred_element_type=F32)
    o_ref[...] = (r + bias_ref[...]).astype(o_ref.dtype)


def _mm_t(a3, wt, gain, bias_col, *, a_cols, out_dtype, tm, tn, name):
    a3, a_blk = _column_window(a3, *a_cols)
    bsz, rows = a3.shape[:2]
    k = a_cols[1]
    n = wt.shape[0]
    return pl.pallas_call(
        _mm_t_body,
        grid=(bsz, rows // tm, n // tn),
        in_specs=[pl.BlockSpec((None, tm, k), lambda bi, i, j: (bi, i, a_blk)),
                  pl.BlockSpec((tn, k), lambda bi, i, j: (j, 0)),
                  pl.BlockSpec((1, k), lambda bi, i, j: (0, 0)),
                  pl.BlockSpec((tn, 1), lambda bi, i, j: (j, 0))],
        out_specs=pl.BlockSpec((None, tn, tm), lambda bi, i, j: (bi, j, i)),
        out_shape=jax.ShapeDtypeStruct((bsz, n, rows), out_dtype),
        scratch_shapes=[pltpu.VMEM((tm, k), BF16)],
        compiler_params=_cparams(("parallel", "parallel", "arbitrary"), VMEM_BIG),
        name=name,
    )(a3, wt, gain.reshape(1, k).astype(F32), bias_col.reshape(n, 1).astype(F32))


def _mmk_body(*refs, n_pairs, has_gate, has_res, res_scale, has_side):
    it = iter(refs)
    a_refs = [next(it) for _ in range(n_pairs)]
    w_refs = [next(it) for _ in range(n_pairs)]
    g_refs = [next(it) for _ in range(n_pairs)] if has_gate else None
    res_ref = next(it) if has_res else None
    side_in = next(it) if has_side else None
    o_ref = next(it)
    if has_side:
        next(it)[...] = side_in[...].astype(BF16)
    acc_refs = [next(it) for _ in range(n_pairs)]
    kk = pl.program_id(3)

    @pl.when(kk == 0)
    def _():
        for acc in acc_refs:
            acc[...] = jnp.zeros_like(acc)

    for a_ref, w_ref, acc in zip(a_refs, w_refs, acc_refs):
        acc[...] += jnp.dot(a_ref[...], w_ref[...], preferred_element_type=F32)

    @pl.when(kk == pl.num_programs(3) - 1)
    def _():
        r = None
        for p, acc in enumerate(acc_refs):
            t = acc[...]
            if has_gate:
                t = t * g_refs[p][...].astype(F32)
            r = t if r is None else r + t
        if has_res:
            r = r + res_scale * res_ref[...]
        o_ref[...] = r.astype(o_ref.dtype)


def _mm_ktiled(a_list, w_list, *, out_dtype, rows, tm, tn, nk, gates=None, res=None, res_scale=1.0, side=None,
               name="mmk"):
    bsz = a_list[0].shape[0]
    n = w_list[0].shape[1]
    npairs = len(a_list)
    tks = [a.shape[2] // nk for a in a_list]
    assert all(a.shape[2] == tk * nk and tk % LANES == 0 for a, tk in zip(a_list, tks))
    o_spec = pl.BlockSpec((None, tm, tn), lambda bi, i, j, kk: (bi, i, j))
    in_specs = ([pl.BlockSpec((None, tm, tk), lambda bi, i, j, kk: (bi, i, kk)) for tk in tks]
                + [pl.BlockSpec((tk, tn), lambda bi, i, j, kk: (kk, j)) for tk in tks])
    args = list(a_list) + list(w_list)
    if gates is not None:
        for p in range(npairs):
            in_specs.append(pl.BlockSpec((None, tm, tn), lambda bi, i, j, kk, p=p: (bi, i, p * (n // tn) + j)))
            args.append(gates)
    if res is not None:
        in_specs.append(o_spec)
        args.append(res)
    ni, nj = rows // tm, n // tn
    out_specs = [o_spec]
    out_shape = [jax.ShapeDtypeStruct((bsz, rows, n), out_dtype)]
    if side is not None:
        spec, shape = _side_specs(side, bsz * ni * nj * nk, lambda bi, i, j, kk: ((bi * ni + i) * nj + j) * nk + kk)
        in_specs.append(spec)
        args.append(side)
        out_specs.append(spec)
        out_shape.append(shape)
    body = functools.partial(_mmk_body, n_pairs=npairs, has_gate=gates is not None, has_res=res is not None,
                             res_scale=res_scale, has_side=side is not None)
    outs = pl.pallas_call(
        body,
        grid=(bsz, ni, nj, nk),
        in_specs=in_specs,
        out_specs=out_specs,
        out_shape=out_shape,
        scratch_shapes=[pltpu.VMEM((tm, tn), F32) for _ in range(npairs)],
        compiler_params=_cparams(("parallel", "parallel", "parallel", "arbitrary") if side is None
                                 else ("arbitrary",) * 4, VMEM_BIG),
        name=name,
    )(*args)
    return outs[0] if side is None else outs


def _silu(x):
    return x * jax.nn.sigmoid(x)


def _ssd_body(xs_ref, b_ref, c_ref, z_ref, dt_ref, wx_ref, wb_ref, wc_ref, bx_ref, bb_ref, bc_ref,
              dtb_ref, alog_ref, dskip_ref, ng_ref, e_ref, o_ref,
              state_ref, extx_ref, extb_ref, extc_ref, y_ref, *, hg, gp):
    c = pl.program_id(2)
    q = CHUNK
    p = SSM_HEAD_DIM
    n = SSM_STATE
    gw = hg * p
    first = c == 0
    row = lax.broadcasted_iota(I32, (q, 1), 0)
    live = jnp.logical_or(jnp.logical_not(first), row >= META_PAD)

    @pl.when(first)
    def _():
        state_ref[...] = jnp.zeros_like(state_ref)
        extx_ref[0:8, :] = jnp.zeros((8, extx_ref.shape[1]), F32)
        extb_ref[0:8, :] = jnp.zeros((8, extb_ref.shape[1]), F32)
        extc_ref[0:8, :] = jnp.zeros((8, extc_ref.shape[1]), F32)

    def conv(ext_ref, u_ref, w_ref, bias_ref):
        @pl.when(first)
        def _():
            ext_ref[8:8 + q, :] = jnp.where(live, u_ref[...].astype(F32), 0.0)

        @pl.when(jnp.logical_not(first))
        def _():
            ext_ref[8:8 + q, :] = u_ref[...].astype(F32)

        acc = bias_ref[...]
        for k in range(SSM_CONV):
            acc = acc + ext_ref[pl.ds(8 - (SSM_CONV - 1) + k, q), :] * w_ref[k:k + 1, :]
        ext_ref[0:8, :] = ext_ref[q:q + 8, :]
        return _silu(acc)

    xs_w = conv(extx_ref, xs_ref, wx_ref, bx_ref)
    bm_w = conv(extb_ref, b_ref, wb_ref, bb_ref)
    cm_w = conv(extc_ref, c_ref, wc_ref, bc_ref)

    r_i = lax.broadcasted_iota(I32, (q, q), 0)
    c_i = lax.broadcasted_iota(I32, (q, q), 1)
    causal = r_i >= c_i
    tri = causal.astype(F32)
    e = e_ref[...]

    def spread(v):
        v_hi = v.astype(BF16)
        v_lo = (v - v_hi.astype(F32)).astype(BF16)
        return jnp.dot(v_hi, e, preferred_element_type=F32) + jnp.dot(v_lo, e, preferred_element_type=F32)

    groups = [dict() for _ in range(gp)]
    for gg, s in enumerate(groups):
        x = dt_ref[gg] + dtb_ref[gg]
        dt = jnp.maximum(x, 0.0) + jnp.log1p(jnp.exp(-jnp.abs(x)))
        dt = jnp.where(live, dt, 0.0)
        a = -jnp.exp(alog_ref[gg])
        a_cs = jnp.dot(tri, dt * a, precision=lax.Precision.HIGHEST, preferred_element_type=F32)
        s.update(dt=dt, a_cs=a_cs, a_cs_t=a_cs.T, ea=jnp.exp(a_cs), de=jnp.exp(a_cs[q - 1:q, :] - a_cs))

    for gg, s in enumerate(groups):
        xs = xs_w[:, gg * gw:(gg + 1) * gw]
        bm = bm_w[:, gg * n:(gg + 1) * n]
        cm_b = cm_w[:, gg * n:(gg + 1) * n].astype(BF16)
        ea_x = spread(s["ea"])
        xdt = xs * spread(s["dt"])
        prev = state_ref[gg]
        s.update(xs=xs, ea_x=ea_x, prev=prev, xdt_b=xdt.astype(BF16),
                 cb=lax.dot_general(cm_b, bm.astype(BF16), (((1,), (1,)), ((), ())), preferred_element_type=F32),
                 y_off=jnp.dot(cm_b, prev.astype(BF16), preferred_element_type=F32) * ea_x,
                 upd=jnp.dot(bm.T.astype(BF16), (xdt * spread(s["de"])).astype(BF16), preferred_element_type=F32))

    def decay_scores(s, j):
        seg = s["a_cs"][:, j:j + 1] - s["a_cs_t"][j:j + 1, :]
        return (jnp.exp(jnp.where(causal, seg, NEG)) * s["cb"]).astype(BF16)

    nxt = [decay_scores(s, 0) for s in groups]
    for j in range(hg):
        for gg, s in enumerate(groups):
            m = nxt[gg]
            if j + 1 < hg:
                nxt[gg] = decay_scores(s, j + 1)
            lo = gg * gw + j * p
            y_ref[:, lo:lo + p] = jnp.dot(m, s["xdt_b"][:, j * p:(j + 1) * p], preferred_element_type=F32)

    for gg, s in enumerate(groups):
        sl = slice(gg * gw, (gg + 1) * gw)
        y = y_ref[:, sl] + s["y_off"] + s["xs"] * dskip_ref[:, sl]
        state_ref[gg] = s["prev"] * s["ea_x"][q - 1:q, :] + s["upd"]
        yz = y * _silu(z_ref[:, sl].astype(F32))
        yn = yz * lax.rsqrt(jnp.mean(yz * yz, axis=-1, keepdims=True) + RMS_EPS) * ng_ref[:, sl]
        o_ref[:, sl] = yn.astype(o_ref.dtype)


def _ssd(xbc3, z3, dt_g, conv_w, conv_b, dtb_g, alog_g, dskip_x, norm_g, d_inner, heads):
    bsz, rows, conv_dim = xbc3.shape
    g, n, p, q = SSM_GROUPS, SSM_STATE, SSM_HEAD_DIM, CHUNK
    gp = 2
    hg = heads // g
    gw = hg * p
    nc = rows // q
    assert gw % LANES == 0 and d_inner % (gp * n) == 0 and hg <= LANES and g % gp == 0
    boff = d_inner // (gp * n)
    coff = boff + g // gp
    expand = (jnp.arange(LANES)[:, None] == (jnp.arange(gw) // p)[None, :]).astype(BF16)

    def chunk(c):
        return (c + nc - 1) % nc

    in_specs = [
        pl.BlockSpec((None, q, gp * gw), lambda b, gi, c: (b, chunk(c), gi)),
        pl.BlockSpec((None, q, gp * n), lambda b, gi, c: (b, chunk(c), boff + gi)),
        pl.BlockSpec((None, q, gp * n), lambda b, gi, c: (b, chunk(c), coff + gi)),
        pl.BlockSpec((None, q, gp * gw), lambda b, gi, c: (b, chunk(c), gi)),
        pl.BlockSpec((None, gp, q, LANES), lambda b, gi, c: (b, gi, chunk(c), 0)),
        pl.BlockSpec((SSM_CONV, gp * gw), lambda b, gi, c: (0, gi)),
        pl.BlockSpec((SSM_CONV, gp * n), lambda b, gi, c: (0, boff + gi)),
        pl.BlockSpec((SSM_CONV, gp * n), lambda b, gi, c: (0, coff + gi)),
        pl.BlockSpec((1, gp * gw), lambda b, gi, c: (0, gi)),
        pl.BlockSpec((1, gp * n), lambda b, gi, c: (0, boff + gi)),
        pl.BlockSpec((1, gp * n), lambda b, gi, c: (0, coff + gi)),
        pl.BlockSpec((gp, 1, LANES), lambda b, gi, c: (gi, 0, 0)),
        pl.BlockSpec((gp, 1, LANES), lambda b, gi, c: (gi, 0, 0)),
        pl.BlockSpec((1, gp * gw), lambda b, gi, c: (0, gi)),
        pl.BlockSpec((1, gp * gw), lambda b, gi, c: (0, gi)),
        pl.BlockSpec((LANES, gw), lambda b, gi, c: (0, 0)),
    ]
    return pl.pallas_call(
        functools.partial(_ssd_body, hg=hg, gp=gp),
        grid=(bsz, g // gp, nc),
        in_specs=in_specs,
        out_specs=pl.BlockSpec((None, q, gp * gw), lambda b, gi, c: (b, chunk(c), gi)),
        out_shape=jax.ShapeDtypeStruct((bsz, rows, d_inner), BF16),
        scratch_shapes=[pltpu.VMEM((gp, n, gw), F32), pltpu.VMEM((q + 8, gp * gw), F32),
                        pltpu.VMEM((q + 8, gp * n), F32), pltpu.VMEM((q + 8, gp * n), F32),
                        pltpu.VMEM((q, gp * gw), F32)],
        compiler_params=_cparams(("parallel", "parallel", "arbitrary")),
        name="ssd",
    )(xbc3, xbc3, xbc3, z3, dt_g, conv_w, conv_w, conv_w, conv_b, conv_b, conv_b, dtb_g, alog_g,
      dskip_x, norm_g, expand)


def _attn_body(qi_ref, ki_ref, q_ref, km_ref, vm_ref, k_ref, v_ref, *rest, heads, n_side):
    side_in, o_ref, side_out = rest[:n_side], rest[n_side], rest[n_side + 1:2 * n_side + 1]
    m_ref, acc_ref = rest[2 * n_side + 1:]
    for src, dst in zip(side_in, side_out):
        dst[...] = src[...].astype(BF16)
    step_id = pl.program_id(2)
    qi, ki = qi_ref[step_id], ki_ref[step_id]
    tq = q_ref.shape[0]

    def step(segments):
        def scores(h):
            out = []
            for kr, _, key0, nk, q0, mask in segments:
                qh = q_ref[q0:, h * MLA_QKP:(h + 1) * MLA_QKP]
                kh = kr[key0:key0 + nk, h * MLA_QKP:(h + 1) * MLA_QKP]
                st = lax.dot_general(kh, qh, (((1,), (1,)), ((), ())), preferred_element_type=F32)
                key = key0 + lax.broadcasted_iota(I32, st.shape, 0)
                if mask == "meta":
                    st = jnp.where(key >= META_PAD, st, NEG)
                elif mask == "causal":
                    st = jnp.where(key <= q0 + lax.broadcasted_iota(I32, st.shape, 1), st, NEG)
                out.append(st)
            return out

        def probs(h, sts):
            m_prev = m_ref[h]
            m_new = m_prev
            for (_, _, _, _, q0, _), st in zip(segments, sts):
                cm = jnp.max(st, axis=0, keepdims=True)
                if q0:
                    cm = jnp.concatenate([jnp.full((1, q0), NEG, F32), cm], axis=1)
                m_new = jnp.maximum(m_new, cm)
            m_ref[h] = m_new
            pts = [jnp.exp2((st - m_new[:, q0:]).astype(BF16)) for (_, _, _, _, q0, _), st in zip(segments, sts)]
            return jnp.exp2(m_prev - m_new), pts

        def accumulate(h, alpha, pts):
            sl = slice(h * MLA_VP, (h + 1) * MLA_VP)
            acc = acc_ref[sl, :] * alpha
            partial = []
            for (_, vr, key0, nk, q0, _), pt in zip(segments, pts):
                contrib = jnp.dot(vr[sl, key0:key0 + nk], pt, preferred_element_type=F32)
                if q0:
                    partial.append((q0, contrib))
                else:
                    acc = acc + contrib
            acc_ref[sl, :] = acc
            for q0, contrib in partial:
                acc_ref[sl, q0:] += contrib

        st_q, pr_q = {}, {}
        for stage in range(heads + 2):
            if stage < heads:
                st_q[stage] = scores(stage)
            if 0 <= stage - 1 < heads:
                pr_q[stage - 1] = probs(stage - 1, st_q.pop(stage - 1))
            if 0 <= stage - 2 < heads:
                accumulate(stage - 2, *pr_q.pop(stage - 2))

    def start():
        m_ref[...] = jnp.full_like(m_ref, NEG)
        acc_ref[...] = jnp.zeros_like(acc_ref)

    def finish():
        for h in range(heads):
            num = acc_ref[h * MLA_VP:h * MLA_VP + MLA_V, :]
            den = acc_ref[h * MLA_VP + MLA_V:h * MLA_VP + MLA_V + 1, :]
            o_ref[:, h * MLA_V:(h + 1) * MLA_V] = (num / den).T.astype(o_ref.dtype)

    tk = k_ref.shape[0]
    meta = (km_ref, vm_ref, 0, km_ref.shape[0], 0, "meta")
    full = (k_ref, v_ref, 0, tk, 0, None)
    half = tk // 2
    if half % (2 * LANES) == 0:
        diag = [(k_ref, v_ref, 0, half, 0, "causal"), (k_ref, v_ref, half, half, half, "causal")]
    else:
        diag = [(k_ref, v_ref, 0, tk, 0, "causal")]

    @pl.when(jnp.logical_and(ki == 0, qi == 0))
    def _():
        start()
        step([meta] + diag)
        finish()

    @pl.when(jnp.logical_and(ki == 0, qi > 0))
    def _():
        start()
        step([meta, full])

    @pl.when(jnp.logical_and(ki > 0, ki < qi))
    def _():
        step([full])

    @pl.when(jnp.logical_and(ki > 0, ki == qi))
    def _():
        step(diag)
        finish()


def _attention(q3, k3, vt3, seq, sides=()):
    bsz = q3.shape[0]
    hp = 8
    t = _pick(seq, (512, 256, 128))
    nq = seq // t
    meta_blk = seq // CHUNK
    qw, vw = hp * MLA_QKP, hp * MLA_VP
    qi_tab = jnp.asarray([qi for qi in range(nq) for _ in range(qi + 1)], I32)
    ki_tab = jnp.asarray([ki for qi in range(nq) for ki in range(qi + 1)], I32)

    def q_idx(b, h, s, qt, kt):
        return (b, qt[s], h)

    def k_idx(b, h, s, qt, kt):
        return (b, kt[s], h)

    def vt_idx(b, h, s, qt, kt):
        return (b, h, kt[s])

    nh, npairs = MLA_HEADS // hp, int(qi_tab.shape[0])
    side_specs = [_side_specs(s_arr, bsz * nh * npairs, lambda b, h, s, qt, kt: (b * nh + h) * npairs + s)
                  for s_arr in sides]
    grid_spec = pltpu.PrefetchScalarGridSpec(
        num_scalar_prefetch=2,
        grid=(bsz, nh, npairs),
        in_specs=[
            pl.BlockSpec((None, t, qw), q_idx),
            pl.BlockSpec((None, CHUNK, qw), lambda b, h, s, qt, kt: (b, meta_blk, h)),
            pl.BlockSpec((None, vw, CHUNK), lambda b, h, s, qt, kt: (b, h, meta_blk)),
            pl.BlockSpec((None, t, qw), k_idx),
            pl.BlockSpec((None, vw, t), vt_idx),
        ] + [spec for spec, _ in side_specs],
        out_specs=[pl.BlockSpec((None, t, hp * MLA_V), q_idx)] + [spec for spec, _ in side_specs],
        scratch_shapes=[pltpu.VMEM((hp, 1, t), F32), pltpu.VMEM((vw, t), F32)],
    )
    return pl.pallas_call(
        functools.partial(_attn_body, heads=hp, n_side=len(sides)),
        grid_spec=grid_spec,
        out_shape=[jax.ShapeDtypeStruct((bsz, seq, MLA_HEADS * MLA_V), BF16)] + [shape for _, shape in side_specs],
        compiler_params=_cparams(("arbitrary", "arbitrary", "arbitrary")),
        name="mla_attention",
    )(qi_tab, ki_tab, q3, k3, vt3, k3, vt3, *sides)


def _router_body(h_ref, wr_ref, rb_ref, e_ref, pos_ref, w_ref, cnt_ref, carry_ref):
    i = pl.program_id(0)
    ne, ng = N_EXPERTS, N_EXPERT_GROUPS
    gs = ne // ng
    tm = h_ref.shape[0]

    @pl.when(i == 0)
    def _():
        carry_ref[...] = jnp.zeros_like(carry_ref)

    logits = lax.dot_general(wr_ref[...], h_ref[...], (((1,), (1,)), ((), ())), precision=lax.Precision.HIGHEST,
                             preferred_element_type=F32)
    scores = jax.nn.sigmoid(logits)
    choice = scores + rb_ref[...]
    sub = lax.broadcasted_iota(I32, (gs, tm), 0)
    grp_rows = []
    for g in range(ng):
        blk = choice[g * gs:(g + 1) * gs, :]
        m1 = jnp.max(blk, axis=0, keepdims=True)
        first = jnp.min(jnp.where(blk == m1, sub, gs), axis=0, keepdims=True)
        m2 = jnp.max(jnp.where(sub == first, -jnp.inf, blk), axis=0, keepdims=True)
        grp_rows.append(m1 + m2)
    grp = jnp.concatenate(grp_rows, axis=0)
    grank = jnp.zeros((ng, tm), I32)
    gidx = lax.broadcasted_iota(I32, (ng, tm), 0)
    for g in range(ng):
        rowv = grp[g:g + 1, :]
        beats = jnp.logical_or(rowv > grp, jnp.logical_and(rowv == grp, gidx > g))
        grank = grank + beats.astype(I32)
    gsel = (grank < TOPK_GROUPS).astype(F32)
    esel = jnp.concatenate([jnp.broadcast_to(gsel[g:g + 1, :], (gs, tm)) for g in range(ng)], axis=0)
    masked = jnp.where(esel > 0.0, choice, -jnp.inf)
    eidx = lax.broadcasted_iota(I32, (ne, tm), 0)
    rank = jnp.zeros((ne, tm), I32)
    for e in range(ne):
        rowv = masked[e:e + 1, :]
        beats = jnp.logical_or(rowv > masked, jnp.logical_and(rowv == masked, eidx > e))
        rank = rank + beats.astype(I32)
    top = jnp.logical_and(rank < TOP_K, esel > 0.0)
    topf = top.astype(F32)
    wsel = jnp.where(top, scores, 0.0)
    wn = wsel / jnp.sum(wsel, axis=0, keepdims=True) * ROUTED_SCALE
    r_i = lax.broadcasted_iota(I32, (tm, tm), 0)
    c_i = lax.broadcasted_iota(I32, (tm, tm), 1)
    before = (r_i < c_i).astype(BF16)
    pos = jnp.dot(topf.astype(BF16), before, preferred_element_type=F32) + carry_ref[:, :1]
    carry_ref[...] = carry_ref[...] + jnp.sum(topf, axis=1, keepdims=True)
    cnt_ref[...] = carry_ref[...]
    e_rows, p_rows, w_rows = [], [], []
    for k in range(TOP_K):
        hit = rank == k
        e_rows.append(jnp.sum(jnp.where(hit, eidx, 0), axis=0, keepdims=True))
        p_rows.append(jnp.sum(jnp.where(hit, pos, 0.0), axis=0, keepdims=True))
        w_rows.append(jnp.sum(jnp.where(hit, wn, 0.0), axis=0, keepdims=True))
    e_ref[...] = jnp.concatenate(e_rows, axis=0)
    pos_ref[...] = jnp.concatenate(p_rows, axis=0).astype(I32)
    w_ref[...] = jnp.concatenate(w_rows, axis=0)


def _router(h2, w_router, router_bias):
    t, d = h2.shape
    tm = _pick(t, (256, 128))
    slot = pl.BlockSpec((TOP_K, tm), lambda i: (0, i))
    rb = jnp.broadcast_to(router_bias.astype(F32)[:, None], (N_EXPERTS, tm))
    return pl.pallas_call(
        _router_body,
        grid=(t // tm,),
        in_specs=[pl.BlockSpec((tm, d), lambda i: (i, 0)), pl.BlockSpec((N_EXPERTS, d), lambda i: (0, 0)),
                  pl.BlockSpec((N_EXPERTS, tm), lambda i: (0, 0))],
        out_specs=[slot, slot, slot, pl.BlockSpec((N_EXPERTS, LANES), lambda i: (0, 0))],
        out_shape=[jax.ShapeDtypeStruct((TOP_K, t), I32), jax.ShapeDtypeStruct((TOP_K, t), I32),
                   jax.ShapeDtypeStruct((TOP_K, t), F32), jax.ShapeDtypeStruct((N_EXPERTS, LANES), F32)],
        scratch_shapes=[pltpu.VMEM((N_EXPERTS, LANES), F32)],
        compiler_params=_cparams(("arbitrary",)),
        name="moe_router",
    )(h2, w_router.T.astype(F32), rb)


def _load_slots(dest_hbm, dest_ref, sem):
    cp = pltpu.make_async_copy(dest_hbm.at[pl.program_id(0)], dest_ref, sem)
    cp.start()
    cp.wait()


def _dispatch_body(cnt_ref, start_ref, dest_hbm, x_ref, xs_ref, dest_ref, zero_ref, sem_ref, *, block_rows):
    i = pl.program_id(0)
    tm = x_ref.shape[0]
    _load_slots(dest_hbm, dest_ref, sem_ref.at[2])

    def row_copy(tok, k):
        return pltpu.make_async_copy(x_ref.at[pl.ds(tok, 1), :], xs_ref.at[pl.ds(dest_ref[k, tok], 1), :], sem_ref.at[0])

    def issue(tok, carry):
        for k in range(TOP_K):
            row_copy(tok, k).start()
        return carry

    lax.fori_loop(0, tm, issue, 0)

    @pl.when(i == 0)
    def _():
        zero_ref[...] = jnp.zeros_like(zero_ref)

        def fill(e, carry):
            cnt = cnt_ref[e]
            padded = (cnt + block_rows - 1) // block_rows * block_rows
            base = start_ref[e] + cnt

            def zcopy(r):
                return pltpu.make_async_copy(zero_ref.at[pl.ds(0, 1), :], xs_ref.at[pl.ds(base + r, 1), :], sem_ref.at[1])

            def zstart(r, c2):
                zcopy(r).start()
                return c2

            def zwait(r, c2):
                zcopy(r).wait()
                return c2

            lax.fori_loop(0, padded - cnt, zstart, 0)
            lax.fori_loop(0, padded - cnt, zwait, 0)
            return carry

        lax.fori_loop(0, N_EXPERTS, fill, 0)

    for _ in range(TOP_K):
        pltpu.make_async_copy(x_ref, xs_ref.at[pl.ds(0, tm), :], sem_ref.at[0]).wait()


def _slot_tiles(dest, tm):
    k, t = dest.shape
    return dest.reshape(k, t // tm, tm).transpose(1, 0, 2)


def _dispatch(hp2, dest, counts, starts, n_rows):
    t, w = hp2.shape
    tm = _pick(t, (256, 128))
    grid_spec = pltpu.PrefetchScalarGridSpec(
        num_scalar_prefetch=2,
        grid=(t // tm,),
        in_specs=[pl.BlockSpec(memory_space=pl.ANY),
                  pl.BlockSpec((tm, w), lambda i, c, s: (i, 0))],
        out_specs=pl.BlockSpec(memory_space=pl.ANY),
        scratch_shapes=[pltpu.SMEM((TOP_K, tm), I32), pltpu.VMEM((8, w), U32), pltpu.SemaphoreType.DMA((3,))],
    )
    return pl.pallas_call(
        functools.partial(_dispatch_body, block_rows=MOE_ROWS),
        grid_spec=grid_spec,
        out_shape=jax.ShapeDtypeStruct((n_rows, w), U32),
        compiler_params=pltpu.CompilerParams(dimension_semantics=("arbitrary",)),
        name="moe_dispatch",
    )(counts, starts, _slot_tiles(dest, tm), hp2)


def _expert_body(be_ref, nu_ref, x_ref, wg_ref, wu_ref, wd_ref, o_ref):
    j = pl.program_id(0)

    @pl.when(j < nu_ref[0])
    def _():
        half = x_ref.shape[1]
        lo, hi = _unpack_halves(x_ref[...])
        lo, hi = lo.astype(BF16), hi.astype(BF16)

        def proj(w_ref):
            return (jnp.dot(lo, w_ref[:half, :], preferred_element_type=F32)
                    + jnp.dot(hi, w_ref[half:, :], preferred_element_type=F32))

        hmid = (_silu(proj(wg_ref)) * proj(wu_ref)).astype(BF16)
        o_ref[...] = _pack_halves(jnp.dot(hmid, wd_ref[...], preferred_element_type=F32))


def _experts(xs, wg, wu, wd, block_e, n_used):
    rows, w = xs.shape
    _, d, ff = wg.shape
    bm = MOE_ROWS
    nb = rows // bm

    def row_idx(j, be, nu):
        return (jnp.minimum(j, nu[0] - 1), 0)

    def w_idx(j, be, nu):
        return (be[jnp.minimum(j, nu[0] - 1)], 0, 0)

    grid_spec = pltpu.PrefetchScalarGridSpec(
        num_scalar_prefetch=2,
        grid=(nb,),
        in_specs=[pl.BlockSpec((bm, w), row_idx),
                  pl.BlockSpec((None, d, ff), w_idx), pl.BlockSpec((None, d, ff), w_idx),
                  pl.BlockSpec((None, ff, d), w_idx)],
        out_specs=pl.BlockSpec((bm, w), row_idx),
    )
    return pl.pallas_call(
        _expert_body,
        grid_spec=grid_spec,
        out_shape=jax.ShapeDtypeStruct((rows, w), U32),
        compiler_params=_cparams(("arbitrary",), VMEM_EXPERTS),
        name="moe_experts",
    )(block_e, n_used, xs, wg, wu, wd)


def _combine_body(dest_hbm, wt_ref, h_ref, ysh_ref, g_ref, b_ref, y_ref, o_ref, dest_ref, gbuf_ref, sem_ref, *, alpha):
    i, n_steps = pl.program_id(0), pl.num_programs(0)
    tm = h_ref.shape[0]

    def fetch(step, slot):
        cp = pltpu.make_async_copy(dest_hbm.at[step], dest_ref.at[slot], sem_ref.at[2])
        cp.start()
        cp.wait()

        def issue(tok, carry):
            for k in range(TOP_K):
                pltpu.make_async_copy(y_ref.at[pl.ds(dest_ref[slot, k, tok], 1), :],
                                      gbuf_ref.at[slot, k, pl.ds(tok, 1), :], sem_ref.at[slot]).start()
            return carry

        lax.fori_loop(0, tm, issue, 0)

    @pl.when(i == 0)
    def _():
        fetch(0, 0)

    @pl.when(i + 1 < n_steps)
    def _():
        fetch(i + 1, (i + 1) % 2)

    slot = i % 2
    for k in range(TOP_K):
        pltpu.make_async_copy(y_ref.at[pl.ds(0, tm), :], gbuf_ref.at[slot, k], sem_ref.at[slot]).wait()
    lo, hi = _unpack_halves(ysh_ref[...])
    wt = wt_ref[...]
    for k in range(TOP_K):
        glo, ghi = _unpack_halves(gbuf_ref[slot, k])
        wk = wt[:, k:k + 1]
        lo = lo + glo * wk
        hi = hi + ghi * wk
    x = alpha * h_ref[...] + jnp.concatenate([lo, hi], axis=1)
    mu = jnp.mean(x, axis=-1, keepdims=True)
    xc = x - mu
    var = jnp.mean(xc * xc, axis=-1, keepdims=True)
    o_ref[...] = xc * lax.rsqrt(var + LN_EPS) * g_ref[...] + b_ref[...]


def _combine(h2, ysh, y_sorted, dest, wt, g, b, alpha):
    t, d = h2.shape
    w = d // 2
    tm = _pick(t, (128,))
    vec = pl.BlockSpec((1, d), lambda i: (0, 0))
    return pl.pallas_call(
        functools.partial(_combine_body, alpha=alpha),
        grid=(t // tm,),
        in_specs=[pl.BlockSpec(memory_space=pl.ANY),
                  pl.BlockSpec((tm, TOP_K), lambda i: (i, 0)),
                  pl.BlockSpec((tm, d), lambda i: (i, 0)),
                  pl.BlockSpec((tm, w), lambda i: (i, 0)),
                  vec, vec,
                  pl.BlockSpec(memory_space=pl.ANY)],
        out_specs=pl.BlockSpec((tm, d), lambda i: (i, 0)),
        out_shape=jax.ShapeDtypeStruct((t, d), F32),
        scratch_shapes=[pltpu.SMEM((2, TOP_K, tm), I32), pltpu.VMEM((2, TOP_K, tm, w), U32),
                        pltpu.SemaphoreType.DMA((3,))],
        compiler_params=_cparams(("arbitrary",)),
        name="moe_combine",
    )(_slot_tiles(dest, tm), wt, h2, ysh, g.reshape(1, d), b.reshape(1, d), y_sorted)


def _rope_tables(pos):
    inv_freq = ROPE_THETA ** (-jnp.arange(0, MLA_ROPE, 2, dtype=F32) / MLA_ROPE)
    ang = pos.astype(F32)[..., None] * inv_freq
    return jnp.cos(ang), jnp.sin(ang)


def kernel(x, positions, meta_tokens, ln_in_g, ln_in_b, w_in, b_gate, conv_w, conv_b, dt_bias, a_log, d_skip,
           ssm_norm_g, w_ssm_proj, q_a_norm_g, w_q_b, kv_a_norm_g, w_kv_b, w_attn_proj, w_out, ln1_g, ln1_b,
           w_router, router_bias, w_exp_gate, w_exp_up, w_exp_down, w_sh_gate, w_sh_up, w_sh_down, ln2_g, ln2_b):
    bsz, seq, d = x.shape
    depth = w_in.shape[0]
    heads = dt_bias.shape[-1]
    d_inner = w_ssm_proj.shape[1]
    conv_dim = conv_w.shape[-1]
    q_rank = w_q_b.shape[1]
    kv_rank = w_kv_b.shape[1]
    hh = MLA_HEADS
    g = SSM_GROUPS
    hg = heads // g
    assert seq % CHUNK == 0 and d % (2 * LANES) == 0
    lp = seq + CHUNK
    t = bsz * seq
    alpha = (2.0 * depth) ** 0.25

    meta_chunk = jnp.concatenate([jnp.zeros((META_PAD, d), x.dtype), meta_tokens.astype(x.dtype)], axis=0)
    pos =jnp.concatenate([positions.astype(I32) + N_META, jnp.zeros((bsz, META_PAD), I32),
                           jnp.broadcast_to(jnp.arange(N_META, dtype=I32), (bsz, N_META))], axis=1)
    cos, sin = _rope_tables(pos)
    zr = jnp.zeros_like(cos)
    rope_c = jnp.concatenate([cos, cos, zr, zr], axis=-1)[:, :seq]
    rope_slo = jnp.concatenate([-sin, zr, zr, zr], axis=-1)[:, :seq]
    rope_shi = jnp.concatenate([zr, sin, zr, zr], axis=-1)[:, :seq]
    k_c = jnp.concatenate([cos, cos, zr, zr], axis=-1)
    k_s = jnp.concatenate([-sin, sin, zr, zr], axis=-1)

    h_f32, h_b16 = _layer_norm_in(x, meta_chunk, ln_in_g, ln_in_b)
    tm_all = _pick(lp, (1408, 1152, 1024, 896, 768, 640, 512, 384, 256, 128))
    tm_seq = _pick(seq, (1024, 512, 256, 128))

    out = None
    for l in range(depth):
        offs = [0]
        for wdt in (d_inner, conv_dim, heads, q_rank, kv_rank, MLA_ROPE, 2 * d):
            offs.append(offs[-1] + wdt)
        wt = jnp.swapaxes(w_in[l], 0, 1)
        w_kr = wt[offs[5]:offs[6]]
        small_cols = q_rank + kv_rank + 2 * MLA_ROPE + heads
        small_n = small_cols + (-small_cols) % 256
        wt_small = jnp.concatenate([wt[offs[3]:offs[4]], wt[offs[4]:offs[5]], w_kr, w_kr, wt[offs[2]:offs[3]],
                                    jnp.zeros((small_n - small_cols, d), F32)], axis=0)

        n_exp, _, ff = w_exp_gate[l].shape
        z3, w_attn_b = _mm(h_b16, wt, w_rows=(offs[0], d_inner), out_dtype=BF16, tm=tm_all, tn=512,
                           side=w_attn_proj[l], name="in_proj_z")
        xbc3, w_ssm_b = _mm(h_b16, wt, w_rows=(offs[1], conv_dim), out_dtype=BF16, tm=tm_all, tn=512,
                            side=w_ssm_proj[l], name="in_proj_xbc")
        small = _mm(h_b16, wt_small, w_rows=(0, small_n), out_dtype=F32, tm=tm_all, tn=256, name="in_proj_small")
        gates, w_out_b = _mm(h_b16, wt, w_rows=(offs[6], 2 * d), out_dtype=BF16, tm=tm_seq, tn=512,
                             n_row_tiles=seq // tm_seq, bias=b_gate[l], act="sigmoid", side=w_out[l],
                             name="in_proj_gates")
        qa_cols, kva_cols = (0, q_rank), (q_rank, kv_rank)
        o = q_rank + kv_rank
        kr2 = small[:, :, o:o + 2 * MLA_ROPE]; o += 2 * MLA_ROPE
        dt_raw = small[:, :, o:o + heads]

        dt_g = jnp.pad(dt_raw.reshape(bsz, lp, g, hg).transpose(0, 2, 1, 3), ((0, 0), (0, 0), (0, 0), (0, LANES - hg)))
        pad_h = lambda v: jnp.pad(v.astype(F32).reshape(g, 1, hg), ((0, 0), (0, 0), (0, LANES - hg)))
        y_ssm = _ssd(xbc3, z3, dt_g, conv_w[l].astype(F32), conv_b[l].reshape(1, conv_dim).astype(F32),
                     pad_h(dt_bias[l]), pad_h(a_log[l]),
                     jnp.repeat(d_skip[l].astype(F32), SSM_HEAD_DIM).reshape(1, d_inner),
                     ssm_norm_g[l].reshape(1, d_inner).astype(F32), d_inner, heads)

        qscale = (MLA_QK ** -0.5) * math.log2(math.e)
        head_pad = ((0, 0), (0, 0), (0, MLA_QKP - MLA_QK))
        w_q = jnp.pad(w_q_b[l].reshape(q_rank, hh, MLA_QK), head_pad).reshape(q_rank, hh * MLA_QKP).astype(BF16)
        q3 = _mm(small, w_q, a_cols=qa_cols, out_dtype=BF16, tm=tm_seq, tn=4 * MLA_QKP, n_row_tiles=seq // tm_seq,
                 gain=q_a_norm_g[l] * qscale, rope=(rope_c, rope_slo, rope_shi), name="q_proj")
        wkv = w_kv_b[l].reshape(kv_rank, hh, MLA_NOPE + MLA_V)
        w_k = jnp.pad(wkv[:, :, :MLA_NOPE], ((0, 0), (0, 0), (0, MLA_QKP - MLA_NOPE)))
        w_k = w_k.reshape(kv_rank, hh * MLA_QKP).astype(BF16)
        w_vt = jnp.pad(wkv[:, :, MLA_NOPE:], ((0, 0), (0, 0), (0, MLA_VP - MLA_V)))
        w_vt = w_vt.reshape(kv_rank, hh * MLA_VP).T.astype(BF16)
        ones_rows = jnp.tile(jnp.concatenate([jnp.zeros((MLA_V,), F32), jnp.ones((MLA_VP - MLA_V,), F32)]), hh)
        k3 = _mm(small, w_k, a_cols=kva_cols, out_dtype=BF16, tm=tm_all, tn=4 * MLA_QKP, gain=kv_a_norm_g[l],
                 kadd=(kr2, k_c, k_s), name="k_proj")
        vt3 = _mm_t(small, w_vt, kv_a_norm_g[l], ones_rows, a_cols=kva_cols, out_dtype=BF16, tm=tm_all, tn=4 * MLA_VP,
                    name="v_proj")
        y_attn, wg_b, wu_b, wd_b = _attention(
            q3, k3, vt3, seq, [w_exp_gate[l].reshape(n_exp * d, ff), w_exp_up[l].reshape(n_exp * d, ff),
                               w_exp_down[l].reshape(n_exp * ff, d)])

        mixed = _mm_ktiled([y_ssm, y_attn], [w_ssm_b, w_attn_b],
                           out_dtype=BF16, rows=seq, tm=tm_seq, tn=_pick(d, (1024, 512)), nk=4,
                           gates=gates, name="branch_proj")
        pre1 = _mm_ktiled([mixed], [w_out_b], out_dtype=F32, rows=seq, tm=tm_seq,
                          tn=_pick(d, (1024, 512)), nk=2, res=h_f32, res_scale=alpha, name="out_proj")
        h1, h1p = _layer_norm_pack(pre1.reshape(t, d), ln1_g[l], ln1_b[l])

        e_slot, pos_slot, w_slot, cnt = _router(h1, w_router[l], router_bias[l])
        counts = cnt[:, 0].astype(I32)
        padded = (counts + MOE_ROWS - 1) // MOE_ROWS * MOE_ROWS
        ends = jnp.cumsum(padded)
        starts = ends - padded
        onehot = e_slot[None] == jnp.arange(N_EXPERTS, dtype=I32)[:, None, None]
        dest = pos_slot + jnp.sum(jnp.where(onehot, starts[:, None, None], 0), axis=0)
        n_blocks = -(-(t * TOP_K) // MOE_ROWS) + N_EXPERTS
        blk0 = jnp.arange(n_blocks, dtype=I32) * MOE_ROWS
        block_e = jnp.minimum(jnp.sum((ends[None, :] <= blk0[:, None]).astype(I32), axis=1), N_EXPERTS - 1)
        n_used = (ends[-1] // MOE_ROWS).reshape(1)
        xs = _dispatch(h1p, dest, counts, starts, n_blocks * MOE_ROWS)
        y_sorted = _experts(xs, wg_b.reshape(n_exp, d, ff), wu_b.reshape(n_exp, d, ff), wd_b.reshape(n_exp, ff, d),
                            block_e, n_used)
        y_shared = _experts(h1p, w_sh_gate[l][None].astype(BF16), w_sh_up[l][None].astype(BF16),
                            w_sh_down[l][None].astype(BF16), jnp.zeros((t // MOE_ROWS,), I32),
                            jnp.full((1,), t // MOE_ROWS, I32))
        out = _combine(h1, y_shared, y_sorted, dest, w_slot.T, ln2_g[l], ln2_b[l], alpha)
        if l + 1 < depth:
            raise NotImplementedError("stacked layers need the meta rows carried through the channel mixer")
    return out.reshape(bsz, seq, d)
```
